```python
import jax
import jax.numpy as jnp
from jax import lax
import numpy as np


D_MODEL = 2048
BATCH = 8
SEQ = 2048
DEPTH = 2

CTX_LEN = 256
GRID_W = 64
CHUNK = 64
Q_BLOCK = 128
EPS = 1e-6
N_BRANCH = 4
BR_W = 512

ML_HEADS = 4
ML_DH = 128
ML_W = ML_HEADS * ML_DH
HG_HEADS = 4
HG_DK = 128
HG_DV = 128
HG_W = HG_HEADS * HG_DV
MLA_HEADS = 4
MLA_Q_RANK = 512
MLA_KV_RANK = 256
MLA_NOPE = 128
MLA_ROPE = 64
MLA_DV = 128
MLA_SCALE = (MLA_NOPE + MLA_ROPE) ** -0.5
ROPE_BASE = 10000.0
SSD_HEADS = 8
SSD_P = 64
SSD_W = SSD_HEADS * SSD_P
SSD_GROUPS = 2
SSD_N = 128
SSD_GN = SSD_GROUPS * SSD_N
SSD_XBC = SSD_W + 2 * SSD_GN
SSD_CONV = 3
N_GROUPS = 4
EXP_PER_GROUP = 8
N_EXPERTS = N_GROUPS * EXP_PER_GROUP
TOP_K = 2
D_EXPERT = 512

IN_SIZES = (ML_W, ML_W, ML_W, ML_W, 4 * ML_HEADS,
            HG_W, HG_W, HG_W, 2 * HG_W,
            MLA_Q_RANK, MLA_KV_RANK, MLA_ROPE,
            SSD_W, SSD_XBC, 2 * SSD_HEADS)
D_IN = sum(IN_SIZES)

F32 = jnp.float32

kernel_name = 'hybrid_mlstm_hgrn2_mla_ssd_hmoe_dit'


def rmsnorm(x, g):
    x32 = x.astype(F32)
    y = x32 * lax.rsqrt(jnp.mean(x32 * x32, axis=-1, keepdims=True) + EPS)
    return (y * g.astype(F32)).astype(x.dtype)


def modulate(x, shift, scale):
    return x * (1.0 + scale) + shift


def split_cols(u, sizes):
    idx, acc = [], 0
    for s in sizes[:-1]:
        acc += s
        idx.append(acc)
    return jnp.split(u, idx, axis=-1)


def split_heads(t, n_heads):
    b, l, _ = t.shape
    return t.reshape(b, l, n_heads, -1).transpose(0, 2, 1, 3)


def merge_heads(t):
    b, n, l, d = t.shape
    return t.transpose(0, 2, 1, 3).reshape(b, l, n * d)


def chunk_tril():
    return jnp.tril(jnp.ones((CHUNK, CHUNK), dtype=bool))


def to_chunks(t):
    b, h, l = t.shape[:3]
    t = t.reshape((b, h, l // CHUNK, CHUNK) + t.shape[3:])
    return jnp.moveaxis(t, 2, 0)


def from_chunks(t):
    t = jnp.moveaxis(t, 0, 2)
    return t.reshape(t.shape[:2] + (t.shape[2] * t.shape[3],) + t.shape[4:])


def chunk_scan(body, inputs, state):
    xs = tuple(to_chunks(t) for t in inputs)
    state, ys = lax.scan(body, state, xs)
    return from_chunks(ys), state


def run_direction(body, ctx_in, lat_in, init, reverse):
    def flip(t):
        return jnp.flip(t, axis=2) if reverse else t
    y_ctx, state = chunk_scan(body, tuple(flip(t) for t in ctx_in), init)
    y_lat, _ = chunk_scan(body, tuple(flip(t) for t in lat_in), state)
    return flip(y_ctx), flip(y_lat)


def mlstm_chunk(carry, inp):
    c_mat, n_vec, m_sc = carry
    q, k, v, i_log, f_log = inp
    tril = chunk_tril()
    f_cum = jnp.cumsum(f_log, axis=-1)
    d_mat = jnp.where(tril, f_cum[..., :, None] - f_cum[..., None, :] + i_log[..., None, :], -jnp.inf)
    from_state = f_cum + m_sc[..., None]
    m_t = jnp.maximum(from_state, jnp.max(d_mat, axis=-1))
    s = jnp.einsum('bhtd,bhsd->bhts', q, k) * jnp.exp(d_mat - m_t[..., None])
    w_state = jnp.exp(from_state - m_t)
    num = jnp.einsum('bhts,bhsv->bhtv', s, v) + w_state[..., None] * jnp.einsum('bhtk,bhkv->bhtv', q, c_mat)
    den = jnp.sum(s, axis=-1) + w_state * jnp.einsum('bhtk,bhk->bht', q, n_vec)
    h = num / jnp.maximum(jnp.abs(den), jnp.exp(-m_t))[..., None]
    f_tot = f_cum[..., -1]
    w_log = f_tot[..., None] - f_cum + i_log
    m_new = jnp.maximum(f_tot + m_sc, jnp.max(w_log, axis=-1))
    decay = jnp.exp(f_tot + m_sc - m_new)
    w_in = jnp.exp(w_log - m_new[..., None])
    c_new = decay[..., None, None] * c_mat + jnp.einsum('bhs,bhsk,bhsv->bhkv', w_in, k, v)
    n_new = decay[..., None] * n_vec + jnp.einsum('bhs,bhsk->bhk', w_in, k)
    return (c_new, n_new, m_new), h


def mlstm_branch(parts_c, parts_l, f_bias):
    def prep(q, k, v, o, gates):
        b, l, _ = q.shape
        qh = split_heads(q.astype(F32), ML_HEADS) * (ML_DH ** -0.5)
        kh = split_heads(k.astype(F32), ML_HEADS)
        vh = split_heads(v.astype(F32), ML_HEADS)
        g = gates.astype(F32).reshape(b, l, 2, 2, ML_HEADS).transpose(2, 3, 0, 4, 1)
        f_log = jax.nn.log_sigmoid(g[:, 1] + f_bias.astype(F32)[:, None, :, None])
        return (qh, kh, vh, g[0, 0], f_log[0]), (qh, kh, vh, g[1, 0], f_log[1]), o
    fc, bc, oc = prep(*parts_c)
    fl, bl, ol = prep(*parts_l)
    b = parts_l[0].shape[0]
    init = (jnp.zeros((b, ML_HEADS, ML_DH, ML_DH), F32), jnp.zeros((b, ML_HEADS, ML_DH), F32),
            jnp.full((b, ML_HEADS), -jnp.inf, F32))
    yc_f, yl_f = run_direction(mlstm_chunk, fc, fl, init, False)
    yc_b, yl_b = run_direction(mlstm_chunk, bc, bl, init, True)

    def finish(y_f, y_b, o):
        return (merge_heads(y_f + y_b) * jax.nn.sigmoid(o.astype(F32))).astype(o.dtype)
    return finish(yc_f, yc_b, oc), finish(yl_f, yl_b, ol)


def hgrn2_chunk(s_state, inp):
    q, k, v, lf = inp
    tril = chunk_tril()
    g_cum = jnp.cumsum(lf, axis=2)
    rel = jnp.where(tril[:, :, None], g_cum[:, :, :, None, :] - g_cum[:, :, None, :, :], -jnp.inf)
    a = jnp.einsum('bhtd,bhsd,bhtsd->bhts', q, k, jnp.exp(rel))
    o = jnp.einsum('bhts,bhsv->bhtv', a, v) + jnp.einsum('bhtd,bhdv->bhtv', q * jnp.exp(g_cum), s_state)
    g_tot = g_cum[:, :, -1]
    k_dec = k * jnp.exp(g_tot[:, :, None, :] - g_cum)
    s_new = jnp.exp(g_tot)[..., None] * s_state + jnp.einsum('bhsd,bhsv->bhdv', k_dec, v)
    return s_new, o


def hgrn2_branch(parts_c, parts_l, lb, norm_g):
    def prep(q, i, g, f_raw):
        b, l, _ = q.shape
        qh = split_heads(jax.nn.silu(q.astype(F32)), HG_HEADS)
        vh = split_heads(i.astype(F32), HG_HEADS)
        f = lb + (1.0 - lb) * jax.nn.sigmoid(f_raw.astype(F32).reshape(b, l, 2, HG_W))
        dirs = []
        for d in range(2):
            fd = split_heads(f[:, :, d], HG_HEADS)
            dirs.append((qh, 1.0 - fd, vh, jnp.log(fd)))
        return dirs[0], dirs[1], g
    fc, bc, gc = prep(*parts_c)
    fl, bl, gl = prep(*parts_l)
    b = parts_l[0].shape[0]
    init = jnp.zeros((b, HG_HEADS, HG_DK, HG_DV), F32)
    yc_f, yl_f = run_direction(hgrn2_chunk, fc, fl, init, False)
    yc_b, yl_b = run_direction(hgrn2_chunk, bc, bl, init, True)

    def finish(y_f, y_b, g):
        bb, h, l, dv = y_f.shape
        o = (y_f + y_b).transpose(0, 2, 1, 3)
        o = rmsnorm(o, norm_g.reshape(HG_HEADS, HG_DV)).reshape(bb, l, HG_W)
        return (o * jax.nn.sigmoid(g.astype(F32))).astype(g.dtype)
    return finish(yc_f, yc_b, gc), finish(yl_f, yl_b, gl)


def axial_rope(x, pos_row, pos_col):
    half = MLA_ROPE // 2
    quarter = half // 2
    inv_freq = ROPE_BASE ** (-jnp.arange(quarter, dtype=F32) / quarter)

    def rot(xa, pos):
        ang = pos.astype(F32)[:, None] * inv_freq
        shape = (ang.shape[0],) + (1,) * (xa.ndim - 3) + (quarter,)
        cos, sin = jnp.cos(ang).reshape(shape), jnp.sin(ang).reshape(shape)
        x1, x2 = xa[..., :quarter], xa[..., quarter:]
        return jnp.concatenate([x1 * cos - x2 * sin, x1 * sin + x2 * cos], axis=-1)
    x32 = x.astype(F32)
    return jnp.concatenate([rot(x32[..., :half], pos_row), rot(x32[..., half:], pos_col)], axis=-1).astype(x.dtype)


def softmax_attend(q, k, v):
    s = jnp.einsum('bhqd,bhkd->bhqk', q, k).astype(F32) * MLA_SCALE
    p = jax.nn.softmax(s, axis=-1)
    return jnp.einsum('bhqk,bhkv->bhqv', p.astype(v.dtype), v)


def mla_branch(parts_c, parts_l, pos, g_q, g_kv, w_uq, w_ukv, emit_ctx):
    def project(cq, ckv, k_rope, rope_pos):
        b, l, _ = cq.shape
        q = (rmsnorm(cq, g_q) @ w_uq).reshape(b, l, MLA_HEADS, MLA_NOPE + MLA_ROPE)
        kv = (rmsnorm(ckv, g_kv) @ w_ukv).reshape(b, l, MLA_HEADS, MLA_NOPE + MLA_DV)
        q_nope, q_rope = q[..., :MLA_NOPE], q[..., MLA_NOPE:]
        k_nope, v = kv[..., :MLA_NOPE], kv[..., MLA_NOPE:]
        if rope_pos is not None:
            q_rope = axial_rope(q_rope, *rope_pos)
            k_rope = axial_rope(k_rope, *rope_pos)
        k_rope = jnp.broadcast_to(k_rope[:, :, None, :], (b, l, MLA_HEADS, MLA_ROPE))
        q = jnp.concatenate([q_nope, q_rope], axis=-1).transpose(0, 2, 1, 3)
        k = jnp.concatenate([k_nope, k_rope], axis=-1).transpose(0, 2, 1, 3)
        return q, k, v.transpose(0, 2, 1, 3)
    qc, kc, vc = project(*parts_c, None)
    ql, kl, vl = project(*parts_l, pos)
    k_all = jnp.concatenate([kc, kl], axis=2)
    v_all = jnp.concatenate([vc, vl], axis=2)
    b, h, s, dh = ql.shape
    n_blk = s // Q_BLOCK
    q_blocks = ql.reshape(b, h, n_blk, Q_BLOCK, dh).transpose(2, 0, 1, 3, 4)

    def attend(q_blk):
        return softmax_attend(q_blk, k_all, v_all)
    ol = lax.map(attend, q_blocks)
    ol = ol.transpose(1, 0, 3, 2, 4).reshape(b, s, h * MLA_DV)
    oc = merge_heads(softmax_attend(qc, kc, vc)) if emit_ctx else None
    return oc, ol


def centred_dwconv(x, w, b):
    pad = (SSD_CONV - 1) // 2
    y = lax.conv_general_dilated(x, w[:, None, :].astype(x.dtype), window_strides=(1,),
                                 padding=[(pad, pad)], dimension_numbers=('NWC', 'WIO', 'NWC'),
                                 feature_group_count=x.shape[-1])
    return y + b.astype(x.dtype)


def ssd_chunk(h_state, inp):
    x, bm, cm, da, dt = inp
    tril = chunk_tril()
    a_cum = jnp.cumsum(da, axis=-1)
    decay = jnp.exp(jnp.where(tril, a_cum[..., :, None] - a_cum[..., None, :], -jnp.inf))
    xdt = x * dt[..., None]
    s = jnp.einsum('bhtn,bhsn->bhts', cm, bm) * decay
    y = jnp.einsum('bhts,bhsp->bhtp', s, xdt) + jnp.exp(a_cum)[..., None] * jnp.einsum('bhtn,bhpn->bhtp', cm, h_state)
    a_tot = a_cum[..., -1]
    w = jnp.exp(a_tot[..., None] - a_cum)
    h_new = jnp.exp(a_tot)[..., None, None] * h_state + jnp.einsum('bhs,bhsp,bhsn->bhpn', w, xdt, bm)
    return h_new, y


def ssd_branch(parts_c, parts_l, conv_w, conv_b, a_log, dt_bias, d_skip, norm_g):
    a_neg = -jnp.exp(a_log.astype(F32))

    def prep(z, xbc, dt_raw):
        b, l, _ = xbc.shape
        xbc = jax.nn.silu(centred_dwconv(xbc, conv_w, conv_b)).astype(F32)
        xs, bm, cm = xbc[..., :SSD_W], xbc[..., SSD_W:SSD_W + SSD_GN], xbc[..., SSD_W + SSD_GN:]
        xh = split_heads(xs, SSD_HEADS)

        def group_to_heads(t):
            t = t.reshape(b, l, SSD_GROUPS, SSD_N)
            return jnp.repeat(t, SSD_HEADS // SSD_GROUPS, axis=2).transpose(0, 2, 1, 3)
        bh, ch = group_to_heads(bm), group_to_heads(cm)
        dt = jax.nn.softplus(dt_raw.astype(F32).reshape(b, l, 2, SSD_HEADS) + dt_bias.astype(F32))
        dt = dt.transpose(2, 0, 3, 1)
        da = dt * a_neg[:, None, :, None]
        return (xh, bh, ch, da[0], dt[0]), (xh, bh, ch, da[1], dt[1]), xh, z
    fc, bc, xc, zc = prep(*parts_c)
    fl, bl, xl, zl = prep(*parts_l)
    b = parts_l[0].shape[0]
    init = jnp.zeros((b, SSD_HEADS, SSD_P, SSD_N), F32)
    yc_f, yl_f = run_direction(ssd_chunk, fc, fl, init, False)
    yc_b, yl_b = run_direction(ssd_chunk, bc, bl, init, True)

    def finish(y_f, y_b, xh, z):
        y = y_f + y_b + d_skip.astype(F32)[None, :, None, None] * xh
        y = merge_heads(y) * jax.nn.silu(z.astype(F32))
        return rmsnorm(y, norm_g).astype(z.dtype)
    return finish(yc_f, yc_b, xc, zc), finish(yl_f, yl_b, xl, zl)


def merge_branches(h, branches, w_gate, w_br, w_out):
    acc = None
    for k in range(N_BRANCH):
        gate = jax.nn.sigmoid((h @ w_gate[k]).astype(F32))
        term = gate * (branches[k] @ w_br[k]).astype(F32)
        acc = term if acc is None else acc + term
    return acc.astype(h.dtype) @ w_out


def mixer_sublayer(hc, hl, pos, lb, w_in, ml_f_bias, hg_norm_g, mla_g_q, mla_g_kv, mla_w_uq, mla_w_ukv,
                   ssd_conv_w, ssd_conv_b, ssd_a_log, ssd_dt_bias, ssd_d, ssd_norm_g,
                   w_gate, w_br, w_out, emit_ctx):
    n_ctx = hc.shape[1]
    h = jnp.concatenate([hc, hl], axis=1)
    parts = split_cols(h @ w_in, IN_SIZES)
    pc = [t[:, :n_ctx] for t in parts]
    pl = [t[:, n_ctx:] for t in parts]
    ml = mlstm_branch(pc[0:5], pl[0:5], ml_f_bias)
    hg = hgrn2_branch(pc[5:9], pl[5:9], lb, hg_norm_g)
    mla = mla_branch(pc[9:12], pl[9:12], pos, mla_g_q, mla_g_kv, mla_w_uq, mla_w_ukv, emit_ctx)
    ssd = ssd_branch(pc[12:15], pl[12:15], ssd_conv_w, ssd_conv_b, ssd_a_log, ssd_dt_bias, ssd_d, ssd_norm_g)
    outs = (ml, hg, mla, ssd)
    if emit_ctx:
        y = merge_branches(h, [jnp.concatenate([o[0], o[1]], axis=1) for o in outs], w_gate, w_br, w_out)
        return y[:, :n_ctx], y[:, n_ctx:]
    return None, merge_branches(hl, [o[1] for o in outs], w_gate, w_br, w_out)


def moe_ffn(h, w_grp, w_exp, w1, w3, w2):
    b, l, _ = h.shape
    grp_prob = jax.nn.softmax((h @ w_grp).astype(F32), axis=-1)
    grp_p, grp_idx = lax.top_k(grp_prob, 1)
    exp_logits = (h @ w_exp).astype(F32).reshape(b, l, N_GROUPS, EXP_PER_GROUP)
    sel = jnp.take_along_axis(exp_logits, grp_idx[..., None], axis=2)[..., 0, :]
    top_v, top_i = lax.top_k(sel, TOP_K)
    w_top = jax.nn.softmax(top_v, axis=-1) * grp_p
    expert_id = grp_idx * EXP_PER_GROUP + top_i
    combine = jnp.sum(jax.nn.one_hot(expert_id, N_EXPERTS, dtype=F32) * w_top[..., None], axis=-2)
    out = None
    for e in range(N_EXPERTS):
        y = (jax.nn.silu(h @ w1[e]) * (h @ w3[e])) @ w2[e]
        term = combine[..., e:e + 1] * y.astype(F32)
        out = term if out is None else out + term
    return out.astype(h.dtype)


def setup_inputs(seed: int = 0) -> dict:
    key = jax.random.key(seed)
    ks = jax.random.split(key, 29)

    def nrm(k, shape, scale):
        return scale * jax.random.normal(k, shape, F32)
    d = D_MODEL
    dt0 = jnp.exp(jax.random.uniform(ks[18], (DEPTH, 2, SSD_HEADS), F32, np.log(1e-3), np.log(1e-1)))
    return {
        'x': nrm(ks[0], (BATCH, SEQ, d), 1.0),
        'c': nrm(ks[1], (BATCH, d), 1.0),
        'ctx': nrm(ks[2], (BATCH, CTX_LEN, d), 1.0),
        'c_ctx': nrm(ks[3], (d,), 1.0),
        'w_ada': nrm(ks[4], (DEPTH, d, 6 * d), 0.2 * d ** -0.5),
        'b_ada': nrm(ks[5], (DEPTH, 6 * d), 0.01),
        'norm_g': 1.0 + nrm(ks[6], (DEPTH, 4, d), 0.05),
        'w_in': nrm(ks[7], (DEPTH, d, D_IN), d ** -0.5),
        'ml_f_bias': jax.random.uniform(ks[8], (DEPTH, 2, ML_HEADS), F32, 3.0, 6.0),
        'hg_lb_logits': nrm(ks[9], (DEPTH + 1, HG_W), 0.5),
        'hg_norm_g': 1.0 + nrm(ks[10], (DEPTH, HG_W), 0.05),
        'mla_g_q': 1.0 + nrm(ks[11], (DEPTH, MLA_Q_RANK), 0.05),
        'mla_g_kv': 1.0 + nrm(ks[12], (DEPTH, MLA_KV_RANK), 0.05),
        'mla_w_uq': nrm(ks[13], (DEPTH, MLA_Q_RANK, MLA_HEADS * (MLA_NOPE + MLA_ROPE)), MLA_Q_RANK ** -0.5),
        'mla_w_ukv': nrm(ks[14], (DEPTH, MLA_KV_RANK, MLA_HEADS * (MLA_NOPE + MLA_DV)), MLA_KV_RANK ** -0.5),
        'ssd_conv_w': nrm(ks[15], (DEPTH, SSD_CONV, SSD_XBC), SSD_CONV ** -0.5),
        'ssd_conv_b': nrm(ks[16], (DEPTH, SSD_XBC), 0.01),
        'ssd_a_log': jnp.log(jax.random.uniform(ks[17], (DEPTH, 2, SSD_HEADS), F32, 1.0, 16.0)),
        'ssd_dt_bias': dt0 + jnp.log(-jnp.expm1(-dt0)),
        'ssd_d': 1.0 + nrm(ks[19], (DEPTH, SSD_HEADS), 0.1),
        'ssd_norm_g': 1.0 + nrm(ks[20], (DEPTH, SSD_W), 0.05),
        'w_gate': nrm(ks[21], (DEPTH, N_BRANCH, d, d), d ** -0.5),
        'w_br': nrm(ks[22], (DEPTH, N_BRANCH, BR_W, d), BR_W ** -0.5),
        'w_out': nrm(ks[23], (DEPTH, d, d), d ** -0.5),
        'moe_w_grp': nrm(ks[24], (DEPTH, d, N_GROUPS), d ** -0.5),
        'moe_w_exp': nrm(ks[25], (DEPTH, d, N_EXPERTS), d ** -0.5),
        'moe_w1': nrm(ks[26], (DEPTH, N_EXPERTS, d, D_EXPERT), d ** -0.5),
        'moe_w3': nrm(ks[27], (DEPTH, N_EXPERTS, d, D_EXPERT), d ** -0.5),
        'moe_w2': nrm(ks[28], (DEPTH, N_EXPERTS, D_EXPERT, d), D_EXPERT ** -0.5),
    }


def reference(x, c, ctx, c_ctx, w_ada, b_ada, norm_g, w_in, ml_f_bias, hg_lb_logits, hg_norm_g,
              mla_g_q, mla_g_kv, mla_w_uq, mla_w_ukv, ssd_conv_w, ssd_conv_b, ssd_a_log, ssd_dt_bias,
              ssd_d, ssd_norm_g, w_gate, w_br, w_out, moe_w_grp, moe_w_exp, moe_w1, moe_w3, moe_w2):
    n_ctx = ctx.shape[1]
    rows = x.shape[1] // GRID_W
    tok = jnp.arange(rows * GRID_W)
    pos = (tok // GRID_W, tok % GRID_W)
    lb_all = jnp.cumsum(jax.nn.softmax(hg_lb_logits.astype(F32), axis=0), axis=0)
    s_c = jax.nn.silu(c)
    s_ctx = jax.nn.silu(c_ctx)
    xl, xc = x, ctx
    for l in range(DEPTH):
        emit_ctx = l < DEPTH - 1
        mod_l = [t[:, None, :] for t in jnp.split(s_c @ w_ada[l] + b_ada[l], 6, axis=-1)]
        mod_c = jnp.split(s_ctx @ w_ada[l] + b_ada[l], 6, axis=-1)
        hl = modulate(rmsnorm(xl, norm_g[l, 0]), mod_l[0], mod_l[1])
        hc = modulate(rmsnorm(xc, norm_g[l, 0]), mod_c[0], mod_c[1])
        yc, yl = mixer_sublayer(hc, hl, pos, lb_all[l], w_in[l], ml_f_bias[l], hg_norm_g[l],
                                mla_g_q[l], mla_g_kv[l], mla_w_uq[l], mla_w_ukv[l],
                                ssd_conv_w[l], ssd_conv_b[l], ssd_a_log[l], ssd_dt_bias[l], ssd_d[l],
                                ssd_norm_g[l], w_gate[l], w_br[l], w_out[l], emit_ctx)
        xl = xl + mod_l[2] * rmsnorm(yl, norm_g[l, 1])
        hl2 = modulate(rmsnorm(xl, norm_g[l, 2]), mod_l[3], mod_l[4])
        if emit_ctx:
            xc = xc + mod_c[2] * rmsnorm(yc, norm_g[l, 1])
            hc2 = modulate(rmsnorm(xc, norm_g[l, 2]), mod_c[3], mod_c[4])
            y2 = moe_ffn(jnp.concatenate([hc2, hl2], axis=1), moe_w_grp[l], moe_w_exp[l], moe_w1[l], moe_w3[l], moe_w2[l])
            xc = xc + mod_c[5] * rmsnorm(y2[:, :n_ctx], norm_g[l, 3])
            yl2 = y2[:, n_ctx:]
        else:
            yl2 = moe_ffn(hl2, moe_w_grp[l], moe_w_exp[l], moe_w1[l], moe_w3[l], moe_w2[l])
        xl = xl + mod_l[5] * rmsnorm(yl2, norm_g[l, 3])
    return xl
```

```python
import functools

import jax
import jax.numpy as jnp
from jax import lax
from jax.experimental import pallas as pl
from jax.experimental.pallas import tpu as pltpu

F32 = jnp.float32
BF16 = jnp.bfloat16

D = 2048
SEQ = 2048
CTX = 256
LT = SEQ + CTX
DEPTH = 2
EPS = 1e-6
GRID_W = 64
NH = 4
DH = 128
BW = 512
MLA_Q_RANK = 512
MLA_KV_RANK = 256
MLA_NOPE = 128
MLA_ROPE = 64
MLA_SCALE = (MLA_NOPE + MLA_ROPE) ** -0.5
ROPE_BASE = 10000.0
SSD_HEADS = 8
SSD_P = 64
SSD_GROUPS = 2
SSD_N = 128
SSD_XBC = 1024
N_GROUPS = 4
EXP_PER_GROUP = 8
N_EXPERTS = 32
D_EXPERT = 512

CS = 128
NLC = SEQ // CS
NCC = CTX // CS
NCH = NLC + NCC
RT = 256
NRT = LT // RT
TE = 256
VMEM_LIMIT = 52 * 1024 * 1024

U_W = 7168
CB_ML_Q, CB_ML_K, CB_ML_V, CB_ML_O = 0, 1, 2, 3
CB_HG_Q, CB_HG_I, CB_HG_G, CB_HG_F = 4, 5, 6, 7
CB_SSD_Z = 9
CB_SSD_XBC = 5
CB_MLA_CQ = 12
CB_MLA_CKV = 26
CB_MLA_KR = 54


def _cparams(sem):
    return pltpu.CompilerParams(dimension_semantics=sem, vmem_limit_bytes=VMEM_LIMIT)


def _dot(a, b):
    return jnp.dot(a, b, preferred_element_type=F32)


def _dot_nt(a, b):
    return lax.dot_general(a, b, (((1,), (1,)), ((), ())), preferred_element_type=F32)


def _sigmoid(x):
    return 1.0 / (1.0 + jnp.exp(-x))


def _silu(x):
    return x * _sigmoid(x)


def _softplus(x):
    return jnp.maximum(x, 0.0) + jnp.log(1.0 + jnp.exp(-jnp.abs(x)))


def _log_sigmoid(x):
    return -_softplus(-x)


def _rms(x):
    return x * lax.rsqrt(jnp.mean(x * x, axis=-1, keepdims=True) + EPS)


def _split3(x):
    a = x.astype(BF16)
    r = x - a.astype(F32)
    b = r.astype(BF16)
    c = (r - b.astype(F32)).astype(BF16)
    return a, b, c


def _cumsum_rows(mb, x):
    a, b, c = _split3(x)
    return _dot(mb, a) + _dot(mb, b) + _dot(mb, c)


def _cumsum_cols(x, mb):
    a, b, c = _split3(x)
    return _dot_nt(a, mb) + _dot_nt(b, mb) + _dot_nt(c, mb)


def _chunk_mask(rev):
    row = lax.broadcasted_iota(jnp.int32, (CS, CS), 0)
    col = lax.broadcasted_iota(jnp.int32, (CS, CS), 1)
    return (col >= row) if rev else (col <= row)


def _chunk_of_step(rev):
    if rev:
        return lambda s: NCH - 1 - s
    return lambda s: (s + NLC) % NCH


def _mm_kernel(a_ref, w_ref, b_ref, o_ref):
    acc = _dot(a_ref[...], w_ref[...].astype(BF16))
    o_ref[...] = (acc + b_ref[...]).astype(o_ref.dtype)


def _mm(a, w, bias, out_dtype, tm, tn):
    m, k = a.shape
    n = w.shape[1]
    return pl.pallas_call(
        _mm_kernel,
        grid=(m // tm, n // tn),
        in_specs=[pl.BlockSpec((tm, k), lambda i, j: (i, 0)),
                  pl.BlockSpec((k, tn), lambda i, j: (0, j)),
                  pl.BlockSpec((1, tn), lambda i, j: (0, j))],
        out_specs=pl.BlockSpec((tm, tn), lambda i, j: (i, j)),
        out_shape=jax.ShapeDtypeStruct((m, n), out_dtype),
        compiler_params=_cparams(("parallel", "arbitrary")),
        name="mm",
    )(a, w, bias)


def _mod_spec(layer, k):
    def imap(b, i):
        row = jnp.where(i == NRT - 1, 8, b)
        return ((layer * 16 + row) * 6 + k, 0, 0)
    return pl.BlockSpec((None, 1, D), imap)


def _norm_spec(layer, k):
    return pl.BlockSpec((None, 1, D), lambda b, i: (layer * 4 + k, 0, 0))


def _row_spec(width, dtype_rows=RT):
    return pl.BlockSpec((None, dtype_rows, width), lambda b, i: (b, i, 0))


def _prenorm_kernel(x_ref, g_ref, sh_ref, sc_ref, h_ref):
    y = _rms(x_ref[...]) * g_ref[...]
    h_ref[...] = (y * (1.0 + sc_ref[...]) + sh_ref[...]).astype(BF16)


def _prenorm(xcat, normv, modv, layer):
    nb = xcat.shape[0]
    return pl.pallas_call(
        _prenorm_kernel,
        grid=(nb, NRT),
        in_specs=[_row_spec(D), _norm_spec(layer, 0), _mod_spec(layer, 0), _mod_spec(layer, 1)],
        out_specs=_row_spec(D),
        out_shape=jax.ShapeDtypeStruct((nb, LT, D), BF16),
        compiler_params=_cparams(("parallel", "parallel")),
        name="prenorm",
    )(xcat, normv, modv, modv)


def _conv_kernel(x_ref, p_ref, n_ref, w_ref, b_ref, o_ref):
    i = pl.program_id(1)
    x = x_ref[...].astype(F32)
    has_prev = jnp.logical_and(i != 0, i != NRT - 1).astype(F32)
    has_next = jnp.logical_and(i != NRT - 2, i != NRT - 1).astype(F32)
    prev_row = p_ref[15:16, :].astype(F32) * has_prev
    next_row = n_ref[0:1, :].astype(F32) * has_next
    ridx = lax.broadcasted_iota(jnp.int32, (RT, 1), 0)
    xm1 = jnp.where(ridx == 0, prev_row, pltpu.roll(x, 1, 0))
    xp1 = jnp.where(ridx == RT - 1, next_row, pltpu.roll(x, RT - 1, 0))
    w = w_ref[...]
    y = w[0:1, :] * xm1 + w[1:2, :] * x + w[2:3, :] * xp1 + b_ref[...]
    o_ref[...] = _silu(y)


def _ssd_conv(u, conv_w, conv_b):
    nb = u.shape[0]
    hb = RT // 16
    last = LT // 16 - 1
    return pl.pallas_call(
        _conv_kernel,
        grid=(nb, NRT),
        in_specs=[pl.BlockSpec((None, RT, SSD_XBC), lambda b, i: (b, i, CB_SSD_XBC)),
                  pl.BlockSpec((None, 16, SSD_XBC), lambda b, i: (b, jnp.maximum(i * hb - 1, 0), CB_SSD_XBC)),
                  pl.BlockSpec((None, 16, SSD_XBC), lambda b, i: (b, jnp.minimum((i + 1) * hb, last), CB_SSD_XBC)),
                  pl.BlockSpec((8, SSD_XBC), lambda b, i: (0, 0)),
                  pl.BlockSpec((1, SSD_XBC), lambda b, i: (0, 0))],
        out_specs=pl.BlockSpec((None, RT, SSD_XBC), lambda b, i: (b, i, 0)),
        out_shape=jax.ShapeDtypeStruct((nb, LT, SSD_XBC), F32),
        compiler_params=_cparams(("parallel", "parallel")),
        name="ssd_conv",
    )(u, u, u, conv_w, conv_b)


def _mlstm_kernel(rev, *refs):
    if rev:
        (q_ref, k_ref, v_ref, gc_ref, gt_ref, bc_ref, bt_ref, yp_ref, og_ref,
         o_ref, c_sc, n_sc, m_sc) = refs
    else:
        (q_ref, k_ref, v_ref, gc_ref, gt_ref, bc_ref, bt_ref,
         o_ref, c_sc, n_sc, m_sc) = refs
    d = 1 if rev else 0

    @pl.when(pl.program_id(1) == 0)
    def _():
        c_sc[...] = jnp.zeros_like(c_sc)
        n_sc[...] = jnp.zeros_like(n_sc)
        m_sc[...] = jnp.full(m_sc.shape, -jnp.inf, F32)

    mask = _chunk_mask(rev)
    mb = jnp.where(mask, 1.0, 0.0).astype(BF16)
    gc = gc_ref[...] + bc_ref[...]
    gt = gt_ref[...] + bt_ref[...]
    flog_c = _log_sigmoid(gc)
    flog_t = _log_sigmoid(gt)
    fcum_c = _cumsum_rows(mb, flog_c)
    fcum_t = _cumsum_cols(flog_t, mb)
    scale = DH ** -0.5
    for h in range(NH):
        ci = d * 8 + h
        cf = d * 8 + 4 + h
        sl = slice(h * DH, (h + 1) * DH)
        q = q_ref[:, sl]
        k = k_ref[:, sl]
        v = v_ref[:, sl]
        i_col = gc[:, ci:ci + 1]
        f_col = flog_c[:, cf:cf + 1]
        fc_col = fcum_c[:, cf:cf + 1]
        i_row = gt[ci:ci + 1, :]
        fc_row = fcum_t[cf:cf + 1, :]
        m_prev = m_sc[h][0:1, 0:1]
        dm = jnp.where(mask, fc_col - fc_row + i_row, -jnp.inf)
        from_state = fc_col + m_prev
        m_t = jnp.maximum(from_state, jnp.max(dm, axis=1, keepdims=True))
        s = _dot_nt(q, k) * scale * jnp.exp(dm - m_t)
        w_state = jnp.exp(from_state - m_t)
        qc = _dot(q, c_sc[h].astype(BF16)) * scale
        num = _dot(s.astype(BF16), v) + w_state * qc
        qn = jnp.sum(q.astype(F32) * n_sc[h][0:1, :], axis=1, keepdims=True) * scale
        den = jnp.sum(s, axis=1, keepdims=True) + w_state * qn
        hout = num / jnp.maximum(jnp.abs(den), jnp.exp(-m_t))
        if rev:
            tot = yp_ref[:, sl] + hout
            o_ref[:, sl] = (tot * _sigmoid(og_ref[:, sl].astype(F32))).astype(o_ref.dtype)
        else:
            o_ref[:, sl] = hout
        f_tot = jnp.sum(f_col, axis=0, keepdims=True)
        w_log = f_tot - fc_col + i_col
        m_new = jnp.maximum(f_tot + m_prev, jnp.max(w_log, axis=0, keepdims=True))
        decay = jnp.exp(f_tot + m_prev - m_new)
        kw = k.astype(F32) * jnp.exp(w_log - m_new)
        c_sc[h] = decay * c_sc[h] + _dot(kw.T.astype(BF16), v)
        n_new = decay * n_sc[h][0:1, :] + jnp.sum(kw, axis=0, keepdims=True)
        n_sc[h] = jnp.broadcast_to(n_new, (8, DH))
        m_sc[h] = jnp.broadcast_to(m_new, (8, DH))


def _u_spec(width, cblk, cmap):
    return pl.BlockSpec((None, CS, width), lambda b, s: (b, cmap(s), cblk))


def _mlstm(u, usm, ust, bias_c, bias_t, rev, yprev=None):
    nb = u.shape[0]
    cmap = _chunk_of_step(rev)
    in_specs = [_u_spec(BW, CB_ML_Q, cmap), _u_spec(BW, CB_ML_K, cmap), _u_spec(BW, CB_ML_V, cmap),
                pl.BlockSpec((None, CS, 128), lambda b, s: (b, cmap(s), 0)),
                pl.BlockSpec((None, None, 32, CS), lambda b, s: (b, cmap(s), 0, 0)),
                pl.BlockSpec((1, 128), lambda b, s: (0, 0)),
                pl.BlockSpec((32, 1), lambda b, s: (0, 0))]
    args = [u, u, u, usm, ust, bias_c, bias_t]
    if rev:
        in_specs += [pl.BlockSpec((None, CS, BW), lambda b, s: (b, cmap(s), 0)),
                     _u_spec(BW, CB_ML_O, cmap)]
        args += [yprev, u]
    return pl.pallas_call(
        functools.partial(_mlstm_kernel, rev),
        grid=(nb, NCH),
        in_specs=in_specs,
        out_specs=pl.BlockSpec((None, CS, BW), lambda b, s: (b, cmap(s), 0)),
        out_shape=jax.ShapeDtypeStruct((nb, LT, BW), BF16 if rev else F32),
        scratch_shapes=[pltpu.VMEM((NH, DH, DH), F32), pltpu.VMEM((NH, 8, DH), F32),
                        pltpu.VMEM((NH, 8, DH), F32)],
        compiler_params=_cparams(("parallel", "arbitrary")),
        name="mlstm_rev" if rev else "mlstm_fwd",
    )(*args)


def _block_ref(g, n, off):
    g3 = g.reshape(CS // n, n, DH)
    r = g3[:, off:off + 1, :]
    return jnp.broadcast_to(r, (CS // n, n, DH)).reshape(CS, DH)


def _hgrn2_kernel(rev, *refs):
    if rev:
        q_ref, i_ref, f_ref, lb_ref, yp_ref, g_ref, ng_ref, o_ref, st_sc = refs
    else:
        q_ref, i_ref, f_ref, lb_ref, o_ref, st_sc = refs

    @pl.when(pl.program_id(1) == 0)
    def _():
        st_sc[...] = jnp.zeros_like(st_sc)

    mask = _chunk_mask(rev)
    mb = jnp.where(mask, 1.0, 0.0).astype(BF16)
    row = lax.broadcasted_iota(jnp.int32, (CS, CS), 0)
    col = lax.broadcasted_iota(jnp.int32, (CS, CS), 1)
    tix = lax.broadcasted_iota(jnp.int32, (CS, 1), 0)
    for h in range(NH):
        sl = slice(h * DH, (h + 1) * DH)
        q = _silu(q_ref[:, sl].astype(F32))
        v = i_ref[:, sl].astype(F32)
        lb = lb_ref[:, sl]
        f = lb + (1.0 - lb) * _sigmoid(f_ref[:, sl].astype(F32))
        k = 1.0 - f
        lf = jnp.log(f)
        g = _cumsum_rows(mb, lf)
        st = st_sc[h]
        o = _dot_nt((q * jnp.exp(g)).astype(BF16), st.astype(BF16))
        a = jnp.zeros((CS, CS), F32)
        n = CS
        while n >= 16:
            half = n // 2
            second = jnp.bitwise_and(tix, n - 1) >= half
            rowside = jnp.logical_not(second) if rev else second
            r = _block_ref(g, n, half if rev else half - 1)
            e = jnp.exp(jnp.where(rowside, g - r, r - g))
            qs = jnp.where(rowside, q * e, 0.0).astype(BF16)
            ks = jnp.where(rowside, 0.0, k * e).astype(BF16)
            sh_n = n.bit_length() - 1
            same = jnp.right_shift(row, sh_n) == jnp.right_shift(col, sh_n)
            a = a + jnp.where(same, _dot_nt(qs, ks), 0.0)
            n = half
        o = o + _dot(a.astype(BF16), v.astype(BF16))
        t8 = jnp.bitwise_and(tix, 7)
        for dl in range(8):
            if dl == 0:
                ksh, gsh, vsh = k, g, v
            else:
                sh = (CS - dl) if rev else dl
                ksh = pltpu.roll(k, sh, 0)
                gsh = pltpu.roll(g, sh, 0)
                vsh = pltpu.roll(v, sh, 0)
            valid = (t8 + dl <= 7) if rev else (t8 >= dl)
            e = jnp.exp(jnp.minimum(g - gsh, 0.0))
            coef = jnp.sum(q * ksh * e, axis=1, keepdims=True)
            o = o + jnp.where(valid, coef, 0.0) * vsh
        if rev:
            tot = yp_ref[:, sl] + o
            y = _rms(tot) * ng_ref[:, sl]
            o_ref[:, sl] = (y * _sigmoid(g_ref[:, sl].astype(F32))).astype(o_ref.dtype)
        else:
            o_ref[:, sl] = o
        g_tot = jnp.sum(lf, axis=0, keepdims=True)
        kdec = k * jnp.exp(g_tot - g)
        st_sc[h] = st * jnp.exp(g_tot) + _dot(v.T.astype(BF16), kdec.astype(BF16))


def _hgrn2(u, lb, rev, yprev=None, norm_g=None):
    nb = u.shape[0]
    cmap = _chunk_of_step(rev)
    d = 1 if rev else 0
    in_specs = [_u_spec(BW, CB_HG_Q, cmap), _u_spec(BW, CB_HG_I, cmap), _u_spec(BW, CB_HG_F + d, cmap),
                pl.BlockSpec((1, BW), lambda b, s: (0, 0))]
    args = [u, u, u, lb]
    if rev:
        in_specs += [pl.BlockSpec((None, CS, BW), lambda b, s: (b, cmap(s), 0)),
                     _u_spec(BW, CB_HG_G, cmap),
                     pl.BlockSpec((1, BW), lambda b, s: (0, 0))]
        args += [yprev, u, norm_g]
    return pl.pallas_call(
        functools.partial(_hgrn2_kernel, rev),
        grid=(nb, NCH),
        in_specs=in_specs,
        out_specs=pl.BlockSpec((None, CS, BW), lambda b, s: (b, cmap(s), 0)),
        out_shape=jax.ShapeDtypeStruct((nb, LT, BW), BF16 if rev else F32),
        scratch_shapes=[pltpu.VMEM((NH, DH, DH), F32)],
        compiler_params=_cparams(("parallel", "arbitrary")),
        name="hgrn2_rev" if rev else "hgrn2_fwd",
    )(*args)


def _expand_heads(cols, lane_head):
    r = cols[3]
    for hh in (2, 1, 0):
        r = jnp.where(lane_head == hh, cols[hh], r)
    return r


def _ssd_kernel(rev, *refs):
    if rev:
        (x_ref, gc_ref, gt_ref, bc_ref, bt_ref, ac_ref, at_ref, yp_ref, z_ref, dk_ref, ng_ref,
         o_ref, ht_sc) = refs
    else:
        x_ref, gc_ref, gt_ref, bc_ref, bt_ref, ac_ref, at_ref, o_ref, ht_sc = refs
    d = 1 if rev else 0

    @pl.when(pl.program_id(1) == 0)
    def _():
        ht_sc[...] = jnp.zeros_like(ht_sc)

    mask = _chunk_mask(rev)
    mb = jnp.where(mask, 1.0, 0.0).astype(BF16)
    dt_c = _softplus(gc_ref[...] + bc_ref[...])
    da_c = dt_c * ac_ref[...]
    acum_c = _cumsum_rows(mb, da_c)
    dt_t = _softplus(gt_ref[...] + bt_ref[...])
    da_t = dt_t * at_ref[...]
    acum_t = _cumsum_cols(da_t, mb)
    gw = 4 * SSD_P
    lane_head = jnp.right_shift(lax.broadcasted_iota(jnp.int32, (1, gw), 1), 6)
    ys = []
    for grp in range(SSD_GROUPS):
        xg = x_ref[:, grp * gw:(grp + 1) * gw]
        bm = x_ref[:, BW + grp * SSD_N:BW + (grp + 1) * SSD_N]
        cm = x_ref[:, BW + SSD_GROUPS * SSD_N + grp * SSD_N:BW + SSD_GROUPS * SSD_N + (grp + 1) * SSD_N]
        cmb = cm.astype(BF16)
        cb = _dot_nt(cmb, bm.astype(BF16))
        ht = ht_sc[grp]
        y_state = _dot(cmb, ht.astype(BF16))
        lanes = [16 + d * 8 + grp * 4 + hh for hh in range(4)]
        dt_cols = [dt_c[:, l:l + 1] for l in lanes]
        a_cols = [acum_c[:, l:l + 1] for l in lanes]
        a_tots = [jnp.sum(da_c[:, l:l + 1], axis=0, keepdims=True) for l in lanes]
        xdt = xg * _expand_heads(dt_cols, lane_head)
        xdtb = xdt.astype(BF16)
        y = jnp.exp(_expand_heads(a_cols, lane_head)) * y_state
        for hh in range(4):
            l = lanes[hh]
            dec = jnp.exp(jnp.where(mask, a_cols[hh] - acum_t[l:l + 1, :], -jnp.inf))
            yh = _dot((cb * dec).astype(BF16), xdtb)
            y = y + jnp.where(lane_head == hh, yh, 0.0)
        ys.append(y)
        w_exp = jnp.exp(_expand_heads([a_tots[hh] - a_cols[hh] for hh in range(4)], lane_head))
        e_row = jnp.exp(_expand_heads(a_tots, lane_head))
        ht_sc[grp] = ht * e_row + _dot(bm.T.astype(BF16), (xdt * w_exp).astype(BF16))
    y = jnp.concatenate(ys, axis=1)
    if rev:
        tot = yp_ref[...] + y + dk_ref[...] * x_ref[:, :BW]
        tot = tot * _silu(z_ref[...].astype(F32))
        o_ref[...] = (_rms(tot) * ng_ref[...]).astype(o_ref.dtype)
    else:
        o_ref[...] = y


def _ssd(xbc, usm, ust, u, bias_c, bias_t, aneg_c, aneg_t, rev, yprev=None, dskip=None, norm_g=None):
    nb = xbc.shape[0]
    cmap = _chunk_of_step(rev)
    in_specs = [pl.BlockSpec((None, CS, SSD_XBC), lambda b, s: (b, cmap(s), 0)),
                pl.BlockSpec((None, CS, 128), lambda b, s: (b, cmap(s), 0)),
                pl.BlockSpec((None, None, 32, CS), lambda b, s: (b, cmap(s), 0, 0)),
                pl.BlockSpec((1, 128), lambda b, s: (0, 0)),
                pl.BlockSpec((32, 1), lambda b, s: (0, 0)),
                pl.BlockSpec((1, 128), lambda b, s: (0, 0)),
                pl.BlockSpec((32, 1), lambda b, s: (0, 0))]
    args = [xbc, usm, ust, bias_c, bias_t, aneg_c, aneg_t]
    if rev:
        in_specs += [pl.BlockSpec((None, CS, BW), lambda b, s: (b, cmap(s), 0)),
                     _u_spec(BW, CB_SSD_Z, cmap),
                     pl.BlockSpec((1, BW), lambda b, s: (0, 0)),
                     pl.BlockSpec((1, BW), lambda b, s: (0, 0))]
        args += [yprev, u, dskip, norm_g]
    return pl.pallas_call(
        functools.partial(_ssd_kernel, rev),
        grid=(nb, NCH),
        in_specs=in_specs,
        out_specs=pl.BlockSpec((None, CS, BW), lambda b, s: (b, cmap(s), 0)),
        out_shape=jax.ShapeDtypeStruct((nb, LT, BW), BF16 if rev else F32),
        scratch_shapes=[pltpu.VMEM((SSD_GROUPS, SSD_N, 4 * SSD_P), F32)],
        compiler_params=_cparams(("parallel", "arbitrary")),
        name="ssd_rev" if rev else "ssd_fwd",
    )(*args)


def _mla_proj_kernel(cq_ref, ckv_ref, kr_ref, gq_ref, gkv_ref, wqa_ref, wqb_ref, wk_ref, wv_ref,
                     rk_ref, cos_ref, sin_ref, q_ref, k_ref, v_ref):
    nq = (_rms(cq_ref[...].astype(F32)) * gq_ref[...]).astype(BF16)
    qa = _dot(nq, wqa_ref[...])
    qb = _dot(nq, wqb_ref[...])
    cos_t = cos_ref[...]
    sin_t = sin_ref[...]
    cosf = jnp.concatenate([cos_t] * NH, axis=1)
    sinf = jnp.concatenate([sin_t] * NH, axis=1)
    q_ref[...] = ((qa * cosf + qb * sinf) * MLA_SCALE).astype(BF16)
    nkv = (_rms(ckv_ref[...].astype(F32)) * gkv_ref[...]).astype(BF16)
    kn = _dot(nkv, wk_ref[...])
    v_ref[...] = _dot(nkv, wv_ref[...]).astype(BF16)
    kr = kr_ref[...]
    krr = kr.astype(F32) * cos_t[:, 128:256] + _dot(kr, rk_ref[...]) * sin_t[:, 128:256]
    zero = jnp.zeros_like(krr)
    k_ref[...] = (kn + jnp.concatenate([zero, krr] * NH, axis=1)).astype(BF16)


def _mla_proj(u, gq, gkv, wqa, wqb, wk, wv, rk, cos_t, sin_t):
    nb = u.shape[0]
    const = lambda shape: pl.BlockSpec(shape, lambda b, i: (0, 0))
    return pl.pallas_call(
        _mla_proj_kernel,
        grid=(nb, NRT),
        in_specs=[pl.BlockSpec((None, RT, MLA_Q_RANK), lambda b, i: (b, i, CB_MLA_CQ)),
                  pl.BlockSpec((None, RT, MLA_KV_RANK), lambda b, i: (b, i, CB_MLA_CKV)),
                  pl.BlockSpec((None, RT, 128), lambda b, i: (b, i, CB_MLA_KR)),
                  const((1, MLA_Q_RANK)), const((1, MLA_KV_RANK)),
                  const((MLA_Q_RANK, NH * 256)), const((MLA_Q_RANK, NH * 256)),
                  const((MLA_KV_RANK, NH * 256)), const((MLA_KV_RANK, NH * DH)),
                  const((128, 128)),
                  pl.BlockSpec((RT, 256), lambda b, i: (i, 0)),
                  pl.BlockSpec((RT, 256), lambda b, i: (i, 0))],
        out_specs=[_row_spec(NH * 256), _row_spec(NH * 256), _row_spec(NH * DH)],
        out_shape=[jax.ShapeDtypeStruct((nb, LT, NH * 256), BF16),
                   jax.ShapeDtypeStruct((nb, LT, NH * 256), BF16),
                   jax.ShapeDtypeStruct((nb, LT, NH * DH), BF16)],
        compiler_params=_cparams(("parallel", "parallel")),
        name="mla_proj",
    )(u, u, u, gq, gkv, wqa, wqb, wk, wv, rk, cos_t, sin_t)


def _attn_kernel(q_ref, k_ref, v_ref, o_ref):
    s = _dot_nt(q_ref[...], k_ref[...])
    kcol = lax.broadcasted_iota(jnp.int32, (1, LT), 1)
    hide = jnp.logical_and(pl.program_id(2) == NRT - 1, kcol < SEQ)
    s = jnp.where(hide, -jnp.inf, s)
    m = jnp.max(s, axis=1, keepdims=True)
    p = jnp.exp(s - m)
    l = jnp.sum(p, axis=1, keepdims=True)
    o_ref[...] = (_dot(p.astype(BF16), v_ref[...]) / l).astype(o_ref.dtype)


def _attention(q, k, v):
    nb = q.shape[0]
    return pl.pallas_call(
        _attn_kernel,
        grid=(nb, NH, NRT),
        in_specs=[pl.BlockSpec((None, RT, 256), lambda b, h, i: (b, i, h)),
                  pl.BlockSpec((None, LT, 256), lambda b, h, i: (b, 0, h)),
                  pl.BlockSpec((None, LT, DH), lambda b, h, i: (b, 0, h))],
        out_specs=pl.BlockSpec((None, RT, DH), lambda b, h, i: (b, i, h)),
        out_shape=jax.ShapeDtypeStruct((nb, LT, NH * DH), BF16),
        compiler_params=_cparams(("parallel", "parallel", "arbitrary")),
        name="mla_attn",
    )(q, k, v)


def _merge_kernel(h_ref, b0_ref, b1_ref, b2_ref, b3_ref, wg_ref, wb_ref, o_ref):
    h = h_ref[...]
    acc = None
    for kk, br in enumerate((b0_ref, b1_ref, b2_ref, b3_ref)):
        gate = _sigmoid(_dot(h, wg_ref[kk]))
        term = gate * _dot(br[...], wb_ref[kk])
        acc = term if acc is None else acc + term
    o_ref[...] = acc.astype(o_ref.dtype)


def _merge(h, branches, wg, wb, tm=1152, tn=256):
    m = h.shape[0]
    bspec = pl.BlockSpec((tm, BW), lambda i, j: (i, 0))
    return pl.pallas_call(
        _merge_kernel,
        grid=(m // tm, D // tn),
        in_specs=[pl.BlockSpec((tm, D), lambda i, j: (i, 0)), bspec, bspec, bspec, bspec,
                  pl.BlockSpec((4, D, tn), lambda i, j: (0, 0, j)),
                  pl.BlockSpec((4, BW, tn), lambda i, j: (0, 0, j))],
        out_specs=pl.BlockSpec((tm, tn), lambda i, j: (i, j)),
        out_shape=jax.ShapeDtypeStruct((m, D), BF16),
        compiler_params=_cparams(("parallel", "arbitrary")),
        name="merge",
    )(h, *branches, wg, wb)


def _outproj_kernel(a_ref, w_ref, x_ref, g1_ref, gate_ref, g2_ref, sh_ref, sc_ref, xo_ref, ho_ref):
    y = _dot(a_ref[...], w_ref[...])
    xn = x_ref[...] + gate_ref[...] * (_rms(y) * g1_ref[...])
    xo_ref[...] = xn
    h2 = _rms(xn) * g2_ref[...]
    ho_ref[...] = h2 * (1.0 + sc_ref[...]) + sh_ref[...]


def _outproj(acc, w_out, xcat, normv, modv, layer):
    nb = xcat.shape[0]
    return pl.pallas_call(
        _outproj_kernel,
        grid=(nb, NRT),
        in_specs=[_row_spec(D), pl.BlockSpec((D, D), lambda b, i: (0, 0)), _row_spec(D),
                  _norm_spec(layer, 1), _mod_spec(layer, 2), _norm_spec(layer, 2),
                  _mod_spec(layer, 3), _mod_spec(layer, 4)],
        out_specs=[_row_spec(D), _row_spec(D)],
        out_shape=[jax.ShapeDtypeStruct((nb, LT, D), F32), jax.ShapeDtypeStruct((nb, LT, D), F32)],
        compiler_params=_cparams(("parallel", "parallel")),
        name="outproj",
    )(acc, w_out, xcat, normv, modv, normv, modv, modv)


def _route_kernel(h_ref, wa_ref, wb_ref, wc_ref, e_ref, w_ref):
    hf = h_ref[...]
    h = hf.astype(BF16)
    hl = (hf - h.astype(F32)).astype(BF16)
    logits = (_dot(h, wa_ref[...]) + _dot(h, wb_ref[...]) + _dot(h, wc_ref[...])
              + _dot(hl, wa_ref[...]) + _dot(hl, wb_ref[...]))
    lane = lax.broadcasted_iota(jnp.int32, (1, 128), 1).astype(F32)
    ninf = -jnp.inf
    is_grp = lane < N_GROUPS
    gl = jnp.where(is_grp, logits, ninf)
    gmax = jnp.max(gl, axis=1, keepdims=True)
    gidx = jnp.min(jnp.where(gl == gmax, lane, 999.0), axis=1, keepdims=True)
    gsum = jnp.sum(jnp.where(is_grp, jnp.exp(gl - gmax), 0.0), axis=1, keepdims=True)
    gp = 1.0 / gsum
    lo = N_GROUPS + EXP_PER_GROUP * gidx
    sel = jnp.logical_and(lane >= lo, lane < lo + EXP_PER_GROUP)
    el = jnp.where(sel, logits, ninf)
    v1 = jnp.max(el, axis=1, keepdims=True)
    i1 = jnp.min(jnp.where(el == v1, lane, 999.0), axis=1, keepdims=True)
    el2 = jnp.where(lane == i1, ninf, el)
    v2 = jnp.max(el2, axis=1, keepdims=True)
    i2 = jnp.min(jnp.where(el2 == v2, lane, 999.0), axis=1, keepdims=True)
    t = jnp.exp(v2 - v1)
    w1 = gp / (1.0 + t)
    w2 = gp * t / (1.0 + t)
    e_ref[...] = jnp.where(lane == 0.0, i1 - N_GROUPS, jnp.where(lane == 1.0, i2 - N_GROUPS, 0.0)).astype(jnp.int32)
    w_ref[...] = jnp.where(lane == 0.0, w1, jnp.where(lane == 1.0, w2, 0.0))


def _route(h2, wr3, tm=1152):
    m = h2.shape[0]
    wspec = pl.BlockSpec((D, 128), lambda i: (0, 0))
    return pl.pallas_call(
        _route_kernel,
        grid=(m // tm,),
        in_specs=[pl.BlockSpec((tm, D), lambda i: (i, 0)), wspec, wspec, wspec],
        out_specs=[pl.BlockSpec((tm, 128), lambda i: (i, 0)), pl.BlockSpec((tm, 128), lambda i: (i, 0))],
        out_shape=[jax.ShapeDtypeStruct((m, 128), jnp.int32), jax.ShapeDtypeStruct((m, 128), F32)],
        compiler_params=_cparams(("parallel",)),
        name="moe_route",
    )(h2, *wr3)


def _expert_kernel(te_ref, nu_ref, rt_ref, h_hbm, w1_ref, w3_ref, w2_ref, o_ref, xbuf, sem):
    i = pl.program_id(0)

    def row_copy(r, tok):
        return pltpu.make_async_copy(h_hbm.at[pl.ds(tok, 1), :], xbuf.at[pl.ds(r, 1), :], sem)

    @pl.when(i < nu_ref[0])
    def _():
        base = i * TE

        def issue(r, c):
            row_copy(r, rt_ref[base + r]).start()
            return c
        lax.fori_loop(0, TE, issue, 0)

        def wait(r, c):
            row_copy(r, 0).wait()
            return c
        lax.fori_loop(0, TE, wait, 0)
        x = xbuf[...].astype(BF16)
        a = _dot(x, w1_ref[...])
        b = _dot(x, w3_ref[...])
        hm = (_silu(a) * b).astype(BF16)
        o_ref[...] = _dot(hm, w2_ref[...])

    @pl.when(i >= nu_ref[0])
    def _():
        o_ref[...] = jnp.zeros_like(o_ref)


def _experts(tile_expert, n_used, row_token, h2, w1, w3, w2, n_tiles):
    grid_spec = pltpu.PrefetchScalarGridSpec(
        num_scalar_prefetch=3,
        grid=(n_tiles,),
        in_specs=[pl.BlockSpec(memory_space=pl.ANY),
                  pl.BlockSpec((None, D, D_EXPERT), lambda i, te, nu, rt: (te[i], 0, 0)),
                  pl.BlockSpec((None, D, D_EXPERT), lambda i, te, nu, rt: (te[i], 0, 0)),
                  pl.BlockSpec((None, D_EXPERT, D), lambda i, te, nu, rt: (te[i], 0, 0))],
        out_specs=pl.BlockSpec((TE, D), lambda i, te, nu, rt: (i, 0)),
        scratch_shapes=[pltpu.VMEM((TE, D), F32), pltpu.SemaphoreType.DMA(())],
    )
    return pl.pallas_call(
        _expert_kernel,
        grid_spec=grid_spec,
        out_shape=jax.ShapeDtypeStruct((n_tiles * TE, D), F32),
        compiler_params=_cparams(("arbitrary",)),
        name="moe_experts",
    )(tile_expert, n_used, row_token, h2, w1, w3, w2)


def _combine_kernel(has_next, pos_ref, ys_hbm, w_ref, x_ref, g3_ref, gate_ref, *rest):
    if has_next:
        g0_ref, sh_ref, sc_ref, xo_ref, ho_ref, buf, sem = rest
    else:
        xo_ref, buf, sem = rest
    t0 = (pl.program_id(0) * NRT + pl.program_id(1)) * RT

    def row_copy(r, kk, p):
        return pltpu.make_async_copy(ys_hbm.at[pl.ds(p, 1), :], buf.at[kk, pl.ds(r, 1), :], sem)

    def issue(r, c):
        row_copy(r, 0, pos_ref[(t0 + r) * 2]).start()
        row_copy(r, 1, pos_ref[(t0 + r) * 2 + 1]).start()
        return c
    lax.fori_loop(0, RT, issue, 0)

    def wait(r, c):
        row_copy(r, 0, 0).wait()
        row_copy(r, 1, 0).wait()
        return c
    lax.fori_loop(0, RT, wait, 0)
    w = w_ref[...]
    y = w[:, 0:1] * buf[0] + w[:, 1:2] * buf[1]
    xn = x_ref[...] + gate_ref[...] * (_rms(y) * g3_ref[...])
    xo_ref[...] = xn
    if has_next:
        hn = _rms(xn) * g0_ref[...]
        ho_ref[...] = (hn * (1.0 + sc_ref[...]) + sh_ref[...]).astype(BF16)


def _combine(pos, ys, wts, xcat, normv, modv, layer):
    nb = xcat.shape[0]
    has_next = layer + 1 < DEPTH

    def lift(spec_fn):
        return spec_fn

    def mod_spec(lyr, k):
        def imap(b, i, pos_ref):
            row = jnp.where(i == NRT - 1, 8, b)
            return ((lyr * 16 + row) * 6 + k, 0, 0)
        return pl.BlockSpec((None, 1, D), imap)

    def norm_spec(lyr, k):
        return pl.BlockSpec((None, 1, D), lambda b, i, pos_ref: (lyr * 4 + k, 0, 0))

    def row_spec(width):
        return pl.BlockSpec((None, RT, width), lambda b, i, pos_ref: (b, i, 0))

    in_specs = [pl.BlockSpec(memory_space=pl.ANY), row_spec(128), row_spec(D),
                norm_spec(layer, 3), mod_spec(layer, 5)]
    args = [ys, wts, xcat, normv, modv]
    out_specs = [row_spec(D)]
    out_shape = [jax.ShapeDtypeStruct((nb, LT, D), F32)]
    if has_next:
        in_specs += [norm_spec(layer + 1, 0), mod_spec(layer + 1, 0), mod_spec(layer + 1, 1)]
        args += [normv, modv, modv]
        out_specs.append(row_spec(D))
        out_shape.append(jax.ShapeDtypeStruct((nb, LT, D), BF16))
    grid_spec = pltpu.PrefetchScalarGridSpec(
        num_scalar_prefetch=1,
        grid=(nb, NRT),
        in_specs=in_specs,
        out_specs=out_specs,
        scratch_shapes=[pltpu.VMEM((2, RT, D), F32), pltpu.SemaphoreType.DMA(())],
    )
    return pl.pallas_call(
        functools.partial(_combine_kernel, has_next),
        grid_spec=grid_spec,
        out_shape=out_shape,
        compiler_params=_cparams(("arbitrary", "arbitrary")),
        name="moe_combine",
    )(pos, *args)


def _dispatch_plan(eid, n_tiles):
    e_flat = eid.reshape(-1)
    n_assign = e_flat.shape[0]
    onehot = (e_flat[:, None] == jnp.arange(N_EXPERTS, dtype=jnp.int32)[None, :]).astype(jnp.int32)
    csum = jnp.cumsum(onehot, axis=0)
    rank = jnp.sum((csum - onehot) * onehot, axis=1)
    counts = csum[-1]
    padded = ((counts + TE - 1) // TE) * TE
    ends = jnp.cumsum(padded)
    offs = ends - padded
    pos = (offs[e_flat] + rank).astype(jnp.int32)
    row_token = jnp.zeros((n_tiles * TE,), jnp.int32).at[pos].set(jnp.arange(n_assign, dtype=jnp.int32) // 2)
    tile_start = jnp.arange(n_tiles, dtype=jnp.int32) * TE
    tile_expert = jnp.minimum(jnp.searchsorted(ends, tile_start, side="right"), N_EXPERTS - 1).astype(jnp.int32)
    n_used = (ends[-1] // TE).astype(jnp.int32).reshape(1)
    return pos, row_token, tile_expert, n_used


def _rope_tables():
    tok = jnp.arange(SEQ)
    pos_row = (tok // GRID_W).astype(F32)
    pos_col = (tok % GRID_W).astype(F32)
    quarter = MLA_ROPE // 4
    inv_freq = ROPE_BASE ** (-jnp.arange(quarter, dtype=F32) / quarter)
    ang_r = pos_row[:, None] * inv_freq
    ang_c = pos_col[:, None] * inv_freq
    cos = jnp.concatenate([jnp.cos(ang_r)] * 2 + [jnp.cos(ang_c)] * 2, axis=1)
    sin = jnp.concatenate([jnp.sin(ang_r)] * 2 + [jnp.sin(ang_c)] * 2, axis=1)
    cos = jnp.concatenate([cos, jnp.ones((CTX, MLA_ROPE), F32)], axis=0)
    sin = jnp.concatenate([sin, jnp.zeros((CTX, MLA_ROPE), F32)], axis=0)
    pad = jnp.zeros((LT, 64), F32)
    cos_t = jnp.concatenate([jnp.ones((LT, 128), F32), cos, pad], axis=1)
    sin_t = jnp.concatenate([jnp.zeros((LT, 128), F32), sin, pad], axis=1)
    return cos_t, sin_t


def _rope_rot_matrix():
    j = jnp.arange(MLA_ROPE)
    first = (j % 32) < 16
    src = jnp.where(first, j + 16, j - 16)
    sign = jnp.where(first, -1.0, 1.0)
    return jnp.zeros((MLA_ROPE, MLA_ROPE), F32).at[src, j].set(sign)


def _layer_params(l, w_in, ml_f_bias, lb_all, hg_norm_g, mla_g_q, mla_g_kv, mla_w_uq, mla_w_ukv,
                  ssd_conv_w, ssd_conv_b, ssd_a_log, ssd_dt_bias, ssd_d, ssd_norm_g, w_gate, w_br, w_out,
                  moe_w_grp, moe_w_exp, moe_w1, moe_w3, moe_w2):
    p = {}
    w = w_in[l]
    zeros = lambda n: jnp.zeros((D, n), F32)
    w_small = jnp.concatenate([w[:, 2048:2064], w[:, 6992:7008], zeros(96)], axis=1)
    w_main = jnp.concatenate([w[:, 0:2048], w[:, 2064:4624], w[:, 5456:5968], w[:, 5968:6992],
                              w[:, 4624:5136], w[:, 5136:5392], w[:, 5392:5456], zeros(64), w_small], axis=1)
    p["w_main"] = w_main.astype(BF16)
    p["w_small"] = w_small.astype(BF16)
    fb = jnp.zeros((2, 2, NH), F32).at[:, 1, :].set(ml_f_bias[l]).reshape(16)
    ml_bias = jnp.concatenate([fb, jnp.zeros((112,), F32)])
    p["ml_bias_c"] = ml_bias.reshape(1, 128)
    p["ml_bias_t"] = ml_bias[:32].reshape(32, 1)
    dtb = jnp.concatenate([jnp.zeros((16,), F32), ssd_dt_bias[l].reshape(16), jnp.zeros((96,), F32)])
    p["ssd_bias_c"] = dtb.reshape(1, 128)
    p["ssd_bias_t"] = dtb[:32].reshape(32, 1)
    aneg = jnp.concatenate([jnp.zeros((16,), F32), -jnp.exp(ssd_a_log[l].astype(F32)).reshape(16),
                            jnp.zeros((96,), F32)])
    p["ssd_aneg_c"] = aneg.reshape(1, 128)
    p["ssd_aneg_t"] = aneg[:32].reshape(32, 1)
    p["lb"] = lb_all[l].reshape(1, BW)
    p["hg_norm_g"] = hg_norm_g[l].reshape(1, BW)
    p["ssd_dskip"] = jnp.repeat(ssd_d[l], SSD_P).reshape(1, BW)
    p["ssd_norm_g"] = ssd_norm_g[l].reshape(1, BW)
    p["conv_w"] = jnp.concatenate([ssd_conv_w[l], jnp.zeros((5, SSD_XBC), F32)], axis=0)
    p["conv_b"] = ssd_conv_b[l].reshape(1, SSD_XBC)
    rot = _rope_rot_matrix()
    wq = mla_w_uq[l].reshape(MLA_Q_RANK, NH, MLA_NOPE + MLA_ROPE)
    zq = jnp.zeros((MLA_Q_RANK, NH, 64), F32)
    wqa = jnp.concatenate([wq, zq], axis=2).reshape(MLA_Q_RANK, NH * 256)
    wq_rot = jnp.einsum("rhj,jk->rhk", wq[:, :, MLA_NOPE:], rot)
    wqb = jnp.concatenate([jnp.zeros((MLA_Q_RANK, NH, MLA_NOPE), F32), wq_rot, zq], axis=2)
    p["wqa"] = wqa.astype(BF16)
    p["wqb"] = wqb.reshape(MLA_Q_RANK, NH * 256).astype(BF16)
    wkv = mla_w_ukv[l].reshape(MLA_KV_RANK, NH, 2 * DH)
    wk = jnp.concatenate([wkv[:, :, :DH], jnp.zeros((MLA_KV_RANK, NH, DH), F32)], axis=2)
    p["wk"] = wk.reshape(MLA_KV_RANK, NH * 256).astype(BF16)
    p["wv"] = wkv[:, :, DH:].reshape(MLA_KV_RANK, NH * DH).astype(BF16)
    p["rk"] = jnp.zeros((128, 128), F32).at[:MLA_ROPE, :MLA_ROPE].set(rot).astype(BF16)
    p["gq"] = mla_g_q[l].reshape(1, MLA_Q_RANK)
    p["gkv"] = mla_g_kv[l].reshape(1, MLA_KV_RANK)
    p["w_gate"] = w_gate[l].astype(BF16)
    p["w_br"] = w_br[l].astype(BF16)
    p["w_out"] = w_out[l].astype(BF16)
    wr = jnp.concatenate([moe_w_grp[l], moe_w_exp[l], jnp.zeros((D, 128 - N_GROUPS - N_EXPERTS), F32)], axis=1)
    a = wr.astype(BF16)
    r = wr - a.astype(F32)
    b = r.astype(BF16)
    c = (r - b.astype(F32)).astype(BF16)
    p["wr3"] = (a, b, c)
    p["w1"] = moe_w1[l].astype(BF16)
    p["w3"] = moe_w3[l].astype(BF16)
    p["w2"] = moe_w2[l].astype(BF16)
    return p


def _mixer(h, p, cos_t, sin_t, nb):
    t = nb * LT
    hf = h.reshape(t, D)
    zero_b = jnp.zeros((1, U_W), F32)
    u = _mm(hf, p["w_main"], zero_b, BF16, 1152, 1024).reshape(nb, LT, U_W)
    usm = _mm(hf, p["w_small"], zero_b[:, :128], F32, 1152, 128).reshape(nb, LT, 128)
    ust = usm[:, :, :32].reshape(nb, NCH, CS, 32).transpose(0, 1, 3, 2)
    ml_f = _mlstm(u, usm, ust, p["ml_bias_c"], p["ml_bias_t"], False)
    ml = _mlstm(u, usm, ust, p["ml_bias_c"], p["ml_bias_t"], True, ml_f)
    hg_f = _hgrn2(u, p["lb"], False)
    hg = _hgrn2(u, p["lb"], True, hg_f, p["hg_norm_g"])
    q, k, v = _mla_proj(u, p["gq"], p["gkv"], p["wqa"], p["wqb"], p["wk"], p["wv"], p["rk"], cos_t, sin_t)
    mla = _attention(q, k, v)
    xbc = _ssd_conv(u, p["conv_w"], p["conv_b"])
    sargs = (xbc, usm, ust, u, p["ssd_bias_c"], p["ssd_bias_t"], p["ssd_aneg_c"], p["ssd_aneg_t"])
    ssd_f = _ssd(*sargs, False)
    ssd = _ssd(*sargs, True, ssd_f, p["ssd_dskip"], p["ssd_norm_g"])
    branches = [b.reshape(t, BW) for b in (ml, hg, mla, ssd)]
    return _merge(hf, branches, p["w_gate"], p["w_br"]).reshape(nb, LT, D)


def kernel(x, c, ctx, c_ctx, w_ada, b_ada, norm_g, w_in, ml_f_bias, hg_lb_logits, hg_norm_g, mla_g_q, mla_g_kv,
           mla_w_uq, mla_w_ukv, ssd_conv_w, ssd_conv_b, ssd_a_log, ssd_dt_bias, ssd_d, ssd_norm_g, w_gate,
           w_br, w_out, moe_w_grp, moe_w_exp, moe_w1, moe_w3, moe_w2):
    nb = x.shape[0]
    t = nb * LT
    xcat = jnp.concatenate([x, ctx], axis=1)
    lb_all = jnp.cumsum(jax.nn.softmax(hg_lb_logits.astype(F32), axis=0), axis=0)
    cvec = jnp.zeros((16, D), F32).at[:nb].set(c).at[8].set(c_ctx)
    s_vec = jax.nn.silu(cvec).astype(BF16)
    modv = jnp.stack([_mm(s_vec, w_ada[l], b_ada[l].reshape(1, 6 * D), F32, 16, 1024) for l in range(DEPTH)])
    modv = modv.reshape(DEPTH * 16 * 6, 1, D)
    normv = norm_g.reshape(DEPTH * 4, 1, D)
    cos_t, sin_t = _rope_tables()
    n_tiles = (2 * t + N_EXPERTS * (TE - 1)) // TE + 1
    h = _prenorm(xcat, normv, modv, 0)
    for l in range(DEPTH):
        p = _layer_params(l, w_in, ml_f_bias, lb_all, hg_norm_g, mla_g_q, mla_g_kv, mla_w_uq, mla_w_ukv,
                          ssd_conv_w, ssd_conv_b, ssd_a_log, ssd_dt_bias, ssd_d, ssd_norm_g, w_gate, w_br,
                          w_out, moe_w_grp, moe_w_exp, moe_w1, moe_w3, moe_w2)
        acc = _mixer(h, p, cos_t, sin_t, nb)
        xcat, h2 = _outproj(acc, p["w_out"], xcat, normv, modv, l)
        h2f = h2.reshape(t, D)
        eid, wts = _route(h2f, p["wr3"])
        pos, row_token, tile_expert, n_used = _dispatch_plan(eid[:, :2], n_tiles)
        ys = _experts(tile_expert, n_used, row_token, h2f, p["w1"], p["w3"], p["w2"], n_tiles)
        outs = _combine(pos, ys, wts.reshape(nb, LT, 128), xcat, normv, modv, l)
        if l + 1 < DEPTH:
            xcat, h = outs
        else:
            xcat = outs[0]
    return xcat[:, :SEQ]
```

```python
import functools

import jax
import jax.numpy as jnp
from jax import lax
from jax.experimental import pallas as pl
from jax.experimental.pallas import tpu as pltpu

F32 = jnp.float32
BF16 = jnp.bfloat16

D = 2048
SEQ = 2048
CTX = 256
LT = SEQ + CTX
DEPTH = 2
EPS = 1e-6
GRID_W = 64
NH = 4
DH = 128
BW = 512
MLA_Q_RANK = 512
MLA_KV_RANK = 256
MLA_NOPE = 128
MLA_ROPE = 64
MLA_SCALE = (MLA_NOPE + MLA_ROPE) ** -0.5
ROPE_BASE = 10000.0
SSD_HEADS = 8
SSD_P = 64
SSD_GROUPS = 2
SSD_N = 128
SSD_XBC = 1024
N_GROUPS = 4
EXP_PER_GROUP = 8
N_EXPERTS = 32
D_EXPERT = 512

CS = 128
NLC = SEQ // CS
NCC = CTX // CS
NCH = NLC + NCC
RT = 256
NRT = LT // RT
TE = 256
VMEM_LIMIT = 52 * 1024 * 1024

U_W = 7168
CB_ML_Q, CB_ML_K, CB_ML_V, CB_ML_O = 0, 1, 2, 3
CB_HG_Q, CB_HG_I, CB_HG_G, CB_HG_F = 4, 5, 6, 7
CB_SSD_Z = 9
CB_SSD_XBC = 5
CB_MLA_CQ = 12
CB_MLA_CKV = 26
CB_MLA_KR = 54


def _cparams(sem):
    return pltpu.CompilerParams(dimension_semantics=sem, vmem_limit_bytes=VMEM_LIMIT)


def _dot(a, b):
    return jnp.dot(a, b, preferred_element_type=F32)


def _dot_nt(a, b):
    return lax.dot_general(a, b, (((1,), (1,)), ((), ())), preferred_element_type=F32)


def _sigmoid(x):
    return 1.0 / (1.0 + jnp.exp(-x))


def _silu(x):
    return x * _sigmoid(x)


def _softplus(x):
    return jnp.maximum(x, 0.0) + jnp.log(1.0 + jnp.exp(-jnp.abs(x)))


def _log_sigmoid(x):
    return -_softplus(-x)


def _rms(x):
    return x * lax.rsqrt(jnp.mean(x * x, axis=-1, keepdims=True) + EPS)


def _split3(x):
    a = x.astype(BF16)
    r = x - a.astype(F32)
    b = r.astype(BF16)
    c = (r - b.astype(F32)).astype(BF16)
    return a, b, c


def _cumsum_rows(mb, x):
    a, b, c = _split3(x)
    return _dot(mb, a) + _dot(mb, b) + _dot(mb, c)


def _cumsum_cols(x, mb):
    a, b, c = _split3(x)
    return _dot_nt(a, mb) + _dot_nt(b, mb) + _dot_nt(c, mb)


def _chunk_mask(rev):
    row = lax.broadcasted_iota(jnp.int32, (CS, CS), 0)
    col = lax.broadcasted_iota(jnp.int32, (CS, CS), 1)
    return (col >= row) if rev else (col <= row)


def _chunk_of_step(rev):
    if rev:
        return lambda s: NCH - 1 - s
    return lambda s: (s + NLC) % NCH


def _mm_kernel(a_ref, w_ref, b_ref, o_ref):
    acc = _dot(a_ref[...], w_ref[...].astype(BF16))
    o_ref[...] = (acc + b_ref[...]).astype(o_ref.dtype)


def _mm(a, w, bias, out_dtype, tm, tn):
    m, k = a.shape
    n = w.shape[1]
    return pl.pallas_call(
        _mm_kernel,
        grid=(m // tm, n // tn),
        in_specs=[pl.BlockSpec((tm, k), lambda i, j: (i, 0)),
                  pl.BlockSpec((k, tn), lambda i, j: (0, j)),
                  pl.BlockSpec((1, tn), lambda i, j: (0, j))],
        out_specs=pl.BlockSpec((tm, tn), lambda i, j: (i, j)),
        out_shape=jax.ShapeDtypeStruct((m, n), out_dtype),
        compiler_params=_cparams(("parallel", "arbitrary")),
        name="mm",
    )(a, w, bias)


def _mod_spec(layer, k):
    def imap(b, i):
        row = jnp.where(i == NRT - 1, 8, b)
        return ((layer * 16 + row) * 6 + k, 0, 0)
    return pl.BlockSpec((None, 1, D), imap)


def _norm_spec(layer, k):
    return pl.BlockSpec((None, 1, D), lambda b, i: (layer * 4 + k, 0, 0))


def _row_spec(width, dtype_rows=RT):
    return pl.BlockSpec((None, dtype_rows, width), lambda b, i: (b, i, 0))


def _prenorm_kernel(x_ref, g_ref, sh_ref, sc_ref, h_ref):
    y = _rms(x_ref[...]) * g_ref[...]
    h_ref[...] = (y * (1.0 + sc_ref[...]) + sh_ref[...]).astype(BF16)


def _prenorm(xcat, normv, modv, layer):
    nb = xcat.shape[0]
    return pl.pallas_call(
        _prenorm_kernel,
        grid=(nb, NRT),
        in_specs=[_row_spec(D), _norm_spec(layer, 0), _mod_spec(layer, 0), _mod_spec(layer, 1)],
        out_specs=_row_spec(D),
        out_shape=jax.ShapeDtypeStruct((nb, LT, D), BF16),
        compiler_params=_cparams(("parallel", "parallel")),
        name="prenorm",
    )(xcat, normv, modv, modv)


def _conv_kernel(x_ref, p_ref, n_ref, w_ref, b_ref, o_ref):
    i = pl.program_id(1)
    x = x_ref[...].astype(F32)
    has_prev = jnp.logical_and(i != 0, i != NRT - 1).astype(F32)
    has_next = jnp.logical_and(i != NRT - 2, i != NRT - 1).astype(F32)
    prev_row = p_ref[15:16, :].astype(F32) * has_prev
    next_row = n_ref[0:1, :].astype(F32) * has_next
    ridx = lax.broadcasted_iota(jnp.int32, (RT, 1), 0)
    xm1 = jnp.where(ridx == 0, prev_row, pltpu.roll(x, 1, 0))
    xp1 = jnp.where(ridx == RT - 1, next_row, pltpu.roll(x, RT - 1, 0))
    w = w_ref[...]
    y = w[0:1, :] * xm1 + w[1:2, :] * x + w[2:3, :] * xp1 + b_ref[...]
    o_ref[...] = _silu(y)


def _ssd_conv(u, conv_w, conv_b):
    nb = u.shape[0]
    hb = RT // 16
    last = LT // 16 - 1
    return pl.pallas_call(
        _conv_kernel,
        grid=(nb, NRT),
        in_specs=[pl.BlockSpec((None, RT, SSD_XBC), lambda b, i: (b, i, CB_SSD_XBC)),
                  pl.BlockSpec((None, 16, SSD_XBC), lambda b, i: (b, jnp.maximum(i * hb - 1, 0), CB_SSD_XBC)),
                  pl.BlockSpec((None, 16, SSD_XBC), lambda b, i: (b, jnp.minimum((i + 1) * hb, last), CB_SSD_XBC)),
                  pl.BlockSpec((8, SSD_XBC), lambda b, i: (0, 0)),
                  pl.BlockSpec((1, SSD_XBC), lambda b, i: (0, 0))],
        out_specs=pl.BlockSpec((None, RT, SSD_XBC), lambda b, i: (b, i, 0)),
        out_shape=jax.ShapeDtypeStruct((nb, LT, SSD_XBC), F32),
        compiler_params=_cparams(("parallel", "parallel")),
        name="ssd_conv",
    )(u, u, u, conv_w, conv_b)


def _mlstm_kernel(rev, *refs):
    if rev:
        (q_ref, k_ref, v_ref, gc_ref, gt_ref, bc_ref, bt_ref, yp_ref, og_ref,
         o_ref, c_sc, n_sc, m_sc) = refs
    else:
        (q_ref, k_ref, v_ref, gc_ref, gt_ref, bc_ref, bt_ref,
         o_ref, c_sc, n_sc, m_sc) = refs
    d = 1 if rev else 0

    @pl.when(pl.program_id(1) == 0)
    def _():
        c_sc[...] = jnp.zeros_like(c_sc)
        n_sc[...] = jnp.zeros_like(n_sc)
        m_sc[...] = jnp.full(m_sc.shape, -jnp.inf, F32)

    mask = _chunk_mask(rev)
    mb = jnp.where(mask, 1.0, 0.0).astype(BF16)
    gc = gc_ref[...] + bc_ref[...]
    gt = gt_ref[...] + bt_ref[...]
    flog_c = _log_sigmoid(gc)
    flog_t = _log_sigmoid(gt)
    fcum_c = _cumsum_rows(mb, flog_c)
    fcum_t = _cumsum_cols(flog_t, mb)
    scale = DH ** -0.5
    for h in range(NH):
        ci = d * 8 + h
        cf = d * 8 + 4 + h
        sl = slice(h * DH, (h + 1) * DH)
        q = q_ref[:, sl]
        k = k_ref[:, sl]
        v = v_ref[:, sl]
        i_col = gc[:, ci:ci + 1]
        f_col = flog_c[:, cf:cf + 1]
        fc_col = fcum_c[:, cf:cf + 1]
        i_row = gt[ci:ci + 1, :]
        fc_row = fcum_t[cf:cf + 1, :]
        m_prev = m_sc[h][0:1, 0:1]
        dm = jnp.where(mask, fc_col - fc_row + i_row, -jnp.inf)
        from_state = fc_col + m_prev
        m_t = jnp.maximum(from_state, jnp.max(dm, axis=1, keepdims=True))
        s = _dot_nt(q, k) * scale * jnp.exp(dm - m_t)
        w_state = jnp.exp(from_state - m_t)
        qc = _dot(q, c_sc[h].astype(BF16)) * scale
        num = _dot(s.astype(BF16), v) + w_state * qc
        qn = jnp.sum(q.astype(F32) * n_sc[h][0:1, :], axis=1, keepdims=True) * scale
        den = jnp.sum(s, axis=1, keepdims=True) + w_state * qn
        hout = num / jnp.maximum(jnp.abs(den), jnp.exp(-m_t))
        if rev:
            tot = yp_ref[:, sl] + hout
            o_ref[:, sl] = (tot * _sigmoid(og_ref[:, sl].astype(F32))).astype(o_ref.dtype)
        else:
            o_ref[:, sl] = hout
        f_tot = jnp.sum(f_col, axis=0, keepdims=True)
        w_log = f_tot - fc_col + i_col
        m_new = jnp.maximum(f_tot + m_prev, jnp.max(w_log, axis=0, keepdims=True))
        decay = jnp.exp(f_tot + m_prev - m_new)
        kw = k.astype(F32) * jnp.exp(w_log - m_new)
        c_sc[h] = decay * c_sc[h] + _dot(kw.T.astype(BF16), v)
        n_new = decay * n_sc[h][0:1, :] + jnp.sum(kw, axis=0, keepdims=True)
        n_sc[h] = jnp.broadcast_to(n_new, (8, DH))
        m_sc[h] = jnp.broadcast_to(m_new, (8, DH))


def _u_spec(width, cblk, cmap):
    return pl.BlockSpec((None, CS, width), lambda b, s: (b, cmap(s), cblk))


def _mlstm(u, usm, ust, bias_c, bias_t, rev, yprev=None):
    nb = u.shape[0]
    cmap = _chunk_of_step(rev)
    in_specs = [_u_spec(BW, CB_ML_Q, cmap), _u_spec(BW, CB_ML_K, cmap), _u_spec(BW, CB_ML_V, cmap),
                pl.BlockSpec((None, CS, 128), lambda b, s: (b, cmap(s), 0)),
                pl.BlockSpec((None, None, 32, CS), lambda b, s: (b, cmap(s), 0, 0)),
                pl.BlockSpec((1, 128), lambda b, s: (0, 0)),
                pl.BlockSpec((32, 1), lambda b, s: (0, 0))]
    args = [u, u, u, usm, ust, bias_c, bias_t]
    if rev:
        in_specs += [pl.BlockSpec((None, CS, BW), lambda b, s: (b, cmap(s), 0)),
                     _u_spec(BW, CB_ML_O, cmap)]
        args += [yprev, u]
    return pl.pallas_call(
        functools.partial(_mlstm_kernel, rev),
        grid=(nb, NCH),
        in_specs=in_specs,
        out_specs=pl.BlockSpec((None, CS, BW), lambda b, s: (b, cmap(s), 0)),
        out_shape=jax.ShapeDtypeStruct((nb, LT, BW), BF16 if rev else F32),
        scratch_shapes=[pltpu.VMEM((NH, DH, DH), F32), pltpu.VMEM((NH, 8, DH), F32),
                        pltpu.VMEM((NH, 8, DH), F32)],
        compiler_params=_cparams(("parallel", "arbitrary")),
        name="mlstm_rev" if rev else "mlstm_fwd",
    )(*args)


def _block_ref(g, n, off):
    g3 = g.reshape(CS // n, n, DH)
    r = g3[:, off:off + 1, :]
    return jnp.broadcast_to(r, (CS // n, n, DH)).reshape(CS, DH)


def _hgrn2_kernel(rev, *refs):
    if rev:
        q_ref, i_ref, f_ref, lb_ref, yp_ref, g_ref, ng_ref, o_ref, st_sc = refs
    else:
        q_ref, i_ref, f_ref, lb_ref, o_ref, st_sc = refs

    @pl.when(pl.program_id(1) == 0)
    def _():
        st_sc[...] = jnp.zeros_like(st_sc)

    mask = _chunk_mask(rev)
    mb = jnp.where(mask, 1.0, 0.0).astype(BF16)
    row = lax.broadcasted_iota(jnp.int32, (CS, CS), 0)
    col = lax.broadcasted_iota(jnp.int32, (CS, CS), 1)
    tix = lax.broadcasted_iota(jnp.int32, (CS, 1), 0)
    for h in range(NH):
        sl = slice(h * DH, (h + 1) * DH)
        q = _silu(q_ref[:, sl].astype(F32))
        v = i_ref[:, sl].astype(F32)
        lb = lb_ref[:, sl]
        f = lb + (1.0 - lb) * _sigmoid(f_ref[:, sl].astype(F32))
        k = 1.0 - f
        lf = jnp.log(f)
        g = _cumsum_rows(mb, lf)
        st = st_sc[h]
        o = _dot_nt((q * jnp.exp(g)).astype(BF16), st.astype(BF16))
        a = jnp.zeros((CS, CS), F32)
        n = CS
        while n >= 16:
            half = n // 2
            second = jnp.bitwise_and(tix, n - 1) >= half
            rowside = jnp.logical_not(second) if rev else second
            r = _block_ref(g, n, half if rev else half - 1)
            e = jnp.exp(jnp.where(rowside, g - r, r - g))
            qs = jnp.where(rowside, q * e, 0.0).astype(BF16)
            ks = jnp.where(rowside, 0.0, k * e).astype(BF16)
            sh_n = n.bit_length() - 1
            same = jnp.right_shift(row, sh_n) == jnp.right_shift(col, sh_n)
            a = a + jnp.where(same, _dot_nt(qs, ks), 0.0)
            n = half
        o = o + _dot(a.astype(BF16), v.astype(BF16))
        t8 = jnp.bitwise_and(tix, 7)
        for dl in range(8):
            if dl == 0:
                ksh, gsh, vsh = k, g, v
            else:
                sh = (CS - dl) if rev else dl
                ksh = pltpu.roll(k, sh, 0)
                gsh = pltpu.roll(g, sh, 0)
                vsh = pltpu.roll(v, sh, 0)
            valid = (t8 + dl <= 7) if rev else (t8 >= dl)
            e = jnp.exp(jnp.minimum(g - gsh, 0.0))
            coef = jnp.sum(q * ksh * e, axis=1, keepdims=True)
            o = o + jnp.where(valid, coef, 0.0) * vsh
        if rev:
            tot = yp_ref[:, sl] + o
            y = _rms(tot) * ng_ref[:, sl]
            o_ref[:, sl] = (y * _sigmoid(g_ref[:, sl].astype(F32))).astype(o_ref.dtype)
        else:
            o_ref[:, sl] = o
        g_tot = jnp.sum(lf, axis=0, keepdims=True)
        kdec = k * jnp.exp(g_tot - g)
        st_sc[h] = st * jnp.exp(g_tot) + _dot(v.T.astype(BF16), kdec.astype(BF16))


def _hgrn2(u, lb, rev, yprev=None, norm_g=None):
    nb = u.shape[0]
    cmap = _chunk_of_step(rev)
    d = 1 if rev else 0
    in_specs = [_u_spec(BW, CB_HG_Q, cmap), _u_spec(BW, CB_HG_I, cmap), _u_spec(BW, CB_HG_F + d, cmap),
                pl.BlockSpec((1, BW), lambda b, s: (0, 0))]
    args = [u, u, u, lb]
    if rev:
        in_specs += [pl.BlockSpec((None, CS, BW), lambda b, s: (b, cmap(s), 0)),
                     _u_spec(BW, CB_HG_G, cmap),
                     pl.BlockSpec((1, BW), lambda b, s: (0, 0))]
        args += [yprev, u, norm_g]
    return pl.pallas_call(
        functools.partial(_hgrn2_kernel, rev),
        grid=(nb, NCH),
        in_specs=in_specs,
        out_specs=pl.BlockSpec((None, CS, BW), lambda b, s: (b, cmap(s), 0)),
        out_shape=jax.ShapeDtypeStruct((nb, LT, BW), BF16 if rev else F32),
        scratch_shapes=[pltpu.VMEM((NH, DH, DH), F32)],
        compiler_params=_cparams(("parallel", "arbitrary")),
        name="hgrn2_rev" if rev else "hgrn2_fwd",
    )(*args)


def _expand_heads(cols, lane_head):
    r = cols[3]
    for hh in (2, 1, 0):
        r = jnp.where(lane_head == hh, cols[hh], r)
    return r


def _ssd_kernel(rev, *refs):
    if rev:
        (x_ref, gc_ref, gt_ref, bc_ref, bt_ref, ac_ref, at_ref, yp_ref, z_ref, dk_ref, ng_ref,
         o_ref, ht_sc) = refs
    else:
        x_ref, gc_ref, gt_ref, bc_ref, bt_ref, ac_ref, at_ref, o_ref, ht_sc = refs
    d = 1 if rev else 0

    @pl.when(pl.program_id(1) == 0)
    def _():
        ht_sc[...] = jnp.zeros_like(ht_sc)

    mask = _chunk_mask(rev)
    mb = jnp.where(mask, 1.0, 0.0).astype(BF16)
    dt_c = _softplus(gc_ref[...] + bc_ref[...])
    da_c = dt_c * ac_ref[...]
    acum_c = _cumsum_rows(mb, da_c)
    dt_t = _softplus(gt_ref[...] + bt_ref[...])
    da_t = dt_t * at_ref[...]
    acum_t = _cumsum_cols(da_t, mb)
    gw = 4 * SSD_P
    lane_head = jnp.right_shift(lax.broadcasted_iota(jnp.int32, (1, gw), 1), 6)
    ys = []
    for grp in range(SSD_GROUPS):
        xg = x_ref[:, grp * gw:(grp + 1) * gw]
        bm = x_ref[:, BW + grp * SSD_N:BW + (grp + 1) * SSD_N]
        cm = x_ref[:, BW + SSD_GROUPS * SSD_N + grp * SSD_N:BW + SSD_GROUPS * SSD_N + (grp + 1) * SSD_N]
        cmb = cm.astype(BF16)
        cb = _dot_nt(cmb, bm.astype(BF16))
        ht = ht_sc[grp]
        y_state = _dot(cmb, ht.astype(BF16))
        lanes = [16 + d * 8 + grp * 4 + hh for hh in range(4)]
        dt_cols = [dt_c[:, l:l + 1] for l in lanes]
        a_cols = [acum_c[:, l:l + 1] for l in lanes]
        a_tots = [jnp.sum(da_c[:, l:l + 1], axis=0, keepdims=True) for l in lanes]
        xdt = xg * _expand_heads(dt_cols, lane_head)
        xdtb = xdt.astype(BF16)
        y = jnp.exp(_expand_heads(a_cols, lane_head)) * y_state
        for hh in range(4):
            l = lanes[hh]
            dec = jnp.exp(jnp.where(mask, a_cols[hh] - acum_t[l:l + 1, :], -jnp.inf))
            yh = _dot((cb * dec).astype(BF16), xdtb)
            y = y + jnp.where(lane_head == hh, yh, 0.0)
        ys.append(y)
        w_exp = jnp.exp(_expand_heads([a_tots[hh] - a_cols[hh] for hh in range(4)], lane_head))
        e_row = jnp.exp(_expand_heads(a_tots, lane_head))
        ht_sc[grp] = ht * e_row + _dot(bm.T.astype(BF16), (xdt * w_exp).astype(BF16))
    y = jnp.concatenate(ys, axis=1)
    if rev:
        tot = yp_ref[...] + y + dk_ref[...] * x_ref[:, :BW]
        tot = tot * _silu(z_ref[...].astype(F32))
        o_ref[...] = (_rms(tot) * ng_ref[...]).astype(o_ref.dtype)
    else:
        o_ref[...] = y


def _ssd(xbc, usm, ust, u, bias_c, bias_t, aneg_c, aneg_t, rev, yprev=None, dskip=None, norm_g=None):
    nb = xbc.shape[0]
    cmap = _chunk_of_step(rev)
    in_specs = [pl.BlockSpec((None, CS, SSD_XBC), lambda b, s: (b, cmap(s), 0)),
                pl.BlockSpec((None, CS, 128), lambda b, s: (b, cmap(s), 0)),
                pl.BlockSpec((None, None, 32, CS), lambda b, s: (b, cmap(s), 0, 0)),
                pl.BlockSpec((1, 128), lambda b, s: (0, 0)),
                pl.BlockSpec((32, 1), lambda b, s: (0, 0)),
                pl.BlockSpec((1, 128), lambda b, s: (0, 0)),
                pl.BlockSpec((32, 1), lambda b, s: (0, 0))]
    args = [xbc, usm, ust, bias_c, bias_t, aneg_c, aneg_t]
    if rev:
        in_specs += [pl.BlockSpec((None, CS, BW), lambda b, s: (b, cmap(s), 0)),
                     _u_spec(BW, CB_SSD_Z, cmap),
                     pl.BlockSpec((1, BW), lambda b, s: (0, 0)),
                     pl.BlockSpec((1, BW), lambda b, s: (0, 0))]
        args += [yprev, u, dskip, norm_g]
    return pl.pallas_call(
        functools.partial(_ssd_kernel, rev),
        grid=(nb, NCH),
        in_specs=in_specs,
        out_specs=pl.BlockSpec((None, CS, BW), lambda b, s: (b, cmap(s), 0)),
        out_shape=jax.ShapeDtypeStruct((nb, LT, BW), BF16 if rev else F32),
        scratch_shapes=[pltpu.VMEM((SSD_GROUPS, SSD_N, 4 * SSD_P), F32)],
        compiler_params=_cparams(("parallel", "arbitrary")),
        name="ssd_rev" if rev else "ssd_fwd",
    )(*args)


def _mla_proj_kernel(cq_ref, ckv_ref, kr_ref, gq_ref, gkv_ref, wqa_ref, wqb_ref, wk_ref, wv_ref,
                     rk_ref, cos_ref, sin_ref, q_ref, k_ref, v_ref):
    nq = (_rms(cq_ref[...].astype(F32)) * gq_ref[...]).astype(BF16)
    qa = _dot(nq, wqa_ref[...])
    qb = _dot(nq, wqb_ref[...])
    cos_t = cos_ref[...]
    sin_t = sin_ref[...]
    cosf = jnp.concatenate([cos_t] * NH, axis=1)
    sinf = jnp.concatenate([sin_t] * NH, axis=1)
    q_ref[...] = ((qa * cosf + qb * sinf) * MLA_SCALE).astype(BF16)
    nkv = (_rms(ckv_ref[...].astype(F32)) * gkv_ref[...]).astype(BF16)
    kn = _dot(nkv, wk_ref[...])
    v_ref[...] = _dot(nkv, wv_ref[...]).astype(BF16)
    kr = kr_ref[...]
    krr = kr.astype(F32) * cos_t[:, 128:256] + _dot(kr, rk_ref[...]) * sin_t[:, 128:256]
    zero = jnp.zeros_like(krr)
    k_ref[...] = (kn + jnp.concatenate([zero, krr] * NH, axis=1)).astype(BF16)


def _mla_proj(u, gq, gkv, wqa, wqb, wk, wv, rk, cos_t, sin_t):
    nb = u.shape[0]
    const = lambda shape: pl.BlockSpec(shape, lambda b, i: (0, 0))
    return pl.pallas_call(
        _mla_proj_kernel,
        grid=(nb, NRT),
        in_specs=[pl.BlockSpec((None, RT, MLA_Q_RANK), lambda b, i: (b, i, CB_MLA_CQ)),
                  pl.BlockSpec((None, RT, MLA_KV_RANK), lambda b, i: (b, i, CB_MLA_CKV)),
                  pl.BlockSpec((None, RT, 128), lambda b, i: (b, i, CB_MLA_KR)),
                  const((1, MLA_Q_RANK)), const((1, MLA_KV_RANK)),
                  const((MLA_Q_RANK, NH * 256)), const((MLA_Q_RANK, NH * 256)),
                  const((MLA_KV_RANK, NH * 256)), const((MLA_KV_RANK, NH * DH)),
                  const((128, 128)),
                  pl.BlockSpec((RT, 256), lambda b, i: (i, 0)),
                  pl.BlockSpec((RT, 256), lambda b, i: (i, 0))],
        out_specs=[_row_spec(NH * 256), _row_spec(NH * 256), _row_spec(NH * DH)],
        out_shape=[jax.ShapeDtypeStruct((nb, LT, NH * 256), BF16),
                   jax.ShapeDtypeStruct((nb, LT, NH * 256), BF16),
                   jax.ShapeDtypeStruct((nb, LT, NH * DH), BF16)],
        compiler_params=_cparams(("parallel", "parallel")),
        name="mla_proj",
    )(u, u, u, gq, gkv, wqa, wqb, wk, wv, rk, cos_t, sin_t)


def _attn_kernel(q_ref, k_ref, v_ref, o_ref):
    s = _dot_nt(q_ref[...], k_ref[...])
    kcol = lax.broadcasted_iota(jnp.int32, (1, LT), 1)
    hide = jnp.logical_and(pl.program_id(2) == NRT - 1, kcol < SEQ)
    s = jnp.where(hide, -jnp.inf, s)
    m = jnp.max(s, axis=1, keepdims=True)
    p = jnp.exp(s - m)
    l = jnp.sum(p, axis=1, keepdims=True)
    o_ref[...] = (_dot(p.astype(BF16), v_ref[...]) / l).astype(o_ref.dtype)


def _attention(q, k, v):
    nb = q.shape[0]
    return pl.pallas_call(
        _attn_kernel,
        grid=(nb, NH, NRT),
        in_specs=[pl.BlockSpec((None, RT, 256), lambda b, h, i: (b, i, h)),
                  pl.BlockSpec((None, LT, 256), lambda b, h, i: (b, 0, h)),
                  pl.BlockSpec((None, LT, DH), lambda b, h, i: (b, 0, h))],
        out_specs=pl.BlockSpec((None, RT, DH), lambda b, h, i: (b, i, h)),
        out_shape=jax.ShapeDtypeStruct((nb, LT, NH * DH), BF16),
        compiler_params=_cparams(("parallel", "parallel", "arbitrary")),
        name="mla_attn",
    )(q, k, v)


def _merge_kernel(h_ref, b0_ref, b1_ref, b2_ref, b3_ref, wg_ref, wb_ref, o_ref):
    h = h_ref[...]
    acc = None
    for kk, br in enumerate((b0_ref, b1_ref, b2_ref, b3_ref)):
        gate = _sigmoid(_dot(h, wg_ref[kk]))
        term = gate * _dot(br[...], wb_ref[kk])
        acc = term if acc is None else acc + term
    o_ref[...] = acc.astype(o_ref.dtype)


def _merge(h, branches, wg, wb, tm=1152, tn=256):
    m = h.shape[0]
    bspec = pl.BlockSpec((tm, BW), lambda i, j: (i, 0))
    return pl.pallas_call(
        _merge_kernel,
        grid=(m // tm, D // tn),
        in_specs=[pl.BlockSpec((tm, D), lambda i, j: (i, 0)), bspec, bspec, bspec, bspec,
                  pl.BlockSpec((4, D, tn), lambda i, j: (0, 0, j)),
                  pl.BlockSpec((4, BW, tn), lambda i, j: (0, 0, j))],
        out_specs=pl.BlockSpec((tm, tn), lambda i, j: (i, j)),
        out_shape=jax.ShapeDtypeStruct((m, D), BF16),
        compiler_params=_cparams(("parallel", "arbitrary")),
        name="merge",
    )(h, *branches, wg, wb)


def _pack_pairs(x):
    n = x.shape[1] // 2
    lo = pltpu.bitcast(x[:, :n].astype(BF16).astype(F32), jnp.uint32)
    hi = pltpu.bitcast(x[:, n:].astype(BF16).astype(F32), jnp.uint32)
    return jnp.bitwise_or(jnp.right_shift(lo, jnp.uint32(16)), jnp.bitwise_and(hi, jnp.uint32(0xFFFF0000)))


def _unpack_pairs(w):
    lo = pltpu.bitcast(jnp.left_shift(w, jnp.uint32(16)), F32)
    hi = pltpu.bitcast(jnp.bitwise_and(w, jnp.uint32(0xFFFF0000)), F32)
    return jnp.concatenate([lo, hi], axis=1)


def _outproj_kernel(a_ref, w_ref, x_ref, g1_ref, gate_ref, g2_ref, sh_ref, sc_ref, wa_ref, wb_ref, wc_ref,
                    xo_ref, hp_ref, e_ref, wt_ref):
    y = _dot(a_ref[...], w_ref[...])
    xn = x_ref[...] + gate_ref[...] * (_rms(y) * g1_ref[...])
    xo_ref[...] = xn
    h2 = _rms(xn) * g2_ref[...]
    h2 = h2 * (1.0 + sc_ref[...]) + sh_ref[...]
    hp_ref[...] = _pack_pairs(h2)
    _route_math(h2, wa_ref, wb_ref, wc_ref, e_ref, wt_ref)


def _outproj(acc, w_out, xcat, normv, modv, wr3, layer):
    nb = xcat.shape[0]
    wspec = pl.BlockSpec((D, 128), lambda b, i: (0, 0))
    return pl.pallas_call(
        _outproj_kernel,
        grid=(nb, NRT),
        in_specs=[_row_spec(D), pl.BlockSpec((D, D), lambda b, i: (0, 0)), _row_spec(D),
                  _norm_spec(layer, 1), _mod_spec(layer, 2), _norm_spec(layer, 2),
                  _mod_spec(layer, 3), _mod_spec(layer, 4), wspec, wspec, wspec],
        out_specs=[_row_spec(D), _row_spec(D // 2), _row_spec(128), _row_spec(128)],
        out_shape=[jax.ShapeDtypeStruct((nb, LT, D), F32), jax.ShapeDtypeStruct((nb, LT, D // 2), jnp.uint32),
                   jax.ShapeDtypeStruct((nb, LT, 128), jnp.int32), jax.ShapeDtypeStruct((nb, LT, 128), F32)],
        compiler_params=_cparams(("parallel", "parallel")),
        name="outproj",
    )(acc, w_out, xcat, normv, modv, normv, modv, modv, *wr3)


def _route_math(hf, wa_ref, wb_ref, wc_ref, e_ref, w_ref):
    h = hf.astype(BF16)
    hl = (hf - h.astype(F32)).astype(BF16)
    logits = (_dot(h, wa_ref[...]) + _dot(h, wb_ref[...]) + _dot(h, wc_ref[...])
              + _dot(hl, wa_ref[...]) + _dot(hl, wb_ref[...]))
    lane = lax.broadcasted_iota(jnp.int32, (1, 128), 1).astype(F32)
    ninf = -jnp.inf
    is_grp = lane < N_GROUPS
    gl = jnp.where(is_grp, logits, ninf)
    gmax = jnp.max(gl, axis=1, keepdims=True)
    gidx = jnp.min(jnp.where(gl == gmax, lane, 999.0), axis=1, keepdims=True)
    gsum = jnp.sum(jnp.where(is_grp, jnp.exp(gl - gmax), 0.0), axis=1, keepdims=True)
    gp = 1.0 / gsum
    lo = N_GROUPS + EXP_PER_GROUP * gidx
    sel = jnp.logical_and(lane >= lo, lane < lo + EXP_PER_GROUP)
    el = jnp.where(sel, logits, ninf)
    v1 = jnp.max(el, axis=1, keepdims=True)
    i1 = jnp.min(jnp.where(el == v1, lane, 999.0), axis=1, keepdims=True)
    el2 = jnp.where(lane == i1, ninf, el)
    v2 = jnp.max(el2, axis=1, keepdims=True)
    i2 = jnp.min(jnp.where(el2 == v2, lane, 999.0), axis=1, keepdims=True)
    t = jnp.exp(v2 - v1)
    w1 = gp / (1.0 + t)
    w2 = gp * t / (1.0 + t)
    e_ref[...] = jnp.where(lane == 0.0, i1 - N_GROUPS, jnp.where(lane == 1.0, i2 - N_GROUPS, 0.0)).astype(jnp.int32)
    w_ref[...] = jnp.where(lane == 0.0, w1, jnp.where(lane == 1.0, w2, 0.0))


GATHER_UNROLL = 8


def _expert_kernel(te_ref, nu_ref, rt_ref, h_hbm, w1_ref, w3_ref, w2_ref, o_ref, xbuf, sem):
    i = pl.program_id(0)
    n_used = nu_ref[0]
    slot = i % 2

    def issue_tile(j, s):
        base = j * TE

        def body(g, c):
            for uu in range(GATHER_UNROLL):
                r = g * GATHER_UNROLL + uu
                pltpu.make_async_copy(h_hbm.at[pl.ds(rt_ref[base + r], 1), :],
                                      xbuf.at[s, pl.ds(r, 1), :], sem.at[s]).start(priority=uu % 2)
            return c
        lax.fori_loop(0, TE // GATHER_UNROLL, body, 0)

    @pl.when(jnp.logical_and(i == 0, n_used > 0))
    def _():
        issue_tile(0, 0)

    @pl.when(i + 1 < n_used)
    def _():
        issue_tile(i + 1, 1 - slot)

    @pl.when(i < n_used)
    def _():
        pltpu.make_async_copy(h_hbm.at[pl.ds(0, TE), :], xbuf.at[slot], sem.at[slot]).wait()
        x = _unpack_pairs(xbuf[slot]).astype(BF16)
        a = _dot(x, w1_ref[...].astype(BF16))
        b = _dot(x, w3_ref[...].astype(BF16))
        hm = (_silu(a) * b).astype(BF16)
        o_ref[...] = _pack_pairs(_dot(hm, w2_ref[...].astype(BF16)))

    @pl.when(i >= n_used)
    def _():
        o_ref[...] = jnp.zeros_like(o_ref)


def _experts(tile_expert, n_used, row_token, hp, w1, w3, w2, n_tiles):
    grid_spec = pltpu.PrefetchScalarGridSpec(
        num_scalar_prefetch=3,
        grid=(n_tiles,),
        in_specs=[pl.BlockSpec(memory_space=pl.ANY),
                  pl.BlockSpec((None, D, D_EXPERT), lambda i, te, nu, rt: (te[i], 0, 0)),
                  pl.BlockSpec((None, D, D_EXPERT), lambda i, te, nu, rt: (te[i], 0, 0)),
                  pl.BlockSpec((None, D_EXPERT, D), lambda i, te, nu, rt: (te[i], 0, 0))],
        out_specs=pl.BlockSpec((TE, D // 2), lambda i, te, nu, rt: (i, 0)),
        scratch_shapes=[pltpu.VMEM((2, TE, D // 2), jnp.uint32), pltpu.SemaphoreType.DMA((2,))],
    )
    return pl.pallas_call(
        _expert_kernel,
        grid_spec=grid_spec,
        out_shape=jax.ShapeDtypeStruct((n_tiles * TE, D // 2), jnp.uint32),
        compiler_params=_cparams(("arbitrary",)),
        name="moe_experts",
    )(tile_expert, n_used, row_token, hp, w1, w3, w2)


def _combine_kernel(has_next, pos_ref, ys_hbm, w_ref, x_ref, g3_ref, gate_ref, *rest):
    if has_next:
        g0_ref, sh_ref, sc_ref, xo_ref, ho_ref, buf, sem = rest
    else:
        xo_ref, buf, sem = rest
    step = pl.program_id(0) * NRT + pl.program_id(1)
    n_steps = pl.num_programs(0) * NRT
    slot = step % 2

    def issue_tile(j, s):
        t0 = j * RT

        def body(g, c):
            for uu in range(GATHER_UNROLL):
                r = g * GATHER_UNROLL + uu
                for kk in range(2):
                    pltpu.make_async_copy(ys_hbm.at[pl.ds(pos_ref[(t0 + r) * 2 + kk], 1), :],
                                          buf.at[s, kk, pl.ds(r, 1), :], sem.at[s]).start(priority=kk)
            return c
        lax.fori_loop(0, RT // GATHER_UNROLL, body, 0)

    @pl.when(step == 0)
    def _():
        issue_tile(0, 0)

    @pl.when(step + 1 < n_steps)
    def _():
        issue_tile(step + 1, 1 - slot)

    for kk in range(2):
        pltpu.make_async_copy(ys_hbm.at[pl.ds(0, RT), :], buf.at[slot, kk], sem.at[slot]).wait()
    w = w_ref[...]
    y = w[:, 0:1] * _unpack_pairs(buf[slot, 0]) + w[:, 1:2] * _unpack_pairs(buf[slot, 1])
    xn = x_ref[...] + gate_ref[...] * (_rms(y) * g3_ref[...])
    xo_ref[...] = xn
    if has_next:
        hn = _rms(xn) * g0_ref[...]
        ho_ref[...] = (hn * (1.0 + sc_ref[...]) + sh_ref[...]).astype(BF16)


def _combine(pos, ys, wts, xcat, normv, modv, layer):
    nb = xcat.shape[0]
    has_next = layer + 1 < DEPTH

    def mod_spec(lyr, k):
        def imap(b, i, pos_ref):
            row = jnp.where(i == NRT - 1, 8, b)
            return ((lyr * 16 + row) * 6 + k, 0, 0)
        return pl.BlockSpec((None, 1, D), imap)

    def norm_spec(lyr, k):
        return pl.BlockSpec((None, 1, D), lambda b, i, pos_ref: (lyr * 4 + k, 0, 0))

    def row_spec(width):
        return pl.BlockSpec((None, RT, width), lambda b, i, pos_ref: (b, i, 0))

    in_specs = [pl.BlockSpec(memory_space=pl.ANY), row_spec(128), row_spec(D),
                norm_spec(layer, 3), mod_spec(layer, 5)]
    args = [ys, wts, xcat, normv, modv]
    out_specs = [row_spec(D)]
    out_shape = [jax.ShapeDtypeStruct((nb, LT, D), F32)]
    if has_next:
        in_specs += [norm_spec(layer + 1, 0), mod_spec(layer + 1, 0), mod_spec(layer + 1, 1)]
        args += [normv, modv, modv]
        out_specs.append(row_spec(D))
        out_shape.append(jax.ShapeDtypeStruct((nb, LT, D), BF16))
    grid_spec = pltpu.PrefetchScalarGridSpec(
        num_scalar_prefetch=1,
        grid=(nb, NRT),
        in_specs=in_specs,
        out_specs=out_specs,
        scratch_shapes=[pltpu.VMEM((2, 2, RT, D // 2), jnp.uint32), pltpu.SemaphoreType.DMA((2,))],
    )
    return pl.pallas_call(
        functools.partial(_combine_kernel, has_next),
        grid_spec=grid_spec,
        out_shape=out_shape,
        compiler_params=_cparams(("arbitrary", "arbitrary")),
        name="moe_combine",
    )(pos, *args)


def _dispatch_plan(eid, n_tiles):
    e_flat = eid.reshape(-1)
    n_assign = e_flat.shape[0]
    onehot = (e_flat[:, None] == jnp.arange(N_EXPERTS, dtype=jnp.int32)[None, :]).astype(jnp.int32)
    csum = jnp.cumsum(onehot, axis=0)
    rank = jnp.sum((csum - onehot) * onehot, axis=1)
    counts = csum[-1]
    padded = ((counts + TE - 1) // TE) * TE
    ends = jnp.cumsum(padded)
    offs = ends - padded
    pos = (offs[e_flat] + rank).astype(jnp.int32)
    row_token = jnp.zeros((n_tiles * TE,), jnp.int32).at[pos].set(jnp.arange(n_assign, dtype=jnp.int32) // 2)
    tile_start = jnp.arange(n_tiles, dtype=jnp.int32) * TE
    tile_expert = jnp.minimum(jnp.searchsorted(ends, tile_start, side="right"), N_EXPERTS - 1).astype(jnp.int32)
    n_used = (ends[-1] // TE).astype(jnp.int32).reshape(1)
    return pos, row_token, tile_expert, n_used


def _rope_tables():
    tok = jnp.arange(SEQ)
    pos_row = (tok // GRID_W).astype(F32)
    pos_col = (tok % GRID_W).astype(F32)
    quarter = MLA_ROPE // 4
    inv_freq = ROPE_BASE ** (-jnp.arange(quarter, dtype=F32) / quarter)
    ang_r = pos_row[:, None] * inv_freq
    ang_c = pos_col[:, None] * inv_freq
    cos = jnp.concatenate([jnp.cos(ang_r)] * 2 + [jnp.cos(ang_c)] * 2, axis=1)
    sin = jnp.concatenate([jnp.sin(ang_r)] * 2 + [jnp.sin(ang_c)] * 2, axis=1)
    cos = jnp.concatenate([cos, jnp.ones((CTX, MLA_ROPE), F32)], axis=0)
    sin = jnp.concatenate([sin, jnp.zeros((CTX, MLA_ROPE), F32)], axis=0)
    pad = jnp.zeros((LT, 64), F32)
    cos_t = jnp.concatenate([jnp.ones((LT, 128), F32), cos, pad], axis=1)
    sin_t = jnp.concatenate([jnp.zeros((LT, 128), F32), sin, pad], axis=1)
    return cos_t, sin_t


def _rope_rot_matrix():
    j = jnp.arange(MLA_ROPE)
    first = (j % 32) < 16
    src = jnp.where(first, j + 16, j - 16)
    sign = jnp.where(first, -1.0, 1.0)
    return jnp.zeros((MLA_ROPE, MLA_ROPE), F32).at[src, j].set(sign)


def _layer_params(l, w_in, ml_f_bias, lb_all, hg_norm_g, mla_g_q, mla_g_kv, mla_w_uq, mla_w_ukv,
                  ssd_conv_w, ssd_conv_b, ssd_a_log, ssd_dt_bias, ssd_d, ssd_norm_g, w_gate, w_br, w_out,
                  moe_w_grp, moe_w_exp, moe_w1, moe_w3, moe_w2):
    p = {}
    w = w_in[l]
    zeros = lambda n: jnp.zeros((D, n), F32)
    w_small = jnp.concatenate([w[:, 2048:2064], w[:, 6992:7008], zeros(96)], axis=1)
    w_main = jnp.concatenate([w[:, 0:2048], w[:, 2064:4624], w[:, 5456:5968], w[:, 5968:6992],
                              w[:, 4624:5136], w[:, 5136:5392], w[:, 5392:5456], zeros(64), w_small], axis=1)
    p["w_main"] = w_main.astype(BF16)
    p["w_small"] = w_small.astype(BF16)
    fb = jnp.zeros((2, 2, NH), F32).at[:, 1, :].set(ml_f_bias[l]).reshape(16)
    ml_bias = jnp.concatenate([fb, jnp.zeros((112,), F32)])
    p["ml_bias_c"] = ml_bias.reshape(1, 128)
    p["ml_bias_t"] = ml_bias[:32].reshape(32, 1)
    dtb = jnp.concatenate([jnp.zeros((16,), F32), ssd_dt_bias[l].reshape(16), jnp.zeros((96,), F32)])
    p["ssd_bias_c"] = dtb.reshape(1, 128)
    p["ssd_bias_t"] = dtb[:32].reshape(32, 1)
    aneg = jnp.concatenate([jnp.zeros((16,), F32), -jnp.exp(ssd_a_log[l].astype(F32)).reshape(16),
                            jnp.zeros((96,), F32)])
    p["ssd_aneg_c"] = aneg.reshape(1, 128)
    p["ssd_aneg_t"] = aneg[:32].reshape(32, 1)
    p["lb"] = lb_all[l].reshape(1, BW)
    p["hg_norm_g"] = hg_norm_g[l].reshape(1, BW)
    p["ssd_dskip"] = jnp.repeat(ssd_d[l], SSD_P).reshape(1, BW)
    p["ssd_norm_g"] = ssd_norm_g[l].reshape(1, BW)
    p["conv_w"] = jnp.concatenate([ssd_conv_w[l], jnp.zeros((5, SSD_XBC), F32)], axis=0)
    p["conv_b"] = ssd_conv_b[l].reshape(1, SSD_XBC)
    rot = _rope_rot_matrix()
    wq = mla_w_uq[l].reshape(MLA_Q_RANK, NH, MLA_NOPE + MLA_ROPE)
    zq = jnp.zeros((MLA_Q_RANK, NH, 64), F32)
    wqa = jnp.concatenate([wq, zq], axis=2).reshape(MLA_Q_RANK, NH * 256)
    wq_rot = jnp.einsum("rhj,jk->rhk", wq[:, :, MLA_NOPE:], rot)
    wqb = jnp.concatenate([jnp.zeros((MLA_Q_RANK, NH, MLA_NOPE), F32), wq_rot, zq], axis=2)
    p["wqa"] = wqa.astype(BF16)
    p["wqb"] = wqb.reshape(MLA_Q_RANK, NH * 256).astype(BF16)
    wkv = mla_w_ukv[l].reshape(MLA_KV_RANK, NH, 2 * DH)
    wk = jnp.concatenate([wkv[:, :, :DH], jnp.zeros((MLA_KV_RANK, NH, DH), F32)], axis=2)
    p["wk"] = wk.reshape(MLA_KV_RANK, NH * 256).astype(BF16)
    p["wv"] = wkv[:, :, DH:].reshape(MLA_KV_RANK, NH * DH).astype(BF16)
    p["rk"] = jnp.zeros((128, 128), F32).at[:MLA_ROPE, :MLA_ROPE].set(rot).astype(BF16)
    p["gq"] = mla_g_q[l].reshape(1, MLA_Q_RANK)
    p["gkv"] = mla_g_kv[l].reshape(1, MLA_KV_RANK)
    p["w_gate"] = w_gate[l].astype(BF16)
    p["w_br"] = w_br[l].astype(BF16)
    p["w_out"] = w_out[l].astype(BF16)
    wr = jnp.concatenate([moe_w_grp[l], moe_w_exp[l], jnp.zeros((D, 128 - N_GROUPS - N_EXPERTS), F32)], axis=1)
    a = wr.astype(BF16)
    r = wr - a.astype(F32)
    b = r.astype(BF16)
    c = (r - b.astype(F32)).astype(BF16)
    p["wr3"] = (a, b, c)
    p["w1"] = moe_w1[l]
    p["w3"] = moe_w3[l]
    p["w2"] = moe_w2[l]
    return p


def _mixer(h, p, cos_t, sin_t, nb):
    t = nb * LT
    hf = h.reshape(t, D)
    zero_b = jnp.zeros((1, U_W), F32)
    u = _mm(hf, p["w_main"], zero_b, BF16, 1152, 1024).reshape(nb, LT, U_W)
    usm = _mm(hf, p["w_small"], zero_b[:, :128], F32, 1152, 128).reshape(nb, LT, 128)
    ust = usm[:, :, :32].reshape(nb, NCH, CS, 32).transpose(0, 1, 3, 2)
    ml_f = _mlstm(u, usm, ust, p["ml_bias_c"], p["ml_bias_t"], False)
    ml = _mlstm(u, usm, ust, p["ml_bias_c"], p["ml_bias_t"], True, ml_f)
    hg_f = _hgrn2(u, p["lb"], False)
    hg = _hgrn2(u, p["lb"], True, hg_f, p["hg_norm_g"])
    q, k, v = _mla_proj(u, p["gq"], p["gkv"], p["wqa"], p["wqb"], p["wk"], p["wv"], p["rk"], cos_t, sin_t)
    mla = _attention(q, k, v)
    xbc = _ssd_conv(u, p["conv_w"], p["conv_b"])
    sargs = (xbc, usm, ust, u, p["ssd_bias_c"], p["ssd_bias_t"], p["ssd_aneg_c"], p["ssd_aneg_t"])
    ssd_f = _ssd(*sargs, False)
    ssd = _ssd(*sargs, True, ssd_f, p["ssd_dskip"], p["ssd_norm_g"])
    branches = [b.reshape(t, BW) for b in (ml, hg, mla, ssd)]
    return _merge(hf, branches, p["w_gate"], p["w_br"]).reshape(nb, LT, D)


def kernel(x, c, ctx, c_ctx, w_ada, b_ada, norm_g, w_in, ml_f_bias, hg_lb_logits, hg_norm_g, mla_g_q, mla_g_kv,
           mla_w_uq, mla_w_ukv, ssd_conv_w, ssd_conv_b, ssd_a_log, ssd_dt_bias, ssd_d, ssd_norm_g, w_gate,
           w_br, w_out, moe_w_grp, moe_w_exp, moe_w1, moe_w3, moe_w2):
    nb = x.shape[0]
    t = nb * LT
    xcat = jnp.concatenate([x, ctx], axis=1)
    lb_all = jnp.cumsum(jax.nn.softmax(hg_lb_logits.astype(F32), axis=0), axis=0)
    cvec = jnp.zeros((16, D), F32).at[:nb].set(c).at[8].set(c_ctx)
    s_vec = jax.nn.silu(cvec).astype(BF16)
    modv = jnp.stack([_mm(s_vec, w_ada[l], b_ada[l].reshape(1, 6 * D), F32, 16, 1024) for l in range(DEPTH)])
    modv = modv.reshape(DEPTH * 16 * 6, 1, D)
    normv = norm_g.reshape(DEPTH * 4, 1, D)
    cos_t, sin_t = _rope_tables()
    n_tiles = (2 * t + N_EXPERTS * (TE - 1)) // TE + 1
    h = _prenorm(xcat, normv, modv, 0)
    for l in range(DEPTH):
        p = _layer_params(l, w_in, ml_f_bias, lb_all, hg_norm_g, mla_g_q, mla_g_kv, mla_w_uq, mla_w_ukv,
                          ssd_conv_w, ssd_conv_b, ssd_a_log, ssd_dt_bias, ssd_d, ssd_norm_g, w_gate, w_br,
                          w_out, moe_w_grp, moe_w_exp, moe_w1, moe_w3, moe_w2)
        acc = _mixer(h, p, cos_t, sin_t, nb)
        xcat, hp, eid, wts = _outproj(acc, p["w_out"], xcat, normv, modv, p["wr3"], l)
        pos, row_token, tile_expert, n_used = _dispatch_plan(eid.reshape(t, 128)[:, :2], n_tiles)
        ys = _experts(tile_expert, n_used, row_token, hp.reshape(t, D // 2), p["w1"], p["w3"], p["w2"], n_tiles)
        outs = _combine(pos, ys, wts, xcat, normv, modv, l)
        if l + 1 < DEPTH:
            xcat, h = outs
        else:
            xcat = outs[0]
    return xcat[:, :SEQ]
```

```python
import functools

import jax
import jax.numpy as jnp
from jax import lax
from jax.experimental import pallas as pl
from jax.experimental.pallas import tpu as pltpu

F32 = jnp.float32
BF16 = jnp.bfloat16

D = 2048
SEQ = 2048
CTX = 256
LT = SEQ + CTX
DEPTH = 2
EPS = 1e-6
GRID_W = 64
NH = 4
DH = 128
BW = 512
MLA_Q_RANK = 512
MLA_KV_RANK = 256
MLA_NOPE = 128
MLA_ROPE = 64
MLA_SCALE = (MLA_NOPE + MLA_ROPE) ** -0.5
ROPE_BASE = 10000.0
SSD_HEADS = 8
SSD_P = 64
SSD_GROUPS = 2
SSD_N = 128
SSD_XBC = 1024
N_GROUPS = 4
EXP_PER_GROUP = 8
N_EXPERTS = 32
D_EXPERT = 512

CS = 128
NLC = SEQ // CS
NCC = CTX // CS
NCH = NLC + NCC
RT = 256
NRT = LT // RT
TE = 256
VMEM_LIMIT = 52 * 1024 * 1024

U_W = 7168
CB_ML_Q, CB_ML_K, CB_ML_V, CB_ML_O = 0, 1, 2, 3
CB_HG_Q, CB_HG_I, CB_HG_G, CB_HG_F = 4, 5, 6, 7
CB_SSD_Z = 9
CB_SSD_XBC = 5
CB_MLA_CQ = 12
CB_MLA_CKV = 26
CB_MLA_KR = 54


def _cparams(sem):
    return pltpu.CompilerParams(dimension_semantics=sem, vmem_limit_bytes=VMEM_LIMIT)


def _dot(a, b):
    return jnp.dot(a, b, preferred_element_type=F32)


def _dot_nt(a, b):
    return lax.dot_general(a, b, (((1,), (1,)), ((), ())), preferred_element_type=F32)


def _sigmoid(x):
    return 1.0 / (1.0 + jnp.exp(-x))


def _silu(x):
    return x * _sigmoid(x)


def _softplus(x):
    return jnp.maximum(x, 0.0) + jnp.log(1.0 + jnp.exp(-jnp.abs(x)))


def _log_sigmoid(x):
    return -_softplus(-x)


def _rms(x):
    return x * lax.rsqrt(jnp.mean(x * x, axis=-1, keepdims=True) + EPS)


def _split3(x):
    a = x.astype(BF16)
    r = x - a.astype(F32)
    b = r.astype(BF16)
    c = (r - b.astype(F32)).astype(BF16)
    return a, b, c


def _cumsum_rows(mb, x):
    a, b, c = _split3(x)
    return _dot(mb, a) + _dot(mb, b) + _dot(mb, c)


def _cumsum_cols(x, mb):
    a, b, c = _split3(x)
    return _dot_nt(a, mb) + _dot_nt(b, mb) + _dot_nt(c, mb)


def _chunk_mask(rev):
    row = lax.broadcasted_iota(jnp.int32, (CS, CS), 0)
    col = lax.broadcasted_iota(jnp.int32, (CS, CS), 1)
    return (col >= row) if rev else (col <= row)


def _chunk_of_step(rev):
    if rev:
        return lambda s: NCH - 1 - s
    return lambda s: (s + NLC) % NCH


def _mm_kernel(a_ref, w_ref, b_ref, o_ref):
    acc = _dot(a_ref[...], w_ref[...].astype(BF16))
    o_ref[...] = (acc + b_ref[...]).astype(o_ref.dtype)


def _mm(a, w, bias, out_dtype, tm, tn):
    m, k = a.shape
    n = w.shape[1]
    return pl.pallas_call(
        _mm_kernel,
        grid=(m // tm, n // tn),
        in_specs=[pl.BlockSpec((tm, k), lambda i, j: (i, 0)),
                  pl.BlockSpec((k, tn), lambda i, j: (0, j)),
                  pl.BlockSpec((1, tn), lambda i, j: (0, j))],
        out_specs=pl.BlockSpec((tm, tn), lambda i, j: (i, j)),
        out_shape=jax.ShapeDtypeStruct((m, n), out_dtype),
        compiler_params=_cparams(("parallel", "arbitrary")),
        name="mm",
    )(a, w, bias)


W_IN_PIECES = ((0, 0, 2048),
               (2048, 2064, 2560),
               (4608, 5456, 512),
               (5120, 5968, 1024),
               (6144, 4624, 512),
               (6656, 5136, 256),
               (6912, 5392, 64),
               (7040, 2048, 16),
               (7056, 6992, 16))
D_IN = 7008


def _win_kernel(w_ref, o_ref, s_ref):
    o_ref[...] = jnp.zeros_like(o_ref)
    for dst, src, width in W_IN_PIECES:
        o_ref[:, dst:dst + width] = w_ref[:, src:src + width].astype(BF16)
    s_ref[...] = o_ref[:, U_W - 128:]


def _win_layout(w_in):
    rows = 256
    return pl.pallas_call(
        _win_kernel,
        grid=(DEPTH, D // rows),
        in_specs=[pl.BlockSpec((None, rows, D_IN), lambda l, i: (l, i, 0))],
        out_specs=[pl.BlockSpec((None, rows, U_W), lambda l, i: (l, i, 0)),
                   pl.BlockSpec((None, rows, 128), lambda l, i: (l, i, 0))],
        out_shape=[jax.ShapeDtypeStruct((DEPTH, D, U_W), BF16), jax.ShapeDtypeStruct((DEPTH, D, 128), BF16)],
        compiler_params=_cparams(("parallel", "parallel")),
        name="win_layout",
    )(w_in)


def _mod_spec(layer, k):
    def imap(b, i):
        row = jnp.where(i == NRT - 1, 8, b)
        return ((layer * 16 + row) * 6 + k, 0, 0)
    return pl.BlockSpec((None, 1, D), imap)


def _norm_spec(layer, k):
    return pl.BlockSpec((None, 1, D), lambda b, i: (layer * 4 + k, 0, 0))


def _row_spec(width, dtype_rows=RT):
    return pl.BlockSpec((None, dtype_rows, width), lambda b, i: (b, i, 0))


def _split_x_specs():
    return [pl.BlockSpec((None, RT, D), lambda b, i: (b, jnp.minimum(i, NRT - 2), 0)),
            pl.BlockSpec((None, RT, D), lambda b, i: (b, 0, 0))]


def _pick_x(x_ref, c_ref):
    return jnp.where(pl.program_id(1) == NRT - 1, c_ref[...], x_ref[...])


def _prenorm_kernel(x_ref, c_ref, g_ref, sh_ref, sc_ref, h_ref):
    y = _rms(_pick_x(x_ref, c_ref)) * g_ref[...]
    h_ref[...] = (y * (1.0 + sc_ref[...]) + sh_ref[...]).astype(BF16)


def _prenorm(x, ctx, normv, modv, layer):
    nb = x.shape[0]
    return pl.pallas_call(
        _prenorm_kernel,
        grid=(nb, NRT),
        in_specs=_split_x_specs() + [_norm_spec(layer, 0), _mod_spec(layer, 0), _mod_spec(layer, 1)],
        out_specs=_row_spec(D),
        out_shape=jax.ShapeDtypeStruct((nb, LT, D), BF16),
        compiler_params=_cparams(("parallel", "parallel")),
        name="prenorm",
    )(x, ctx, normv, modv, modv)


def _conv_kernel(x_ref, p_ref, n_ref, w_ref, b_ref, o_ref):
    i = pl.program_id(1)
    x = x_ref[...].astype(F32)
    has_prev = jnp.logical_and(i != 0, i != NRT - 1).astype(F32)
    has_next = jnp.logical_and(i != NRT - 2, i != NRT - 1).astype(F32)
    prev_row = p_ref[15:16, :].astype(F32) * has_prev
    next_row = n_ref[0:1, :].astype(F32) * has_next
    ridx = lax.broadcasted_iota(jnp.int32, (RT, 1), 0)
    xm1 = jnp.where(ridx == 0, prev_row, pltpu.roll(x, 1, 0))
    xp1 = jnp.where(ridx == RT - 1, next_row, pltpu.roll(x, RT - 1, 0))
    w = w_ref[...]
    y = w[0:1, :] * xm1 + w[1:2, :] * x + w[2:3, :] * xp1 + b_ref[...]
    o_ref[...] = _silu(y)


def _ssd_conv(u, conv_w, conv_b):
    nb = u.shape[0]
    hb = RT // 16
    last = LT // 16 - 1
    return pl.pallas_call(
        _conv_kernel,
        grid=(nb, NRT),
        in_specs=[pl.BlockSpec((None, RT, SSD_XBC), lambda b, i: (b, i, CB_SSD_XBC)),
                  pl.BlockSpec((None, 16, SSD_XBC), lambda b, i: (b, jnp.maximum(i * hb - 1, 0), CB_SSD_XBC)),
                  pl.BlockSpec((None, 16, SSD_XBC), lambda b, i: (b, jnp.minimum((i + 1) * hb, last), CB_SSD_XBC)),
                  pl.BlockSpec((8, SSD_XBC), lambda b, i: (0, 0)),
                  pl.BlockSpec((1, SSD_XBC), lambda b, i: (0, 0))],
        out_specs=pl.BlockSpec((None, RT, SSD_XBC), lambda b, i: (b, i, 0)),
        out_shape=jax.ShapeDtypeStruct((nb, LT, SSD_XBC), F32),
        compiler_params=_cparams(("parallel", "parallel")),
        name="ssd_conv",
    )(u, u, u, conv_w, conv_b)


def _mlstm_kernel(rev, *refs):
    if rev:
        (q_ref, k_ref, v_ref, gc_ref, gt_ref, bc_ref, bt_ref, yp_ref, og_ref,
         o_ref, c_sc, n_sc, m_sc) = refs
    else:
        (q_ref, k_ref, v_ref, gc_ref, gt_ref, bc_ref, bt_ref,
         o_ref, c_sc, n_sc, m_sc) = refs
    d = 1 if rev else 0

    @pl.when(pl.program_id(1) == 0)
    def _():
        c_sc[...] = jnp.zeros_like(c_sc)
        n_sc[...] = jnp.zeros_like(n_sc)
        m_sc[...] = jnp.full(m_sc.shape, -jnp.inf, F32)

    mask = _chunk_mask(rev)
    mb = jnp.where(mask, 1.0, 0.0).astype(BF16)
    gc = gc_ref[...] + bc_ref[...]
    gt = gt_ref[...] + bt_ref[...]
    flog_c = _log_sigmoid(gc)
    flog_t = _log_sigmoid(gt)
    fcum_c = _cumsum_rows(mb, flog_c)
    fcum_t = _cumsum_cols(flog_t, mb)
    scale = DH ** -0.5
    for h in range(NH):
        ci = d * 8 + h
        cf = d * 8 + 4 + h
        sl = slice(h * DH, (h + 1) * DH)
        q = q_ref[:, sl]
        k = k_ref[:, sl]
        v = v_ref[:, sl]
        i_col = gc[:, ci:ci + 1]
        f_col = flog_c[:, cf:cf + 1]
        fc_col = fcum_c[:, cf:cf + 1]
        i_row = gt[ci:ci + 1, :]
        fc_row = fcum_t[cf:cf + 1, :]
        m_prev = m_sc[h][0:1, 0:1]
        dm = jnp.where(mask, fc_col - fc_row + i_row, -jnp.inf)
        from_state = fc_col + m_prev
        m_t = jnp.maximum(from_state, jnp.max(dm, axis=1, keepdims=True))
        s = _dot_nt(q, k) * scale * jnp.exp(dm - m_t)
        w_state = jnp.exp(from_state - m_t)
        qc = _dot(q, c_sc[h].astype(BF16)) * scale
        num = _dot(s.astype(BF16), v) + w_state * qc
        qn = jnp.sum(q.astype(F32) * n_sc[h][0:1, :], axis=1, keepdims=True) * scale
        den = jnp.sum(s, axis=1, keepdims=True) + w_state * qn
        hout = num / jnp.maximum(jnp.abs(den), jnp.exp(-m_t))
        if rev:
            tot = yp_ref[:, sl] + hout
            o_ref[:, sl] = (tot * _sigmoid(og_ref[:, sl].astype(F32))).astype(o_ref.dtype)
        else:
            o_ref[:, sl] = hout
        f_tot = jnp.sum(f_col, axis=0, keepdims=True)
        w_log = f_tot - fc_col + i_col
        m_new = jnp.maximum(f_tot + m_prev, jnp.max(w_log, axis=0, keepdims=True))
        decay = jnp.exp(f_tot + m_prev - m_new)
        kw = k.astype(F32) * jnp.exp(w_log - m_new)
        c_sc[h] = decay * c_sc[h] + _dot(kw.T.astype(BF16), v)
        n_new = decay * n_sc[h][0:1, :] + jnp.sum(kw, axis=0, keepdims=True)
        n_sc[h] = jnp.broadcast_to(n_new, (8, DH))
        m_sc[h] = jnp.broadcast_to(m_new, (8, DH))


def _u_spec(width, cblk, cmap):
    return pl.BlockSpec((None, CS, width), lambda b, s: (b, cmap(s), cblk))


def _mlstm(u, usm, ust, bias_c, bias_t, rev, yprev=None):
    nb = u.shape[0]
    cmap = _chunk_of_step(rev)
    in_specs = [_u_spec(BW, CB_ML_Q, cmap), _u_spec(BW, CB_ML_K, cmap), _u_spec(BW, CB_ML_V, cmap),
                pl.BlockSpec((None, CS, 128), lambda b, s: (b, cmap(s), 0)),
                pl.BlockSpec((None, None, 32, CS), lambda b, s: (b, cmap(s), 0, 0)),
                pl.BlockSpec((1, 128), lambda b, s: (0, 0)),
                pl.BlockSpec((32, 1), lambda b, s: (0, 0))]
    args = [u, u, u, usm, ust, bias_c, bias_t]
    if rev:
        in_specs += [pl.BlockSpec((None, CS, BW), lambda b, s: (b, cmap(s), 0)),
                     _u_spec(BW, CB_ML_O, cmap)]
        args += [yprev, u]
    return pl.pallas_call(
        functools.partial(_mlstm_kernel, rev),
        grid=(nb, NCH),
        in_specs=in_specs,
        out_specs=pl.BlockSpec((None, CS, BW), lambda b, s: (b, cmap(s), 0)),
        out_shape=jax.ShapeDtypeStruct((nb, LT, BW), BF16 if rev else F32),
        scratch_shapes=[pltpu.VMEM((NH, DH, DH), F32), pltpu.VMEM((NH, 8, DH), F32),
                        pltpu.VMEM((NH, 8, DH), F32)],
        compiler_params=_cparams(("parallel", "arbitrary")),
        name="mlstm_rev" if rev else "mlstm_fwd",
    )(*args)


def _block_ref(g, n, off):
    g3 = g.reshape(CS // n, n, DH)
    r = g3[:, off:off + 1, :]
    return jnp.broadcast_to(r, (CS // n, n, DH)).reshape(CS, DH)


def _hgrn2_kernel(rev, *refs):
    if rev:
        q_ref, i_ref, f_ref, lb_ref, yp_ref, g_ref, ng_ref, o_ref, st_sc = refs
    else:
        q_ref, i_ref, f_ref, lb_ref, o_ref, st_sc = refs

    @pl.when(pl.program_id(1) == 0)
    def _():
        st_sc[...] = jnp.zeros_like(st_sc)

    mask = _chunk_mask(rev)
    mb = jnp.where(mask, 1.0, 0.0).astype(BF16)
    row = lax.broadcasted_iota(jnp.int32, (CS, CS), 0)
    col = lax.broadcasted_iota(jnp.int32, (CS, CS), 1)
    tix = lax.broadcasted_iota(jnp.int32, (CS, 1), 0)
    for h in range(NH):
        sl = slice(h * DH, (h + 1) * DH)
        q = _silu(q_ref[:, sl].astype(F32))
        v = i_ref[:, sl].astype(F32)
        lb = lb_ref[:, sl]
        f = lb + (1.0 - lb) * _sigmoid(f_ref[:, sl].astype(F32))
        k = 1.0 - f
        lf = jnp.log(f)
        g = _cumsum_rows(mb, lf)
        st = st_sc[h]
        o = _dot_nt((q * jnp.exp(g)).astype(BF16), st.astype(BF16))
        a = jnp.zeros((CS, CS), F32)
        n = CS
        while n >= 16:
            half = n // 2
            second = jnp.bitwise_and(tix, n - 1) >= half
            rowside = jnp.logical_not(second) if rev else second
            r = _block_ref(g, n, half if rev else half - 1)
            e = jnp.exp(jnp.where(rowside, g - r, r - g))
            qs = jnp.where(rowside, q * e, 0.0).astype(BF16)
            ks = jnp.where(rowside, 0.0, k * e).astype(BF16)
            sh_n = n.bit_length() - 1
            same = jnp.right_shift(row, sh_n) == jnp.right_shift(col, sh_n)
            a = a + jnp.where(same, _dot_nt(qs, ks), 0.0)
            n = half
        o = o + _dot(a.astype(BF16), v.astype(BF16))
        t8 = jnp.bitwise_and(tix, 7)
        for dl in range(8):
            if dl == 0:
                ksh, gsh, vsh = k, g, v
            else:
                sh = (CS - dl) if rev else dl
                ksh = pltpu.roll(k, sh, 0)
                gsh = pltpu.roll(g, sh, 0)
                vsh = pltpu.roll(v, sh, 0)
            valid = (t8 + dl <= 7) if rev else (t8 >= dl)
            e = jnp.exp(jnp.minimum(g - gsh, 0.0))
            coef = jnp.sum(q * ksh * e, axis=1, keepdims=True)
            o = o + jnp.where(valid, coef, 0.0) * vsh
        if rev:
            tot = yp_ref[:, sl] + o
            y = _rms(tot) * ng_ref[:, sl]
            o_ref[:, sl] = (y * _sigmoid(g_ref[:, sl].astype(F32))).astype(o_ref.dtype)
        else:
            o_ref[:, sl] = o
        g_tot = jnp.sum(lf, axis=0, keepdims=True)
        kdec = k * jnp.exp(g_tot - g)
        st_sc[h] = st * jnp.exp(g_tot) + _dot(v.T.astype(BF16), kdec.astype(BF16))


def _hgrn2(u, lb, rev, yprev=None, norm_g=None):
    nb = u.shape[0]
    cmap = _chunk_of_step(rev)
    d = 1 if rev else 0
    in_specs = [_u_spec(BW, CB_HG_Q, cmap), _u_spec(BW, CB_HG_I, cmap), _u_spec(BW, CB_HG_F + d, cmap),
                pl.BlockSpec((1, BW), lambda b, s: (0, 0))]
    args = [u, u, u, lb]
    if rev:
        in_specs += [pl.BlockSpec((None, CS, BW), lambda b, s: (b, cmap(s), 0)),
                     _u_spec(BW, CB_HG_G, cmap),
                     pl.BlockSpec((1, BW), lambda b, s: (0, 0))]
        args += [yprev, u, norm_g]
    return pl.pallas_call(
        functools.partial(_hgrn2_kernel, rev),
        grid=(nb, NCH),
        in_specs=in_specs,
        out_specs=pl.BlockSpec((None, CS, BW), lambda b, s: (b, cmap(s), 0)),
        out_shape=jax.ShapeDtypeStruct((nb, LT, BW), BF16 if rev else F32),
        scratch_shapes=[pltpu.VMEM((NH, DH, DH), F32)],
        compiler_params=_cparams(("parallel", "arbitrary")),
        name="hgrn2_rev" if rev else "hgrn2_fwd",
    )(*args)


def _expand_heads(cols, lane_head):
    r = cols[3]
    for hh in (2, 1, 0):
        r = jnp.where(lane_head == hh, cols[hh], r)
    return r


def _ssd_kernel(rev, *refs):
    if rev:
        (x_ref, gc_ref, gt_ref, bc_ref, bt_ref, ac_ref, at_ref, yp_ref, z_ref, dk_ref, ng_ref,
         o_ref, ht_sc) = refs
    else:
        x_ref, gc_ref, gt_ref, bc_ref, bt_ref, ac_ref, at_ref, o_ref, ht_sc = refs
    d = 1 if rev else 0

    @pl.when(pl.program_id(1) == 0)
    def _():
        ht_sc[...] = jnp.zeros_like(ht_sc)

    mask = _chunk_mask(rev)
    mb = jnp.where(mask, 1.0, 0.0).astype(BF16)
    dt_c = _softplus(gc_ref[...] + bc_ref[...])
    da_c = dt_c * ac_ref[...]
    acum_c = _cumsum_rows(mb, da_c)
    dt_t = _softplus(gt_ref[...] + bt_ref[...])
    da_t = dt_t * at_ref[...]
    acum_t = _cumsum_cols(da_t, mb)
    gw = 4 * SSD_P
    lane_head = jnp.right_shift(lax.broadcasted_iota(jnp.int32, (1, gw), 1), 6)
    ys = []
    for grp in range(SSD_GROUPS):
        xg = x_ref[:, grp * gw:(grp + 1) * gw]
        bm = x_ref[:, BW + grp * SSD_N:BW + (grp + 1) * SSD_N]
        cm = x_ref[:, BW + SSD_GROUPS * SSD_N + grp * SSD_N:BW + SSD_GROUPS * SSD_N + (grp + 1) * SSD_N]
        cmb = cm.astype(BF16)
        cb = _dot_nt(cmb, bm.astype(BF16))
        ht = ht_sc[grp]
        y_state = _dot(cmb, ht.astype(BF16))
        lanes = [16 + d * 8 + grp * 4 + hh for hh in range(4)]
        dt_cols = [dt_c[:, l:l + 1] for l in lanes]
        a_cols = [acum_c[:, l:l + 1] for l in lanes]
        a_tots = [jnp.sum(da_c[:, l:l + 1], axis=0, keepdims=True) for l in lanes]
        xdt = xg * _expand_heads(dt_cols, lane_head)
        xdtb = xdt.astype(BF16)
        y = jnp.exp(_expand_heads(a_cols, lane_head)) * y_state
        for hh in range(4):
            l = lanes[hh]
            dec = jnp.exp(jnp.where(mask, a_cols[hh] - acum_t[l:l + 1, :], -jnp.inf))
            yh = _dot((cb * dec).astype(BF16), xdtb)
            y = y + jnp.where(lane_head == hh, yh, 0.0)
        ys.append(y)
        w_exp = jnp.exp(_expand_heads([a_tots[hh] - a_cols[hh] for hh in range(4)], lane_head))
        e_row = jnp.exp(_expand_heads(a_tots, lane_head))
        ht_sc[grp] = ht * e_row + _dot(bm.T.astype(BF16), (xdt * w_exp).astype(BF16))
    y = jnp.concatenate(ys, axis=1)
    if rev:
        tot = yp_ref[...] + y + dk_ref[...] * x_ref[:, :BW]
        tot = tot * _silu(z_ref[...].astype(F32))
        o_ref[...] = (_rms(tot) * ng_ref[...]).astype(o_ref.dtype)
    else:
        o_ref[...] = y


def _ssd(xbc, usm, ust, u, bias_c, bias_t, aneg_c, aneg_t, rev, yprev=None, dskip=None, norm_g=None):
    nb = xbc.shape[0]
    cmap = _chunk_of_step(rev)
    in_specs = [pl.BlockSpec((None, CS, SSD_XBC), lambda b, s: (b, cmap(s), 0)),
                pl.BlockSpec((None, CS, 128), lambda b, s: (b, cmap(s), 0)),
                pl.BlockSpec((None, None, 32, CS), lambda b, s: (b, cmap(s), 0, 0)),
                pl.BlockSpec((1, 128), lambda b, s: (0, 0)),
                pl.BlockSpec((32, 1), lambda b, s: (0, 0)),
                pl.BlockSpec((1, 128), lambda b, s: (0, 0)),
                pl.BlockSpec((32, 1), lambda b, s: (0, 0))]
    args = [xbc, usm, ust, bias_c, bias_t, aneg_c, aneg_t]
    if rev:
        in_specs += [pl.BlockSpec((None, CS, BW), lambda b, s: (b, cmap(s), 0)),
                     _u_spec(BW, CB_SSD_Z, cmap),
                     pl.BlockSpec((1, BW), lambda b, s: (0, 0)),
                     pl.BlockSpec((1, BW), lambda b, s: (0, 0))]
        args += [yprev, u, dskip, norm_g]
    return pl.pallas_call(
        functools.partial(_ssd_kernel, rev),
        grid=(nb, NCH),
        in_specs=in_specs,
        out_specs=pl.BlockSpec((None, CS, BW), lambda b, s: (b, cmap(s), 0)),
        out_shape=jax.ShapeDtypeStruct((nb, LT, BW), BF16 if rev else F32),
        scratch_shapes=[pltpu.VMEM((SSD_GROUPS, SSD_N, 4 * SSD_P), F32)],
        compiler_params=_cparams(("parallel", "arbitrary")),
        name="ssd_rev" if rev else "ssd_fwd",
    )(*args)


def _mla_proj_kernel(cq_ref, ckv_ref, kr_ref, gq_ref, gkv_ref, wqa_ref, wqb_ref, wk_ref, wv_ref,
                     rk_ref, cos_ref, sin_ref, q_ref, k_ref, v_ref):
    nq = (_rms(cq_ref[...].astype(F32)) * gq_ref[...]).astype(BF16)
    qa = _dot(nq, wqa_ref[...])
    qb = _dot(nq, wqb_ref[...])
    cos_t = cos_ref[...]
    sin_t = sin_ref[...]
    cosf = jnp.concatenate([cos_t] * NH, axis=1)
    sinf = jnp.concatenate([sin_t] * NH, axis=1)
    q_ref[...] = ((qa * cosf + qb * sinf) * MLA_SCALE).astype(BF16)
    nkv = (_rms(ckv_ref[...].astype(F32)) * gkv_ref[...]).astype(BF16)
    kn = _dot(nkv, wk_ref[...])
    v_ref[...] = _dot(nkv, wv_ref[...]).astype(BF16)
    kr = kr_ref[...]
    krr = kr.astype(F32) * cos_t[:, 128:256] + _dot(kr, rk_ref[...]) * sin_t[:, 128:256]
    zero = jnp.zeros_like(krr)
    k_ref[...] = (kn + jnp.concatenate([zero, krr] * NH, axis=1)).astype(BF16)


def _mla_proj(u, gq, gkv, wqa, wqb, wk, wv, rk, cos_t, sin_t):
    nb = u.shape[0]
    const = lambda shape: pl.BlockSpec(shape, lambda b, i: (0, 0))
    return pl.pallas_call(
        _mla_proj_kernel,
        grid=(nb, NRT),
        in_specs=[pl.BlockSpec((None, RT, MLA_Q_RANK), lambda b, i: (b, i, CB_MLA_CQ)),
                  pl.BlockSpec((None, RT, MLA_KV_RANK), lambda b, i: (b, i, CB_MLA_CKV)),
                  pl.BlockSpec((None, RT, 128), lambda b, i: (b, i, CB_MLA_KR)),
                  const((1, MLA_Q_RANK)), const((1, MLA_KV_RANK)),
                  const((MLA_Q_RANK, NH * 256)), const((MLA_Q_RANK, NH * 256)),
                  const((MLA_KV_RANK, NH * 256)), const((MLA_KV_RANK, NH * DH)),
                  const((128, 128)),
                  pl.BlockSpec((RT, 256), lambda b, i: (i, 0)),
                  pl.BlockSpec((RT, 256), lambda b, i: (i, 0))],
        out_specs=[_row_spec(NH * 256), _row_spec(NH * 256), _row_spec(NH * DH)],
        out_shape=[jax.ShapeDtypeStruct((nb, LT, NH * 256), BF16),
                   jax.ShapeDtypeStruct((nb, LT, NH * 256), BF16),
                   jax.ShapeDtypeStruct((nb, LT, NH * DH), BF16)],
        compiler_params=_cparams(("parallel", "parallel")),
        name="mla_proj",
    )(u, u, u, gq, gkv, wqa, wqb, wk, wv, rk, cos_t, sin_t)


def _attn_kernel(q_ref, k_ref, v_ref, o_ref):
    s = _dot_nt(q_ref[...], k_ref[...])
    kcol = lax.broadcasted_iota(jnp.int32, (1, LT), 1)
    hide = jnp.logical_and(pl.program_id(2) == NRT - 1, kcol < SEQ)
    s = jnp.where(hide, -jnp.inf, s)
    m = jnp.max(s, axis=1, keepdims=True)
    p = jnp.exp(s - m)
    l = jnp.sum(p, axis=1, keepdims=True)
    o_ref[...] = (_dot(p.astype(BF16), v_ref[...]) / l).astype(o_ref.dtype)


def _attention(q, k, v):
    nb = q.shape[0]
    return pl.pallas_call(
        _attn_kernel,
        grid=(nb, NH, NRT),
        in_specs=[pl.BlockSpec((None, RT, 256), lambda b, h, i: (b, i, h)),
                  pl.BlockSpec((None, LT, 256), lambda b, h, i: (b, 0, h)),
                  pl.BlockSpec((None, LT, DH), lambda b, h, i: (b, 0, h))],
        out_specs=pl.BlockSpec((None, RT, DH), lambda b, h, i: (b, i, h)),
        out_shape=jax.ShapeDtypeStruct((nb, LT, NH * DH), BF16),
        compiler_params=_cparams(("parallel", "parallel", "arbitrary")),
        name="mla_attn",
    )(q, k, v)


def _merge_kernel(h_ref, b0_ref, b1_ref, b2_ref, b3_ref, wg_ref, wb_ref, o_ref):
    h = h_ref[...]
    acc = None
    for kk, br in enumerate((b0_ref, b1_ref, b2_ref, b3_ref)):
        gate = _sigmoid(_dot(h, wg_ref[kk]))
        term = gate * _dot(br[...], wb_ref[kk])
        acc = term if acc is None else acc + term
    o_ref[...] = acc.astype(o_ref.dtype)


def _merge(h, branches, wg, wb, tm=1152, tn=256):
    m = h.shape[0]
    bspec = pl.BlockSpec((tm, BW), lambda i, j: (i, 0))
    return pl.pallas_call(
        _merge_kernel,
        grid=(m // tm, D // tn),
        in_specs=[pl.BlockSpec((tm, D), lambda i, j: (i, 0)), bspec, bspec, bspec, bspec,
                  pl.BlockSpec((4, D, tn), lambda i, j: (0, 0, j)),
                  pl.BlockSpec((4, BW, tn), lambda i, j: (0, 0, j))],
        out_specs=pl.BlockSpec((tm, tn), lambda i, j: (i, j)),
        out_shape=jax.ShapeDtypeStruct((m, D), BF16),
        compiler_params=_cparams(("parallel", "arbitrary")),
        name="merge",
    )(h, *branches, wg, wb)


def _pack_pairs(x):
    n = x.shape[1] // 2
    lo = pltpu.bitcast(x[:, :n].astype(BF16).astype(F32), jnp.uint32)
    hi = pltpu.bitcast(x[:, n:].astype(BF16).astype(F32), jnp.uint32)
    return jnp.bitwise_or(jnp.right_shift(lo, jnp.uint32(16)), jnp.bitwise_and(hi, jnp.uint32(0xFFFF0000)))


def _unpack_pairs(w):
    lo = pltpu.bitcast(jnp.left_shift(w, jnp.uint32(16)), F32)
    hi = pltpu.bitcast(jnp.bitwise_and(w, jnp.uint32(0xFFFF0000)), F32)
    return jnp.concatenate([lo, hi], axis=1)


def _outproj_kernel(first, a_ref, w_ref, *refs):
    if first:
        x_ref, c_ref = refs[:2]
        x_in = _pick_x(x_ref, c_ref)
        refs = refs[2:]
    else:
        x_in = refs[0][...]
        refs = refs[1:]
    g1_ref, gate_ref, g2_ref, sh_ref, sc_ref, wa_ref, wb_ref, wc_ref, xo_ref, hp_ref, e_ref, wt_ref = refs
    y = _dot(a_ref[...], w_ref[...])
    xn = x_in + gate_ref[...] * (_rms(y) * g1_ref[...])
    xo_ref[...] = xn
    h2 = _rms(xn) * g2_ref[...]
    h2 = h2 * (1.0 + sc_ref[...]) + sh_ref[...]
    hp_ref[...] = _pack_pairs(h2)
    _route_math(h2, wa_ref, wb_ref, wc_ref, e_ref, wt_ref)


def _outproj(acc, w_out, xs, normv, modv, wr3, layer):
    nb = acc.shape[0]
    first = len(xs) == 2
    wspec = pl.BlockSpec((D, 128), lambda b, i: (0, 0))
    return pl.pallas_call(
        functools.partial(_outproj_kernel, first),
        grid=(nb, NRT),
        in_specs=[_row_spec(D), pl.BlockSpec((D, D), lambda b, i: (0, 0))]
                 + (_split_x_specs() if first else [_row_spec(D)])
                 + [_norm_spec(layer, 1), _mod_spec(layer, 2), _norm_spec(layer, 2),
                    _mod_spec(layer, 3), _mod_spec(layer, 4), wspec, wspec, wspec],
        out_specs=[_row_spec(D), _row_spec(D // 2), _row_spec(128), _row_spec(128)],
        out_shape=[jax.ShapeDtypeStruct((nb, LT, D), F32), jax.ShapeDtypeStruct((nb, LT, D // 2), jnp.uint32),
                   jax.ShapeDtypeStruct((nb, LT, 128), jnp.int32), jax.ShapeDtypeStruct((nb, LT, 128), F32)],
        compiler_params=_cparams(("parallel", "parallel")),
        name="outproj",
    )(acc, w_out, *xs, normv, modv, normv, modv, modv, *wr3)


def _route_math(hf, wa_ref, wb_ref, wc_ref, e_ref, w_ref):
    h = hf.astype(BF16)
    hl = (hf - h.astype(F32)).astype(BF16)
    logits = (_dot(h, wa_ref[...]) + _dot(h, wb_ref[...]) + _dot(h, wc_ref[...])
              + _dot(hl, wa_ref[...]) + _dot(hl, wb_ref[...]))
    lane = lax.broadcasted_iota(jnp.int32, (1, 128), 1).astype(F32)
    ninf = -jnp.inf
    is_grp = lane < N_GROUPS
    gl = jnp.where(is_grp, logits, ninf)
    gmax = jnp.max(gl, axis=1, keepdims=True)
    gidx = jnp.min(jnp.where(gl == gmax, lane, 999.0), axis=1, keepdims=True)
    gsum = jnp.sum(jnp.where(is_grp, jnp.exp(gl - gmax), 0.0), axis=1, keepdims=True)
    gp = 1.0 / gsum
    lo = N_GROUPS + EXP_PER_GROUP * gidx
    sel = jnp.logical_and(lane >= lo, lane < lo + EXP_PER_GROUP)
    el = jnp.where(sel, logits, ninf)
    v1 = jnp.max(el, axis=1, keepdims=True)
    i1 = jnp.min(jnp.where(el == v1, lane, 999.0), axis=1, keepdims=True)
    el2 = jnp.where(lane == i1, ninf, el)
    v2 = jnp.max(el2, axis=1, keepdims=True)
    i2 = jnp.min(jnp.where(el2 == v2, lane, 999.0), axis=1, keepdims=True)
    t = jnp.exp(v2 - v1)
    w1 = gp / (1.0 + t)
    w2 = gp * t / (1.0 + t)
    e_ref[...] = jnp.where(lane == 0.0, i1 - N_GROUPS, jnp.where(lane == 1.0, i2 - N_GROUPS, 0.0)).astype(jnp.int32)
    w_ref[...] = jnp.where(lane == 0.0, w1, jnp.where(lane == 1.0, w2, 0.0))


GATHER_UNROLL = 8


def _expert_kernel(te_ref, nu_ref, rt_ref, h_hbm, w1_ref, w3_ref, w2_ref, o_ref, xbuf, sem):
    i = pl.program_id(0)
    n_used = nu_ref[0]
    slot = i % 2

    def issue_tile(j, s):
        base = j * TE

        def body(g, c):
            for uu in range(GATHER_UNROLL):
                r = g * GATHER_UNROLL + uu
                pltpu.make_async_copy(h_hbm.at[pl.ds(rt_ref[base + r], 1), :],
                                      xbuf.at[s, pl.ds(r, 1), :], sem.at[s]).start(priority=1)
            return c
        lax.fori_loop(0, TE // GATHER_UNROLL, body, 0)

    @pl.when(jnp.logical_and(i == 0, n_used > 0))
    def _():
        issue_tile(0, 0)

    @pl.when(i + 1 < n_used)
    def _():
        issue_tile(i + 1, 1 - slot)

    @pl.when(i < n_used)
    def _():
        pltpu.make_async_copy(h_hbm.at[pl.ds(0, TE), :], xbuf.at[slot], sem.at[slot]).wait()
        x = _unpack_pairs(xbuf[slot]).astype(BF16)
        a = _dot(x, w1_ref[...].astype(BF16))
        b = _dot(x, w3_ref[...].astype(BF16))
        hm = (_silu(a) * b).astype(BF16)
        o_ref[...] = _pack_pairs(_dot(hm, w2_ref[...].astype(BF16)))

    @pl.when(i >= n_used)
    def _():
        o_ref[...] = jnp.zeros_like(o_ref)


def _experts(tile_expert, n_used, row_token, hp, w1, w3, w2, n_tiles, layer):
    wmap = lambda i, te, nu, rt: (layer * N_EXPERTS + te[i], 0, 0)
    grid_spec = pltpu.PrefetchScalarGridSpec(
        num_scalar_prefetch=3,
        grid=(n_tiles,),
        in_specs=[pl.BlockSpec(memory_space=pl.ANY),
                  pl.BlockSpec((None, D, D_EXPERT), wmap),
                  pl.BlockSpec((None, D, D_EXPERT), wmap),
                  pl.BlockSpec((None, D_EXPERT, D), wmap)],
        out_specs=pl.BlockSpec((TE, D // 2), lambda i, te, nu, rt: (i, 0)),
        scratch_shapes=[pltpu.VMEM((2, TE, D // 2), jnp.uint32), pltpu.SemaphoreType.DMA((2,))],
    )
    return pl.pallas_call(
        _expert_kernel,
        grid_spec=grid_spec,
        out_shape=jax.ShapeDtypeStruct((n_tiles * TE, D // 2), jnp.uint32),
        compiler_params=_cparams(("arbitrary",)),
        name="moe_experts",
    )(tile_expert, n_used, row_token, hp, w1, w3, w2)


def _combine_kernel(has_next, pos_ref, ys_hbm, w_ref, x_ref, g3_ref, gate_ref, *rest):
    if has_next:
        g0_ref, sh_ref, sc_ref, xo_ref, ho_ref, buf, sem = rest
    else:
        xo_ref, buf, sem = rest
    step = pl.program_id(0) * pl.num_programs(1) + pl.program_id(1)
    n_steps = pl.num_programs(0) * pl.num_programs(1)
    slot = step % 2

    def issue_tile(j, s):
        t0 = j * RT

        def body(g, c):
            for uu in range(GATHER_UNROLL):
                r = g * GATHER_UNROLL + uu
                for kk in range(2):
                    pltpu.make_async_copy(ys_hbm.at[pl.ds(pos_ref[(t0 + r) * 2 + kk], 1), :],
                                          buf.at[s, kk, pl.ds(r, 1), :], sem.at[s]).start(priority=kk)
            return c
        lax.fori_loop(0, RT // GATHER_UNROLL, body, 0)

    @pl.when(step == 0)
    def _():
        issue_tile(0, 0)

    @pl.when(step + 1 < n_steps)
    def _():
        issue_tile(step + 1, 1 - slot)

    for kk in range(2):
        pltpu.make_async_copy(ys_hbm.at[pl.ds(0, RT), :], buf.at[slot, kk], sem.at[slot]).wait()
    w = w_ref[...]
    y = w[:, 0:1] * _unpack_pairs(buf[slot, 0]) + w[:, 1:2] * _unpack_pairs(buf[slot, 1])
    xn = x_ref[...] + gate_ref[...] * (_rms(y) * g3_ref[...])
    xo_ref[...] = xn
    if has_next:
        hn = _rms(xn) * g0_ref[...]
        ho_ref[...] = (hn * (1.0 + sc_ref[...]) + sh_ref[...]).astype(BF16)


def _combine(pos, ys, wts, xcat, normv, modv, layer):
    nb = xcat.shape[0]
    has_next = layer + 1 < DEPTH

    def mod_spec(lyr, k):
        def imap(b, i, pos_ref):
            row = jnp.where(i == NRT - 1, 8, b)
            return ((lyr * 16 + row) * 6 + k, 0, 0)
        return pl.BlockSpec((None, 1, D), imap)

    def norm_spec(lyr, k):
        return pl.BlockSpec((None, 1, D), lambda b, i, pos_ref: (lyr * 4 + k, 0, 0))

    def row_spec(width):
        return pl.BlockSpec((None, RT, width), lambda b, i, pos_ref: (b, i, 0))

    in_specs = [pl.BlockSpec(memory_space=pl.ANY), row_spec(128), row_spec(D),
                norm_spec(layer, 3), mod_spec(layer, 5)]
    args = [ys, wts, xcat, normv, modv]
    out_specs = [row_spec(D)]
    n_row_tiles = NRT if has_next else SEQ // RT
    out_shape = [jax.ShapeDtypeStruct((nb, n_row_tiles * RT, D), F32)]
    if has_next:
        in_specs += [norm_spec(layer + 1, 0), mod_spec(layer + 1, 0), mod_spec(layer + 1, 1)]
        args += [normv, modv, modv]
        out_specs.append(row_spec(D))
        out_shape.append(jax.ShapeDtypeStruct((nb, LT, D), BF16))
    grid_spec = pltpu.PrefetchScalarGridSpec(
        num_scalar_prefetch=1,
        grid=(nb, n_row_tiles),
        in_specs=in_specs,
        out_specs=out_specs,
        scratch_shapes=[pltpu.VMEM((2, 2, RT, D // 2), jnp.uint32), pltpu.SemaphoreType.DMA((2,))],
    )
    return pl.pallas_call(
        functools.partial(_combine_kernel, has_next),
        grid_spec=grid_spec,
        out_shape=out_shape,
        compiler_params=_cparams(("arbitrary", "arbitrary")),
        name="moe_combine",
    )(pos, *args)


def _plan_kernel(e_ref, pos_ref, cnt_ref, rank_sc, carry_sc):
    phase = pl.program_id(0)
    i = pl.program_id(1)
    lane = lax.broadcasted_iota(jnp.int32, (1, 128), 1)
    e = e_ref[...]
    oh1 = lane == e[:, 0:1]
    oh2 = lane == e[:, 1:2]

    @pl.when(jnp.logical_and(phase == 0, i == 0))
    def _():
        carry_sc[...] = jnp.zeros_like(carry_sc)

    @pl.when(phase == 0)
    def _():
        s = jnp.where(jnp.logical_or(oh1, oh2), 1.0, 0.0)
        row = lax.broadcasted_iota(jnp.int32, (RT, RT), 0)
        col = lax.broadcasted_iota(jnp.int32, (RT, RT), 1)
        lower = jnp.where(col < row, 1.0, 0.0).astype(BF16)
        carry = carry_sc[0:1, :]
        before = _dot(lower, s.astype(BF16)) + carry
        r1 = jnp.sum(jnp.where(oh1, before, 0.0), axis=1, keepdims=True)
        r2 = jnp.sum(jnp.where(oh2, before, 0.0), axis=1, keepdims=True)
        rank_sc[i] = jnp.where(lane == 0, r1, jnp.where(lane == 1, r2, 0.0))
        carry_sc[...] = jnp.broadcast_to(carry + jnp.sum(s, axis=0, keepdims=True), carry_sc.shape)
        cnt_ref[...] = carry_sc[...]

    @pl.when(phase == 1)
    def _():
        counts = carry_sc[0:1, :]
        padded = jnp.floor((counts + (TE - 1)) * (1.0 / TE)) * TE
        r128 = lax.broadcasted_iota(jnp.int32, (128, 128), 0)
        c128 = lax.broadcasted_iota(jnp.int32, (128, 128), 1)
        upper = jnp.where(r128 < c128, 1.0, 0.0).astype(BF16)
        pa, pb, pc = _split3(jnp.broadcast_to(padded, (8, 128)))
        offs = (_dot(pa, upper) + _dot(pb, upper) + _dot(pc, upper))[0:1, :]
        rank = rank_sc[i]
        p1 = rank[:, 0:1] + jnp.sum(jnp.where(oh1, offs, 0.0), axis=1, keepdims=True)
        p2 = rank[:, 1:2] + jnp.sum(jnp.where(oh2, offs, 0.0), axis=1, keepdims=True)
        pos_ref[...] = jnp.where(lane == 0, p1, jnp.where(lane == 1, p2, 0.0)).astype(jnp.int32)


def _plan(eid, n_row_tiles):
    nb = eid.shape[0]
    nt = nb * n_row_tiles
    emap = lambda p, i: (i // n_row_tiles, i % n_row_tiles, 0)
    return pl.pallas_call(
        _plan_kernel,
        grid=(2, nt),
        in_specs=[pl.BlockSpec((None, RT, 128), emap)],
        out_specs=[pl.BlockSpec((RT, 128), lambda p, i: (p * i, 0)),
                   pl.BlockSpec((8, 128), lambda p, i: (0, 0))],
        out_shape=[jax.ShapeDtypeStruct((nt * RT, 128), jnp.int32), jax.ShapeDtypeStruct((8, 128), F32)],
        scratch_shapes=[pltpu.VMEM((nt, RT, 128), F32), pltpu.VMEM((8, 128), F32)],
        compiler_params=_cparams(("arbitrary", "arbitrary")),
        name="moe_plan",
    )(eid)


def _dispatch_tiles(pos, counts, n_tiles, n_row_tiles, nb):
    cnt = counts[0, :N_EXPERTS].astype(jnp.int32)
    padded = ((cnt + TE - 1) // TE) * TE
    ends = jnp.cumsum(padded)
    tile_start = jnp.arange(n_tiles, dtype=jnp.int32) * TE
    tile_expert = jnp.minimum(jnp.searchsorted(ends, tile_start, side="right"), N_EXPERTS - 1).astype(jnp.int32)
    n_used = (ends[-1] // TE).astype(jnp.int32).reshape(1)
    pos2 = pos[:, :2]
    tok = (jnp.arange(nb, dtype=jnp.int32)[:, None] * LT
           + jnp.arange(n_row_tiles * RT, dtype=jnp.int32)[None, :]).reshape(-1)
    tok2 = jnp.broadcast_to(tok[:, None], pos2.shape)
    row_token = jnp.zeros((n_tiles * TE,), jnp.int32).at[pos2.reshape(-1)].set(tok2.reshape(-1))
    return pos2.reshape(-1), row_token, tile_expert, n_used


def _rope_tables():
    tok = jnp.arange(SEQ)
    pos_row = (tok // GRID_W).astype(F32)
    pos_col = (tok % GRID_W).astype(F32)
    quarter = MLA_ROPE // 4
    inv_freq = ROPE_BASE ** (-jnp.arange(quarter, dtype=F32) / quarter)
    ang_r = pos_row[:, None] * inv_freq
    ang_c = pos_col[:, None] * inv_freq
    cos = jnp.concatenate([jnp.cos(ang_r)] * 2 + [jnp.cos(ang_c)] * 2, axis=1)
    sin = jnp.concatenate([jnp.sin(ang_r)] * 2 + [jnp.sin(ang_c)] * 2, axis=1)
    cos = jnp.concatenate([cos, jnp.ones((CTX, MLA_ROPE), F32)], axis=0)
    sin = jnp.concatenate([sin, jnp.zeros((CTX, MLA_ROPE), F32)], axis=0)
    pad = jnp.zeros((LT, 64), F32)
    cos_t = jnp.concatenate([jnp.ones((LT, 128), F32), cos, pad], axis=1)
    sin_t = jnp.concatenate([jnp.zeros((LT, 128), F32), sin, pad], axis=1)
    return cos_t, sin_t


def _rope_rot_matrix():
    j = jnp.arange(MLA_ROPE)
    first = (j % 32) < 16
    src = jnp.where(first, j + 16, j - 16)
    sign = jnp.where(first, -1.0, 1.0)
    return jnp.zeros((MLA_ROPE, MLA_ROPE), F32).at[src, j].set(sign)


def _layer_params(l, w_main, w_small, ml_f_bias, lb_all, hg_norm_g, mla_g_q, mla_g_kv, mla_w_uq, mla_w_ukv,
                  ssd_conv_w, ssd_conv_b, ssd_a_log, ssd_dt_bias, ssd_d, ssd_norm_g, w_gate, w_br, w_out,
                  moe_w_grp, moe_w_exp):
    p = {}
    p["w_main"] = w_main[l]
    p["w_small"] = w_small[l]
    fb = jnp.zeros((2, 2, NH), F32).at[:, 1, :].set(ml_f_bias[l]).reshape(16)
    ml_bias = jnp.concatenate([fb, jnp.zeros((112,), F32)])
    p["ml_bias_c"] = ml_bias.reshape(1, 128)
    p["ml_bias_t"] = ml_bias[:32].reshape(32, 1)
    dtb = jnp.concatenate([jnp.zeros((16,), F32), ssd_dt_bias[l].reshape(16), jnp.zeros((96,), F32)])
    p["ssd_bias_c"] = dtb.reshape(1, 128)
    p["ssd_bias_t"] = dtb[:32].reshape(32, 1)
    aneg = jnp.concatenate([jnp.zeros((16,), F32), -jnp.exp(ssd_a_log[l].astype(F32)).reshape(16),
                            jnp.zeros((96,), F32)])
    p["ssd_aneg_c"] = aneg.reshape(1, 128)
    p["ssd_aneg_t"] = aneg[:32].reshape(32, 1)
    p["lb"] = lb_all[l].reshape(1, BW)
    p["hg_norm_g"] = hg_norm_g[l].reshape(1, BW)
    p["ssd_dskip"] = jnp.repeat(ssd_d[l], SSD_P).reshape(1, BW)
    p["ssd_norm_g"] = ssd_norm_g[l].reshape(1, BW)
    p["conv_w"] = jnp.concatenate([ssd_conv_w[l], jnp.zeros((5, SSD_XBC), F32)], axis=0)
    p["conv_b"] = ssd_conv_b[l].reshape(1, SSD_XBC)
    rot = _rope_rot_matrix()
    wq = mla_w_uq[l].reshape(MLA_Q_RANK, NH, MLA_NOPE + MLA_ROPE)
    zq = jnp.zeros((MLA_Q_RANK, NH, 64), F32)
    wqa = jnp.concatenate([wq, zq], axis=2).reshape(MLA_Q_RANK, NH * 256)
    wq_rot = jnp.einsum("rhj,jk->rhk", wq[:, :, MLA_NOPE:], rot)
    wqb = jnp.concatenate([jnp.zeros((MLA_Q_RANK, NH, MLA_NOPE), F32), wq_rot, zq], axis=2)
    p["wqa"] = wqa.astype(BF16)
    p["wqb"] = wqb.reshape(MLA_Q_RANK, NH * 256).astype(BF16)
    wkv = mla_w_ukv[l].reshape(MLA_KV_RANK, NH, 2 * DH)
    wk = jnp.concatenate([wkv[:, :, :DH], jnp.zeros((MLA_KV_RANK, NH, DH), F32)], axis=2)
    p["wk"] = wk.reshape(MLA_KV_RANK, NH * 256).astype(BF16)
    p["wv"] = wkv[:, :, DH:].reshape(MLA_KV_RANK, NH * DH).astype(BF16)
    p["rk"] = jnp.zeros((128, 128), F32).at[:MLA_ROPE, :MLA_ROPE].set(rot).astype(BF16)
    p["gq"] = mla_g_q[l].reshape(1, MLA_Q_RANK)
    p["gkv"] = mla_g_kv[l].reshape(1, MLA_KV_RANK)
    p["w_gate"] = w_gate[l].astype(BF16)
    p["w_br"] = w_br[l].astype(BF16)
    p["w_out"] = w_out[l].astype(BF16)
    wr = jnp.concatenate([moe_w_grp[l], moe_w_exp[l], jnp.zeros((D, 128 - N_GROUPS - N_EXPERTS), F32)], axis=1)
    a = wr.astype(BF16)
    r = wr - a.astype(F32)
    b = r.astype(BF16)
    c = (r - b.astype(F32)).astype(BF16)
    p["wr3"] = (a, b, c)
    return p


def _mixer(h, p, cos_t, sin_t, nb):
    t = nb * LT
    hf = h.reshape(t, D)
    zero_b = jnp.zeros((1, U_W), F32)
    u = _mm(hf, p["w_main"], zero_b, BF16, 1152, 1024).reshape(nb, LT, U_W)
    usm = _mm(hf, p["w_small"], zero_b[:, :128], F32, 1152, 128).reshape(nb, LT, 128)
    ust = usm[:, :, :32].reshape(nb, NCH, CS, 32).transpose(0, 1, 3, 2)
    ml_f = _mlstm(u, usm, ust, p["ml_bias_c"], p["ml_bias_t"], False)
    ml = _mlstm(u, usm, ust, p["ml_bias_c"], p["ml_bias_t"], True, ml_f)
    hg_f = _hgrn2(u, p["lb"], False)
    hg = _hgrn2(u, p["lb"], True, hg_f, p["hg_norm_g"])
    q, k, v = _mla_proj(u, p["gq"], p["gkv"], p["wqa"], p["wqb"], p["wk"], p["wv"], p["rk"], cos_t, sin_t)
    mla = _attention(q, k, v)
    xbc = _ssd_conv(u, p["conv_w"], p["conv_b"])
    sargs = (xbc, usm, ust, u, p["ssd_bias_c"], p["ssd_bias_t"], p["ssd_aneg_c"], p["ssd_aneg_t"])
    ssd_f = _ssd(*sargs, False)
    ssd = _ssd(*sargs, True, ssd_f, p["ssd_dskip"], p["ssd_norm_g"])
    branches = [b.reshape(t, BW) for b in (ml, hg, mla, ssd)]
    return _merge(hf, branches, p["w_gate"], p["w_br"]).reshape(nb, LT, D)


def kernel(x, c, ctx, c_ctx, w_ada, b_ada, norm_g, w_in, ml_f_bias, hg_lb_logits, hg_norm_g, mla_g_q, mla_g_kv,
           mla_w_uq, mla_w_ukv, ssd_conv_w, ssd_conv_b, ssd_a_log, ssd_dt_bias, ssd_d, ssd_norm_g, w_gate,
           w_br, w_out, moe_w_grp, moe_w_exp, moe_w1, moe_w3, moe_w2):
    nb = x.shape[0]
    t = nb * LT
    lb_all = jnp.cumsum(jax.nn.softmax(hg_lb_logits.astype(F32), axis=0), axis=0)
    cvec = jnp.zeros((16, D), F32).at[:nb].set(c).at[8].set(c_ctx)
    s_vec = jax.nn.silu(cvec).astype(BF16)
    modv = jnp.stack([_mm(s_vec, w_ada[l], b_ada[l].reshape(1, 6 * D), F32, 16, 1024) for l in range(DEPTH)])
    modv = modv.reshape(DEPTH * 16 * 6, 1, D)
    normv = norm_g.reshape(DEPTH * 4, 1, D)
    cos_t, sin_t = _rope_tables()
    w_main, w_small = _win_layout(w_in)
    w1 = moe_w1.reshape(DEPTH * N_EXPERTS, D, D_EXPERT)
    w3 = moe_w3.reshape(DEPTH * N_EXPERTS, D, D_EXPERT)
    w2 = moe_w2.reshape(DEPTH * N_EXPERTS, D_EXPERT, D)
    h = _prenorm(x, ctx, normv, modv, 0)
    xs = (x, ctx)
    for l in range(DEPTH):
        p = _layer_params(l, w_main, w_small, ml_f_bias, lb_all, hg_norm_g, mla_g_q, mla_g_kv, mla_w_uq,
                          mla_w_ukv, ssd_conv_w, ssd_conv_b, ssd_a_log, ssd_dt_bias, ssd_d, ssd_norm_g, w_gate,
                          w_br, w_out, moe_w_grp, moe_w_exp)
        acc = _mixer(h, p, cos_t, sin_t, nb)
        xcat, hp, eid, wts = _outproj(acc, p["w_out"], xs, normv, modv, p["wr3"], l)
        n_row_tiles = NRT if l + 1 < DEPTH else SEQ // RT
        n_tiles = (2 * nb * n_row_tiles * RT + N_EXPERTS * (TE - 1)) // TE + 1
        pos128, counts = _plan(eid, n_row_tiles)
        pos, row_token, tile_expert, n_used = _dispatch_tiles(pos128, counts, n_tiles, n_row_tiles, nb)
        ys = _experts(tile_expert, n_used, row_token, hp.reshape(t, D // 2), w1, w3, w2, n_tiles, l)
        outs = _combine(pos, ys, wts, xcat, normv, modv, l)
        if l + 1 < DEPTH:
            xcat, h = outs
            xs = (xcat,)
    return outs[0]
```

```python
import functools

import jax
import jax.numpy as jnp
from jax import lax
from jax.experimental import pallas as pl
from jax.experimental.pallas import tpu as pltpu

F32 = jnp.float32
BF16 = jnp.bfloat16

D = 2048
SEQ = 2048
CTX = 256
LT = SEQ + CTX
DEPTH = 2
EPS = 1e-6
GRID_W = 64
NH = 4
DH = 128
BW = 512
MLA_Q_RANK = 512
MLA_KV_RANK = 256
MLA_NOPE = 128
MLA_ROPE = 64
MLA_SCALE = (MLA_NOPE + MLA_ROPE) ** -0.5
ROPE_BASE = 10000.0
SSD_HEADS = 8
SSD_P = 64
SSD_GROUPS = 2
SSD_N = 128
SSD_XBC = 1024
N_GROUPS = 4
EXP_PER_GROUP = 8
N_EXPERTS = 32
D_EXPERT = 512

CS = 128
NLC = SEQ // CS
NCC = CTX // CS
NCH = NLC + NCC
RT = 256
NRT = LT // RT
TE = 256
VMEM_LIMIT = 52 * 1024 * 1024

U_W = 7168
CB_ML_Q, CB_ML_K, CB_ML_V, CB_ML_O = 0, 1, 2, 3
CB_HG_Q, CB_HG_I, CB_HG_G, CB_HG_F = 4, 5, 6, 7
CB_SSD_Z = 9
CB_SSD_XBC = 5
CB_MLA_CQ = 12
CB_MLA_CKV = 26
CB_MLA_KR = 54


def _cparams(sem):
    return pltpu.CompilerParams(dimension_semantics=sem, vmem_limit_bytes=VMEM_LIMIT)


def _dot(a, b):
    return jnp.dot(a, b, preferred_element_type=F32)


def _dot_nt(a, b):
    return lax.dot_general(a, b, (((1,), (1,)), ((), ())), preferred_element_type=F32)


def _sigmoid(x):
    return 1.0 / (1.0 + jnp.exp(-x))


def _silu(x):
    return x * _sigmoid(x)


def _softplus(x):
    return jnp.maximum(x, 0.0) + jnp.log(1.0 + jnp.exp(-jnp.abs(x)))


def _log_sigmoid(x):
    return -_softplus(-x)


def _rms(x):
    return x * lax.rsqrt(jnp.mean(x * x, axis=-1, keepdims=True) + EPS)


def _split3(x):
    a = x.astype(BF16)
    r = x - a.astype(F32)
    b = r.astype(BF16)
    c = (r - b.astype(F32)).astype(BF16)
    return a, b, c


def _cumsum_rows(mb, x):
    a, b, c = _split3(x)
    return _dot(mb, a) + _dot(mb, b) + _dot(mb, c)


def _cumsum_cols(x, mb):
    a, b, c = _split3(x)
    return _dot_nt(a, mb) + _dot_nt(b, mb) + _dot_nt(c, mb)


def _chunk_mask(rev):
    row = lax.broadcasted_iota(jnp.int32, (CS, CS), 0)
    col = lax.broadcasted_iota(jnp.int32, (CS, CS), 1)
    return (col >= row) if rev else (col <= row)


def _chunk_of_step(rev):
    if rev:
        return lambda s: NCH - 1 - s
    return lambda s: (s + NLC) % NCH


def _mm_kernel(a_ref, w_ref, b_ref, o_ref):
    acc = _dot(a_ref[...], w_ref[...].astype(BF16))
    o_ref[...] = (acc + b_ref[...]).astype(o_ref.dtype)


def _mm(a, w, bias, out_dtype, tm, tn):
    m, k = a.shape
    n = w.shape[1]
    return pl.pallas_call(
        _mm_kernel,
        grid=(m // tm, n // tn),
        in_specs=[pl.BlockSpec((tm, k), lambda i, j: (i, 0)),
                  pl.BlockSpec((k, tn), lambda i, j: (0, j)),
                  pl.BlockSpec((1, tn), lambda i, j: (0, j))],
        out_specs=pl.BlockSpec((tm, tn), lambda i, j: (i, j)),
        out_shape=jax.ShapeDtypeStruct((m, n), out_dtype),
        compiler_params=_cparams(("parallel", "arbitrary")),
        name="mm",
    )(a, w, bias)


W_IN_PIECES = ((0, 0, 2048),
               (2048, 2064, 2560),
               (4608, 5456, 512),
               (5120, 5968, 1024),
               (6144, 4624, 512),
               (6656, 5136, 256),
               (6912, 5392, 64),
               (7040, 2048, 16),
               (7056, 6992, 16))
D_IN = 7008


def _win_kernel(w_ref, o_ref, s_ref):
    o_ref[...] = jnp.zeros_like(o_ref)
    for dst, src, width in W_IN_PIECES:
        o_ref[:, dst:dst + width] = w_ref[:, src:src + width].astype(BF16)
    s_ref[...] = o_ref[:, U_W - 128:]


def _win_layout(w_in):
    rows = 256
    return pl.pallas_call(
        _win_kernel,
        grid=(DEPTH, D // rows),
        in_specs=[pl.BlockSpec((None, rows, D_IN), lambda l, i: (l, i, 0))],
        out_specs=[pl.BlockSpec((None, rows, U_W), lambda l, i: (l, i, 0)),
                   pl.BlockSpec((None, rows, 128), lambda l, i: (l, i, 0))],
        out_shape=[jax.ShapeDtypeStruct((DEPTH, D, U_W), BF16), jax.ShapeDtypeStruct((DEPTH, D, 128), BF16)],
        compiler_params=_cparams(("parallel", "parallel")),
        name="win_layout",
    )(w_in)


def _adaln_mods(s_vec, w_ada, b_ada):
    tn = 1024
    return pl.pallas_call(
        _mm_kernel,
        grid=(DEPTH, 6 * D // tn),
        in_specs=[pl.BlockSpec((16, D), lambda l, j: (0, 0)),
                  pl.BlockSpec((None, D, tn), lambda l, j: (l, 0, j)),
                  pl.BlockSpec((None, 1, tn), lambda l, j: (l, 0, j))],
        out_specs=pl.BlockSpec((None, 16, tn), lambda l, j: (l, 0, j)),
        out_shape=jax.ShapeDtypeStruct((DEPTH, 16, 6 * D), F32),
        compiler_params=_cparams(("parallel", "arbitrary")),
        name="adaln_mods",
    )(s_vec, w_ada, b_ada.reshape(DEPTH, 1, 6 * D))


def _mod_spec(layer, k):
    def imap(b, i):
        row = jnp.where(i == NRT - 1, 8, b)
        return ((layer * 16 + row) * 6 + k, 0, 0)
    return pl.BlockSpec((None, 1, D), imap)


def _norm_spec(layer, k):
    return pl.BlockSpec((None, 1, D), lambda b, i: (layer * 4 + k, 0, 0))


def _row_spec(width, dtype_rows=RT):
    return pl.BlockSpec((None, dtype_rows, width), lambda b, i: (b, i, 0))


def _split_x_specs():
    return [pl.BlockSpec((None, RT, D), lambda b, i: (b, jnp.minimum(i, NRT - 2), 0)),
            pl.BlockSpec((None, RT, D), lambda b, i: (b, 0, 0))]


def _pick_x(x_ref, c_ref):
    return jnp.where(pl.program_id(1) == NRT - 1, c_ref[...], x_ref[...])


def _prenorm_kernel(x_ref, c_ref, g_ref, sh_ref, sc_ref, h_ref):
    y = _rms(_pick_x(x_ref, c_ref)) * g_ref[...]
    h_ref[...] = (y * (1.0 + sc_ref[...]) + sh_ref[...]).astype(BF16)


def _prenorm(x, ctx, normv, modv, layer):
    nb = x.shape[0]
    return pl.pallas_call(
        _prenorm_kernel,
        grid=(nb, NRT),
        in_specs=_split_x_specs() + [_norm_spec(layer, 0), _mod_spec(layer, 0), _mod_spec(layer, 1)],
        out_specs=_row_spec(D),
        out_shape=jax.ShapeDtypeStruct((nb, LT, D), BF16),
        compiler_params=_cparams(("parallel", "parallel")),
        name="prenorm",
    )(x, ctx, normv, modv, modv)


def _conv_kernel(x_ref, p_ref, n_ref, w_ref, b_ref, o_ref):
    i = pl.program_id(1)
    x = x_ref[...].astype(F32)
    has_prev = jnp.logical_and(i != 0, i != NRT - 1).astype(F32)
    has_next = jnp.logical_and(i != NRT - 2, i != NRT - 1).astype(F32)
    prev_row = p_ref[15:16, :].astype(F32) * has_prev
    next_row = n_ref[0:1, :].astype(F32) * has_next
    ridx = lax.broadcasted_iota(jnp.int32, (RT, 1), 0)
    xm1 = jnp.where(ridx == 0, prev_row, pltpu.roll(x, 1, 0))
    xp1 = jnp.where(ridx == RT - 1, next_row, pltpu.roll(x, RT - 1, 0))
    w = w_ref[...]
    y = w[0:1, :] * xm1 + w[1:2, :] * x + w[2:3, :] * xp1 + b_ref[...]
    o_ref[...] = _silu(y)


def _ssd_conv(u, conv_w, conv_b):
    nb = u.shape[0]
    hb = RT // 16
    last = LT // 16 - 1
    return pl.pallas_call(
        _conv_kernel,
        grid=(nb, NRT),
        in_specs=[pl.BlockSpec((None, RT, SSD_XBC), lambda b, i: (b, i, CB_SSD_XBC)),
                  pl.BlockSpec((None, 16, SSD_XBC), lambda b, i: (b, jnp.maximum(i * hb - 1, 0), CB_SSD_XBC)),
                  pl.BlockSpec((None, 16, SSD_XBC), lambda b, i: (b, jnp.minimum((i + 1) * hb, last), CB_SSD_XBC)),
                  pl.BlockSpec((8, SSD_XBC), lambda b, i: (0, 0)),
                  pl.BlockSpec((1, SSD_XBC), lambda b, i: (0, 0))],
        out_specs=pl.BlockSpec((None, RT, SSD_XBC), lambda b, i: (b, i, 0)),
        out_shape=jax.ShapeDtypeStruct((nb, LT, SSD_XBC), F32),
        compiler_params=_cparams(("parallel", "parallel")),
        name="ssd_conv",
    )(u, u, u, conv_w, conv_b)


SCAN_BB = 2


def _mlstm_kernel(rev, bpb, *refs):
    if rev:
        (qs_ref, ks_ref, vs_ref, gc_ref, gt_ref, bc_ref, bt_ref, yps_ref, ogs_ref,
         os_ref, c_sc, n_sc, m_sc) = refs
    else:
        (qs_ref, ks_ref, vs_ref, gc_ref, gt_ref, bc_ref, bt_ref,
         os_ref, c_sc, n_sc, m_sc) = refs
    d = 1 if rev else 0

    @pl.when(pl.program_id(1) == 0)
    def _():
        c_sc[...] = jnp.zeros_like(c_sc)
        n_sc[...] = jnp.zeros_like(n_sc)
        m_sc[...] = jnp.full(m_sc.shape, -jnp.inf, F32)

    mask = _chunk_mask(rev)
    mb = jnp.where(mask, 1.0, 0.0).astype(BF16)
    scale = DH ** -0.5
    for bb, h in [(bb, h) for bb in range(bpb) for h in range(NH)]:
        if h == 0:
            gc = gc_ref[bb] + bc_ref[...]
            gt = gt_ref[bb] + bt_ref[...]
            flog_c = _log_sigmoid(gc)
            flog_t = _log_sigmoid(gt)
            fcum_c = _cumsum_rows(mb, flog_c)
            fcum_t = _cumsum_cols(flog_t, mb)
        q_ref, k_ref, v_ref, o_ref = qs_ref.at[bb], ks_ref.at[bb], vs_ref.at[bb], os_ref.at[bb]
        if rev:
            yp_ref, og_ref = yps_ref.at[bb], ogs_ref.at[bb]
        c_h, n_h, m_h = c_sc.at[bb * NH + h], n_sc.at[bb * NH + h], m_sc.at[bb * NH + h]
        ci = d * 8 + h
        cf = d * 8 + 4 + h
        sl = slice(h * DH, (h + 1) * DH)
        q = q_ref[:, sl]
        k = k_ref[:, sl]
        v = v_ref[:, sl]
        i_col = gc[:, ci:ci + 1]
        f_col = flog_c[:, cf:cf + 1]
        fc_col = fcum_c[:, cf:cf + 1]
        i_row = gt[ci:ci + 1, :]
        fc_row = fcum_t[cf:cf + 1, :]
        m_prev = m_h[0:1, 0:1]
        dm = jnp.where(mask, fc_col - fc_row + i_row, -jnp.inf)
        from_state = fc_col + m_prev
        m_t = jnp.maximum(from_state, jnp.max(dm, axis=1, keepdims=True))
        s = _dot_nt(q, k) * scale * jnp.exp(dm - m_t)
        w_state = jnp.exp(from_state - m_t)
        qc = _dot(q, c_h[...].astype(BF16)) * scale
        num = _dot(s.astype(BF16), v) + w_state * qc
        qn = jnp.sum(q.astype(F32) * n_h[0:1, :], axis=1, keepdims=True) * scale
        den = jnp.sum(s, axis=1, keepdims=True) + w_state * qn
        hout = num / jnp.maximum(jnp.abs(den), jnp.exp(-m_t))
        if rev:
            tot = yp_ref[:, sl] + hout
            o_ref[:, sl] = (tot * _sigmoid(og_ref[:, sl].astype(F32))).astype(o_ref.dtype)
        else:
            o_ref[:, sl] = hout
        f_tot = jnp.sum(f_col, axis=0, keepdims=True)
        w_log = f_tot - fc_col + i_col
        m_new = jnp.maximum(f_tot + m_prev, jnp.max(w_log, axis=0, keepdims=True))
        decay = jnp.exp(f_tot + m_prev - m_new)
        kw = k.astype(F32) * jnp.exp(w_log - m_new)
        c_h[...] = decay * c_h[...] + _dot(kw.T.astype(BF16), v)
        n_new = decay * n_h[0:1, :] + jnp.sum(kw, axis=0, keepdims=True)
        n_h[...] = jnp.broadcast_to(n_new, (8, DH))
        m_h[...] = jnp.broadcast_to(m_new, (8, DH))


def _u_spec(width, cblk, cmap):
    return pl.BlockSpec((None, CS, width), lambda b, s: (b, cmap(s), cblk))


def _mlstm(u, usm, ust, bias_c, bias_t, rev, yprev=None):
    nb = u.shape[0]
    bpb = SCAN_BB if nb % SCAN_BB == 0 else 1
    cmap = _chunk_of_step(rev)
    uspec = lambda cblk: pl.BlockSpec((bpb, CS, BW), lambda b, s: (b, cmap(s), cblk))
    in_specs = [uspec(CB_ML_Q), uspec(CB_ML_K), uspec(CB_ML_V),
                pl.BlockSpec((bpb, CS, 128), lambda b, s: (b, cmap(s), 0)),
                pl.BlockSpec((bpb, None, 32, CS), lambda b, s: (b, cmap(s), 0, 0)),
                pl.BlockSpec((1, 128), lambda b, s: (0, 0)),
                pl.BlockSpec((32, 1), lambda b, s: (0, 0))]
    args = [u, u, u, usm, ust, bias_c, bias_t]
    if rev:
        in_specs += [uspec(0), uspec(CB_ML_O)]
        args += [yprev, u]
    return pl.pallas_call(
        functools.partial(_mlstm_kernel, rev, bpb),
        grid=(nb // bpb, NCH),
        in_specs=in_specs,
        out_specs=uspec(0),
        out_shape=jax.ShapeDtypeStruct((nb, LT, BW), BF16 if rev else F32),
        scratch_shapes=[pltpu.VMEM((bpb * NH, DH, DH), F32), pltpu.VMEM((bpb * NH, 8, DH), F32),
                        pltpu.VMEM((bpb * NH, 8, DH), F32)],
        compiler_params=_cparams(("parallel", "arbitrary")),
        name="mlstm_rev" if rev else "mlstm_fwd",
    )(*args)


def _block_ref(g, n, off):
    g3 = g.reshape(CS // n, n, DH)
    r = g3[:, off:off + 1, :]
    return jnp.broadcast_to(r, (CS // n, n, DH)).reshape(CS, DH)


def _hgrn2_kernel(rev, *refs):
    if rev:
        q_ref, i_ref, f_ref, lb_ref, yp_ref, g_ref, ng_ref, o_ref, st_sc = refs
    else:
        q_ref, i_ref, f_ref, lb_ref, o_ref, st_sc = refs

    @pl.when(pl.program_id(1) == 0)
    def _():
        st_sc[...] = jnp.zeros_like(st_sc)

    mask = _chunk_mask(rev)
    mb = jnp.where(mask, 1.0, 0.0).astype(BF16)
    row = lax.broadcasted_iota(jnp.int32, (CS, CS), 0)
    col = lax.broadcasted_iota(jnp.int32, (CS, CS), 1)
    tix = lax.broadcasted_iota(jnp.int32, (CS, 1), 0)
    for h in range(NH):
        sl = slice(h * DH, (h + 1) * DH)
        q = _silu(q_ref[:, sl].astype(F32))
        v = i_ref[:, sl].astype(F32)
        lb = lb_ref[:, sl]
        f = lb + (1.0 - lb) * _sigmoid(f_ref[:, sl].astype(F32))
        k = 1.0 - f
        lf = jnp.log(f)
        g = _cumsum_rows(mb, lf)
        st = st_sc[h]
        o = _dot_nt((q * jnp.exp(g)).astype(BF16), st.astype(BF16))
        a = jnp.zeros((CS, CS), F32)
        n = CS
        while n >= 16:
            half = n // 2
            second = jnp.bitwise_and(tix, n - 1) >= half
            rowside = jnp.logical_not(second) if rev else second
            r = _block_ref(g, n, half if rev else half - 1)
            e = jnp.exp(jnp.where(rowside, g - r, r - g))
            qs = jnp.where(rowside, q * e, 0.0).astype(BF16)
            ks = jnp.where(rowside, 0.0, k * e).astype(BF16)
            sh_n = n.bit_length() - 1
            same = jnp.right_shift(row, sh_n) == jnp.right_shift(col, sh_n)
            a = a + jnp.where(same, _dot_nt(qs, ks), 0.0)
            n = half
        o = o + _dot(a.astype(BF16), v.astype(BF16))
        t8 = jnp.bitwise_and(tix, 7)
        k3, g3, v3 = (a3.reshape(CS // 8, 8, DH) for a3 in (k, g, v))
        for dl in range(8):
            if dl == 0:
                ksh, gsh, vsh = k, g, v
            else:
                sh = (8 - dl) if rev else dl
                ksh = pltpu.roll(k3, sh, 1).reshape(CS, DH)
                gsh = pltpu.roll(g3, sh, 1).reshape(CS, DH)
                vsh = pltpu.roll(v3, sh, 1).reshape(CS, DH)
            valid = (t8 + dl <= 7) if rev else (t8 >= dl)
            e = jnp.exp(jnp.minimum(g - gsh, 0.0))
            coef = jnp.sum(q * ksh * e, axis=1, keepdims=True)
            o = o + jnp.where(valid, coef, 0.0) * vsh
        if rev:
            tot = yp_ref[:, sl] + o
            y = _rms(tot) * ng_ref[:, sl]
            o_ref[:, sl] = (y * _sigmoid(g_ref[:, sl].astype(F32))).astype(o_ref.dtype)
        else:
            o_ref[:, sl] = o
        g_tot = jnp.sum(lf, axis=0, keepdims=True)
        kdec = k * jnp.exp(g_tot - g)
        st_sc[h] = st * jnp.exp(g_tot) + _dot(v.T.astype(BF16), kdec.astype(BF16))


def _hgrn2(u, lb, rev, yprev=None, norm_g=None):
    nb = u.shape[0]
    cmap = _chunk_of_step(rev)
    d = 1 if rev else 0
    in_specs = [_u_spec(BW, CB_HG_Q, cmap), _u_spec(BW, CB_HG_I, cmap), _u_spec(BW, CB_HG_F + d, cmap),
                pl.BlockSpec((1, BW), lambda b, s: (0, 0))]
    args = [u, u, u, lb]
    if rev:
        in_specs += [pl.BlockSpec((None, CS, BW), lambda b, s: (b, cmap(s), 0)),
                     _u_spec(BW, CB_HG_G, cmap),
                     pl.BlockSpec((1, BW), lambda b, s: (0, 0))]
        args += [yprev, u, norm_g]
    return pl.pallas_call(
        functools.partial(_hgrn2_kernel, rev),
        grid=(nb, NCH),
        in_specs=in_specs,
        out_specs=pl.BlockSpec((None, CS, BW), lambda b, s: (b, cmap(s), 0)),
        out_shape=jax.ShapeDtypeStruct((nb, LT, BW), BF16 if rev else F32),
        scratch_shapes=[pltpu.VMEM((NH, DH, DH), F32)],
        compiler_params=_cparams(("parallel", "arbitrary")),
        name="hgrn2_rev" if rev else "hgrn2_fwd",
    )(*args)


def _expand_heads(cols, lane_head):
    r = cols[3]
    for hh in (2, 1, 0):
        r = jnp.where(lane_head == hh, cols[hh], r)
    return r


def _ssd_kernel(rev, *refs):
    if rev:
        (x_ref, gc_ref, gt_ref, bc_ref, bt_ref, ac_ref, at_ref, yp_ref, z_ref, dk_ref, ng_ref,
         o_ref, ht_sc) = refs
    else:
        x_ref, gc_ref, gt_ref, bc_ref, bt_ref, ac_ref, at_ref, o_ref, ht_sc = refs
    d = 1 if rev else 0

    @pl.when(pl.program_id(1) == 0)
    def _():
        ht_sc[...] = jnp.zeros_like(ht_sc)

    mask = _chunk_mask(rev)
    mb = jnp.where(mask, 1.0, 0.0).astype(BF16)
    dt_c = _softplus(gc_ref[...] + bc_ref[...])
    da_c = dt_c * ac_ref[...]
    acum_c = _cumsum_rows(mb, da_c)
    dt_t = _softplus(gt_ref[...] + bt_ref[...])
    da_t = dt_t * at_ref[...]
    acum_t = _cumsum_cols(da_t, mb)
    gw = 4 * SSD_P
    lane_head = jnp.right_shift(lax.broadcasted_iota(jnp.int32, (1, gw), 1), 6)
    ys = []
    for grp in range(SSD_GROUPS):
        xg = x_ref[:, grp * gw:(grp + 1) * gw]
        bm = x_ref[:, BW + grp * SSD_N:BW + (grp + 1) * SSD_N]
        cm = x_ref[:, BW + SSD_GROUPS * SSD_N + grp * SSD_N:BW + SSD_GROUPS * SSD_N + (grp + 1) * SSD_N]
        cmb = cm.astype(BF16)
        cb = _dot_nt(cmb, bm.astype(BF16))
        ht = ht_sc[grp]
        y_state = _dot(cmb, ht.astype(BF16))
        lanes = [16 + d * 8 + grp * 4 + hh for hh in range(4)]
        dt_cols = [dt_c[:, l:l + 1] for l in lanes]
        a_cols = [acum_c[:, l:l + 1] for l in lanes]
        a_tots = [jnp.sum(da_c[:, l:l + 1], axis=0, keepdims=True) for l in lanes]
        xdt = xg * _expand_heads(dt_cols, lane_head)
        xdtb = xdt.astype(BF16)
        y = jnp.exp(_expand_heads(a_cols, lane_head)) * y_state
        for hh in range(4):
            l = lanes[hh]
            dec = jnp.exp(jnp.where(mask, a_cols[hh] - acum_t[l:l + 1, :], -jnp.inf))
            yh = _dot((cb * dec).astype(BF16), xdtb)
            y = y + jnp.where(lane_head == hh, yh, 0.0)
        ys.append(y)
        w_exp = jnp.exp(_expand_heads([a_tots[hh] - a_cols[hh] for hh in range(4)], lane_head))
        e_row = jnp.exp(_expand_heads(a_tots, lane_head))
        ht_sc[grp] = ht * e_row + _dot(bm.T.astype(BF16), (xdt * w_exp).astype(BF16))
    y = jnp.concatenate(ys, axis=1)
    if rev:
        tot = yp_ref[...] + y + dk_ref[...] * x_ref[:, :BW]
        tot = tot * _silu(z_ref[...].astype(F32))
        o_ref[...] = (_rms(tot) * ng_ref[...]).astype(o_ref.dtype)
    else:
        o_ref[...] = y


def _ssd(xbc, usm, ust, u, bias_c, bias_t, aneg_c, aneg_t, rev, yprev=None, dskip=None, norm_g=None):
    nb = xbc.shape[0]
    cmap = _chunk_of_step(rev)
    in_specs = [pl.BlockSpec((None, CS, SSD_XBC), lambda b, s: (b, cmap(s), 0)),
                pl.BlockSpec((None, CS, 128), lambda b, s: (b, cmap(s), 0)),
                pl.BlockSpec((None, None, 32, CS), lambda b, s: (b, cmap(s), 0, 0)),
                pl.BlockSpec((1, 128), lambda b, s: (0, 0)),
                pl.BlockSpec((32, 1), lambda b, s: (0, 0)),
                pl.BlockSpec((1, 128), lambda b, s: (0, 0)),
                pl.BlockSpec((32, 1), lambda b, s: (0, 0))]
    args = [xbc, usm, ust, bias_c, bias_t, aneg_c, aneg_t]
    if rev:
        in_specs += [pl.BlockSpec((None, CS, BW), lambda b, s: (b, cmap(s), 0)),
                     _u_spec(BW, CB_SSD_Z, cmap),
                     pl.BlockSpec((1, BW), lambda b, s: (0, 0)),
                     pl.BlockSpec((1, BW), lambda b, s: (0, 0))]
        args += [yprev, u, dskip, norm_g]
    return pl.pallas_call(
        functools.partial(_ssd_kernel, rev),
        grid=(nb, NCH),
        in_specs=in_specs,
        out_specs=pl.BlockSpec((None, CS, BW), lambda b, s: (b, cmap(s), 0)),
        out_shape=jax.ShapeDtypeStruct((nb, LT, BW), BF16 if rev else F32),
        scratch_shapes=[pltpu.VMEM((SSD_GROUPS, SSD_N, 4 * SSD_P), F32)],
        compiler_params=_cparams(("parallel", "arbitrary")),
        name="ssd_rev" if rev else "ssd_fwd",
    )(*args)


def _mla_proj_kernel(cq_ref, ckv_ref, kr_ref, gq_ref, gkv_ref, wqa_ref, wqb_ref, wk_ref, wv_ref,
                     rk_ref, cos_ref, sin_ref, q_ref, k_ref, v_ref):
    nq = (_rms(cq_ref[...].astype(F32)) * gq_ref[...]).astype(BF16)
    qa = _dot(nq, wqa_ref[...])
    qb = _dot(nq, wqb_ref[...])
    cos_t = cos_ref[...]
    sin_t = sin_ref[...]
    cosf = jnp.concatenate([cos_t] * NH, axis=1)
    sinf = jnp.concatenate([sin_t] * NH, axis=1)
    q_ref[...] = ((qa * cosf + qb * sinf) * MLA_SCALE).astype(BF16)
    nkv = (_rms(ckv_ref[...].astype(F32)) * gkv_ref[...]).astype(BF16)
    kn = _dot(nkv, wk_ref[...])
    v_ref[...] = _dot(nkv, wv_ref[...]).astype(BF16)
    kr = kr_ref[...]
    krr = kr.astype(F32) * cos_t[:, 128:256] + _dot(kr, rk_ref[...]) * sin_t[:, 128:256]
    zero = jnp.zeros_like(krr)
    k_ref[...] = (kn + jnp.concatenate([zero, krr] * NH, axis=1)).astype(BF16)


def _mla_proj(u, gq, gkv, wqa, wqb, wk, wv, rk, cos_t, sin_t):
    nb = u.shape[0]
    const = lambda shape: pl.BlockSpec(shape, lambda b, i: (0, 0))
    return pl.pallas_call(
        _mla_proj_kernel,
        grid=(nb, NRT),
        in_specs=[pl.BlockSpec((None, RT, MLA_Q_RANK), lambda b, i: (b, i, CB_MLA_CQ)),
                  pl.BlockSpec((None, RT, MLA_KV_RANK), lambda b, i: (b, i, CB_MLA_CKV)),
                  pl.BlockSpec((None, RT, 128), lambda b, i: (b, i, CB_MLA_KR)),
                  const((1, MLA_Q_RANK)), const((1, MLA_KV_RANK)),
                  const((MLA_Q_RANK, NH * 256)), const((MLA_Q_RANK, NH * 256)),
                  const((MLA_KV_RANK, NH * 256)), const((MLA_KV_RANK, NH * DH)),
                  const((128, 128)),
                  pl.BlockSpec((RT, 256), lambda b, i: (i, 0)),
                  pl.BlockSpec((RT, 256), lambda b, i: (i, 0))],
        out_specs=[_row_spec(NH * 256), _row_spec(NH * 256), _row_spec(NH * DH)],
        out_shape=[jax.ShapeDtypeStruct((nb, LT, NH * 256), BF16),
                   jax.ShapeDtypeStruct((nb, LT, NH * 256), BF16),
                   jax.ShapeDtypeStruct((nb, LT, NH * DH), BF16)],
        compiler_params=_cparams(("parallel", "parallel")),
        name="mla_proj",
    )(u, u, u, gq, gkv, wqa, wqb, wk, wv, rk, cos_t, sin_t)


AT = 768


def _attn_kernel(q_ref, k_ref, v_ref, o_ref):
    s = _dot_nt(q_ref[...], k_ref[...])

    def finish(sc):
        m = jnp.max(sc, axis=1, keepdims=True)
        p = jnp.exp(sc - m)
        l = jnp.sum(p, axis=1, keepdims=True)
        o_ref[...] = (_dot(p.astype(BF16), v_ref[...]) / l).astype(o_ref.dtype)

    last = pl.program_id(2) == LT // AT - 1

    @pl.when(last)
    def _():
        qrow = lax.broadcasted_iota(jnp.int32, (AT, 1), 0)
        kcol = lax.broadcasted_iota(jnp.int32, (1, LT), 1)
        hide = jnp.logical_and(qrow >= AT - CTX, kcol < SEQ)
        finish(jnp.where(hide, -jnp.inf, s))

    @pl.when(jnp.logical_not(last))
    def _():
        finish(s)


def _attention(q, k, v):
    nb = q.shape[0]
    return pl.pallas_call(
        _attn_kernel,
        grid=(nb, NH, LT // AT),
        in_specs=[pl.BlockSpec((None, AT, 256), lambda b, h, i: (b, i, h)),
                  pl.BlockSpec((None, LT, 256), lambda b, h, i: (b, 0, h)),
                  pl.BlockSpec((None, LT, DH), lambda b, h, i: (b, 0, h))],
        out_specs=pl.BlockSpec((None, AT, DH), lambda b, h, i: (b, i, h)),
        out_shape=jax.ShapeDtypeStruct((nb, LT, NH * DH), BF16),
        compiler_params=_cparams(("parallel", "parallel", "arbitrary")),
        name="mla_attn",
    )(q, k, v)


def _merge_kernel(h_ref, b0_ref, b1_ref, b2_ref, b3_ref, wg_ref, wb_ref, o_ref):
    h = h_ref[...]
    acc = None
    for kk, br in enumerate((b0_ref, b1_ref, b2_ref, b3_ref)):
        gate = _sigmoid(_dot(h, wg_ref[kk]))
        term = gate * _dot(br[...], wb_ref[kk])
        acc = term if acc is None else acc + term
    o_ref[...] = acc.astype(o_ref.dtype)


def _merge(h, branches, wg, wb, rows, tn=256):
    nb = h.shape[0]
    tm = rows // 2
    rmap = lambda i, j: (i // 2, i % 2, 0)
    bspec = pl.BlockSpec((None, tm, BW), rmap)
    return pl.pallas_call(
        _merge_kernel,
        grid=(nb * 2, D // tn),
        in_specs=[pl.BlockSpec((None, tm, D), rmap), bspec, bspec, bspec, bspec,
                  pl.BlockSpec((4, D, tn), lambda i, j: (0, 0, j)),
                  pl.BlockSpec((4, BW, tn), lambda i, j: (0, 0, j))],
        out_specs=pl.BlockSpec((None, tm, tn), lambda i, j: (i // 2, i % 2, j)),
        out_shape=jax.ShapeDtypeStruct((nb, rows, D), BF16),
        compiler_params=_cparams(("parallel", "arbitrary")),
        name="merge",
    )(h, *branches, wg, wb)


def _outproj_kernel(first, a_ref, w_ref, *refs):
    if first:
        x_ref, c_ref = refs[:2]
        x_in = _pick_x(x_ref, c_ref)
        refs = refs[2:]
    else:
        x_in = refs[0][...]
        refs = refs[1:]
    g1_ref, gate_ref, g2_ref, sh_ref, sc_ref, wa_ref, wb_ref, wc_ref, xo_ref, hp_ref, e_ref, wt_ref = refs
    y = _dot(a_ref[...], w_ref[...])
    xn = x_in + gate_ref[...] * (_rms(y) * g1_ref[...])
    xo_ref[...] = xn
    h2 = _rms(xn) * g2_ref[...]
    h2 = h2 * (1.0 + sc_ref[...]) + sh_ref[...]
    hp_ref[...] = h2
    _route_math(h2, wa_ref, wb_ref, wc_ref, e_ref, wt_ref)


def _outproj(acc, w_out, xs, normv, modv, wr3, layer):
    nb, rows, _ = acc.shape
    first = len(xs) == 2
    wspec = pl.BlockSpec((D, 128), lambda b, i: (0, 0))
    return pl.pallas_call(
        functools.partial(_outproj_kernel, first),
        grid=(nb, rows // RT),
        in_specs=[_row_spec(D), pl.BlockSpec((D, D), lambda b, i: (0, 0))]
                 + (_split_x_specs() if first else [_row_spec(D)])
                 + [_norm_spec(layer, 1), _mod_spec(layer, 2), _norm_spec(layer, 2),
                    _mod_spec(layer, 3), _mod_spec(layer, 4), wspec, wspec, wspec],
        out_specs=[_row_spec(D), _row_spec(D), _row_spec(128), _row_spec(128)],
        out_shape=[jax.ShapeDtypeStruct((nb, rows, D), F32), jax.ShapeDtypeStruct((nb, rows, D), F32),
                   jax.ShapeDtypeStruct((nb, rows, 128), jnp.int32), jax.ShapeDtypeStruct((nb, rows, 128), F32)],
        compiler_params=_cparams(("parallel", "parallel")),
        name="outproj",
    )(acc, w_out, *xs, normv, modv, normv, modv, modv, *wr3)


def _route_math(hf, wa_ref, wb_ref, wc_ref, e_ref, w_ref):
    h = hf.astype(BF16)
    hl = (hf - h.astype(F32)).astype(BF16)
    logits = (_dot(h, wa_ref[...]) + _dot(h, wb_ref[...]) + _dot(h, wc_ref[...])
              + _dot(hl, wa_ref[...]) + _dot(hl, wb_ref[...]))
    lane = lax.broadcasted_iota(jnp.int32, (1, 128), 1).astype(F32)
    ninf = -jnp.inf
    is_grp = lane < N_GROUPS
    gl = jnp.where(is_grp, logits, ninf)
    gmax = jnp.max(gl, axis=1, keepdims=True)
    gidx = jnp.min(jnp.where(gl == gmax, lane, 999.0), axis=1, keepdims=True)
    gsum = jnp.sum(jnp.where(is_grp, jnp.exp(gl - gmax), 0.0), axis=1, keepdims=True)
    gp = 1.0 / gsum
    lo = N_GROUPS + EXP_PER_GROUP * gidx
    sel = jnp.logical_and(lane >= lo, lane < lo + EXP_PER_GROUP)
    el = jnp.where(sel, logits, ninf)
    v1 = jnp.max(el, axis=1, keepdims=True)
    i1 = jnp.min(jnp.where(el == v1, lane, 999.0), axis=1, keepdims=True)
    el2 = jnp.where(lane == i1, ninf, el)
    v2 = jnp.max(el2, axis=1, keepdims=True)
    i2 = jnp.min(jnp.where(el2 == v2, lane, 999.0), axis=1, keepdims=True)
    t = jnp.exp(v2 - v1)
    w1 = gp / (1.0 + t)
    w2 = gp * t / (1.0 + t)
    e_ref[...] = jnp.where(lane == 0.0, i1 - N_GROUPS, jnp.where(lane == 1.0, i2 - N_GROUPS, 0.0)).astype(jnp.int32)
    w_ref[...] = jnp.where(lane == 0.0, w1, jnp.where(lane == 1.0, w2, 0.0))


GATHER_UNROLL = 8


def _expert_kernel(te_ref, nu_ref, rt_ref, h_hbm, w1_ref, w3_ref, w2_ref, o_ref, xbuf, hm_sc, sem):
    i = pl.program_id(0)
    n_used = nu_ref[0]
    slot = i % 2

    def issue_tile(j, s):
        base = j * TE

        def body(g, c):
            for uu in range(GATHER_UNROLL):
                r = g * GATHER_UNROLL + uu
                pltpu.make_async_copy(h_hbm.at[pl.ds(rt_ref[base + r], 1), :],
                                      xbuf.at[s, pl.ds(r, 1), :], sem.at[s]).start(priority=1)
            return c
        lax.fori_loop(0, TE // GATHER_UNROLL, body, 0)

    def wait_tile(s):
        pltpu.make_async_copy(h_hbm.at[pl.ds(0, TE), :], xbuf.at[s], sem.at[s]).wait()

    @pl.when(jnp.logical_and(i == 0, n_used > 0))
    def _():
        issue_tile(0, 0)

    @pl.when(i < n_used)
    def _():
        wait_tile(slot)
        nbase = jnp.minimum(i + 1, n_used - 1) * TE
        oslot = 1 - slot

        def issue_rows(r0, r1):
            for r in range(r0, r1):
                pltpu.make_async_copy(h_hbm.at[pl.ds(rt_ref[nbase + r], 1), :],
                                      xbuf.at[oslot, pl.ds(r, 1), :], sem.at[oslot]).start(priority=1)

        x = xbuf[slot].astype(BF16)
        cw = 256
        n1, n2 = D_EXPERT // cw, D // cw
        per = TE // (n1 + n2) + 1
        done = 0
        for j in range(n1):
            issue_rows(done, min(done + per, TE))
            done = min(done + per, TE)
            cs = slice(j * cw, (j + 1) * cw)
            a = _dot(x, w1_ref[:, cs].astype(BF16))
            b = _dot(x, w3_ref[:, cs].astype(BF16))
            hm_sc[:, cs] = (_silu(a) * b).astype(BF16)
        hm = hm_sc[...]
        for j in range(n2):
            issue_rows(done, min(done + per, TE))
            done = min(done + per, TE)
            cs = slice(j * cw, (j + 1) * cw)
            o_ref[:, cs] = _dot(hm, w2_ref[:, cs].astype(BF16))

        @pl.when(i + 1 >= n_used)
        def _():
            wait_tile(oslot)

    @pl.when(i >= n_used)
    def _():
        o_ref[...] = jnp.zeros_like(o_ref)


def _experts(tile_expert, n_used, row_token, hp, w1, w3, w2, n_tiles, layer):
    wmap = lambda i, te, nu, rt: (layer * N_EXPERTS + te[i], 0, 0)
    grid_spec = pltpu.PrefetchScalarGridSpec(
        num_scalar_prefetch=3,
        grid=(n_tiles,),
        in_specs=[pl.BlockSpec(memory_space=pl.ANY),
                  pl.BlockSpec((None, D, D_EXPERT), wmap),
                  pl.BlockSpec((None, D, D_EXPERT), wmap),
                  pl.BlockSpec((None, D_EXPERT, D), wmap)],
        out_specs=pl.BlockSpec((TE, D), lambda i, te, nu, rt: (i, 0)),
        scratch_shapes=[pltpu.VMEM((2, TE, D), F32), pltpu.VMEM((TE, D_EXPERT), BF16),
                        pltpu.SemaphoreType.DMA((2,))],
    )
    return pl.pallas_call(
        _expert_kernel,
        grid_spec=grid_spec,
        out_shape=jax.ShapeDtypeStruct((n_tiles * TE, D), F32),
        compiler_params=_cparams(("arbitrary",)),
        name="moe_experts",
    )(tile_expert, n_used, row_token, hp, w1, w3, w2)


def _combine_kernel(has_next, pos_ref, ys_hbm, w_ref, x_ref, g3_ref, gate_ref, *rest):
    if has_next:
        g0_ref, sh_ref, sc_ref, xo_ref, ho_ref, buf, sem = rest
    else:
        xo_ref, buf, sem = rest
    step = pl.program_id(0) * pl.num_programs(1) + pl.program_id(1)
    n_steps = pl.num_programs(0) * pl.num_programs(1)
    slot = step % 2

    def issue_tile(j, s):
        t0 = j * RT

        def body(g, c):
            for uu in range(GATHER_UNROLL):
                r = g * GATHER_UNROLL + uu
                for kk in range(2):
                    pltpu.make_async_copy(ys_hbm.at[pl.ds(pos_ref[(t0 + r) * 2 + kk], 1), :],
                                          buf.at[s, kk, pl.ds(r, 1), :], sem.at[s]).start(priority=kk)
            return c
        lax.fori_loop(0, RT // GATHER_UNROLL, body, 0)

    def wait_tile(s):
        for kk in range(2):
            pltpu.make_async_copy(ys_hbm.at[pl.ds(0, RT), :], buf.at[s, kk], sem.at[s]).wait()

    @pl.when(step == 0)
    def _():
        issue_tile(0, 0)

    wait_tile(slot)
    nt0 = jnp.minimum(step + 1, n_steps - 1) * RT
    oslot = 1 - slot
    rows = 32
    for c in range(RT // rows):
        for r in range(c * rows, (c + 1) * rows):
            for kk in range(2):
                pltpu.make_async_copy(ys_hbm.at[pl.ds(pos_ref[(nt0 + r) * 2 + kk], 1), :],
                                      buf.at[oslot, kk, pl.ds(r, 1), :], sem.at[oslot]).start(priority=kk)
        rs = slice(c * rows, (c + 1) * rows)
        w = w_ref[rs, :]
        y = w[:, 0:1] * buf[slot, 0, rs, :] + w[:, 1:2] * buf[slot, 1, rs, :]
        xn = x_ref[rs, :] + gate_ref[...] * (_rms(y) * g3_ref[...])
        xo_ref[rs, :] = xn
        if has_next:
            hn = _rms(xn) * g0_ref[...]
            ho_ref[rs, :] = (hn * (1.0 + sc_ref[...]) + sh_ref[...]).astype(BF16)

    @pl.when(step + 1 >= n_steps)
    def _():
        wait_tile(oslot)


def _combine(pos, ys, wts, xcat, normv, modv, layer):
    nb = xcat.shape[0]
    has_next = layer + 1 < DEPTH

    def mod_spec(lyr, k):
        def imap(b, i, pos_ref):
            row = jnp.where(i == NRT - 1, 8, b)
            return ((lyr * 16 + row) * 6 + k, 0, 0)
        return pl.BlockSpec((None, 1, D), imap)

    def norm_spec(lyr, k):
        return pl.BlockSpec((None, 1, D), lambda b, i, pos_ref: (lyr * 4 + k, 0, 0))

    def row_spec(width):
        return pl.BlockSpec((None, RT, width), lambda b, i, pos_ref: (b, i, 0))

    in_specs = [pl.BlockSpec(memory_space=pl.ANY), row_spec(128), row_spec(D),
                norm_spec(layer, 3), mod_spec(layer, 5)]
    args = [ys, wts, xcat, normv, modv]
    out_specs = [row_spec(D)]
    n_row_tiles = NRT if has_next else SEQ // RT
    out_shape = [jax.ShapeDtypeStruct((nb, n_row_tiles * RT, D), F32)]
    if has_next:
        in_specs += [norm_spec(layer + 1, 0), mod_spec(layer + 1, 0), mod_spec(layer + 1, 1)]
        args += [normv, modv, modv]
        out_specs.append(row_spec(D))
        out_shape.append(jax.ShapeDtypeStruct((nb, LT, D), BF16))
    grid_spec = pltpu.PrefetchScalarGridSpec(
        num_scalar_prefetch=1,
        grid=(nb, n_row_tiles),
        in_specs=in_specs,
        out_specs=out_specs,
        scratch_shapes=[pltpu.VMEM((2, 2, RT, D), F32), pltpu.SemaphoreType.DMA((2,))],
    )
    return pl.pallas_call(
        functools.partial(_combine_kernel, has_next),
        grid_spec=grid_spec,
        out_shape=out_shape,
        compiler_params=_cparams(("arbitrary", "arbitrary")),
        name="moe_combine",
    )(pos, *args)


def _plan_kernel(e_ref, pos_ref, cnt_ref, rank_sc, carry_sc):
    phase = pl.program_id(0)
    i = pl.program_id(1)
    lane = lax.broadcasted_iota(jnp.int32, (1, 128), 1)
    e = e_ref[...]
    oh1 = lane == e[:, 0:1]
    oh2 = lane == e[:, 1:2]

    @pl.when(jnp.logical_and(phase == 0, i == 0))
    def _():
        carry_sc[...] = jnp.zeros_like(carry_sc)

    @pl.when(phase == 0)
    def _():
        s = jnp.where(jnp.logical_or(oh1, oh2), 1.0, 0.0)
        row = lax.broadcasted_iota(jnp.int32, (RT, RT), 0)
        col = lax.broadcasted_iota(jnp.int32, (RT, RT), 1)
        lower = jnp.where(col < row, 1.0, 0.0).astype(BF16)
        carry = carry_sc[0:1, :]
        before = _dot(lower, s.astype(BF16)) + carry
        r1 = jnp.sum(jnp.where(oh1, before, 0.0), axis=1, keepdims=True)
        r2 = jnp.sum(jnp.where(oh2, before, 0.0), axis=1, keepdims=True)
        rank_sc[i] = jnp.where(lane == 0, r1, jnp.where(lane == 1, r2, 0.0))
        carry_sc[...] = jnp.broadcast_to(carry + jnp.sum(s, axis=0, keepdims=True), carry_sc.shape)
        cnt_ref[...] = carry_sc[...]

    @pl.when(phase == 1)
    def _():
        counts = carry_sc[0:1, :]
        padded = jnp.floor((counts + (TE - 1)) * (1.0 / TE)) * TE
        r128 = lax.broadcasted_iota(jnp.int32, (128, 128), 0)
        c128 = lax.broadcasted_iota(jnp.int32, (128, 128), 1)
        upper = jnp.where(r128 < c128, 1.0, 0.0).astype(BF16)
        pa, pb, pc = _split3(jnp.broadcast_to(padded, (8, 128)))
        offs = (_dot(pa, upper) + _dot(pb, upper) + _dot(pc, upper))[0:1, :]
        rank = rank_sc[i]
        p1 = rank[:, 0:1] + jnp.sum(jnp.where(oh1, offs, 0.0), axis=1, keepdims=True)
        p2 = rank[:, 1:2] + jnp.sum(jnp.where(oh2, offs, 0.0), axis=1, keepdims=True)
        pos_ref[...] = jnp.where(lane == 0, p1, jnp.where(lane == 1, p2, 0.0)).astype(jnp.int32)


def _plan(eid, n_row_tiles):
    nb = eid.shape[0]
    nt = nb * n_row_tiles
    emap = lambda p, i: (i // n_row_tiles, i % n_row_tiles, 0)
    return pl.pallas_call(
        _plan_kernel,
        grid=(2, nt),
        in_specs=[pl.BlockSpec((None, RT, 128), emap)],
        out_specs=[pl.BlockSpec((RT, 128), lambda p, i: (p * i, 0)),
                   pl.BlockSpec((8, 128), lambda p, i: (0, 0))],
        out_shape=[jax.ShapeDtypeStruct((nt * RT, 128), jnp.int32), jax.ShapeDtypeStruct((8, 128), F32)],
        scratch_shapes=[pltpu.VMEM((nt, RT, 128), F32), pltpu.VMEM((8, 128), F32)],
        compiler_params=_cparams(("arbitrary", "arbitrary")),
        name="moe_plan",
    )(eid)


def _dispatch_tiles(pos, counts, n_tiles, n_row_tiles, nb):
    cnt = counts[0, :N_EXPERTS].astype(jnp.int32)
    padded = ((cnt + TE - 1) // TE) * TE
    ends = jnp.cumsum(padded)
    tile_start = jnp.arange(n_tiles, dtype=jnp.int32) * TE
    tile_expert = jnp.sum((ends[None, :] <= tile_start[:, None]).astype(jnp.int32), axis=1)
    tile_expert = jnp.minimum(tile_expert, N_EXPERTS - 1)
    n_used = (ends[-1] // TE).astype(jnp.int32).reshape(1)
    pos2 = pos[:, :2]
    tok = jnp.arange(nb * n_row_tiles * RT, dtype=jnp.int32)
    tok2 = jnp.broadcast_to(tok[:, None], pos2.shape)
    row_token = jnp.zeros((n_tiles * TE,), jnp.int32).at[pos2.reshape(-1)].set(tok2.reshape(-1))
    return pos2.reshape(-1), row_token, tile_expert, n_used


def _rope_tables():
    tok = jnp.arange(SEQ)
    pos_row = (tok // GRID_W).astype(F32)
    pos_col = (tok % GRID_W).astype(F32)
    quarter = MLA_ROPE // 4
    inv_freq = ROPE_BASE ** (-jnp.arange(quarter, dtype=F32) / quarter)
    ang_r = pos_row[:, None] * inv_freq
    ang_c = pos_col[:, None] * inv_freq
    cos = jnp.concatenate([jnp.cos(ang_r)] * 2 + [jnp.cos(ang_c)] * 2, axis=1)
    sin = jnp.concatenate([jnp.sin(ang_r)] * 2 + [jnp.sin(ang_c)] * 2, axis=1)
    cos = jnp.concatenate([cos, jnp.ones((CTX, MLA_ROPE), F32)], axis=0)
    sin = jnp.concatenate([sin, jnp.zeros((CTX, MLA_ROPE), F32)], axis=0)
    pad = jnp.zeros((LT, 64), F32)
    cos_t = jnp.concatenate([jnp.ones((LT, 128), F32), cos, pad], axis=1)
    sin_t = jnp.concatenate([jnp.zeros((LT, 128), F32), sin, pad], axis=1)
    return cos_t, sin_t


def _rope_rot_matrix():
    j = jnp.arange(MLA_ROPE)
    first = (j % 32) < 16
    src = jnp.where(first, j + 16, j - 16)
    sign = jnp.where(first, -1.0, 1.0)
    return jnp.zeros((MLA_ROPE, MLA_ROPE), F32).at[src, j].set(sign)


def _layer_params(l, w_main, w_small, ml_f_bias, lb_all, hg_norm_g, mla_g_q, mla_g_kv, mla_w_uq, mla_w_ukv,
                  ssd_conv_w, ssd_conv_b, ssd_a_log, ssd_dt_bias, ssd_d, ssd_norm_g, w_gate, w_br, w_out,
                  moe_w_grp, moe_w_exp):
    p = {}
    p["w_main"] = w_main[l]
    p["w_small"] = w_small[l]
    fb = jnp.zeros((2, 2, NH), F32).at[:, 1, :].set(ml_f_bias[l]).reshape(16)
    ml_bias = jnp.concatenate([fb, jnp.zeros((112,), F32)])
    p["ml_bias_c"] = ml_bias.reshape(1, 128)
    p["ml_bias_t"] = ml_bias[:32].reshape(32, 1)
    dtb = jnp.concatenate([jnp.zeros((16,), F32), ssd_dt_bias[l].reshape(16), jnp.zeros((96,), F32)])
    p["ssd_bias_c"] = dtb.reshape(1, 128)
    p["ssd_bias_t"] = dtb[:32].reshape(32, 1)
    aneg = jnp.concatenate([jnp.zeros((16,), F32), -jnp.exp(ssd_a_log[l].astype(F32)).reshape(16),
                            jnp.zeros((96,), F32)])
    p["ssd_aneg_c"] = aneg.reshape(1, 128)
    p["ssd_aneg_t"] = aneg[:32].reshape(32, 1)
    p["lb"] = lb_all[l].reshape(1, BW)
    p["hg_norm_g"] = hg_norm_g[l].reshape(1, BW)
    p["ssd_dskip"] = jnp.repeat(ssd_d[l], SSD_P).reshape(1, BW)
    p["ssd_norm_g"] = ssd_norm_g[l].reshape(1, BW)
    p["conv_w"] = jnp.concatenate([ssd_conv_w[l], jnp.zeros((5, SSD_XBC), F32)], axis=0)
    p["conv_b"] = ssd_conv_b[l].reshape(1, SSD_XBC)
    rot = _rope_rot_matrix()
    wq = mla_w_uq[l].reshape(MLA_Q_RANK, NH, MLA_NOPE + MLA_ROPE)
    zq = jnp.zeros((MLA_Q_RANK, NH, 64), F32)
    wqa = jnp.concatenate([wq, zq], axis=2).reshape(MLA_Q_RANK, NH * 256)
    wq_rot = jnp.einsum("rhj,jk->rhk", wq[:, :, MLA_NOPE:], rot)
    wqb = jnp.concatenate([jnp.zeros((MLA_Q_RANK, NH, MLA_NOPE), F32), wq_rot, zq], axis=2)
    p["wqa"] = wqa.astype(BF16)
    p["wqb"] = wqb.reshape(MLA_Q_RANK, NH * 256).astype(BF16)
    wkv = mla_w_ukv[l].reshape(MLA_KV_RANK, NH, 2 * DH)
    wk = jnp.concatenate([wkv[:, :, :DH], jnp.zeros((MLA_KV_RANK, NH, DH), F32)], axis=2)
    p["wk"] = wk.reshape(MLA_KV_RANK, NH * 256).astype(BF16)
    p["wv"] = wkv[:, :, DH:].reshape(MLA_KV_RANK, NH * DH).astype(BF16)
    p["rk"] = jnp.zeros((128, 128), F32).at[:MLA_ROPE, :MLA_ROPE].set(rot).astype(BF16)
    p["gq"] = mla_g_q[l].reshape(1, MLA_Q_RANK)
    p["gkv"] = mla_g_kv[l].reshape(1, MLA_KV_RANK)
    p["w_gate"] = w_gate[l].astype(BF16)
    p["w_br"] = w_br[l].astype(BF16)
    p["w_out"] = w_out[l].astype(BF16)
    wr = jnp.concatenate([moe_w_grp[l], moe_w_exp[l], jnp.zeros((D, 128 - N_GROUPS - N_EXPERTS), F32)], axis=1)
    a = wr.astype(BF16)
    r = wr - a.astype(F32)
    b = r.astype(BF16)
    c = (r - b.astype(F32)).astype(BF16)
    p["wr3"] = (a, b, c)
    return p


def _mixer(h, p, cos_t, sin_t, nb, out_rows):
    t = nb * LT
    hf = h.reshape(t, D)
    zero_b = jnp.zeros((1, U_W), F32)
    u = _mm(hf, p["w_main"], zero_b, BF16, 1152, 1024).reshape(nb, LT, U_W)
    usm = _mm(hf, p["w_small"], zero_b[:, :128], F32, 1152, 128).reshape(nb, LT, 128)
    ust = usm[:, :, :32].reshape(nb, NCH, CS, 32).transpose(0, 1, 3, 2)
    ml_f = _mlstm(u, usm, ust, p["ml_bias_c"], p["ml_bias_t"], False)
    ml = _mlstm(u, usm, ust, p["ml_bias_c"], p["ml_bias_t"], True, ml_f)
    hg_f = _hgrn2(u, p["lb"], False)
    hg = _hgrn2(u, p["lb"], True, hg_f, p["hg_norm_g"])
    q, k, v = _mla_proj(u, p["gq"], p["gkv"], p["wqa"], p["wqb"], p["wk"], p["wv"], p["rk"], cos_t, sin_t)
    mla = _attention(q, k, v)
    xbc = _ssd_conv(u, p["conv_w"], p["conv_b"])
    sargs = (xbc, usm, ust, u, p["ssd_bias_c"], p["ssd_bias_t"], p["ssd_aneg_c"], p["ssd_aneg_t"])
    ssd_f = _ssd(*sargs, False)
    ssd = _ssd(*sargs, True, ssd_f, p["ssd_dskip"], p["ssd_norm_g"])
    return _merge(h, (ml, hg, mla, ssd), p["w_gate"], p["w_br"], out_rows)


def kernel(x, c, ctx, c_ctx, w_ada, b_ada, norm_g, w_in, ml_f_bias, hg_lb_logits, hg_norm_g, mla_g_q, mla_g_kv,
           mla_w_uq, mla_w_ukv, ssd_conv_w, ssd_conv_b, ssd_a_log, ssd_dt_bias, ssd_d, ssd_norm_g, w_gate,
           w_br, w_out, moe_w_grp, moe_w_exp, moe_w1, moe_w3, moe_w2):
    nb = x.shape[0]
    t = nb * LT
    lb_all = jnp.cumsum(jax.nn.softmax(hg_lb_logits.astype(F32), axis=0), axis=0)
    cvec = jnp.zeros((16, D), F32).at[:nb].set(c).at[8].set(c_ctx)
    s_vec = jax.nn.silu(cvec).astype(BF16)
    modv = _adaln_mods(s_vec, w_ada, b_ada).reshape(DEPTH * 16 * 6, 1, D)
    normv = norm_g.reshape(DEPTH * 4, 1, D)
    cos_t, sin_t = _rope_tables()
    w_main, w_small = _win_layout(w_in)
    w1 = moe_w1.reshape(DEPTH * N_EXPERTS, D, D_EXPERT)
    w3 = moe_w3.reshape(DEPTH * N_EXPERTS, D, D_EXPERT)
    w2 = moe_w2.reshape(DEPTH * N_EXPERTS, D_EXPERT, D)
    h = _prenorm(x, ctx, normv, modv, 0)
    xs = (x, ctx)
    for l in range(DEPTH):
        p = _layer_params(l, w_main, w_small, ml_f_bias, lb_all, hg_norm_g, mla_g_q, mla_g_kv, mla_w_uq,
                          mla_w_ukv, ssd_conv_w, ssd_conv_b, ssd_a_log, ssd_dt_bias, ssd_d, ssd_norm_g, w_gate,
                          w_br, w_out, moe_w_grp, moe_w_exp)
        rows = LT if l + 1 < DEPTH else SEQ
        n_row_tiles = rows // RT
        acc = _mixer(h, p, cos_t, sin_t, nb, rows)
        xcat, hp, eid, wts = _outproj(acc, p["w_out"], xs, normv, modv, p["wr3"], l)
        n_tiles = (2 * nb * rows + N_EXPERTS * (TE - 1)) // TE + 1
        pos128, counts = _plan(eid, n_row_tiles)
        pos, row_token, tile_expert, n_used = _dispatch_tiles(pos128, counts, n_tiles, n_row_tiles, nb)
        ys = _experts(tile_expert, n_used, row_token, hp.reshape(nb * rows, D), w1, w3, w2, n_tiles, l)
        outs = _combine(pos, ys, wts, xcat, normv, modv, l)
        if l + 1 < DEPTH:
            xcat, h = outs
            xs = (xcat,)
    return outs[0]
```

```python
import functools

import jax
import jax.numpy as jnp
from jax import lax
from jax.experimental import pallas as pl
from jax.experimental.pallas import tpu as pltpu

F32 = jnp.float32
BF16 = jnp.bfloat16

D = 2048
SEQ = 2048
CTX = 256
LT = SEQ + CTX
DEPTH = 2
EPS = 1e-6
GRID_W = 64
NH = 4
DH = 128
BW = 512
MLA_Q_RANK = 512
MLA_KV_RANK = 256
MLA_NOPE = 128
MLA_ROPE = 64
MLA_SCALE = (MLA_NOPE + MLA_ROPE) ** -0.5
ROPE_BASE = 10000.0
SSD_HEADS = 8
SSD_P = 64
SSD_GROUPS = 2
SSD_N = 128
SSD_XBC = 1024
N_GROUPS = 4
EXP_PER_GROUP = 8
N_EXPERTS = 32
D_EXPERT = 512

CS = 128
NLC = SEQ // CS
NCC = CTX // CS
NCH = NLC + NCC
RT = 256
NRT = LT // RT
TE = 256
VMEM_LIMIT = 52 * 1024 * 1024

U_W = 7168
CB_ML_Q, CB_ML_K, CB_ML_V, CB_ML_O = 0, 1, 2, 3
CB_HG_Q, CB_HG_I, CB_HG_G, CB_HG_F = 4, 5, 6, 7
CB_SSD_Z = 9
CB_SSD_XBC = 5
CB_MLA_CQ = 12
CB_MLA_CKV = 26
CB_MLA_KR = 54


def _cparams(sem):
    return pltpu.CompilerParams(dimension_semantics=sem, vmem_limit_bytes=VMEM_LIMIT)


def _dot(a, b):
    return jnp.dot(a, b, preferred_element_type=F32)


def _dot_nt(a, b):
    return lax.dot_general(a, b, (((1,), (1,)), ((), ())), preferred_element_type=F32)


def _sigmoid(x):
    return 1.0 / (1.0 + jnp.exp(-x))


def _silu(x):
    return x * _sigmoid(x)


def _softplus(x):
    return jnp.maximum(x, 0.0) + jnp.log(1.0 + jnp.exp(-jnp.abs(x)))


def _log_sigmoid(x):
    return -_softplus(-x)


def _rms(x):
    return x * lax.rsqrt(jnp.mean(x * x, axis=-1, keepdims=True) + EPS)


def _split3(x):
    a = x.astype(BF16)
    r = x - a.astype(F32)
    b = r.astype(BF16)
    c = (r - b.astype(F32)).astype(BF16)
    return a, b, c


def _cumsum_rows(mb, x):
    a, b, c = _split3(x)
    return _dot(mb, a) + _dot(mb, b) + _dot(mb, c)


def _cumsum_cols(x, mb):
    a, b, c = _split3(x)
    return _dot_nt(a, mb) + _dot_nt(b, mb) + _dot_nt(c, mb)


def _chunk_mask(rev):
    row = lax.broadcasted_iota(jnp.int32, (CS, CS), 0)
    col = lax.broadcasted_iota(jnp.int32, (CS, CS), 1)
    return (col >= row) if rev else (col <= row)


def _chunk_of_step(rev):
    if rev:
        return lambda s: NCH - 1 - s
    return lambda s: (s + NLC) % NCH


def _mm_kernel(a_ref, w_ref, b_ref, o_ref):
    acc = _dot(a_ref[...], w_ref[...].astype(BF16))
    o_ref[...] = (acc + b_ref[...]).astype(o_ref.dtype)


def _mm(a, w, bias, out_dtype, tm, tn):
    m, k = a.shape
    n = w.shape[1]
    return pl.pallas_call(
        _mm_kernel,
        grid=(m // tm, n // tn),
        in_specs=[pl.BlockSpec((tm, k), lambda i, j: (i, 0)),
                  pl.BlockSpec((k, tn), lambda i, j: (0, j)),
                  pl.BlockSpec((1, tn), lambda i, j: (0, j))],
        out_specs=pl.BlockSpec((tm, tn), lambda i, j: (i, j)),
        out_shape=jax.ShapeDtypeStruct((m, n), out_dtype),
        compiler_params=_cparams(("parallel", "arbitrary")),
        name="mm",
    )(a, w, bias)


W_IN_PIECES = ((0, 0, 2048),
               (2048, 2064, 2560),
               (4608, 5456, 512),
               (5120, 5968, 1024),
               (6144, 4624, 512),
               (6656, 5136, 256),
               (6912, 5392, 64),
               (7040, 2048, 16),
               (7056, 6992, 16))
D_IN = 7008


def _win_kernel(w_ref, o_ref, s_ref):
    o_ref[...] = jnp.zeros_like(o_ref)
    for dst, src, width in W_IN_PIECES:
        o_ref[:, dst:dst + width] = w_ref[:, src:src + width].astype(BF16)
    s_ref[...] = o_ref[:, U_W - 128:]


def _win_layout(w_in):
    rows = 256
    return pl.pallas_call(
        _win_kernel,
        grid=(DEPTH, D // rows),
        in_specs=[pl.BlockSpec((None, rows, D_IN), lambda l, i: (l, i, 0))],
        out_specs=[pl.BlockSpec((None, rows, U_W), lambda l, i: (l, i, 0)),
                   pl.BlockSpec((None, rows, 128), lambda l, i: (l, i, 0))],
        out_shape=[jax.ShapeDtypeStruct((DEPTH, D, U_W), BF16), jax.ShapeDtypeStruct((DEPTH, D, 128), BF16)],
        compiler_params=_cparams(("parallel", "parallel")),
        name="win_layout",
    )(w_in)


def _adaln_mods(s_vec, w_ada, b_ada):
    tn = 1024
    return pl.pallas_call(
        _mm_kernel,
        grid=(DEPTH, 6 * D // tn),
        in_specs=[pl.BlockSpec((16, D), lambda l, j: (0, 0)),
                  pl.BlockSpec((None, D, tn), lambda l, j: (l, 0, j)),
                  pl.BlockSpec((None, 1, tn), lambda l, j: (l, 0, j))],
        out_specs=pl.BlockSpec((None, 16, tn), lambda l, j: (l, 0, j)),
        out_shape=jax.ShapeDtypeStruct((DEPTH, 16, 6 * D), F32),
        compiler_params=_cparams(("parallel", "arbitrary")),
        name="adaln_mods",
    )(s_vec, w_ada, b_ada.reshape(DEPTH, 1, 6 * D))


def _mod_spec(layer, k):
    def imap(b, i):
        row = jnp.where(i == NRT - 1, 8, b)
        return ((layer * 16 + row) * 6 + k, 0, 0)
    return pl.BlockSpec((None, 1, D), imap)


def _norm_spec(layer, k):
    return pl.BlockSpec((None, 1, D), lambda b, i: (layer * 4 + k, 0, 0))


def _row_spec(width, dtype_rows=RT):
    return pl.BlockSpec((None, dtype_rows, width), lambda b, i: (b, i, 0))


def _split_x_specs():
    return [pl.BlockSpec((None, RT, D), lambda b, i: (b, jnp.minimum(i, NRT - 2), 0)),
            pl.BlockSpec((None, RT, D), lambda b, i: (b, 0, 0))]


def _pick_x(x_ref, c_ref):
    return jnp.where(pl.program_id(1) == NRT - 1, c_ref[...], x_ref[...])


def _prenorm_kernel(x_ref, c_ref, g_ref, sh_ref, sc_ref, h_ref):
    y = _rms(_pick_x(x_ref, c_ref)) * g_ref[...]
    h_ref[...] = (y * (1.0 + sc_ref[...]) + sh_ref[...]).astype(BF16)


def _prenorm(x, ctx, normv, modv, layer):
    nb = x.shape[0]
    return pl.pallas_call(
        _prenorm_kernel,
        grid=(nb, NRT),
        in_specs=_split_x_specs() + [_norm_spec(layer, 0), _mod_spec(layer, 0), _mod_spec(layer, 1)],
        out_specs=_row_spec(D),
        out_shape=jax.ShapeDtypeStruct((nb, LT, D), BF16),
        compiler_params=_cparams(("parallel", "parallel")),
        name="prenorm",
    )(x, ctx, normv, modv, modv)


def _conv_kernel(x_ref, p_ref, n_ref, w_ref, b_ref, o_ref):
    i = pl.program_id(1)
    x = x_ref[...].astype(F32)
    has_prev = jnp.logical_and(i != 0, i != NRT - 1).astype(F32)
    has_next = jnp.logical_and(i != NRT - 2, i != NRT - 1).astype(F32)
    prev_row = p_ref[15:16, :].astype(F32) * has_prev
    next_row = n_ref[0:1, :].astype(F32) * has_next
    ridx = lax.broadcasted_iota(jnp.int32, (RT, 1), 0)
    xm1 = jnp.where(ridx == 0, prev_row, pltpu.roll(x, 1, 0))
    xp1 = jnp.where(ridx == RT - 1, next_row, pltpu.roll(x, RT - 1, 0))
    w = w_ref[...]
    y = w[0:1, :] * xm1 + w[1:2, :] * x + w[2:3, :] * xp1 + b_ref[...]
    o_ref[...] = _silu(y)


def _ssd_conv(u, conv_w, conv_b):
    nb = u.shape[0]
    hb = RT // 16
    last = LT // 16 - 1
    return pl.pallas_call(
        _conv_kernel,
        grid=(nb, NRT),
        in_specs=[pl.BlockSpec((None, RT, SSD_XBC), lambda b, i: (b, i, CB_SSD_XBC)),
                  pl.BlockSpec((None, 16, SSD_XBC), lambda b, i: (b, jnp.maximum(i * hb - 1, 0), CB_SSD_XBC)),
                  pl.BlockSpec((None, 16, SSD_XBC), lambda b, i: (b, jnp.minimum((i + 1) * hb, last), CB_SSD_XBC)),
                  pl.BlockSpec((8, SSD_XBC), lambda b, i: (0, 0)),
                  pl.BlockSpec((1, SSD_XBC), lambda b, i: (0, 0))],
        out_specs=pl.BlockSpec((None, RT, SSD_XBC), lambda b, i: (b, i, 0)),
        out_shape=jax.ShapeDtypeStruct((nb, LT, SSD_XBC), F32),
        compiler_params=_cparams(("parallel", "parallel")),
        name="ssd_conv",
    )(u, u, u, conv_w, conv_b)


SCAN_BB = 2


def _mlstm_kernel(rev, bpb, *refs):
    if rev:
        (qs_ref, ks_ref, vs_ref, gc_ref, gt_ref, bc_ref, bt_ref, yps_ref, ogs_ref,
         os_ref, c_sc, n_sc, m_sc) = refs
    else:
        (qs_ref, ks_ref, vs_ref, gc_ref, gt_ref, bc_ref, bt_ref,
         os_ref, c_sc, n_sc, m_sc) = refs
    d = 1 if rev else 0

    @pl.when(pl.program_id(1) == 0)
    def _():
        c_sc[...] = jnp.zeros_like(c_sc)
        n_sc[...] = jnp.zeros_like(n_sc)
        m_sc[...] = jnp.full(m_sc.shape, -jnp.inf, F32)

    mask = _chunk_mask(rev)
    mb = jnp.where(mask, 1.0, 0.0).astype(BF16)
    scale = DH ** -0.5
    for bb, h in [(bb, h) for bb in range(bpb) for h in range(NH)]:
        if h == 0:
            gc = gc_ref[bb] + bc_ref[...]
            gt = gt_ref[bb] + bt_ref[...]
            flog_c = _log_sigmoid(gc)
            flog_t = _log_sigmoid(gt)
            fcum_c = _cumsum_rows(mb, flog_c)
            fcum_t = _cumsum_cols(flog_t, mb)
        q_ref, k_ref, v_ref, o_ref = qs_ref.at[bb], ks_ref.at[bb], vs_ref.at[bb], os_ref.at[bb]
        if rev:
            yp_ref, og_ref = yps_ref.at[bb], ogs_ref.at[bb]
        c_h, n_h, m_h = c_sc.at[bb * NH + h], n_sc.at[bb * NH + h], m_sc.at[bb * NH + h]
        ci = d * 8 + h
        cf = d * 8 + 4 + h
        sl = slice(h * DH, (h + 1) * DH)
        q = q_ref[:, sl]
        k = k_ref[:, sl]
        v = v_ref[:, sl]
        i_col = gc[:, ci:ci + 1]
        f_col = flog_c[:, cf:cf + 1]
        fc_col = fcum_c[:, cf:cf + 1]
        i_row = gt[ci:ci + 1, :]
        fc_row = fcum_t[cf:cf + 1, :]
        m_prev = m_h[0:1, 0:1]
        dm = jnp.where(mask, fc_col - fc_row + i_row, -jnp.inf)
        from_state = fc_col + m_prev
        m_t = jnp.maximum(from_state, jnp.max(dm, axis=1, keepdims=True))
        s = _dot_nt(q, k) * scale * jnp.exp(dm - m_t)
        w_state = jnp.exp(from_state - m_t)
        qc = _dot(q, c_h[...].astype(BF16)) * scale
        num = _dot(s.astype(BF16), v) + w_state * qc
        qn = jnp.sum(q.astype(F32) * n_h[0:1, :], axis=1, keepdims=True) * scale
        den = jnp.sum(s, axis=1, keepdims=True) + w_state * qn
        hout = num / jnp.maximum(jnp.abs(den), jnp.exp(-m_t))
        if rev:
            tot = yp_ref[:, sl] + hout
            o_ref[:, sl] = (tot * _sigmoid(og_ref[:, sl].astype(F32))).astype(o_ref.dtype)
        else:
            o_ref[:, sl] = hout
        f_tot = jnp.sum(f_col, axis=0, keepdims=True)
        w_log = f_tot - fc_col + i_col
        m_new = jnp.maximum(f_tot + m_prev, jnp.max(w_log, axis=0, keepdims=True))
        decay = jnp.exp(f_tot + m_prev - m_new)
        kw = k.astype(F32) * jnp.exp(w_log - m_new)
        c_h[...] = decay * c_h[...] + _dot(kw.T.astype(BF16), v)
        n_new = decay * n_h[0:1, :] + jnp.sum(kw, axis=0, keepdims=True)
        n_h[...] = jnp.broadcast_to(n_new, (8, DH))
        m_h[...] = jnp.broadcast_to(m_new, (8, DH))


def _u_spec(width, cblk, cmap):
    return pl.BlockSpec((None, CS, width), lambda b, s: (b, cmap(s), cblk))


def _mlstm(u, usm, ust, bias_c, bias_t, rev, yprev=None):
    nb = u.shape[0]
    bpb = SCAN_BB if nb % SCAN_BB == 0 else 1
    cmap = _chunk_of_step(rev)
    uspec = lambda cblk: pl.BlockSpec((bpb, CS, BW), lambda b, s: (b, cmap(s), cblk))
    in_specs = [uspec(CB_ML_Q), uspec(CB_ML_K), uspec(CB_ML_V),
                pl.BlockSpec((bpb, CS, 128), lambda b, s: (b, cmap(s), 0)),
                pl.BlockSpec((bpb, None, 32, CS), lambda b, s: (b, cmap(s), 0, 0)),
                pl.BlockSpec((1, 128), lambda b, s: (0, 0)),
                pl.BlockSpec((32, 1), lambda b, s: (0, 0))]
    args = [u, u, u, usm, ust, bias_c, bias_t]
    if rev:
        in_specs += [uspec(0), uspec(CB_ML_O)]
        args += [yprev, u]
    return pl.pallas_call(
        functools.partial(_mlstm_kernel, rev, bpb),
        grid=(nb // bpb, NCH),
        in_specs=in_specs,
        out_specs=uspec(0),
        out_shape=jax.ShapeDtypeStruct((nb, LT, BW), BF16 if rev else F32),
        scratch_shapes=[pltpu.VMEM((bpb * NH, DH, DH), F32), pltpu.VMEM((bpb * NH, 8, DH), F32),
                        pltpu.VMEM((bpb * NH, 8, DH), F32)],
        compiler_params=_cparams(("parallel", "arbitrary")),
        name="mlstm_rev" if rev else "mlstm_fwd",
    )(*args)


def _block_ref(g, n, off):
    g3 = g.reshape(CS // n, n, DH)
    r = g3[:, off:off + 1, :]
    return jnp.broadcast_to(r, (CS // n, n, DH)).reshape(CS, DH)


def _hgrn2_kernel(rev, *refs):
    if rev:
        q_ref, i_ref, f_ref, lb_ref, yp_ref, g_ref, ng_ref, o_ref, st_sc = refs
    else:
        q_ref, i_ref, f_ref, lb_ref, o_ref, st_sc = refs

    @pl.when(pl.program_id(1) == 0)
    def _():
        st_sc[...] = jnp.zeros_like(st_sc)

    mask = _chunk_mask(rev)
    mb = jnp.where(mask, 1.0, 0.0).astype(BF16)
    row = lax.broadcasted_iota(jnp.int32, (CS, CS), 0)
    col = lax.broadcasted_iota(jnp.int32, (CS, CS), 1)
    tix = lax.broadcasted_iota(jnp.int32, (CS, 1), 0)
    for h in range(NH):
        sl = slice(h * DH, (h + 1) * DH)
        q = _silu(q_ref[:, sl].astype(F32))
        v = i_ref[:, sl].astype(F32)
        lb = lb_ref[:, sl]
        f = lb + (1.0 - lb) * _sigmoid(f_ref[:, sl].astype(F32))
        k = 1.0 - f
        lf = jnp.log(f)
        g = _cumsum_rows(mb, lf)
        st = st_sc[h]
        o = _dot_nt((q * jnp.exp(g)).astype(BF16), st.astype(BF16))
        a = jnp.zeros((CS, CS), F32)
        n = CS
        while n >= 16:
            half = n // 2
            second = jnp.bitwise_and(tix, n - 1) >= half
            rowside = jnp.logical_not(second) if rev else second
            r = _block_ref(g, n, half if rev else half - 1)
            e = jnp.exp(jnp.where(rowside, g - r, r - g))
            qs = jnp.where(rowside, q * e, 0.0).astype(BF16)
            ks = jnp.where(rowside, 0.0, k * e).astype(BF16)
            sh_n = n.bit_length() - 1
            same = jnp.right_shift(row, sh_n) == jnp.right_shift(col, sh_n)
            a = a + jnp.where(same, _dot_nt(qs, ks), 0.0)
            n = half
        o = o + _dot(a.astype(BF16), v.astype(BF16))
        t8 = jnp.bitwise_and(tix, 7)
        k3, g3, v3 = (a3.reshape(CS // 8, 8, DH) for a3 in (k, g, v))
        for dl in range(8):
            if dl == 0:
                ksh, gsh, vsh = k, g, v
            else:
                sh = (8 - dl) if rev else dl
                ksh = pltpu.roll(k3, sh, 1).reshape(CS, DH)
                gsh = pltpu.roll(g3, sh, 1).reshape(CS, DH)
                vsh = pltpu.roll(v3, sh, 1).reshape(CS, DH)
            valid = (t8 + dl <= 7) if rev else (t8 >= dl)
            e = jnp.exp(jnp.minimum(g - gsh, 0.0))
            coef = jnp.sum(q * ksh * e, axis=1, keepdims=True)
            o = o + jnp.where(valid, coef, 0.0) * vsh
        if rev:
            tot = yp_ref[:, sl] + o
            y = _rms(tot) * ng_ref[:, sl]
            o_ref[:, sl] = (y * _sigmoid(g_ref[:, sl].astype(F32))).astype(o_ref.dtype)
        else:
            o_ref[:, sl] = o
        g_tot = jnp.sum(lf, axis=0, keepdims=True)
        kdec = k * jnp.exp(g_tot - g)
        st_sc[h] = st * jnp.exp(g_tot) + _dot(v.T.astype(BF16), kdec.astype(BF16))


def _hgrn2(u, lb, rev, yprev=None, norm_g=None):
    nb = u.shape[0]
    cmap = _chunk_of_step(rev)
    d = 1 if rev else 0
    in_specs = [_u_spec(BW, CB_HG_Q, cmap), _u_spec(BW, CB_HG_I, cmap), _u_spec(BW, CB_HG_F + d, cmap),
                pl.BlockSpec((1, BW), lambda b, s: (0, 0))]
    args = [u, u, u, lb]
    if rev:
        in_specs += [pl.BlockSpec((None, CS, BW), lambda b, s: (b, cmap(s), 0)),
                     _u_spec(BW, CB_HG_G, cmap),
                     pl.BlockSpec((1, BW), lambda b, s: (0, 0))]
        args += [yprev, u, norm_g]
    return pl.pallas_call(
        functools.partial(_hgrn2_kernel, rev),
        grid=(nb, NCH),
        in_specs=in_specs,
        out_specs=pl.BlockSpec((None, CS, BW), lambda b, s: (b, cmap(s), 0)),
        out_shape=jax.ShapeDtypeStruct((nb, LT, BW), BF16 if rev else F32),
        scratch_shapes=[pltpu.VMEM((NH, DH, DH), F32)],
        compiler_params=_cparams(("parallel", "arbitrary")),
        name="hgrn2_rev" if rev else "hgrn2_fwd",
    )(*args)


def _expand_heads(cols, lane_head):
    r = cols[3]
    for hh in (2, 1, 0):
        r = jnp.where(lane_head == hh, cols[hh], r)
    return r


def _ssd_kernel(rev, *refs):
    if rev:
        (x_ref, gc_ref, gt_ref, bc_ref, bt_ref, ac_ref, at_ref, yp_ref, z_ref, dk_ref, ng_ref,
         o_ref, ht_sc) = refs
    else:
        x_ref, gc_ref, gt_ref, bc_ref, bt_ref, ac_ref, at_ref, o_ref, ht_sc = refs
    d = 1 if rev else 0

    @pl.when(pl.program_id(1) == 0)
    def _():
        ht_sc[...] = jnp.zeros_like(ht_sc)

    mask = _chunk_mask(rev)
    mb = jnp.where(mask, 1.0, 0.0).astype(BF16)
    dt_c = _softplus(gc_ref[...] + bc_ref[...])
    da_c = dt_c * ac_ref[...]
    acum_c = _cumsum_rows(mb, da_c)
    dt_t = _softplus(gt_ref[...] + bt_ref[...])
    da_t = dt_t * at_ref[...]
    acum_t = _cumsum_cols(da_t, mb)
    gw = 4 * SSD_P
    lane_head = jnp.right_shift(lax.broadcasted_iota(jnp.int32, (1, gw), 1), 6)
    ys = []
    for grp in range(SSD_GROUPS):
        xg = x_ref[:, grp * gw:(grp + 1) * gw]
        bm = x_ref[:, BW + grp * SSD_N:BW + (grp + 1) * SSD_N]
        cm = x_ref[:, BW + SSD_GROUPS * SSD_N + grp * SSD_N:BW + SSD_GROUPS * SSD_N + (grp + 1) * SSD_N]
        cmb = cm.astype(BF16)
        cb = _dot_nt(cmb, bm.astype(BF16))
        ht = ht_sc[grp]
        y_state = _dot(cmb, ht.astype(BF16))
        lanes = [16 + d * 8 + grp * 4 + hh for hh in range(4)]
        dt_cols = [dt_c[:, l:l + 1] for l in lanes]
        a_cols = [acum_c[:, l:l + 1] for l in lanes]
        a_tots = [jnp.sum(da_c[:, l:l + 1], axis=0, keepdims=True) for l in lanes]
        xdt = xg * _expand_heads(dt_cols, lane_head)
        xdtb = xdt.astype(BF16)
        y = jnp.exp(_expand_heads(a_cols, lane_head)) * y_state
        for hh in range(4):
            l = lanes[hh]
            dec = jnp.exp(jnp.where(mask, a_cols[hh] - acum_t[l:l + 1, :], -jnp.inf))
            yh = _dot((cb * dec).astype(BF16), xdtb)
            y = y + jnp.where(lane_head == hh, yh, 0.0)
        ys.append(y)
        w_exp = jnp.exp(_expand_heads([a_tots[hh] - a_cols[hh] for hh in range(4)], lane_head))
        e_row = jnp.exp(_expand_heads(a_tots, lane_head))
        ht_sc[grp] = ht * e_row + _dot(bm.T.astype(BF16), (xdt * w_exp).astype(BF16))
    y = jnp.concatenate(ys, axis=1)
    if rev:
        tot = yp_ref[...] + y + dk_ref[...] * x_ref[:, :BW]
        tot = tot * _silu(z_ref[...].astype(F32))
        o_ref[...] = (_rms(tot) * ng_ref[...]).astype(o_ref.dtype)
    else:
        o_ref[...] = y


def _ssd(xbc, usm, ust, u, bias_c, bias_t, aneg_c, aneg_t, rev, yprev=None, dskip=None, norm_g=None):
    nb = xbc.shape[0]
    cmap = _chunk_of_step(rev)
    in_specs = [pl.BlockSpec((None, CS, SSD_XBC), lambda b, s: (b, cmap(s), 0)),
                pl.BlockSpec((None, CS, 128), lambda b, s: (b, cmap(s), 0)),
                pl.BlockSpec((None, None, 32, CS), lambda b, s: (b, cmap(s), 0, 0)),
                pl.BlockSpec((1, 128), lambda b, s: (0, 0)),
                pl.BlockSpec((32, 1), lambda b, s: (0, 0)),
                pl.BlockSpec((1, 128), lambda b, s: (0, 0)),
                pl.BlockSpec((32, 1), lambda b, s: (0, 0))]
    args = [xbc, usm, ust, bias_c, bias_t, aneg_c, aneg_t]
    if rev:
        in_specs += [pl.BlockSpec((None, CS, BW), lambda b, s: (b, cmap(s), 0)),
                     _u_spec(BW, CB_SSD_Z, cmap),
                     pl.BlockSpec((1, BW), lambda b, s: (0, 0)),
                     pl.BlockSpec((1, BW), lambda b, s: (0, 0))]
        args += [yprev, u, dskip, norm_g]
    return pl.pallas_call(
        functools.partial(_ssd_kernel, rev),
        grid=(nb, NCH),
        in_specs=in_specs,
        out_specs=pl.BlockSpec((None, CS, BW), lambda b, s: (b, cmap(s), 0)),
        out_shape=jax.ShapeDtypeStruct((nb, LT, BW), BF16 if rev else F32),
        scratch_shapes=[pltpu.VMEM((SSD_GROUPS, SSD_N, 4 * SSD_P), F32)],
        compiler_params=_cparams(("parallel", "arbitrary")),
        name="ssd_rev" if rev else "ssd_fwd",
    )(*args)


def _mla_proj_kernel(cq_ref, ckv_ref, kr_ref, gq_ref, gkv_ref, wqa_ref, wqb_ref, wk_ref, wv_ref,
                     rk_ref, cos_ref, sin_ref, q_ref, k_ref, v_ref):
    nq = (_rms(cq_ref[...].astype(F32)) * gq_ref[...]).astype(BF16)
    qa = _dot(nq, wqa_ref[...])
    qb = _dot(nq, wqb_ref[...])
    cos_t = cos_ref[...]
    sin_t = sin_ref[...]
    cosf = jnp.concatenate([cos_t] * NH, axis=1)
    sinf = jnp.concatenate([sin_t] * NH, axis=1)
    q_ref[...] = ((qa * cosf + qb * sinf) * MLA_SCALE).astype(BF16)
    nkv = (_rms(ckv_ref[...].astype(F32)) * gkv_ref[...]).astype(BF16)
    kn = _dot(nkv, wk_ref[...])
    v_ref[...] = _dot(nkv, wv_ref[...]).astype(BF16)
    kr = kr_ref[...]
    krr = kr.astype(F32) * cos_t[:, 128:256] + _dot(kr, rk_ref[...]) * sin_t[:, 128:256]
    zero = jnp.zeros_like(krr)
    k_ref[...] = (kn + jnp.concatenate([zero, krr] * NH, axis=1)).astype(BF16)


def _mla_proj(u, gq, gkv, wqa, wqb, wk, wv, rk, cos_t, sin_t):
    nb = u.shape[0]
    const = lambda shape: pl.BlockSpec(shape, lambda b, i: (0, 0))
    return pl.pallas_call(
        _mla_proj_kernel,
        grid=(nb, NRT),
        in_specs=[pl.BlockSpec((None, RT, MLA_Q_RANK), lambda b, i: (b, i, CB_MLA_CQ)),
                  pl.BlockSpec((None, RT, MLA_KV_RANK), lambda b, i: (b, i, CB_MLA_CKV)),
                  pl.BlockSpec((None, RT, 128), lambda b, i: (b, i, CB_MLA_KR)),
                  const((1, MLA_Q_RANK)), const((1, MLA_KV_RANK)),
                  const((MLA_Q_RANK, NH * 256)), const((MLA_Q_RANK, NH * 256)),
                  const((MLA_KV_RANK, NH * 256)), const((MLA_KV_RANK, NH * DH)),
                  const((128, 128)),
                  pl.BlockSpec((RT, 256), lambda b, i: (i, 0)),
                  pl.BlockSpec((RT, 256), lambda b, i: (i, 0))],
        out_specs=[_row_spec(NH * 256), _row_spec(NH * 256), _row_spec(NH * DH)],
        out_shape=[jax.ShapeDtypeStruct((nb, LT, NH * 256), BF16),
                   jax.ShapeDtypeStruct((nb, LT, NH * 256), BF16),
                   jax.ShapeDtypeStruct((nb, LT, NH * DH), BF16)],
        compiler_params=_cparams(("parallel", "parallel")),
        name="mla_proj",
    )(u, u, u, gq, gkv, wqa, wqb, wk, wv, rk, cos_t, sin_t)


AT = 768


def _attn_kernel(q_ref, k_ref, v_ref, o_ref):
    s = _dot_nt(q_ref[...], k_ref[...])

    def finish(sc):
        m = jnp.max(sc, axis=1, keepdims=True)
        p = jnp.exp(sc - m)
        l = jnp.sum(p, axis=1, keepdims=True)
        o_ref[...] = (_dot(p.astype(BF16), v_ref[...]) / l).astype(o_ref.dtype)

    last = pl.program_id(2) == LT // AT - 1

    @pl.when(last)
    def _():
        qrow = lax.broadcasted_iota(jnp.int32, (AT, 1), 0)
        kcol = lax.broadcasted_iota(jnp.int32, (1, LT), 1)
        hide = jnp.logical_and(qrow >= AT - CTX, kcol < SEQ)
        finish(jnp.where(hide, -jnp.inf, s))

    @pl.when(jnp.logical_not(last))
    def _():
        finish(s)


def _attention(q, k, v):
    nb = q.shape[0]
    return pl.pallas_call(
        _attn_kernel,
        grid=(nb, NH, LT // AT),
        in_specs=[pl.BlockSpec((None, AT, 256), lambda b, h, i: (b, i, h)),
                  pl.BlockSpec((None, LT, 256), lambda b, h, i: (b, 0, h)),
                  pl.BlockSpec((None, LT, DH), lambda b, h, i: (b, 0, h))],
        out_specs=pl.BlockSpec((None, AT, DH), lambda b, h, i: (b, i, h)),
        out_shape=jax.ShapeDtypeStruct((nb, LT, NH * DH), BF16),
        compiler_params=_cparams(("parallel", "parallel", "arbitrary")),
        name="mla_attn",
    )(q, k, v)


def _merge_kernel(h_ref, b0_ref, b1_ref, b2_ref, b3_ref, wg_ref, wb_ref, o_ref):
    h = h_ref[...]
    acc = None
    for kk, br in enumerate((b0_ref, b1_ref, b2_ref, b3_ref)):
        gate = _sigmoid(_dot(h, wg_ref[kk]))
        term = gate * _dot(br[...], wb_ref[kk])
        acc = term if acc is None else acc + term
    o_ref[...] = acc.astype(o_ref.dtype)


def _merge(h, branches, wg, wb, rows, tn=256):
    nb = h.shape[0]
    tm = rows // 2
    rmap = lambda i, j: (i // 2, i % 2, 0)
    bspec = pl.BlockSpec((None, tm, BW), rmap)
    return pl.pallas_call(
        _merge_kernel,
        grid=(nb * 2, D // tn),
        in_specs=[pl.BlockSpec((None, tm, D), rmap), bspec, bspec, bspec, bspec,
                  pl.BlockSpec((4, D, tn), lambda i, j: (0, 0, j)),
                  pl.BlockSpec((4, BW, tn), lambda i, j: (0, 0, j))],
        out_specs=pl.BlockSpec((None, tm, tn), lambda i, j: (i // 2, i % 2, j)),
        out_shape=jax.ShapeDtypeStruct((nb, rows, D), BF16),
        compiler_params=_cparams(("parallel", "arbitrary")),
        name="merge",
    )(h, *branches, wg, wb)


TOK_ROWS = D // 128


def _store_token_major(ref, val, col0):
    n = val.shape[0]
    for j in range(val.shape[1] // 128):
        ref[pl.ds(col0 // 128 + j, n, stride=TOK_ROWS), :] = val[:, j * 128:(j + 1) * 128]


def _load_token_major(ref, tok0, n):
    return jnp.concatenate([ref[pl.ds(tok0 * TOK_ROWS + j, n, stride=TOK_ROWS), :] for j in range(TOK_ROWS)],
                           axis=1)


def _outproj_kernel(first, a_ref, w_ref, *refs):
    if first:
        x_ref, c_ref = refs[:2]
        x_in = _pick_x(x_ref, c_ref)
        refs = refs[2:]
    else:
        x_in = refs[0][...]
        refs = refs[1:]
    g1_ref, gate_ref, g2_ref, sh_ref, sc_ref, wa_ref, wb_ref, wc_ref, xo_ref, hp_ref, e_ref, wt_ref = refs
    y = _dot(a_ref[...], w_ref[...])
    xn = x_in + gate_ref[...] * (_rms(y) * g1_ref[...])
    xo_ref[...] = xn
    h2 = _rms(xn) * g2_ref[...]
    h2 = h2 * (1.0 + sc_ref[...]) + sh_ref[...]
    _store_token_major(hp_ref, h2, 0)
    _route_math(h2, wa_ref, wb_ref, wc_ref, e_ref, wt_ref)


def _outproj(acc, w_out, xs, normv, modv, wr3, layer):
    nb, rows, _ = acc.shape
    first = len(xs) == 2
    wspec = pl.BlockSpec((D, 128), lambda b, i: (0, 0))
    return pl.pallas_call(
        functools.partial(_outproj_kernel, first),
        grid=(nb, rows // RT),
        in_specs=[_row_spec(D), pl.BlockSpec((D, D), lambda b, i: (0, 0))]
                 + (_split_x_specs() if first else [_row_spec(D)])
                 + [_norm_spec(layer, 1), _mod_spec(layer, 2), _norm_spec(layer, 2),
                    _mod_spec(layer, 3), _mod_spec(layer, 4), wspec, wspec, wspec],
        out_specs=[_row_spec(D), pl.BlockSpec((None, RT * TOK_ROWS, 128), lambda b, i: (b, i, 0)),
                   _row_spec(128), _row_spec(128)],
        out_shape=[jax.ShapeDtypeStruct((nb, rows, D), F32),
                   jax.ShapeDtypeStruct((nb, rows * TOK_ROWS, 128), F32),
                   jax.ShapeDtypeStruct((nb, rows, 128), jnp.int32), jax.ShapeDtypeStruct((nb, rows, 128), F32)],
        compiler_params=_cparams(("parallel", "parallel")),
        name="outproj",
    )(acc, w_out, *xs, normv, modv, normv, modv, modv, *wr3)


def _route_math(hf, wa_ref, wb_ref, wc_ref, e_ref, w_ref):
    h = hf.astype(BF16)
    hl = (hf - h.astype(F32)).astype(BF16)
    logits = (_dot(h, wa_ref[...]) + _dot(h, wb_ref[...]) + _dot(h, wc_ref[...])
              + _dot(hl, wa_ref[...]) + _dot(hl, wb_ref[...]))
    lane = lax.broadcasted_iota(jnp.int32, (1, 128), 1).astype(F32)
    ninf = -jnp.inf
    is_grp = lane < N_GROUPS
    gl = jnp.where(is_grp, logits, ninf)
    gmax = jnp.max(gl, axis=1, keepdims=True)
    gidx = jnp.min(jnp.where(gl == gmax, lane, 999.0), axis=1, keepdims=True)
    gsum = jnp.sum(jnp.where(is_grp, jnp.exp(gl - gmax), 0.0), axis=1, keepdims=True)
    gp = 1.0 / gsum
    lo = N_GROUPS + EXP_PER_GROUP * gidx
    sel = jnp.logical_and(lane >= lo, lane < lo + EXP_PER_GROUP)
    el = jnp.where(sel, logits, ninf)
    v1 = jnp.max(el, axis=1, keepdims=True)
    i1 = jnp.min(jnp.where(el == v1, lane, 999.0), axis=1, keepdims=True)
    el2 = jnp.where(lane == i1, ninf, el)
    v2 = jnp.max(el2, axis=1, keepdims=True)
    i2 = jnp.min(jnp.where(el2 == v2, lane, 999.0), axis=1, keepdims=True)
    t = jnp.exp(v2 - v1)
    w1 = gp / (1.0 + t)
    w2 = gp * t / (1.0 + t)
    e_ref[...] = jnp.where(lane == 0.0, i1 - N_GROUPS, jnp.where(lane == 1.0, i2 - N_GROUPS, 0.0)).astype(jnp.int32)
    w_ref[...] = jnp.where(lane == 0.0, w1, jnp.where(lane == 1.0, w2, 0.0))


GATHER_UNROLL = 8


def _expert_kernel(te_ref, nu_ref, rt_ref, h_hbm, w1_ref, w3_ref, w2_ref, o_ref, xbuf, hm_sc, sem):
    i = pl.program_id(0)
    n_used = nu_ref[0]
    slot = i % 2

    def issue_tile(j, s):
        base = j * TE

        def body(g, c):
            for uu in range(GATHER_UNROLL):
                r = g * GATHER_UNROLL + uu
                pltpu.make_async_copy(h_hbm.at[pl.ds(pl.multiple_of(rt_ref[base + r], TOK_ROWS), TOK_ROWS), :],
                                      xbuf.at[s, pl.ds(pl.multiple_of(r * TOK_ROWS, TOK_ROWS), TOK_ROWS), :],
                                      sem.at[s]).start(priority=1)
            return c
        lax.fori_loop(0, TE // GATHER_UNROLL, body, 0)

    def wait_tile(s):
        pltpu.make_async_copy(h_hbm.at[pl.ds(0, TE * TOK_ROWS), :], xbuf.at[s], sem.at[s]).wait()

    @pl.when(jnp.logical_and(i == 0, n_used > 0))
    def _():
        issue_tile(0, 0)

    @pl.when(i < n_used)
    def _():
        wait_tile(slot)
        nbase = jnp.minimum(i + 1, n_used - 1) * TE
        oslot = 1 - slot

        def issue_rows(r0, r1):
            for r in range(r0, r1):
                pltpu.make_async_copy(h_hbm.at[pl.ds(pl.multiple_of(rt_ref[nbase + r], TOK_ROWS), TOK_ROWS), :],
                                      xbuf.at[oslot, pl.ds(r * TOK_ROWS, TOK_ROWS), :],
                                      sem.at[oslot]).start(priority=1)

        x = _load_token_major(xbuf.at[slot], 0, TE).astype(BF16)
        cw = 256
        n1, n2 = D_EXPERT // cw, D // cw
        per = TE // (n1 + n2) + 1
        done = 0
        for j in range(n1):
            issue_rows(done, min(done + per, TE))
            done = min(done + per, TE)
            cs = slice(j * cw, (j + 1) * cw)
            a = _dot(x, w1_ref[:, cs].astype(BF16))
            b = _dot(x, w3_ref[:, cs].astype(BF16))
            hm_sc[:, cs] = (_silu(a) * b).astype(BF16)
        hm = hm_sc[...]
        for j in range(n2):
            issue_rows(done, min(done + per, TE))
            done = min(done + per, TE)
            cs = slice(j * cw, (j + 1) * cw)
            _store_token_major(o_ref, _dot(hm, w2_ref[:, cs].astype(BF16)), j * cw)

        @pl.when(i + 1 >= n_used)
        def _():
            wait_tile(oslot)

    @pl.when(i >= n_used)
    def _():
        o_ref[...] = jnp.zeros_like(o_ref)


def _experts(tile_expert, n_used, row_token, hp, w1, w3, w2, n_tiles, layer):
    wmap = lambda i, te, nu, rt: (layer * N_EXPERTS + te[i], 0, 0)
    grid_spec = pltpu.PrefetchScalarGridSpec(
        num_scalar_prefetch=3,
        grid=(n_tiles,),
        in_specs=[pl.BlockSpec(memory_space=pl.ANY),
                  pl.BlockSpec((None, D, D_EXPERT), wmap),
                  pl.BlockSpec((None, D, D_EXPERT), wmap),
                  pl.BlockSpec((None, D_EXPERT, D), wmap)],
        out_specs=pl.BlockSpec((TE * TOK_ROWS, 128), lambda i, te, nu, rt: (i, 0)),
        scratch_shapes=[pltpu.VMEM((2, TE * TOK_ROWS, 128), F32), pltpu.VMEM((TE, D_EXPERT), BF16),
                        pltpu.SemaphoreType.DMA((2,))],
    )
    return pl.pallas_call(
        _expert_kernel,
        grid_spec=grid_spec,
        out_shape=jax.ShapeDtypeStruct((n_tiles * TE * TOK_ROWS, 128), F32),
        compiler_params=_cparams(("arbitrary",)),
        name="moe_experts",
    )(tile_expert, n_used, row_token, hp, w1, w3, w2)


def _combine_kernel(has_next, pos_ref, ys_hbm, w_ref, x_ref, g3_ref, gate_ref, *rest):
    if has_next:
        g0_ref, sh_ref, sc_ref, xo_ref, ho_ref, buf, sem = rest
    else:
        xo_ref, buf, sem = rest
    step = pl.program_id(0) * pl.num_programs(1) + pl.program_id(1)
    n_steps = pl.num_programs(0) * pl.num_programs(1)
    slot = step % 2

    def issue_tile(j, s):
        t0 = j * RT

        def body(g, c):
            for uu in range(GATHER_UNROLL):
                r = g * GATHER_UNROLL + uu
                for kk in range(2):
                    src = pl.multiple_of(pos_ref[(t0 + r) * 2 + kk], TOK_ROWS)
                    pltpu.make_async_copy(ys_hbm.at[pl.ds(src, TOK_ROWS), :],
                                          buf.at[s, kk, pl.ds(pl.multiple_of(r * TOK_ROWS, TOK_ROWS), TOK_ROWS), :],
                                          sem.at[s]).start(priority=kk)
            return c
        lax.fori_loop(0, RT // GATHER_UNROLL, body, 0)

    def wait_tile(s):
        for kk in range(2):
            pltpu.make_async_copy(ys_hbm.at[pl.ds(0, RT * TOK_ROWS), :], buf.at[s, kk], sem.at[s]).wait()

    @pl.when(step == 0)
    def _():
        issue_tile(0, 0)

    wait_tile(slot)
    nt0 = jnp.minimum(step + 1, n_steps - 1) * RT
    oslot = 1 - slot
    rows = 32
    for c in range(RT // rows):
        for r in range(c * rows, (c + 1) * rows):
            for kk in range(2):
                src = pl.multiple_of(pos_ref[(nt0 + r) * 2 + kk], TOK_ROWS)
                pltpu.make_async_copy(ys_hbm.at[pl.ds(src, TOK_ROWS), :],
                                      buf.at[oslot, kk, pl.ds(r * TOK_ROWS, TOK_ROWS), :],
                                      sem.at[oslot]).start(priority=kk)
        rs = slice(c * rows, (c + 1) * rows)
        w = w_ref[rs, :]
        y = (w[:, 0:1] * _load_token_major(buf.at[slot, 0], c * rows, rows)
             + w[:, 1:2] * _load_token_major(buf.at[slot, 1], c * rows, rows))
        xn = x_ref[rs, :] + gate_ref[...] * (_rms(y) * g3_ref[...])
        xo_ref[rs, :] = xn
        if has_next:
            hn = _rms(xn) * g0_ref[...]
            ho_ref[rs, :] = (hn * (1.0 + sc_ref[...]) + sh_ref[...]).astype(BF16)

    @pl.when(step + 1 >= n_steps)
    def _():
        wait_tile(oslot)


def _combine(pos, ys, wts, xcat, normv, modv, layer):
    nb = xcat.shape[0]
    has_next = layer + 1 < DEPTH

    def mod_spec(lyr, k):
        def imap(b, i, pos_ref):
            row = jnp.where(i == NRT - 1, 8, b)
            return ((lyr * 16 + row) * 6 + k, 0, 0)
        return pl.BlockSpec((None, 1, D), imap)

    def norm_spec(lyr, k):
        return pl.BlockSpec((None, 1, D), lambda b, i, pos_ref: (lyr * 4 + k, 0, 0))

    def row_spec(width):
        return pl.BlockSpec((None, RT, width), lambda b, i, pos_ref: (b, i, 0))

    in_specs = [pl.BlockSpec(memory_space=pl.ANY), row_spec(128), row_spec(D),
                norm_spec(layer, 3), mod_spec(layer, 5)]
    args = [ys, wts, xcat, normv, modv]
    out_specs = [row_spec(D)]
    n_row_tiles = NRT if has_next else SEQ // RT
    out_shape = [jax.ShapeDtypeStruct((nb, n_row_tiles * RT, D), F32)]
    if has_next:
        in_specs += [norm_spec(layer + 1, 0), mod_spec(layer + 1, 0), mod_spec(layer + 1, 1)]
        args += [normv, modv, modv]
        out_specs.append(row_spec(D))
        out_shape.append(jax.ShapeDtypeStruct((nb, LT, D), BF16))
    grid_spec = pltpu.PrefetchScalarGridSpec(
        num_scalar_prefetch=1,
        grid=(nb, n_row_tiles),
        in_specs=in_specs,
        out_specs=out_specs,
        scratch_shapes=[pltpu.VMEM((2, 2, RT * TOK_ROWS, 128), F32), pltpu.SemaphoreType.DMA((2,))],
    )
    return pl.pallas_call(
        functools.partial(_combine_kernel, has_next),
        grid_spec=grid_spec,
        out_shape=out_shape,
        compiler_params=_cparams(("arbitrary", "arbitrary")),
        name="moe_combine",
    )(pos, *args)


def _plan_kernel(e_ref, pos_ref, cnt_ref, rank_sc, carry_sc):
    phase = pl.program_id(0)
    i = pl.program_id(1)
    lane = lax.broadcasted_iota(jnp.int32, (1, 128), 1)
    e = e_ref[...]
    oh1 = lane == e[:, 0:1]
    oh2 = lane == e[:, 1:2]

    @pl.when(jnp.logical_and(phase == 0, i == 0))
    def _():
        carry_sc[...] = jnp.zeros_like(carry_sc)

    @pl.when(phase == 0)
    def _():
        s = jnp.where(jnp.logical_or(oh1, oh2), 1.0, 0.0)
        row = lax.broadcasted_iota(jnp.int32, (RT, RT), 0)
        col = lax.broadcasted_iota(jnp.int32, (RT, RT), 1)
        lower = jnp.where(col < row, 1.0, 0.0).astype(BF16)
        carry = carry_sc[0:1, :]
        before = _dot(lower, s.astype(BF16)) + carry
        r1 = jnp.sum(jnp.where(oh1, before, 0.0), axis=1, keepdims=True)
        r2 = jnp.sum(jnp.where(oh2, before, 0.0), axis=1, keepdims=True)
        rank_sc[i] = jnp.where(lane == 0, r1, jnp.where(lane == 1, r2, 0.0))
        carry_sc[...] = jnp.broadcast_to(carry + jnp.sum(s, axis=0, keepdims=True), carry_sc.shape)
        cnt_ref[...] = carry_sc[...]

    @pl.when(phase == 1)
    def _():
        counts = carry_sc[0:1, :]
        padded = jnp.floor((counts + (TE - 1)) * (1.0 / TE)) * TE
        r128 = lax.broadcasted_iota(jnp.int32, (128, 128), 0)
        c128 = lax.broadcasted_iota(jnp.int32, (128, 128), 1)
        upper = jnp.where(r128 < c128, 1.0, 0.0).astype(BF16)
        pa, pb, pc = _split3(jnp.broadcast_to(padded, (8, 128)))
        offs = (_dot(pa, upper) + _dot(pb, upper) + _dot(pc, upper))[0:1, :]
        rank = rank_sc[i]
        p1 = rank[:, 0:1] + jnp.sum(jnp.where(oh1, offs, 0.0), axis=1, keepdims=True)
        p2 = rank[:, 1:2] + jnp.sum(jnp.where(oh2, offs, 0.0), axis=1, keepdims=True)
        pos_ref[...] = jnp.where(lane == 0, p1, jnp.where(lane == 1, p2, 0.0)).astype(jnp.int32)


def _plan(eid, n_row_tiles):
    nb = eid.shape[0]
    nt = nb * n_row_tiles
    emap = lambda p, i: (i // n_row_tiles, i % n_row_tiles, 0)
    return pl.pallas_call(
        _plan_kernel,
        grid=(2, nt),
        in_specs=[pl.BlockSpec((None, RT, 128), emap)],
        out_specs=[pl.BlockSpec((RT, 128), lambda p, i: (p * i, 0)),
                   pl.BlockSpec((8, 128), lambda p, i: (0, 0))],
        out_shape=[jax.ShapeDtypeStruct((nt * RT, 128), jnp.int32), jax.ShapeDtypeStruct((8, 128), F32)],
        scratch_shapes=[pltpu.VMEM((nt, RT, 128), F32), pltpu.VMEM((8, 128), F32)],
        compiler_params=_cparams(("arbitrary", "arbitrary")),
        name="moe_plan",
    )(eid)


def _dispatch_tiles(pos, counts, n_tiles, n_row_tiles, nb):
    cnt = counts[0, :N_EXPERTS].astype(jnp.int32)
    padded = ((cnt + TE - 1) // TE) * TE
    ends = jnp.cumsum(padded)
    tile_start = jnp.arange(n_tiles, dtype=jnp.int32) * TE
    tile_expert = jnp.sum((ends[None, :] <= tile_start[:, None]).astype(jnp.int32), axis=1)
    tile_expert = jnp.minimum(tile_expert, N_EXPERTS - 1)
    n_used = (ends[-1] // TE).astype(jnp.int32).reshape(1)
    pos2 = pos[:, :2]
    tok = jnp.arange(nb * n_row_tiles * RT, dtype=jnp.int32)
    tok2 = jnp.broadcast_to(tok[:, None], pos2.shape)
    row_token = jnp.zeros((n_tiles * TE,), jnp.int32).at[pos2.reshape(-1)].set(tok2.reshape(-1))
    return pos2.reshape(-1) * TOK_ROWS, row_token * TOK_ROWS, tile_expert, n_used


def _rope_tables():
    tok = jnp.arange(SEQ)
    pos_row = (tok // GRID_W).astype(F32)
    pos_col = (tok % GRID_W).astype(F32)
    quarter = MLA_ROPE // 4
    inv_freq = ROPE_BASE ** (-jnp.arange(quarter, dtype=F32) / quarter)
    ang_r = pos_row[:, None] * inv_freq
    ang_c = pos_col[:, None] * inv_freq
    cos = jnp.concatenate([jnp.cos(ang_r)] * 2 + [jnp.cos(ang_c)] * 2, axis=1)
    sin = jnp.concatenate([jnp.sin(ang_r)] * 2 + [jnp.sin(ang_c)] * 2, axis=1)
    cos = jnp.concatenate([cos, jnp.ones((CTX, MLA_ROPE), F32)], axis=0)
    sin = jnp.concatenate([sin, jnp.zeros((CTX, MLA_ROPE), F32)], axis=0)
    pad = jnp.zeros((LT, 64), F32)
    cos_t = jnp.concatenate([jnp.ones((LT, 128), F32), cos, pad], axis=1)
    sin_t = jnp.concatenate([jnp.zeros((LT, 128), F32), sin, pad], axis=1)
    return cos_t, sin_t


def _rope_rot_matrix():
    j = jnp.arange(MLA_ROPE)
    first = (j % 32) < 16
    src = jnp.where(first, j + 16, j - 16)
    sign = jnp.where(first, -1.0, 1.0)
    return jnp.zeros((MLA_ROPE, MLA_ROPE), F32).at[src, j].set(sign)


def _layer_params(l, w_main, w_small, ml_f_bias, lb_all, hg_norm_g, mla_g_q, mla_g_kv, mla_w_uq, mla_w_ukv,
                  ssd_conv_w, ssd_conv_b, ssd_a_log, ssd_dt_bias, ssd_d, ssd_norm_g, w_gate, w_br, w_out,
                  moe_w_grp, moe_w_exp):
    p = {}
    p["w_main"] = w_main[l]
    p["w_small"] = w_small[l]
    fb = jnp.zeros((2, 2, NH), F32).at[:, 1, :].set(ml_f_bias[l]).reshape(16)
    ml_bias = jnp.concatenate([fb, jnp.zeros((112,), F32)])
    p["ml_bias_c"] = ml_bias.reshape(1, 128)
    p["ml_bias_t"] = ml_bias[:32].reshape(32, 1)
    dtb = jnp.concatenate([jnp.zeros((16,), F32), ssd_dt_bias[l].reshape(16), jnp.zeros((96,), F32)])
    p["ssd_bias_c"] = dtb.reshape(1, 128)
    p["ssd_bias_t"] = dtb[:32].reshape(32, 1)
    aneg = jnp.concatenate([jnp.zeros((16,), F32), -jnp.exp(ssd_a_log[l].astype(F32)).reshape(16),
                            jnp.zeros((96,), F32)])
    p["ssd_aneg_c"] = aneg.reshape(1, 128)
    p["ssd_aneg_t"] = aneg[:32].reshape(32, 1)
    p["lb"] = lb_all[l].reshape(1, BW)
    p["hg_norm_g"] = hg_norm_g[l].reshape(1, BW)
    p["ssd_dskip"] = jnp.repeat(ssd_d[l], SSD_P).reshape(1, BW)
    p["ssd_norm_g"] = ssd_norm_g[l].reshape(1, BW)
    p["conv_w"] = jnp.concatenate([ssd_conv_w[l], jnp.zeros((5, SSD_XBC), F32)], axis=0)
    p["conv_b"] = ssd_conv_b[l].reshape(1, SSD_XBC)
    rot = _rope_rot_matrix()
    wq = mla_w_uq[l].reshape(MLA_Q_RANK, NH, MLA_NOPE + MLA_ROPE)
    zq = jnp.zeros((MLA_Q_RANK, NH, 64), F32)
    wqa = jnp.concatenate([wq, zq], axis=2).reshape(MLA_Q_RANK, NH * 256)
    wq_rot = jnp.einsum("rhj,jk->rhk", wq[:, :, MLA_NOPE:], rot)
    wqb = jnp.concatenate([jnp.zeros((MLA_Q_RANK, NH, MLA_NOPE), F32), wq_rot, zq], axis=2)
    p["wqa"] = wqa.astype(BF16)
    p["wqb"] = wqb.reshape(MLA_Q_RANK, NH * 256).astype(BF16)
    wkv = mla_w_ukv[l].reshape(MLA_KV_RANK, NH, 2 * DH)
    wk = jnp.concatenate([wkv[:, :, :DH], jnp.zeros((MLA_KV_RANK, NH, DH), F32)], axis=2)
    p["wk"] = wk.reshape(MLA_KV_RANK, NH * 256).astype(BF16)
    p["wv"] = wkv[:, :, DH:].reshape(MLA_KV_RANK, NH * DH).astype(BF16)
    p["rk"] = jnp.zeros((128, 128), F32).at[:MLA_ROPE, :MLA_ROPE].set(rot).astype(BF16)
    p["gq"] = mla_g_q[l].reshape(1, MLA_Q_RANK)
    p["gkv"] = mla_g_kv[l].reshape(1, MLA_KV_RANK)
    p["w_gate"] = w_gate[l].astype(BF16)
    p["w_br"] = w_br[l].astype(BF16)
    p["w_out"] = w_out[l].astype(BF16)
    wr = jnp.concatenate([moe_w_grp[l], moe_w_exp[l], jnp.zeros((D, 128 - N_GROUPS - N_EXPERTS), F32)], axis=1)
    a = wr.astype(BF16)
    r = wr - a.astype(F32)
    b = r.astype(BF16)
    c = (r - b.astype(F32)).astype(BF16)
    p["wr3"] = (a, b, c)
    return p


def _mixer(h, p, cos_t, sin_t, nb, out_rows):
    t = nb * LT
    hf = h.reshape(t, D)
    zero_b = jnp.zeros((1, U_W), F32)
    u = _mm(hf, p["w_main"], zero_b, BF16, 1152, 1024).reshape(nb, LT, U_W)
    usm = _mm(hf, p["w_small"], zero_b[:, :128], F32, 1152, 128).reshape(nb, LT, 128)
    ust = usm[:, :, :32].reshape(nb, NCH, CS, 32).transpose(0, 1, 3, 2)
    ml_f = _mlstm(u, usm, ust, p["ml_bias_c"], p["ml_bias_t"], False)
    ml = _mlstm(u, usm, ust, p["ml_bias_c"], p["ml_bias_t"], True, ml_f)
    hg_f = _hgrn2(u, p["lb"], False)
    hg = _hgrn2(u, p["lb"], True, hg_f, p["hg_norm_g"])
    q, k, v = _mla_proj(u, p["gq"], p["gkv"], p["wqa"], p["wqb"], p["wk"], p["wv"], p["rk"], cos_t, sin_t)
    mla = _attention(q, k, v)
    xbc = _ssd_conv(u, p["conv_w"], p["conv_b"])
    sargs = (xbc, usm, ust, u, p["ssd_bias_c"], p["ssd_bias_t"], p["ssd_aneg_c"], p["ssd_aneg_t"])
    ssd_f = _ssd(*sargs, False)
    ssd = _ssd(*sargs, True, ssd_f, p["ssd_dskip"], p["ssd_norm_g"])
    return _merge(h, (ml, hg, mla, ssd), p["w_gate"], p["w_br"], out_rows)


def kernel(x, c, ctx, c_ctx, w_ada, b_ada, norm_g, w_in, ml_f_bias, hg_lb_logits, hg_norm_g, mla_g_q, mla_g_kv,
           mla_w_uq, mla_w_ukv, ssd_conv_w, ssd_conv_b, ssd_a_log, ssd_dt_bias, ssd_d, ssd_norm_g, w_gate,
           w_br, w_out, moe_w_grp, moe_w_exp, moe_w1, moe_w3, moe_w2):
    nb = x.shape[0]
    t = nb * LT
    lb_all = jnp.cumsum(jax.nn.softmax(hg_lb_logits.astype(F32), axis=0), axis=0)
    cvec = jnp.zeros((16, D), F32).at[:nb].set(c).at[8].set(c_ctx)
    s_vec = jax.nn.silu(cvec).astype(BF16)
    modv = _adaln_mods(s_vec, w_ada, b_ada).reshape(DEPTH * 16 * 6, 1, D)
    normv = norm_g.reshape(DEPTH * 4, 1, D)
    cos_t, sin_t = _rope_tables()
    w_main, w_small = _win_layout(w_in)
    w1 = moe_w1.reshape(DEPTH * N_EXPERTS, D, D_EXPERT)
    w3 = moe_w3.reshape(DEPTH * N_EXPERTS, D, D_EXPERT)
    w2 = moe_w2.reshape(DEPTH * N_EXPERTS, D_EXPERT, D)
    h = _prenorm(x, ctx, normv, modv, 0)
    xs = (x, ctx)
    for l in range(DEPTH):
        p = _layer_params(l, w_main, w_small, ml_f_bias, lb_all, hg_norm_g, mla_g_q, mla_g_kv, mla_w_uq,
                          mla_w_ukv, ssd_conv_w, ssd_conv_b, ssd_a_log, ssd_dt_bias, ssd_d, ssd_norm_g, w_gate,
                          w_br, w_out, moe_w_grp, moe_w_exp)
        rows = LT if l + 1 < DEPTH else SEQ
        n_row_tiles = rows // RT
        acc = _mixer(h, p, cos_t, sin_t, nb, rows)
        xcat, hp, eid, wts = _outproj(acc, p["w_out"], xs, normv, modv, p["wr3"], l)
        n_tiles = (2 * nb * rows + N_EXPERTS * (TE - 1)) // TE + 1
        pos128, counts = _plan(eid, n_row_tiles)
        pos, row_token, tile_expert, n_used = _dispatch_tiles(pos128, counts, n_tiles, n_row_tiles, nb)
        ys = _experts(tile_expert, n_used, row_token, hp.reshape(nb * rows * TOK_ROWS, 128), w1, w3, w2, n_tiles, l)
        outs = _combine(pos, ys, wts, xcat, normv, modv, l)
        if l + 1 < DEPTH:
            xcat, h = outs
            xs = (xcat,)
    return outs[0]
```

```python
import functools

import jax
import jax.numpy as jnp
from jax import lax
from jax.experimental import pallas as pl
from jax.experimental.pallas import tpu as pltpu

F32 = jnp.float32
BF16 = jnp.bfloat16

D = 2048
SEQ = 2048
CTX = 256
LT = SEQ + CTX
DEPTH = 2
EPS = 1e-6
GRID_W = 64
NH = 4
DH = 128
BW = 512
MLA_Q_RANK = 512
MLA_KV_RANK = 256
MLA_NOPE = 128
MLA_ROPE = 64
MLA_SCALE = (MLA_NOPE + MLA_ROPE) ** -0.5
ROPE_BASE = 10000.0
SSD_HEADS = 8
SSD_P = 64
SSD_GROUPS = 2
SSD_N = 128
SSD_XBC = 1024
N_GROUPS = 4
EXP_PER_GROUP = 8
N_EXPERTS = 32
D_EXPERT = 512

CS = 128
NLC = SEQ // CS
NCC = CTX // CS
NCH = NLC + NCC
RT = 256
NRT = LT // RT
TE = 256
VMEM_LIMIT = 52 * 1024 * 1024

U_W = 7168
CB_ML_Q, CB_ML_K, CB_ML_V, CB_ML_O = 0, 1, 2, 3
CB_HG_Q, CB_HG_I, CB_HG_G, CB_HG_F = 4, 5, 6, 7
CB_SSD_Z = 9
CB_SSD_XBC = 5
CB_MLA_CQ = 12
CB_MLA_CKV = 26
CB_MLA_KR = 54


def _cparams(sem):
    return pltpu.CompilerParams(dimension_semantics=sem, vmem_limit_bytes=VMEM_LIMIT)


def _dot(a, b):
    return jnp.dot(a, b, preferred_element_type=F32)


def _dot_nt(a, b):
    return lax.dot_general(a, b, (((1,), (1,)), ((), ())), preferred_element_type=F32)


def _sigmoid(x):
    return 1.0 / (1.0 + jnp.exp(-x))


def _silu(x):
    return x * _sigmoid(x)


def _softplus(x):
    return jnp.maximum(x, 0.0) + jnp.log(1.0 + jnp.exp(-jnp.abs(x)))


def _log_sigmoid(x):
    return -_softplus(-x)


def _rms(x):
    return x * lax.rsqrt(jnp.mean(x * x, axis=-1, keepdims=True) + EPS)


def _split3(x):
    a = x.astype(BF16)
    r = x - a.astype(F32)
    b = r.astype(BF16)
    c = (r - b.astype(F32)).astype(BF16)
    return a, b, c


def _cumsum_rows(mb, x):
    a, b, c = _split3(x)
    return _dot(mb, a) + _dot(mb, b) + _dot(mb, c)


def _cumsum_cols(x, mb):
    a, b, c = _split3(x)
    return _dot_nt(a, mb) + _dot_nt(b, mb) + _dot_nt(c, mb)


def _chunk_mask(rev):
    row = lax.broadcasted_iota(jnp.int32, (CS, CS), 0)
    col = lax.broadcasted_iota(jnp.int32, (CS, CS), 1)
    return (col >= row) if rev else (col <= row)


def _chunk_of_step(rev):
    if rev:
        return lambda s: NCH - 1 - s
    return lambda s: (s + NLC) % NCH


def _mm_kernel(a_ref, w_ref, b_ref, o_ref):
    acc = _dot(a_ref[...], w_ref[...].astype(BF16))
    o_ref[...] = (acc + b_ref[...]).astype(o_ref.dtype)


def _mm(a, w, bias, out_dtype, tm, tn):
    m, k = a.shape
    n = w.shape[1]
    return pl.pallas_call(
        _mm_kernel,
        grid=(m // tm, n // tn),
        in_specs=[pl.BlockSpec((tm, k), lambda i, j: (i, 0)),
                  pl.BlockSpec((k, tn), lambda i, j: (0, j)),
                  pl.BlockSpec((1, tn), lambda i, j: (0, j))],
        out_specs=pl.BlockSpec((tm, tn), lambda i, j: (i, j)),
        out_shape=jax.ShapeDtypeStruct((m, n), out_dtype),
        compiler_params=_cparams(("parallel", "arbitrary")),
        name="mm",
    )(a, w, bias)


W_IN_PIECES = ((0, 0, 2048),
               (2048, 2064, 2560),
               (4608, 5456, 512),
               (5120, 5968, 1024),
               (6144, 4624, 512),
               (6656, 5136, 256),
               (6912, 5392, 64),
               (7040, 2048, 16),
               (7056, 6992, 16))
D_IN = 7008


def _win_kernel(w_ref, o_ref, s_ref):
    o_ref[...] = jnp.zeros_like(o_ref)
    for dst, src, width in W_IN_PIECES:
        o_ref[:, dst:dst + width] = w_ref[:, src:src + width].astype(BF16)
    s_ref[...] = o_ref[:, U_W - 128:]


def _win_layout(w_in):
    rows = 256
    return pl.pallas_call(
        _win_kernel,
        grid=(DEPTH, D // rows),
        in_specs=[pl.BlockSpec((None, rows, D_IN), lambda l, i: (l, i, 0))],
        out_specs=[pl.BlockSpec((None, rows, U_W), lambda l, i: (l, i, 0)),
                   pl.BlockSpec((None, rows, 128), lambda l, i: (l, i, 0))],
        out_shape=[jax.ShapeDtypeStruct((DEPTH, D, U_W), BF16), jax.ShapeDtypeStruct((DEPTH, D, 128), BF16)],
        compiler_params=_cparams(("parallel", "parallel")),
        name="win_layout",
    )(w_in)


def _adaln_mods(s_vec, w_ada, b_ada):
    tn = 1024
    return pl.pallas_call(
        _mm_kernel,
        grid=(DEPTH, 6 * D // tn),
        in_specs=[pl.BlockSpec((16, D), lambda l, j: (0, 0)),
                  pl.BlockSpec((None, D, tn), lambda l, j: (l, 0, j)),
                  pl.BlockSpec((None, 1, tn), lambda l, j: (l, 0, j))],
        out_specs=pl.BlockSpec((None, 16, tn), lambda l, j: (l, 0, j)),
        out_shape=jax.ShapeDtypeStruct((DEPTH, 16, 6 * D), F32),
        compiler_params=_cparams(("parallel", "arbitrary")),
        name="adaln_mods",
    )(s_vec, w_ada, b_ada.reshape(DEPTH, 1, 6 * D))


def _mod_spec(layer, k):
    def imap(b, i):
        row = jnp.where(i == NRT - 1, 8, b)
        return ((layer * 16 + row) * 6 + k, 0, 0)
    return pl.BlockSpec((None, 1, D), imap)


def _norm_spec(layer, k):
    return pl.BlockSpec((None, 1, D), lambda b, i: (layer * 4 + k, 0, 0))


def _row_spec(width, dtype_rows=RT):
    return pl.BlockSpec((None, dtype_rows, width), lambda b, i: (b, i, 0))


def _split_x_specs():
    return [pl.BlockSpec((None, RT, D), lambda b, i: (b, jnp.minimum(i, NRT - 2), 0)),
            pl.BlockSpec((None, RT, D), lambda b, i: (b, 0, 0))]


def _pick_x(x_ref, c_ref):
    return jnp.where(pl.program_id(1) == NRT - 1, c_ref[...], x_ref[...])


def _prenorm_kernel(x_ref, c_ref, g_ref, sh_ref, sc_ref, h_ref):
    y = _rms(_pick_x(x_ref, c_ref)) * g_ref[...]
    h_ref[...] = (y * (1.0 + sc_ref[...]) + sh_ref[...]).astype(BF16)


def _prenorm(x, ctx, normv, modv, layer):
    nb = x.shape[0]
    return pl.pallas_call(
        _prenorm_kernel,
        grid=(nb, NRT),
        in_specs=_split_x_specs() + [_norm_spec(layer, 0), _mod_spec(layer, 0), _mod_spec(layer, 1)],
        out_specs=_row_spec(D),
        out_shape=jax.ShapeDtypeStruct((nb, LT, D), BF16),
        compiler_params=_cparams(("parallel", "parallel")),
        name="prenorm",
    )(x, ctx, normv, modv, modv)


def _conv_kernel(x_ref, p_ref, n_ref, w_ref, b_ref, o_ref):
    i = pl.program_id(1)
    x = x_ref[...].astype(F32)
    has_prev = jnp.logical_and(i != 0, i != NRT - 1).astype(F32)
    has_next = jnp.logical_and(i != NRT - 2, i != NRT - 1).astype(F32)
    prev_row = p_ref[15:16, :].astype(F32) * has_prev
    next_row = n_ref[0:1, :].astype(F32) * has_next
    ridx = lax.broadcasted_iota(jnp.int32, (RT, 1), 0)
    xm1 = jnp.where(ridx == 0, prev_row, pltpu.roll(x, 1, 0))
    xp1 = jnp.where(ridx == RT - 1, next_row, pltpu.roll(x, RT - 1, 0))
    w = w_ref[...]
    y = w[0:1, :] * xm1 + w[1:2, :] * x + w[2:3, :] * xp1 + b_ref[...]
    o_ref[...] = _silu(y)


def _ssd_conv(u, conv_w, conv_b):
    nb = u.shape[0]
    hb = RT // 16
    last = LT // 16 - 1
    return pl.pallas_call(
        _conv_kernel,
        grid=(nb, NRT),
        in_specs=[pl.BlockSpec((None, RT, SSD_XBC), lambda b, i: (b, i, CB_SSD_XBC)),
                  pl.BlockSpec((None, 16, SSD_XBC), lambda b, i: (b, jnp.maximum(i * hb - 1, 0), CB_SSD_XBC)),
                  pl.BlockSpec((None, 16, SSD_XBC), lambda b, i: (b, jnp.minimum((i + 1) * hb, last), CB_SSD_XBC)),
                  pl.BlockSpec((8, SSD_XBC), lambda b, i: (0, 0)),
                  pl.BlockSpec((1, SSD_XBC), lambda b, i: (0, 0))],
        out_specs=pl.BlockSpec((None, RT, SSD_XBC), lambda b, i: (b, i, 0)),
        out_shape=jax.ShapeDtypeStruct((nb, LT, SSD_XBC), F32),
        compiler_params=_cparams(("parallel", "parallel")),
        name="ssd_conv",
    )(u, u, u, conv_w, conv_b)


SCAN_BB = 2


def _mlstm_kernel(rev, bpb, *refs):
    if rev:
        (qs_ref, ks_ref, vs_ref, gc_ref, gt_ref, bc_ref, bt_ref, yps_ref, ogs_ref,
         os_ref, c_sc, n_sc, m_sc) = refs
    else:
        (qs_ref, ks_ref, vs_ref, gc_ref, gt_ref, bc_ref, bt_ref,
         os_ref, c_sc, n_sc, m_sc) = refs
    d = 1 if rev else 0

    @pl.when(pl.program_id(1) == 0)
    def _():
        c_sc[...] = jnp.zeros_like(c_sc)
        n_sc[...] = jnp.zeros_like(n_sc)
        m_sc[...] = jnp.full(m_sc.shape, -jnp.inf, F32)

    mask = _chunk_mask(rev)
    mb = jnp.where(mask, 1.0, 0.0).astype(BF16)
    scale = DH ** -0.5
    for bb, h in [(bb, h) for bb in range(bpb) for h in range(NH)]:
        if h == 0:
            gc = gc_ref[bb] + bc_ref[...]
            gt = gt_ref[bb] + bt_ref[...]
            flog_c = _log_sigmoid(gc)
            flog_t = _log_sigmoid(gt)
            fcum_c = _cumsum_rows(mb, flog_c)
            fcum_t = _cumsum_cols(flog_t, mb)
        q_ref, k_ref, v_ref, o_ref = qs_ref.at[bb], ks_ref.at[bb], vs_ref.at[bb], os_ref.at[bb]
        if rev:
            yp_ref, og_ref = yps_ref.at[bb], ogs_ref.at[bb]
        c_h, n_h, m_h = c_sc.at[bb * NH + h], n_sc.at[bb * NH + h], m_sc.at[bb * NH + h]
        ci = d * 8 + h
        cf = d * 8 + 4 + h
        sl = slice(h * DH, (h + 1) * DH)
        q = q_ref[:, sl]
        k = k_ref[:, sl]
        v = v_ref[:, sl]
        i_col = gc[:, ci:ci + 1]
        f_col = flog_c[:, cf:cf + 1]
        fc_col = fcum_c[:, cf:cf + 1]
        i_row = gt[ci:ci + 1, :]
        fc_row = fcum_t[cf:cf + 1, :]
        m_prev = m_h[0:1, 0:1]
        dm = jnp.where(mask, fc_col - fc_row + i_row, -jnp.inf)
        from_state = fc_col + m_prev
        m_t = jnp.maximum(from_state, jnp.max(dm, axis=1, keepdims=True))
        s = _dot_nt(q, k) * scale * jnp.exp(dm - m_t)
        w_state = jnp.exp(from_state - m_t)
        qc = _dot(q, c_h[...].astype(BF16)) * scale
        num = _dot(s.astype(BF16), v) + w_state * qc
        qn = jnp.sum(q.astype(F32) * n_h[0:1, :], axis=1, keepdims=True) * scale
        den = jnp.sum(s, axis=1, keepdims=True) + w_state * qn
        hout = num / jnp.maximum(jnp.abs(den), jnp.exp(-m_t))
        if rev:
            tot = yp_ref[:, sl] + hout
            o_ref[:, sl] = (tot * _sigmoid(og_ref[:, sl].astype(F32))).astype(o_ref.dtype)
        else:
            o_ref[:, sl] = hout
        f_tot = jnp.sum(f_col, axis=0, keepdims=True)
        w_log = f_tot - fc_col + i_col
        m_new = jnp.maximum(f_tot + m_prev, jnp.max(w_log, axis=0, keepdims=True))
        decay = jnp.exp(f_tot + m_prev - m_new)
        kw = k.astype(F32) * jnp.exp(w_log - m_new)
        c_h[...] = decay * c_h[...] + _dot(kw.T.astype(BF16), v)
        n_new = decay * n_h[0:1, :] + jnp.sum(kw, axis=0, keepdims=True)
        n_h[...] = jnp.broadcast_to(n_new, (8, DH))
        m_h[...] = jnp.broadcast_to(m_new, (8, DH))


def _u_spec(width, cblk, cmap):
    return pl.BlockSpec((None, CS, width), lambda b, s: (b, cmap(s), cblk))


def _mlstm(u, usm, ust, bias_c, bias_t, rev, yprev=None):
    nb = u.shape[0]
    bpb = SCAN_BB if nb % SCAN_BB == 0 else 1
    cmap = _chunk_of_step(rev)
    uspec = lambda cblk: pl.BlockSpec((bpb, CS, BW), lambda b, s: (b, cmap(s), cblk))
    in_specs = [uspec(CB_ML_Q), uspec(CB_ML_K), uspec(CB_ML_V),
                pl.BlockSpec((bpb, CS, 128), lambda b, s: (b, cmap(s), 0)),
                pl.BlockSpec((bpb, None, 32, CS), lambda b, s: (b, cmap(s), 0, 0)),
                pl.BlockSpec((1, 128), lambda b, s: (0, 0)),
                pl.BlockSpec((32, 1), lambda b, s: (0, 0))]
    args = [u, u, u, usm, ust, bias_c, bias_t]
    if rev:
        in_specs += [uspec(0), uspec(CB_ML_O)]
        args += [yprev, u]
    return pl.pallas_call(
        functools.partial(_mlstm_kernel, rev, bpb),
        grid=(nb // bpb, NCH),
        in_specs=in_specs,
        out_specs=uspec(0),
        out_shape=jax.ShapeDtypeStruct((nb, LT, BW), BF16 if rev else F32),
        scratch_shapes=[pltpu.VMEM((bpb * NH, DH, DH), F32), pltpu.VMEM((bpb * NH, 8, DH), F32),
                        pltpu.VMEM((bpb * NH, 8, DH), F32)],
        compiler_params=_cparams(("parallel", "arbitrary")),
        name="mlstm_rev" if rev else "mlstm_fwd",
    )(*args)


def _block_ref(g, n, off):
    g3 = g.reshape(CS // n, n, DH)
    r = g3[:, off:off + 1, :]
    return jnp.broadcast_to(r, (CS // n, n, DH)).reshape(CS, DH)


def _hgrn2_kernel(rev, *refs):
    if rev:
        q_ref, i_ref, f_ref, lb_ref, yp_ref, g_ref, ng_ref, o_ref, st_sc = refs
    else:
        q_ref, i_ref, f_ref, lb_ref, o_ref, st_sc = refs

    @pl.when(pl.program_id(1) == 0)
    def _():
        st_sc[...] = jnp.zeros_like(st_sc)

    mask = _chunk_mask(rev)
    mb = jnp.where(mask, 1.0, 0.0).astype(BF16)
    row = lax.broadcasted_iota(jnp.int32, (CS, CS), 0)
    col = lax.broadcasted_iota(jnp.int32, (CS, CS), 1)
    tix = lax.broadcasted_iota(jnp.int32, (CS, 1), 0)
    for h in range(NH):
        sl = slice(h * DH, (h + 1) * DH)
        q = _silu(q_ref[:, sl].astype(F32))
        v = i_ref[:, sl].astype(F32)
        lb = lb_ref[:, sl]
        f = lb + (1.0 - lb) * _sigmoid(f_ref[:, sl].astype(F32))
        k = 1.0 - f
        lf = jnp.log(f)
        g = _cumsum_rows(mb, lf)
        st = st_sc[h]
        o = _dot_nt((q * jnp.exp(g)).astype(BF16), st.astype(BF16))
        a = jnp.zeros((CS, CS), F32)
        n = CS
        while n >= 16:
            half = n // 2
            second = jnp.bitwise_and(tix, n - 1) >= half
            rowside = jnp.logical_not(second) if rev else second
            r = _block_ref(g, n, half if rev else half - 1)
            e = jnp.exp(jnp.where(rowside, g - r, r - g))
            qs = jnp.where(rowside, q * e, 0.0).astype(BF16)
            ks = jnp.where(rowside, 0.0, k * e).astype(BF16)
            sh_n = n.bit_length() - 1
            same = jnp.right_shift(row, sh_n) == jnp.right_shift(col, sh_n)
            a = a + jnp.where(same, _dot_nt(qs, ks), 0.0)
            n = half
        o = o + _dot(a.astype(BF16), v.astype(BF16))
        t8 = jnp.bitwise_and(tix, 7)
        k3, g3, v3 = (a3.reshape(CS // 8, 8, DH) for a3 in (k, g, v))
        for dl in range(8):
            if dl == 0:
                ksh, gsh, vsh = k, g, v
            else:
                sh = (8 - dl) if rev else dl
                ksh = pltpu.roll(k3, sh, 1).reshape(CS, DH)
                gsh = pltpu.roll(g3, sh, 1).reshape(CS, DH)
                vsh = pltpu.roll(v3, sh, 1).reshape(CS, DH)
            valid = (t8 + dl <= 7) if rev else (t8 >= dl)
            e = jnp.exp(jnp.minimum(g - gsh, 0.0))
            coef = jnp.sum(q * ksh * e, axis=1, keepdims=True)
            o = o + jnp.where(valid, coef, 0.0) * vsh
        if rev:
            tot = yp_ref[:, sl] + o
            y = _rms(tot) * ng_ref[:, sl]
            o_ref[:, sl] = (y * _sigmoid(g_ref[:, sl].astype(F32))).astype(o_ref.dtype)
        else:
            o_ref[:, sl] = o
        g_tot = jnp.sum(lf, axis=0, keepdims=True)
        kdec = k * jnp.exp(g_tot - g)
        st_sc[h] = st * jnp.exp(g_tot) + _dot(v.T.astype(BF16), kdec.astype(BF16))


def _hgrn2(u, lb, rev, yprev=None, norm_g=None):
    nb = u.shape[0]
    cmap = _chunk_of_step(rev)
    d = 1 if rev else 0
    in_specs = [_u_spec(BW, CB_HG_Q, cmap), _u_spec(BW, CB_HG_I, cmap), _u_spec(BW, CB_HG_F + d, cmap),
                pl.BlockSpec((1, BW), lambda b, s: (0, 0))]
    args = [u, u, u, lb]
    if rev:
        in_specs += [pl.BlockSpec((None, CS, BW), lambda b, s: (b, cmap(s), 0)),
                     _u_spec(BW, CB_HG_G, cmap),
                     pl.BlockSpec((1, BW), lambda b, s: (0, 0))]
        args += [yprev, u, norm_g]
    return pl.pallas_call(
        functools.partial(_hgrn2_kernel, rev),
        grid=(nb, NCH),
        in_specs=in_specs,
        out_specs=pl.BlockSpec((None, CS, BW), lambda b, s: (b, cmap(s), 0)),
        out_shape=jax.ShapeDtypeStruct((nb, LT, BW), BF16 if rev else F32),
        scratch_shapes=[pltpu.VMEM((NH, DH, DH), F32)],
        compiler_params=_cparams(("parallel", "arbitrary")),
        name="hgrn2_rev" if rev else "hgrn2_fwd",
    )(*args)


def _expand_heads(cols, lane_head):
    r = cols[3]
    for hh in (2, 1, 0):
        r = jnp.where(lane_head == hh, cols[hh], r)
    return r


def _ssd_kernel(rev, *refs):
    if rev:
        (x_ref, gc_ref, gt_ref, bc_ref, bt_ref, ac_ref, at_ref, yp_ref, z_ref, dk_ref, ng_ref,
         o_ref, ht_sc) = refs
    else:
        x_ref, gc_ref, gt_ref, bc_ref, bt_ref, ac_ref, at_ref, o_ref, ht_sc = refs
    d = 1 if rev else 0

    @pl.when(pl.program_id(1) == 0)
    def _():
        ht_sc[...] = jnp.zeros_like(ht_sc)

    mask = _chunk_mask(rev)
    mb = jnp.where(mask, 1.0, 0.0).astype(BF16)
    dt_c = _softplus(gc_ref[...] + bc_ref[...])
    da_c = dt_c * ac_ref[...]
    acum_c = _cumsum_rows(mb, da_c)
    dt_t = _softplus(gt_ref[...] + bt_ref[...])
    da_t = dt_t * at_ref[...]
    acum_t = _cumsum_cols(da_t, mb)
    gw = 4 * SSD_P
    lane_head = jnp.right_shift(lax.broadcasted_iota(jnp.int32, (1, gw), 1), 6)
    ys = []
    for grp in range(SSD_GROUPS):
        xg = x_ref[:, grp * gw:(grp + 1) * gw]
        bm = x_ref[:, BW + grp * SSD_N:BW + (grp + 1) * SSD_N]
        cm = x_ref[:, BW + SSD_GROUPS * SSD_N + grp * SSD_N:BW + SSD_GROUPS * SSD_N + (grp + 1) * SSD_N]
        cmb = cm.astype(BF16)
        cb = _dot_nt(cmb, bm.astype(BF16))
        ht = ht_sc[grp]
        y_state = _dot(cmb, ht.astype(BF16))
        lanes = [16 + d * 8 + grp * 4 + hh for hh in range(4)]
        dt_cols = [dt_c[:, l:l + 1] for l in lanes]
        a_cols = [acum_c[:, l:l + 1] for l in lanes]
        a_tots = [jnp.sum(da_c[:, l:l + 1], axis=0, keepdims=True) for l in lanes]
        xdt = xg * _expand_heads(dt_cols, lane_head)
        xdtb = xdt.astype(BF16)
        y = jnp.exp(_expand_heads(a_cols, lane_head)) * y_state
        for hh in range(4):
            l = lanes[hh]
            dec = jnp.exp(jnp.where(mask, a_cols[hh] - acum_t[l:l + 1, :], -jnp.inf))
            yh = _dot((cb * dec).astype(BF16), xdtb)
            y = y + jnp.where(lane_head == hh, yh, 0.0)
        ys.append(y)
        w_exp = jnp.exp(_expand_heads([a_tots[hh] - a_cols[hh] for hh in range(4)], lane_head))
        e_row = jnp.exp(_expand_heads(a_tots, lane_head))
        ht_sc[grp] = ht * e_row + _dot(bm.T.astype(BF16), (xdt * w_exp).astype(BF16))
    y = jnp.concatenate(ys, axis=1)
    if rev:
        tot = yp_ref[...] + y + dk_ref[...] * x_ref[:, :BW]
        tot = tot * _silu(z_ref[...].astype(F32))
        o_ref[...] = (_rms(tot) * ng_ref[...]).astype(o_ref.dtype)
    else:
        o_ref[...] = y


def _ssd(xbc, usm, ust, u, bias_c, bias_t, aneg_c, aneg_t, rev, yprev=None, dskip=None, norm_g=None):
    nb = xbc.shape[0]
    cmap = _chunk_of_step(rev)
    in_specs = [pl.BlockSpec((None, CS, SSD_XBC), lambda b, s: (b, cmap(s), 0)),
                pl.BlockSpec((None, CS, 128), lambda b, s: (b, cmap(s), 0)),
                pl.BlockSpec((None, None, 32, CS), lambda b, s: (b, cmap(s), 0, 0)),
                pl.BlockSpec((1, 128), lambda b, s: (0, 0)),
                pl.BlockSpec((32, 1), lambda b, s: (0, 0)),
                pl.BlockSpec((1, 128), lambda b, s: (0, 0)),
                pl.BlockSpec((32, 1), lambda b, s: (0, 0))]
    args = [xbc, usm, ust, bias_c, bias_t, aneg_c, aneg_t]
    if rev:
        in_specs += [pl.BlockSpec((None, CS, BW), lambda b, s: (b, cmap(s), 0)),
                     _u_spec(BW, CB_SSD_Z, cmap),
                     pl.BlockSpec((1, BW), lambda b, s: (0, 0)),
                     pl.BlockSpec((1, BW), lambda b, s: (0, 0))]
        args += [yprev, u, dskip, norm_g]
    return pl.pallas_call(
        functools.partial(_ssd_kernel, rev),
        grid=(nb, NCH),
        in_specs=in_specs,
        out_specs=pl.BlockSpec((None, CS, BW), lambda b, s: (b, cmap(s), 0)),
        out_shape=jax.ShapeDtypeStruct((nb, LT, BW), BF16 if rev else F32),
        scratch_shapes=[pltpu.VMEM((SSD_GROUPS, SSD_N, 4 * SSD_P), F32)],
        compiler_params=_cparams(("parallel", "arbitrary")),
        name="ssd_rev" if rev else "ssd_fwd",
    )(*args)


def _mla_proj_kernel(cq_ref, ckv_ref, kr_ref, gq_ref, gkv_ref, wqa_ref, wqb_ref, wk_ref, wv_ref,
                     rk_ref, cos_ref, sin_ref, q_ref, k_ref, v_ref):
    nq = (_rms(cq_ref[...].astype(F32)) * gq_ref[...]).astype(BF16)
    qa = _dot(nq, wqa_ref[...])
    qb = _dot(nq, wqb_ref[...])
    cos_t = cos_ref[...]
    sin_t = sin_ref[...]
    cosf = jnp.concatenate([cos_t] * NH, axis=1)
    sinf = jnp.concatenate([sin_t] * NH, axis=1)
    q_ref[...] = ((qa * cosf + qb * sinf) * MLA_SCALE).astype(BF16)
    nkv = (_rms(ckv_ref[...].astype(F32)) * gkv_ref[...]).astype(BF16)
    kn = _dot(nkv, wk_ref[...])
    v_ref[...] = _dot(nkv, wv_ref[...]).astype(BF16)
    kr = kr_ref[...]
    krr = kr.astype(F32) * cos_t[:, 128:256] + _dot(kr, rk_ref[...]) * sin_t[:, 128:256]
    zero = jnp.zeros_like(krr)
    k_ref[...] = (kn + jnp.concatenate([zero, krr] * NH, axis=1)).astype(BF16)


def _mla_proj(u, gq, gkv, wqa, wqb, wk, wv, rk, cos_t, sin_t):
    nb = u.shape[0]
    const = lambda shape: pl.BlockSpec(shape, lambda b, i: (0, 0))
    return pl.pallas_call(
        _mla_proj_kernel,
        grid=(nb, NRT),
        in_specs=[pl.BlockSpec((None, RT, MLA_Q_RANK), lambda b, i: (b, i, CB_MLA_CQ)),
                  pl.BlockSpec((None, RT, MLA_KV_RANK), lambda b, i: (b, i, CB_MLA_CKV)),
                  pl.BlockSpec((None, RT, 128), lambda b, i: (b, i, CB_MLA_KR)),
                  const((1, MLA_Q_RANK)), const((1, MLA_KV_RANK)),
                  const((MLA_Q_RANK, NH * 256)), const((MLA_Q_RANK, NH * 256)),
                  const((MLA_KV_RANK, NH * 256)), const((MLA_KV_RANK, NH * DH)),
                  const((128, 128)),
                  pl.BlockSpec((RT, 256), lambda b, i: (i, 0)),
                  pl.BlockSpec((RT, 256), lambda b, i: (i, 0))],
        out_specs=[_row_spec(NH * 256), _row_spec(NH * 256), _row_spec(NH * DH)],
        out_shape=[jax.ShapeDtypeStruct((nb, LT, NH * 256), BF16),
                   jax.ShapeDtypeStruct((nb, LT, NH * 256), BF16),
                   jax.ShapeDtypeStruct((nb, LT, NH * DH), BF16)],
        compiler_params=_cparams(("parallel", "parallel")),
        name="mla_proj",
    )(u, u, u, gq, gkv, wqa, wqb, wk, wv, rk, cos_t, sin_t)


AT = 768


def _attn_kernel(q_ref, k_ref, v_ref, o_ref):
    s = _dot_nt(q_ref[...], k_ref[...])

    def finish(sc):
        m = jnp.max(sc, axis=1, keepdims=True)
        p = jnp.exp(sc - m)
        l = jnp.sum(p, axis=1, keepdims=True)
        o_ref[...] = (_dot(p.astype(BF16), v_ref[...]) / l).astype(o_ref.dtype)

    last = pl.program_id(2) == LT // AT - 1

    @pl.when(last)
    def _():
        qrow = lax.broadcasted_iota(jnp.int32, (AT, 1), 0)
        kcol = lax.broadcasted_iota(jnp.int32, (1, LT), 1)
        hide = jnp.logical_and(qrow >= AT - CTX, kcol < SEQ)
        finish(jnp.where(hide, -jnp.inf, s))

    @pl.when(jnp.logical_not(last))
    def _():
        finish(s)


def _attention(q, k, v):
    nb = q.shape[0]
    return pl.pallas_call(
        _attn_kernel,
        grid=(nb, NH, LT // AT),
        in_specs=[pl.BlockSpec((None, AT, 256), lambda b, h, i: (b, i, h)),
                  pl.BlockSpec((None, LT, 256), lambda b, h, i: (b, 0, h)),
                  pl.BlockSpec((None, LT, DH), lambda b, h, i: (b, 0, h))],
        out_specs=pl.BlockSpec((None, AT, DH), lambda b, h, i: (b, i, h)),
        out_shape=jax.ShapeDtypeStruct((nb, LT, NH * DH), BF16),
        compiler_params=_cparams(("parallel", "parallel", "arbitrary")),
        name="mla_attn",
    )(q, k, v)


def _merge_kernel(h_ref, b0_ref, b1_ref, b2_ref, b3_ref, wg_ref, wb_ref, o_ref):
    h = h_ref[...]
    acc = None
    for kk, br in enumerate((b0_ref, b1_ref, b2_ref, b3_ref)):
        gate = _sigmoid(_dot(h, wg_ref[kk]))
        term = gate * _dot(br[...], wb_ref[kk])
        acc = term if acc is None else acc + term
    o_ref[...] = acc.astype(o_ref.dtype)


def _merge(h, branches, wg, wb, rows, tn=256):
    nb = h.shape[0]
    tm = rows // 2
    rmap = lambda i, j: (i // 2, i % 2, 0)
    bspec = pl.BlockSpec((None, tm, BW), rmap)
    return pl.pallas_call(
        _merge_kernel,
        grid=(nb * 2, D // tn),
        in_specs=[pl.BlockSpec((None, tm, D), rmap), bspec, bspec, bspec, bspec,
                  pl.BlockSpec((4, D, tn), lambda i, j: (0, 0, j)),
                  pl.BlockSpec((4, BW, tn), lambda i, j: (0, 0, j))],
        out_specs=pl.BlockSpec((None, tm, tn), lambda i, j: (i // 2, i % 2, j)),
        out_shape=jax.ShapeDtypeStruct((nb, rows, D), BF16),
        compiler_params=_cparams(("parallel", "arbitrary")),
        name="merge",
    )(h, *branches, wg, wb)


def _outproj_kernel(first, a_ref, w_ref, *refs):
    if first:
        x_ref, c_ref = refs[:2]
        x_in = _pick_x(x_ref, c_ref)
        refs = refs[2:]
    else:
        x_in = refs[0][...]
        refs = refs[1:]
    g1_ref, gate_ref, g2_ref, sh_ref, sc_ref, wa_ref, wb_ref, wc_ref, xo_ref, hp_ref, e_ref, wt_ref = refs
    y = _dot(a_ref[...], w_ref[...])
    xn = x_in + gate_ref[...] * (_rms(y) * g1_ref[...])
    xo_ref[...] = xn
    h2 = _rms(xn) * g2_ref[...]
    h2 = h2 * (1.0 + sc_ref[...]) + sh_ref[...]
    hp_ref[...] = h2
    _route_math(h2, wa_ref, wb_ref, wc_ref, e_ref, wt_ref)


def _outproj(acc, w_out, xs, normv, modv, wr3, layer):
    nb, rows, _ = acc.shape
    first = len(xs) == 2
    wspec = pl.BlockSpec((D, 128), lambda b, i: (0, 0))
    return pl.pallas_call(
        functools.partial(_outproj_kernel, first),
        grid=(nb, rows // RT),
        in_specs=[_row_spec(D), pl.BlockSpec((D, D), lambda b, i: (0, 0))]
                 + (_split_x_specs() if first else [_row_spec(D)])
                 + [_norm_spec(layer, 1), _mod_spec(layer, 2), _norm_spec(layer, 2),
                    _mod_spec(layer, 3), _mod_spec(layer, 4), wspec, wspec, wspec],
        out_specs=[_row_spec(D), _row_spec(D), _row_spec(128), _row_spec(128)],
        out_shape=[jax.ShapeDtypeStruct((nb, rows, D), F32), jax.ShapeDtypeStruct((nb, rows, D), F32),
                   jax.ShapeDtypeStruct((nb, rows, 128), jnp.int32), jax.ShapeDtypeStruct((nb, rows, 128), F32)],
        compiler_params=_cparams(("parallel", "parallel")),
        name="outproj",
    )(acc, w_out, *xs, normv, modv, normv, modv, modv, *wr3)


def _route_math(hf, wa_ref, wb_ref, wc_ref, e_ref, w_ref):
    h = hf.astype(BF16)
    hl = (hf - h.astype(F32)).astype(BF16)
    logits = _dot(h, wa_ref[...]) + _dot(h, wb_ref[...]) + _dot(hl, wa_ref[...])
    lane = lax.broadcasted_iota(jnp.int32, (1, 128), 1).astype(F32)
    ninf = -jnp.inf
    is_grp = lane < N_GROUPS
    gl = jnp.where(is_grp, logits, ninf)
    gmax = jnp.max(gl, axis=1, keepdims=True)
    gidx = jnp.min(jnp.where(gl == gmax, lane, 999.0), axis=1, keepdims=True)
    gsum = jnp.sum(jnp.where(is_grp, jnp.exp(gl - gmax), 0.0), axis=1, keepdims=True)
    gp = 1.0 / gsum
    lo = N_GROUPS + EXP_PER_GROUP * gidx
    sel = jnp.logical_and(lane >= lo, lane < lo + EXP_PER_GROUP)
    el = jnp.where(sel, logits, ninf)
    v1 = jnp.max(el, axis=1, keepdims=True)
    i1 = jnp.min(jnp.where(el == v1, lane, 999.0), axis=1, keepdims=True)
    el2 = jnp.where(lane == i1, ninf, el)
    v2 = jnp.max(el2, axis=1, keepdims=True)
    i2 = jnp.min(jnp.where(el2 == v2, lane, 999.0), axis=1, keepdims=True)
    t = jnp.exp(v2 - v1)
    w1 = gp / (1.0 + t)
    w2 = gp * t / (1.0 + t)
    e_ref[...] = jnp.where(lane == 0.0, i1 - N_GROUPS, jnp.where(lane == 1.0, i2 - N_GROUPS, 0.0)).astype(jnp.int32)
    w_ref[...] = jnp.where(lane == 0.0, w1, jnp.where(lane == 1.0, w2, 0.0))


GATHER_UNROLL = 8


def _expert_kernel(te_ref, nu_ref, rt_ref, h_hbm, w1_ref, w3_ref, w2_ref, o_ref, xbuf, hm_sc, w1b, w3b, w2b, sem):
    i = pl.program_id(0)
    n_used = nu_ref[0]
    slot = i % 2

    def issue_tile(j, s):
        base = j * TE

        def body(g, c):
            for uu in range(GATHER_UNROLL):
                r = g * GATHER_UNROLL + uu
                pltpu.make_async_copy(h_hbm.at[pl.ds(rt_ref[base + r], 1), :],
                                      xbuf.at[s, pl.ds(r, 1), :], sem.at[s]).start(priority=1)
            return c
        lax.fori_loop(0, TE // GATHER_UNROLL, body, 0)

    def wait_tile(s):
        pltpu.make_async_copy(h_hbm.at[pl.ds(0, TE), :], xbuf.at[s], sem.at[s]).wait()

    @pl.when(jnp.logical_and(i == 0, n_used > 0))
    def _():
        issue_tile(0, 0)

    @pl.when(i < n_used)
    def _():
        wait_tile(slot)
        nbase = jnp.minimum(i + 1, n_used - 1) * TE
        oslot = 1 - slot

        def issue_rows(r0, r1):
            for r in range(r0, r1):
                pltpu.make_async_copy(h_hbm.at[pl.ds(rt_ref[nbase + r], 1), :],
                                      xbuf.at[oslot, pl.ds(r, 1), :], sem.at[oslot]).start(priority=1)

        @pl.when(jnp.logical_or(i == 0, te_ref[i] != te_ref[jnp.maximum(i - 1, 0)]))
        def _():
            w1b[...] = w1_ref[...].astype(BF16)
            w3b[...] = w3_ref[...].astype(BF16)
            w2b[...] = w2_ref[...].astype(BF16)

        x = xbuf[slot].astype(BF16)
        cw = 256
        n1, n2 = D_EXPERT // cw, D // cw
        per = TE // (n1 + n2) + 1
        done = 0
        for j in range(n1):
            issue_rows(done, min(done + per, TE))
            done = min(done + per, TE)
            cs = slice(j * cw, (j + 1) * cw)
            a = _dot(x, w1b[:, cs])
            b = _dot(x, w3b[:, cs])
            hm_sc[:, cs] = (_silu(a) * b).astype(BF16)
        hm = hm_sc[...]
        for j in range(n2):
            issue_rows(done, min(done + per, TE))
            done = min(done + per, TE)
            cs = slice(j * cw, (j + 1) * cw)
            o_ref[:, cs] = _dot(hm, w2b[:, cs])

        @pl.when(i + 1 >= n_used)
        def _():
            wait_tile(oslot)

    @pl.when(i >= n_used)
    def _():
        o_ref[...] = jnp.zeros_like(o_ref)


def _experts(tile_expert, n_used, row_token, hp, w1, w3, w2, n_tiles, layer):
    wmap = lambda i, te, nu, rt: (layer * N_EXPERTS + te[i], 0, 0)
    grid_spec = pltpu.PrefetchScalarGridSpec(
        num_scalar_prefetch=3,
        grid=(n_tiles,),
        in_specs=[pl.BlockSpec(memory_space=pl.ANY),
                  pl.BlockSpec((None, D, D_EXPERT), wmap),
                  pl.BlockSpec((None, D, D_EXPERT), wmap),
                  pl.BlockSpec((None, D_EXPERT, D), wmap)],
        out_specs=pl.BlockSpec((TE, D), lambda i, te, nu, rt: (i, 0)),
        scratch_shapes=[pltpu.VMEM((2, TE, D), F32), pltpu.VMEM((TE, D_EXPERT), BF16),
                        pltpu.VMEM((D, D_EXPERT), BF16), pltpu.VMEM((D, D_EXPERT), BF16),
                        pltpu.VMEM((D_EXPERT, D), BF16),
                        pltpu.SemaphoreType.DMA((2,))],
    )
    return pl.pallas_call(
        _expert_kernel,
        grid_spec=grid_spec,
        out_shape=jax.ShapeDtypeStruct((n_tiles * TE, D), F32),
        compiler_params=_cparams(("arbitrary",)),
        name="moe_experts",
    )(tile_expert, n_used, row_token, hp, w1, w3, w2)


def _combine_kernel(has_next, pos_ref, ys_hbm, w_ref, x_ref, g3_ref, gate_ref, *rest):
    if has_next:
        g0_ref, sh_ref, sc_ref, xo_ref, ho_ref, buf, sem = rest
    else:
        xo_ref, buf, sem = rest
    step = pl.program_id(0) * pl.num_programs(1) + pl.program_id(1)
    n_steps = pl.num_programs(0) * pl.num_programs(1)
    slot = step % 2

    def issue_tile(j, s):
        t0 = j * RT

        def body(g, c):
            for uu in range(GATHER_UNROLL):
                r = g * GATHER_UNROLL + uu
                for kk in range(2):
                    pltpu.make_async_copy(ys_hbm.at[pl.ds(pos_ref[(t0 + r) * 2 + kk], 1), :],
                                          buf.at[s, kk, pl.ds(r, 1), :], sem.at[s]).start(priority=kk)
            return c
        lax.fori_loop(0, RT // GATHER_UNROLL, body, 0)

    def wait_tile(s):
        for kk in range(2):
            pltpu.make_async_copy(ys_hbm.at[pl.ds(0, RT), :], buf.at[s, kk], sem.at[s]).wait()

    @pl.when(step == 0)
    def _():
        issue_tile(0, 0)

    wait_tile(slot)
    nt0 = jnp.minimum(step + 1, n_steps - 1) * RT
    oslot = 1 - slot
    rows = 32
    for c in range(RT // rows):
        for r in range(c * rows, (c + 1) * rows):
            for kk in range(2):
                pltpu.make_async_copy(ys_hbm.at[pl.ds(pos_ref[(nt0 + r) * 2 + kk], 1), :],
                                      buf.at[oslot, kk, pl.ds(r, 1), :], sem.at[oslot]).start(priority=kk)
        rs = slice(c * rows, (c + 1) * rows)
        w = w_ref[rs, :]
        y = w[:, 0:1] * buf[slot, 0, rs, :] + w[:, 1:2] * buf[slot, 1, rs, :]
        xn = x_ref[rs, :] + gate_ref[...] * (_rms(y) * g3_ref[...])
        xo_ref[rs, :] = xn
        if has_next:
            hn = _rms(xn) * g0_ref[...]
            ho_ref[rs, :] = (hn * (1.0 + sc_ref[...]) + sh_ref[...]).astype(BF16)

    @pl.when(step + 1 >= n_steps)
    def _():
        wait_tile(oslot)


def _combine(pos, ys, wts, xcat, normv, modv, layer):
    nb = xcat.shape[0]
    has_next = layer + 1 < DEPTH

    def mod_spec(lyr, k):
        def imap(b, i, pos_ref):
            row = jnp.where(i == NRT - 1, 8, b)
            return ((lyr * 16 + row) * 6 + k, 0, 0)
        return pl.BlockSpec((None, 1, D), imap)

    def norm_spec(lyr, k):
        return pl.BlockSpec((None, 1, D), lambda b, i, pos_ref: (lyr * 4 + k, 0, 0))

    def row_spec(width):
        return pl.BlockSpec((None, RT, width), lambda b, i, pos_ref: (b, i, 0))

    in_specs = [pl.BlockSpec(memory_space=pl.ANY), row_spec(128), row_spec(D),
                norm_spec(layer, 3), mod_spec(layer, 5)]
    args = [ys, wts, xcat, normv, modv]
    out_specs = [row_spec(D)]
    n_row_tiles = NRT if has_next else SEQ // RT
    out_shape = [jax.ShapeDtypeStruct((nb, n_row_tiles * RT, D), F32)]
    if has_next:
        in_specs += [norm_spec(layer + 1, 0), mod_spec(layer + 1, 0), mod_spec(layer + 1, 1)]
        args += [normv, modv, modv]
        out_specs.append(row_spec(D))
        out_shape.append(jax.ShapeDtypeStruct((nb, LT, D), BF16))
    grid_spec = pltpu.PrefetchScalarGridSpec(
        num_scalar_prefetch=1,
        grid=(nb, n_row_tiles),
        in_specs=in_specs,
        out_specs=out_specs,
        scratch_shapes=[pltpu.VMEM((2, 2, RT, D), F32), pltpu.SemaphoreType.DMA((2,))],
    )
    return pl.pallas_call(
        functools.partial(_combine_kernel, has_next),
        grid_spec=grid_spec,
        out_shape=out_shape,
        compiler_params=_cparams(("arbitrary", "arbitrary")),
        name="moe_combine",
    )(pos, *args)


def _plan_kernel(e_ref, pos_ref, cnt_ref, rank_sc, carry_sc):
    phase = pl.program_id(0)
    i = pl.program_id(1)
    lane = lax.broadcasted_iota(jnp.int32, (1, 128), 1)
    e = e_ref[...]
    oh1 = lane == e[:, 0:1]
    oh2 = lane == e[:, 1:2]

    @pl.when(jnp.logical_and(phase == 0, i == 0))
    def _():
        carry_sc[...] = jnp.zeros_like(carry_sc)

    @pl.when(phase == 0)
    def _():
        s = jnp.where(jnp.logical_or(oh1, oh2), 1.0, 0.0)
        row = lax.broadcasted_iota(jnp.int32, (RT, RT), 0)
        col = lax.broadcasted_iota(jnp.int32, (RT, RT), 1)
        lower = jnp.where(col < row, 1.0, 0.0).astype(BF16)
        carry = carry_sc[0:1, :]
        before = _dot(lower, s.astype(BF16)) + carry
        r1 = jnp.sum(jnp.where(oh1, before, 0.0), axis=1, keepdims=True)
        r2 = jnp.sum(jnp.where(oh2, before, 0.0), axis=1, keepdims=True)
        rank_sc[i] = jnp.where(lane == 0, r1, jnp.where(lane == 1, r2, 0.0))
        carry_sc[...] = jnp.broadcast_to(carry + jnp.sum(s, axis=0, keepdims=True), carry_sc.shape)
        cnt_ref[...] = carry_sc[...]

    @pl.when(phase == 1)
    def _():
        counts = carry_sc[0:1, :]
        padded = jnp.floor((counts + (TE - 1)) * (1.0 / TE)) * TE
        r128 = lax.broadcasted_iota(jnp.int32, (128, 128), 0)
        c128 = lax.broadcasted_iota(jnp.int32, (128, 128), 1)
        upper = jnp.where(r128 < c128, 1.0, 0.0).astype(BF16)
        pa, pb, pc = _split3(jnp.broadcast_to(padded, (8, 128)))
        offs = (_dot(pa, upper) + _dot(pb, upper) + _dot(pc, upper))[0:1, :]
        rank = rank_sc[i]
        p1 = rank[:, 0:1] + jnp.sum(jnp.where(oh1, offs, 0.0), axis=1, keepdims=True)
        p2 = rank[:, 1:2] + jnp.sum(jnp.where(oh2, offs, 0.0), axis=1, keepdims=True)
        pos_ref[...] = jnp.where(lane == 0, p1, jnp.where(lane == 1, p2, 0.0)).astype(jnp.int32)


def _plan(eid, n_row_tiles):
    nb = eid.shape[0]
    nt = nb * n_row_tiles
    emap = lambda p, i: (i // n_row_tiles, i % n_row_tiles, 0)
    return pl.pallas_call(
        _plan_kernel,
        grid=(2, nt),
        in_specs=[pl.BlockSpec((None, RT, 128), emap)],
        out_specs=[pl.BlockSpec((RT, 128), lambda p, i: (p * i, 0)),
                   pl.BlockSpec((8, 128), lambda p, i: (0, 0))],
        out_shape=[jax.ShapeDtypeStruct((nt * RT, 128), jnp.int32), jax.ShapeDtypeStruct((8, 128), F32)],
        scratch_shapes=[pltpu.VMEM((nt, RT, 128), F32), pltpu.VMEM((8, 128), F32)],
        compiler_params=_cparams(("arbitrary", "arbitrary")),
        name="moe_plan",
    )(eid)


def _dispatch_tiles(pos, counts, n_tiles, n_row_tiles, nb):
    cnt = counts[0, :N_EXPERTS].astype(jnp.int32)
    padded = ((cnt + TE - 1) // TE) * TE
    ends = jnp.cumsum(padded)
    tile_start = jnp.arange(n_tiles, dtype=jnp.int32) * TE
    tile_expert = jnp.sum((ends[None, :] <= tile_start[:, None]).astype(jnp.int32), axis=1)
    tile_expert = jnp.minimum(tile_expert, N_EXPERTS - 1)
    n_used = (ends[-1] // TE).astype(jnp.int32).reshape(1)
    pos2 = pos[:, :2]
    tok = jnp.arange(nb * n_row_tiles * RT, dtype=jnp.int32)
    tok2 = jnp.broadcast_to(tok[:, None], pos2.shape)
    row_token = jnp.zeros((n_tiles * TE,), jnp.int32).at[pos2.reshape(-1)].set(tok2.reshape(-1))
    return pos2.reshape(-1), row_token, tile_expert, n_used


def _rope_tables():
    tok = jnp.arange(SEQ)
    pos_row = (tok // GRID_W).astype(F32)
    pos_col = (tok % GRID_W).astype(F32)
    quarter = MLA_ROPE // 4
    inv_freq = ROPE_BASE ** (-jnp.arange(quarter, dtype=F32) / quarter)
    ang_r = pos_row[:, None] * inv_freq
    ang_c = pos_col[:, None] * inv_freq
    cos = jnp.concatenate([jnp.cos(ang_r)] * 2 + [jnp.cos(ang_c)] * 2, axis=1)
    sin = jnp.concatenate([jnp.sin(ang_r)] * 2 + [jnp.sin(ang_c)] * 2, axis=1)
    cos = jnp.concatenate([cos, jnp.ones((CTX, MLA_ROPE), F32)], axis=0)
    sin = jnp.concatenate([sin, jnp.zeros((CTX, MLA_ROPE), F32)], axis=0)
    pad = jnp.zeros((LT, 64), F32)
    cos_t = jnp.concatenate([jnp.ones((LT, 128), F32), cos, pad], axis=1)
    sin_t = jnp.concatenate([jnp.zeros((LT, 128), F32), sin, pad], axis=1)
    return cos_t, sin_t


def _rope_rot_matrix():
    j = jnp.arange(MLA_ROPE)
    first = (j % 32) < 16
    src = jnp.where(first, j + 16, j - 16)
    sign = jnp.where(first, -1.0, 1.0)
    return jnp.zeros((MLA_ROPE, MLA_ROPE), F32).at[src, j].set(sign)


def _layer_params(l, w_main, w_small, ml_f_bias, lb_all, hg_norm_g, mla_g_q, mla_g_kv, mla_w_uq, mla_w_ukv,
                  ssd_conv_w, ssd_conv_b, ssd_a_log, ssd_dt_bias, ssd_d, ssd_norm_g, w_gate, w_br, w_out,
                  moe_w_grp, moe_w_exp):
    p = {}
    p["w_main"] = w_main[l]
    p["w_small"] = w_small[l]
    fb = jnp.zeros((2, 2, NH), F32).at[:, 1, :].set(ml_f_bias[l]).reshape(16)
    ml_bias = jnp.concatenate([fb, jnp.zeros((112,), F32)])
    p["ml_bias_c"] = ml_bias.reshape(1, 128)
    p["ml_bias_t"] = ml_bias[:32].reshape(32, 1)
    dtb = jnp.concatenate([jnp.zeros((16,), F32), ssd_dt_bias[l].reshape(16), jnp.zeros((96,), F32)])
    p["ssd_bias_c"] = dtb.reshape(1, 128)
    p["ssd_bias_t"] = dtb[:32].reshape(32, 1)
    aneg = jnp.concatenate([jnp.zeros((16,), F32), -jnp.exp(ssd_a_log[l].astype(F32)).reshape(16),
                            jnp.zeros((96,), F32)])
    p["ssd_aneg_c"] = aneg.reshape(1, 128)
    p["ssd_aneg_t"] = aneg[:32].reshape(32, 1)
    p["lb"] = lb_all[l].reshape(1, BW)
    p["hg_norm_g"] = hg_norm_g[l].reshape(1, BW)
    p["ssd_dskip"] = jnp.repeat(ssd_d[l], SSD_P).reshape(1, BW)
    p["ssd_norm_g"] = ssd_norm_g[l].reshape(1, BW)
    p["conv_w"] = jnp.concatenate([ssd_conv_w[l], jnp.zeros((5, SSD_XBC), F32)], axis=0)
    p["conv_b"] = ssd_conv_b[l].reshape(1, SSD_XBC)
    rot = _rope_rot_matrix()
    wq = mla_w_uq[l].reshape(MLA_Q_RANK, NH, MLA_NOPE + MLA_ROPE)
    zq = jnp.zeros((MLA_Q_RANK, NH, 64), F32)
    wqa = jnp.concatenate([wq, zq], axis=2).reshape(MLA_Q_RANK, NH * 256)
    wq_rot = jnp.einsum("rhj,jk->rhk", wq[:, :, MLA_NOPE:], rot)
    wqb = jnp.concatenate([jnp.zeros((MLA_Q_RANK, NH, MLA_NOPE), F32), wq_rot, zq], axis=2)
    p["wqa"] = wqa.astype(BF16)
    p["wqb"] = wqb.reshape(MLA_Q_RANK, NH * 256).astype(BF16)
    wkv = mla_w_ukv[l].reshape(MLA_KV_RANK, NH, 2 * DH)
    wk = jnp.concatenate([wkv[:, :, :DH], jnp.zeros((MLA_KV_RANK, NH, DH), F32)], axis=2)
    p["wk"] = wk.reshape(MLA_KV_RANK, NH * 256).astype(BF16)
    p["wv"] = wkv[:, :, DH:].reshape(MLA_KV_RANK, NH * DH).astype(BF16)
    p["rk"] = jnp.zeros((128, 128), F32).at[:MLA_ROPE, :MLA_ROPE].set(rot).astype(BF16)
    p["gq"] = mla_g_q[l].reshape(1, MLA_Q_RANK)
    p["gkv"] = mla_g_kv[l].reshape(1, MLA_KV_RANK)
    p["w_gate"] = w_gate[l].astype(BF16)
    p["w_br"] = w_br[l].astype(BF16)
    p["w_out"] = w_out[l].astype(BF16)
    wr = jnp.concatenate([moe_w_grp[l], moe_w_exp[l], jnp.zeros((D, 128 - N_GROUPS - N_EXPERTS), F32)], axis=1)
    a = wr.astype(BF16)
    r = wr - a.astype(F32)
    b = r.astype(BF16)
    c = (r - b.astype(F32)).astype(BF16)
    p["wr3"] = (a, b, c)
    return p


def _mixer(h, p, cos_t, sin_t, nb, out_rows):
    t = nb * LT
    hf = h.reshape(t, D)
    zero_b = jnp.zeros((1, U_W), F32)
    u = _mm(hf, p["w_main"], zero_b, BF16, 1152, 1024).reshape(nb, LT, U_W)
    usm = _mm(hf, p["w_small"], zero_b[:, :128], F32, 1152, 128).reshape(nb, LT, 128)
    ust = usm[:, :, :32].reshape(nb, NCH, CS, 32).transpose(0, 1, 3, 2)
    ml_f = _mlstm(u, usm, ust, p["ml_bias_c"], p["ml_bias_t"], False)
    ml = _mlstm(u, usm, ust, p["ml_bias_c"], p["ml_bias_t"], True, ml_f)
    hg_f = _hgrn2(u, p["lb"], False)
    hg = _hgrn2(u, p["lb"], True, hg_f, p["hg_norm_g"])
    q, k, v = _mla_proj(u, p["gq"], p["gkv"], p["wqa"], p["wqb"], p["wk"], p["wv"], p["rk"], cos_t, sin_t)
    mla = _attention(q, k, v)
    xbc = _ssd_conv(u, p["conv_w"], p["conv_b"])
    sargs = (xbc, usm, ust, u, p["ssd_bias_c"], p["ssd_bias_t"], p["ssd_aneg_c"], p["ssd_aneg_t"])
    ssd_f = _ssd(*sargs, False)
    ssd = _ssd(*sargs, True, ssd_f, p["ssd_dskip"], p["ssd_norm_g"])
    return _merge(h, (ml, hg, mla, ssd), p["w_gate"], p["w_br"], out_rows)


def kernel(x, c, ctx, c_ctx, w_ada, b_ada, norm_g, w_in, ml_f_bias, hg_lb_logits, hg_norm_g, mla_g_q, mla_g_kv,
           mla_w_uq, mla_w_ukv, ssd_conv_w, ssd_conv_b, ssd_a_log, ssd_dt_bias, ssd_d, ssd_norm_g, w_gate,
           w_br, w_out, moe_w_grp, moe_w_exp, moe_w1, moe_w3, moe_w2):
    nb = x.shape[0]
    t = nb * LT
    lb_all = jnp.cumsum(jax.nn.softmax(hg_lb_logits.astype(F32), axis=0), axis=0)
    cvec = jnp.zeros((16, D), F32).at[:nb].set(c).at[8].set(c_ctx)
    s_vec = jax.nn.silu(cvec).astype(BF16)
    modv = _adaln_mods(s_vec, w_ada, b_ada).reshape(DEPTH * 16 * 6, 1, D)
    normv = norm_g.reshape(DEPTH * 4, 1, D)
    cos_t, sin_t = _rope_tables()
    w_main, w_small = _win_layout(w_in)
    w1 = moe_w1.reshape(DEPTH * N_EXPERTS, D, D_EXPERT)
    w3 = moe_w3.reshape(DEPTH * N_EXPERTS, D, D_EXPERT)
    w2 = moe_w2.reshape(DEPTH * N_EXPERTS, D_EXPERT, D)
    h = _prenorm(x, ctx, normv, modv, 0)
    xs = (x, ctx)
    for l in range(DEPTH):
        p = _layer_params(l, w_main, w_small, ml_f_bias, lb_all, hg_norm_g, mla_g_q, mla_g_kv, mla_w_uq,
                          mla_w_ukv, ssd_conv_w, ssd_conv_b, ssd_a_log, ssd_dt_bias, ssd_d, ssd_norm_g, w_gate,
                          w_br, w_out, moe_w_grp, moe_w_exp)
        rows = LT if l + 1 < DEPTH else SEQ
        n_row_tiles = rows // RT
        acc = _mixer(h, p, cos_t, sin_t, nb, rows)
        xcat, hp, eid, wts = _outproj(acc, p["w_out"], xs, normv, modv, p["wr3"], l)
        n_tiles = (2 * nb * rows + N_EXPERTS * (TE - 1)) // TE + 1
        pos128, counts = _plan(eid, n_row_tiles)
        pos, row_token, tile_expert, n_used = _dispatch_tiles(pos128, counts, n_tiles, n_row_tiles, nb)
        ys = _experts(tile_expert, n_used, row_token, hp.reshape(nb * rows, D), w1, w3, w2, n_tiles, l)
        outs = _combine(pos, ys, wts, xcat, normv, modv, l)
        if l + 1 < DEPTH:
            xcat, h = outs
            xs = (xcat,)
    return outs[0]
```

```python
import functools

import jax
import jax.numpy as jnp
from jax import lax
from jax.experimental import pallas as pl
from jax.experimental.pallas import tpu as pltpu

F32 = jnp.float32
BF16 = jnp.bfloat16

D = 2048
SEQ = 2048
CTX = 256
LT = SEQ + CTX
DEPTH = 2
EPS = 1e-6
GRID_W = 64
NH = 4
DH = 128
BW = 512
MLA_Q_RANK = 512
MLA_KV_RANK = 256
MLA_NOPE = 128
MLA_ROPE = 64
MLA_SCALE = (MLA_NOPE + MLA_ROPE) ** -0.5
ROPE_BASE = 10000.0
SSD_HEADS = 8
SSD_P = 64
SSD_GROUPS = 2
SSD_N = 128
SSD_XBC = 1024
N_GROUPS = 4
EXP_PER_GROUP = 8
N_EXPERTS = 32
D_EXPERT = 512

CS = 128
NLC = SEQ // CS
NCC = CTX // CS
NCH = NLC + NCC
RT = 256
NRT = LT // RT
TE = 256
VMEM_LIMIT = 52 * 1024 * 1024

U_W = 7168
CB_ML_Q, CB_ML_K, CB_ML_V, CB_ML_O = 0, 1, 2, 3
CB_HG_Q, CB_HG_I, CB_HG_G, CB_HG_F = 4, 5, 6, 7
CB_SSD_Z = 9
CB_SSD_XBC = 5
CB_MLA_CQ = 12
CB_MLA_CKV = 26
CB_MLA_KR = 54


def _cparams(sem):
    return pltpu.CompilerParams(dimension_semantics=sem, vmem_limit_bytes=VMEM_LIMIT)


def _dot(a, b):
    return jnp.dot(a, b, preferred_element_type=F32)


def _dot_nt(a, b):
    return lax.dot_general(a, b, (((1,), (1,)), ((), ())), preferred_element_type=F32)


def _sigmoid(x):
    return 1.0 / (1.0 + jnp.exp(-x))


def _silu(x):
    return x * _sigmoid(x)


def _softplus(x):
    return jnp.maximum(x, 0.0) + jnp.log(1.0 + jnp.exp(-jnp.abs(x)))


def _log_sigmoid(x):
    return -_softplus(-x)


def _rms(x):
    return x * lax.rsqrt(jnp.mean(x * x, axis=-1, keepdims=True) + EPS)


def _split3(x):
    a = x.astype(BF16)
    r = x - a.astype(F32)
    b = r.astype(BF16)
    c = (r - b.astype(F32)).astype(BF16)
    return a, b, c


def _cumsum_rows(mb, x):
    a, b, c = _split3(x)
    return _dot(mb, a) + _dot(mb, b) + _dot(mb, c)


def _cumsum_cols(x, mb):
    a, b, c = _split3(x)
    return _dot_nt(a, mb) + _dot_nt(b, mb) + _dot_nt(c, mb)


def _chunk_mask(rev):
    row = lax.broadcasted_iota(jnp.int32, (CS, CS), 0)
    col = lax.broadcasted_iota(jnp.int32, (CS, CS), 1)
    return (col >= row) if rev else (col <= row)


def _chunk_of_step(rev):
    if rev:
        return lambda s: NCH - 1 - s
    return lambda s: (s + NLC) % NCH


def _mm_kernel(a_ref, w_ref, b_ref, o_ref):
    acc = _dot(a_ref[...], w_ref[...].astype(BF16))
    o_ref[...] = (acc + b_ref[...]).astype(o_ref.dtype)


def _mm(a, w, bias, out_dtype, tm, tn):
    m, k = a.shape
    n = w.shape[1]
    return pl.pallas_call(
        _mm_kernel,
        grid=(m // tm, n // tn),
        in_specs=[pl.BlockSpec((tm, k), lambda i, j: (i, 0)),
                  pl.BlockSpec((k, tn), lambda i, j: (0, j)),
                  pl.BlockSpec((1, tn), lambda i, j: (0, j))],
        out_specs=pl.BlockSpec((tm, tn), lambda i, j: (i, j)),
        out_shape=jax.ShapeDtypeStruct((m, n), out_dtype),
        compiler_params=_cparams(("parallel", "arbitrary")),
        name="mm",
    )(a, w, bias)


W_IN_PIECES = ((0, 0, 2048),
               (2048, 2064, 2560),
               (4608, 5456, 512),
               (5120, 5968, 1024),
               (6144, 4624, 512),
               (6656, 5136, 256),
               (6912, 5392, 64),
               (7040, 2048, 16),
               (7056, 6992, 16))
D_IN = 7008


def _win_kernel(w_ref, o_ref, s_ref):
    o_ref[...] = jnp.zeros_like(o_ref)
    for dst, src, width in W_IN_PIECES:
        o_ref[:, dst:dst + width] = w_ref[:, src:src + width].astype(BF16)
    s_ref[...] = o_ref[:, U_W - 128:]


def _win_layout(w_in):
    rows = 256
    return pl.pallas_call(
        _win_kernel,
        grid=(DEPTH, D // rows),
        in_specs=[pl.BlockSpec((None, rows, D_IN), lambda l, i: (l, i, 0))],
        out_specs=[pl.BlockSpec((None, rows, U_W), lambda l, i: (l, i, 0)),
                   pl.BlockSpec((None, rows, 128), lambda l, i: (l, i, 0))],
        out_shape=[jax.ShapeDtypeStruct((DEPTH, D, U_W), BF16), jax.ShapeDtypeStruct((DEPTH, D, 128), BF16)],
        compiler_params=_cparams(("parallel", "parallel")),
        name="win_layout",
    )(w_in)


def _adaln_mods(s_vec, w_ada, b_ada):
    tn = 1024
    return pl.pallas_call(
        _mm_kernel,
        grid=(DEPTH, 6 * D // tn),
        in_specs=[pl.BlockSpec((16, D), lambda l, j: (0, 0)),
                  pl.BlockSpec((None, D, tn), lambda l, j: (l, 0, j)),
                  pl.BlockSpec((None, 1, tn), lambda l, j: (l, 0, j))],
        out_specs=pl.BlockSpec((None, 16, tn), lambda l, j: (l, 0, j)),
        out_shape=jax.ShapeDtypeStruct((DEPTH, 16, 6 * D), F32),
        compiler_params=_cparams(("parallel", "arbitrary")),
        name="adaln_mods",
    )(s_vec, w_ada, b_ada.reshape(DEPTH, 1, 6 * D))


def _mod_spec(layer, k):
    def imap(b, i):
        row = jnp.where(i == NRT - 1, 8, b)
        return ((layer * 16 + row) * 6 + k, 0, 0)
    return pl.BlockSpec((None, 1, D), imap)


def _norm_spec(layer, k):
    return pl.BlockSpec((None, 1, D), lambda b, i: (layer * 4 + k, 0, 0))


def _row_spec(width, dtype_rows=RT):
    return pl.BlockSpec((None, dtype_rows, width), lambda b, i: (b, i, 0))


def _split_x_specs():
    return [pl.BlockSpec((None, RT, D), lambda b, i: (b, jnp.minimum(i, NRT - 2), 0)),
            pl.BlockSpec((None, RT, D), lambda b, i: (b, 0, 0))]


def _pick_x(x_ref, c_ref):
    return jnp.where(pl.program_id(1) == NRT - 1, c_ref[...], x_ref[...])


def _prenorm_kernel(x_ref, c_ref, g_ref, sh_ref, sc_ref, h_ref):
    y = _rms(_pick_x(x_ref, c_ref)) * g_ref[...]
    h_ref[...] = (y * (1.0 + sc_ref[...]) + sh_ref[...]).astype(BF16)


def _prenorm(x, ctx, normv, modv, layer):
    nb = x.shape[0]
    return pl.pallas_call(
        _prenorm_kernel,
        grid=(nb, NRT),
        in_specs=_split_x_specs() + [_norm_spec(layer, 0), _mod_spec(layer, 0), _mod_spec(layer, 1)],
        out_specs=_row_spec(D),
        out_shape=jax.ShapeDtypeStruct((nb, LT, D), BF16),
        compiler_params=_cparams(("parallel", "parallel")),
        name="prenorm",
    )(x, ctx, normv, modv, modv)


def _conv_kernel(x_ref, p_ref, n_ref, w_ref, b_ref, o_ref):
    i = pl.program_id(1)
    x = x_ref[...].astype(F32)
    has_prev = jnp.logical_and(i != 0, i != NRT - 1).astype(F32)
    has_next = jnp.logical_and(i != NRT - 2, i != NRT - 1).astype(F32)
    prev_row = p_ref[15:16, :].astype(F32) * has_prev
    next_row = n_ref[0:1, :].astype(F32) * has_next
    ridx = lax.broadcasted_iota(jnp.int32, (RT, 1), 0)
    xm1 = jnp.where(ridx == 0, prev_row, pltpu.roll(x, 1, 0))
    xp1 = jnp.where(ridx == RT - 1, next_row, pltpu.roll(x, RT - 1, 0))
    w = w_ref[...]
    y = w[0:1, :] * xm1 + w[1:2, :] * x + w[2:3, :] * xp1 + b_ref[...]
    o_ref[...] = _silu(y)


def _ssd_conv(u, conv_w, conv_b):
    nb = u.shape[0]
    hb = RT // 16
    last = LT // 16 - 1
    return pl.pallas_call(
        _conv_kernel,
        grid=(nb, NRT),
        in_specs=[pl.BlockSpec((None, RT, SSD_XBC), lambda b, i: (b, i, CB_SSD_XBC)),
                  pl.BlockSpec((None, 16, SSD_XBC), lambda b, i: (b, jnp.maximum(i * hb - 1, 0), CB_SSD_XBC)),
                  pl.BlockSpec((None, 16, SSD_XBC), lambda b, i: (b, jnp.minimum((i + 1) * hb, last), CB_SSD_XBC)),
                  pl.BlockSpec((8, SSD_XBC), lambda b, i: (0, 0)),
                  pl.BlockSpec((1, SSD_XBC), lambda b, i: (0, 0))],
        out_specs=pl.BlockSpec((None, RT, SSD_XBC), lambda b, i: (b, i, 0)),
        out_shape=jax.ShapeDtypeStruct((nb, LT, SSD_XBC), F32),
        compiler_params=_cparams(("parallel", "parallel")),
        name="ssd_conv",
    )(u, u, u, conv_w, conv_b)


SCAN_BB = 2


def _mlstm_kernel(rev, bpb, *refs):
    if rev:
        (qs_ref, ks_ref, vs_ref, gc_ref, gt_ref, bc_ref, bt_ref, yps_ref, ogs_ref,
         os_ref, c_sc, n_sc, m_sc) = refs
    else:
        (qs_ref, ks_ref, vs_ref, gc_ref, gt_ref, bc_ref, bt_ref,
         os_ref, c_sc, n_sc, m_sc) = refs
    d = 1 if rev else 0

    @pl.when(pl.program_id(1) == 0)
    def _():
        c_sc[...] = jnp.zeros_like(c_sc)
        n_sc[...] = jnp.zeros_like(n_sc)
        m_sc[...] = jnp.full(m_sc.shape, -jnp.inf, F32)

    mask = _chunk_mask(rev)
    mb = jnp.where(mask, 1.0, 0.0).astype(BF16)
    scale = DH ** -0.5
    for bb, h in [(bb, h) for bb in range(bpb) for h in range(NH)]:
        if h == 0:
            gc = gc_ref[bb] + bc_ref[...]
            gt = gt_ref[bb] + bt_ref[...]
            flog_c = _log_sigmoid(gc)
            flog_t = _log_sigmoid(gt)
            fcum_c = _cumsum_rows(mb, flog_c)
            fcum_t = _cumsum_cols(flog_t, mb)
        q_ref, k_ref, v_ref, o_ref = qs_ref.at[bb], ks_ref.at[bb], vs_ref.at[bb], os_ref.at[bb]
        if rev:
            yp_ref, og_ref = yps_ref.at[bb], ogs_ref.at[bb]
        c_h, n_h, m_h = c_sc.at[bb * NH + h], n_sc.at[bb * NH + h], m_sc.at[bb * NH + h]
        ci = d * 8 + h
        cf = d * 8 + 4 + h
        sl = slice(h * DH, (h + 1) * DH)
        q = q_ref[:, sl]
        k = k_ref[:, sl]
        v = v_ref[:, sl]
        i_col = gc[:, ci:ci + 1]
        f_col = flog_c[:, cf:cf + 1]
        fc_col = fcum_c[:, cf:cf + 1]
        i_row = gt[ci:ci + 1, :]
        fc_row = fcum_t[cf:cf + 1, :]
        m_prev = m_h[0:1, 0:1]
        dm = jnp.where(mask, fc_col - fc_row + i_row, -jnp.inf)
        from_state = fc_col + m_prev
        m_t = jnp.maximum(from_state, jnp.max(dm, axis=1, keepdims=True))
        s = _dot_nt(q, k) * scale * jnp.exp(dm - m_t)
        w_state = jnp.exp(from_state - m_t)
        qc = _dot(q, c_h[...].astype(BF16)) * scale
        num = _dot(s.astype(BF16), v) + w_state * qc
        qn = jnp.sum(q.astype(F32) * n_h[0:1, :], axis=1, keepdims=True) * scale
        den = jnp.sum(s, axis=1, keepdims=True) + w_state * qn
        hout = num / jnp.maximum(jnp.abs(den), jnp.exp(-m_t))
        if rev:
            tot = yp_ref[:, sl] + hout
            o_ref[:, sl] = (tot * _sigmoid(og_ref[:, sl].astype(F32))).astype(o_ref.dtype)
        else:
            o_ref[:, sl] = hout
        f_tot = jnp.sum(f_col, axis=0, keepdims=True)
        w_log = f_tot - fc_col + i_col
        m_new = jnp.maximum(f_tot + m_prev, jnp.max(w_log, axis=0, keepdims=True))
        decay = jnp.exp(f_tot + m_prev - m_new)
        kw = k.astype(F32) * jnp.exp(w_log - m_new)
        c_h[...] = decay * c_h[...] + _dot(kw.T.astype(BF16), v)
        n_new = decay * n_h[0:1, :] + jnp.sum(kw, axis=0, keepdims=True)
        n_h[...] = jnp.broadcast_to(n_new, (8, DH))
        m_h[...] = jnp.broadcast_to(m_new, (8, DH))


def _mlstm(u, usm, ust, bias_c, bias_t, rev, yprev=None):
    nb = u.shape[0]
    bpb = SCAN_BB if nb % SCAN_BB == 0 else 1
    cmap = _chunk_of_step(rev)
    uspec = lambda cblk: pl.BlockSpec((bpb, CS, BW), lambda b, s: (b, cmap(s), cblk))
    in_specs = [uspec(CB_ML_Q), uspec(CB_ML_K), uspec(CB_ML_V),
                pl.BlockSpec((bpb, CS, 128), lambda b, s: (b, cmap(s), 0)),
                pl.BlockSpec((bpb, None, 32, CS), lambda b, s: (b, cmap(s), 0, 0)),
                pl.BlockSpec((1, 128), lambda b, s: (0, 0)),
                pl.BlockSpec((32, 1), lambda b, s: (0, 0))]
    args = [u, u, u, usm, ust, bias_c, bias_t]
    if rev:
        in_specs += [uspec(0), uspec(CB_ML_O)]
        args += [yprev, u]
    return pl.pallas_call(
        functools.partial(_mlstm_kernel, rev, bpb),
        grid=(nb // bpb, NCH),
        in_specs=in_specs,
        out_specs=uspec(0),
        out_shape=jax.ShapeDtypeStruct((nb, LT, BW), BF16 if rev else F32),
        scratch_shapes=[pltpu.VMEM((bpb * NH, DH, DH), F32), pltpu.VMEM((bpb * NH, 8, DH), F32),
                        pltpu.VMEM((bpb * NH, 8, DH), F32)],
        compiler_params=_cparams(("parallel", "arbitrary")),
        name="mlstm_rev" if rev else "mlstm_fwd",
    )(*args)


def _block_ref(g, n, off):
    g3 = g.reshape(CS // n, n, DH)
    r = g3[:, off:off + 1, :]
    return jnp.broadcast_to(r, (CS // n, n, DH)).reshape(CS, DH)


def _hgrn2_kernel(rev, bpb, *refs):
    if rev:
        qs_ref, is_ref, fs_ref, lb_ref, yps_ref, gs_ref, ng_ref, os_ref, st_sc = refs
    else:
        qs_ref, is_ref, fs_ref, lb_ref, os_ref, st_sc = refs

    @pl.when(pl.program_id(1) == 0)
    def _():
        st_sc[...] = jnp.zeros_like(st_sc)

    mask = _chunk_mask(rev)
    mb = jnp.where(mask, 1.0, 0.0).astype(BF16)
    row = lax.broadcasted_iota(jnp.int32, (CS, CS), 0)
    col = lax.broadcasted_iota(jnp.int32, (CS, CS), 1)
    tix = lax.broadcasted_iota(jnp.int32, (CS, 1), 0)
    for bb, h in [(bb, h) for bb in range(bpb) for h in range(NH)]:
        q_ref, i_ref, f_ref, o_ref = qs_ref.at[bb], is_ref.at[bb], fs_ref.at[bb], os_ref.at[bb]
        if rev:
            yp_ref, g_ref = yps_ref.at[bb], gs_ref.at[bb]
        st_h = st_sc.at[bb * NH + h]
        sl = slice(h * DH, (h + 1) * DH)
        q = _silu(q_ref[:, sl].astype(F32))
        v = i_ref[:, sl].astype(F32)
        lb = lb_ref[:, sl]
        f = lb + (1.0 - lb) * _sigmoid(f_ref[:, sl].astype(F32))
        k = 1.0 - f
        lf = jnp.log(f)
        g = _cumsum_rows(mb, lf)
        st = st_h[...]
        o = _dot_nt((q * jnp.exp(g)).astype(BF16), st.astype(BF16))
        a = jnp.zeros((CS, CS), F32)
        n = CS
        while n >= 16:
            half = n // 2
            second = jnp.bitwise_and(tix, n - 1) >= half
            rowside = jnp.logical_not(second) if rev else second
            r = _block_ref(g, n, half if rev else half - 1)
            e = jnp.exp(jnp.where(rowside, g - r, r - g))
            qs = jnp.where(rowside, q * e, 0.0).astype(BF16)
            ks = jnp.where(rowside, 0.0, k * e).astype(BF16)
            sh_n = n.bit_length() - 1
            same = jnp.right_shift(row, sh_n) == jnp.right_shift(col, sh_n)
            a = a + jnp.where(same, _dot_nt(qs, ks), 0.0)
            n = half
        o = o + _dot(a.astype(BF16), v.astype(BF16))
        t8 = jnp.bitwise_and(tix, 7)
        k3, g3, v3 = (a3.reshape(CS // 8, 8, DH) for a3 in (k, g, v))
        for dl in range(8):
            if dl == 0:
                ksh, gsh, vsh = k, g, v
            else:
                sh = (8 - dl) if rev else dl
                ksh = pltpu.roll(k3, sh, 1).reshape(CS, DH)
                gsh = pltpu.roll(g3, sh, 1).reshape(CS, DH)
                vsh = pltpu.roll(v3, sh, 1).reshape(CS, DH)
            valid = (t8 + dl <= 7) if rev else (t8 >= dl)
            e = jnp.exp(jnp.minimum(g - gsh, 0.0))
            coef = jnp.sum(q * ksh * e, axis=1, keepdims=True)
            o = o + jnp.where(valid, coef, 0.0) * vsh
        if rev:
            tot = yp_ref[:, sl] + o
            y = _rms(tot) * ng_ref[:, sl]
            o_ref[:, sl] = (y * _sigmoid(g_ref[:, sl].astype(F32))).astype(o_ref.dtype)
        else:
            o_ref[:, sl] = o
        g_tot = jnp.sum(lf, axis=0, keepdims=True)
        kdec = k * jnp.exp(g_tot - g)
        st_h[...] = st * jnp.exp(g_tot) + _dot(v.T.astype(BF16), kdec.astype(BF16))


def _hgrn2(u, lb, rev, yprev=None, norm_g=None):
    nb = u.shape[0]
    bpb = SCAN_BB if nb % SCAN_BB == 0 else 1
    cmap = _chunk_of_step(rev)
    d = 1 if rev else 0
    uspec = lambda cblk: pl.BlockSpec((bpb, CS, BW), lambda b, s: (b, cmap(s), cblk))
    in_specs = [uspec(CB_HG_Q), uspec(CB_HG_I), uspec(CB_HG_F + d),
                pl.BlockSpec((1, BW), lambda b, s: (0, 0))]
    args = [u, u, u, lb]
    if rev:
        in_specs += [uspec(0), uspec(CB_HG_G), pl.BlockSpec((1, BW), lambda b, s: (0, 0))]
        args += [yprev, u, norm_g]
    return pl.pallas_call(
        functools.partial(_hgrn2_kernel, rev, bpb),
        grid=(nb // bpb, NCH),
        in_specs=in_specs,
        out_specs=uspec(0),
        out_shape=jax.ShapeDtypeStruct((nb, LT, BW), BF16 if rev else F32),
        scratch_shapes=[pltpu.VMEM((bpb * NH, DH, DH), F32)],
        compiler_params=_cparams(("parallel", "arbitrary")),
        name="hgrn2_rev" if rev else "hgrn2_fwd",
    )(*args)


def _expand_heads(cols, lane_head):
    r = cols[3]
    for hh in (2, 1, 0):
        r = jnp.where(lane_head == hh, cols[hh], r)
    return r


def _ssd_kernel(rev, bpb, *refs):
    if rev:
        (xs_ref, gcs_ref, gts_ref, bc_ref, bt_ref, ac_ref, at_ref, yps_ref, zs_ref, dk_ref, ng_ref,
         os_ref, ht_sc) = refs
    else:
        xs_ref, gcs_ref, gts_ref, bc_ref, bt_ref, ac_ref, at_ref, os_ref, ht_sc = refs

    @pl.when(pl.program_id(1) == 0)
    def _():
        ht_sc[...] = jnp.zeros_like(ht_sc)

    for bb in range(bpb):
        extra = (yps_ref.at[bb], zs_ref.at[bb], dk_ref, ng_ref) if rev else ()
        _ssd_chunk(rev, xs_ref.at[bb], gcs_ref[bb], gts_ref[bb], bc_ref, bt_ref, ac_ref, at_ref,
                   os_ref.at[bb], ht_sc, bb * SSD_GROUPS, *extra)


def _ssd_chunk(rev, x_ref, gc, gt, bc_ref, bt_ref, ac_ref, at_ref, o_ref, ht_sc, ht0, yp_ref=None, z_ref=None,
               dk_ref=None, ng_ref=None):
    d = 1 if rev else 0
    mask = _chunk_mask(rev)
    mb = jnp.where(mask, 1.0, 0.0).astype(BF16)
    dt_c = _softplus(gc + bc_ref[...])
    da_c = dt_c * ac_ref[...]
    acum_c = _cumsum_rows(mb, da_c)
    dt_t = _softplus(gt + bt_ref[...])
    da_t = dt_t * at_ref[...]
    acum_t = _cumsum_cols(da_t, mb)
    gw = 4 * SSD_P
    lane_head = jnp.right_shift(lax.broadcasted_iota(jnp.int32, (1, gw), 1), 6)
    ys = []
    for grp in range(SSD_GROUPS):
        xg = x_ref[:, grp * gw:(grp + 1) * gw]
        bm = x_ref[:, BW + grp * SSD_N:BW + (grp + 1) * SSD_N]
        cm = x_ref[:, BW + SSD_GROUPS * SSD_N + grp * SSD_N:BW + SSD_GROUPS * SSD_N + (grp + 1) * SSD_N]
        cmb = cm.astype(BF16)
        cb = _dot_nt(cmb, bm.astype(BF16))
        ht = ht_sc[ht0 + grp]
        y_state = _dot(cmb, ht.astype(BF16))
        lanes = [16 + d * 8 + grp * 4 + hh for hh in range(4)]
        dt_cols = [dt_c[:, l:l + 1] for l in lanes]
        a_cols = [acum_c[:, l:l + 1] for l in lanes]
        a_tots = [jnp.sum(da_c[:, l:l + 1], axis=0, keepdims=True) for l in lanes]
        xdt = xg * _expand_heads(dt_cols, lane_head)
        xdtb = xdt.astype(BF16)
        y = jnp.exp(_expand_heads(a_cols, lane_head)) * y_state
        for hh in range(4):
            l = lanes[hh]
            dec = jnp.exp(jnp.where(mask, a_cols[hh] - acum_t[l:l + 1, :], -jnp.inf))
            yh = _dot((cb * dec).astype(BF16), xdtb)
            y = y + jnp.where(lane_head == hh, yh, 0.0)
        ys.append(y)
        w_exp = jnp.exp(_expand_heads([a_tots[hh] - a_cols[hh] for hh in range(4)], lane_head))
        e_row = jnp.exp(_expand_heads(a_tots, lane_head))
        ht_sc[ht0 + grp] = ht * e_row + _dot(bm.T.astype(BF16), (xdt * w_exp).astype(BF16))
    y = jnp.concatenate(ys, axis=1)
    if rev:
        tot = yp_ref[...] + y + dk_ref[...] * x_ref[:, :BW]
        tot = tot * _silu(z_ref[...].astype(F32))
        o_ref[...] = (_rms(tot) * ng_ref[...]).astype(o_ref.dtype)
    else:
        o_ref[...] = y


def _ssd(xbc, usm, ust, u, bias_c, bias_t, aneg_c, aneg_t, rev, yprev=None, dskip=None, norm_g=None):
    nb = xbc.shape[0]
    bpb = SCAN_BB if nb % SCAN_BB == 0 else 1
    cmap = _chunk_of_step(rev)
    in_specs = [pl.BlockSpec((bpb, CS, SSD_XBC), lambda b, s: (b, cmap(s), 0)),
                pl.BlockSpec((bpb, CS, 128), lambda b, s: (b, cmap(s), 0)),
                pl.BlockSpec((bpb, None, 32, CS), lambda b, s: (b, cmap(s), 0, 0)),
                pl.BlockSpec((1, 128), lambda b, s: (0, 0)),
                pl.BlockSpec((32, 1), lambda b, s: (0, 0)),
                pl.BlockSpec((1, 128), lambda b, s: (0, 0)),
                pl.BlockSpec((32, 1), lambda b, s: (0, 0))]
    args = [xbc, usm, ust, bias_c, bias_t, aneg_c, aneg_t]
    if rev:
        in_specs += [pl.BlockSpec((bpb, CS, BW), lambda b, s: (b, cmap(s), 0)),
                     pl.BlockSpec((bpb, CS, BW), lambda b, s: (b, cmap(s), CB_SSD_Z)),
                     pl.BlockSpec((1, BW), lambda b, s: (0, 0)),
                     pl.BlockSpec((1, BW), lambda b, s: (0, 0))]
        args += [yprev, u, dskip, norm_g]
    return pl.pallas_call(
        functools.partial(_ssd_kernel, rev, bpb),
        grid=(nb // bpb, NCH),
        in_specs=in_specs,
        out_specs=pl.BlockSpec((bpb, CS, BW), lambda b, s: (b, cmap(s), 0)),
        out_shape=jax.ShapeDtypeStruct((nb, LT, BW), BF16 if rev else F32),
        scratch_shapes=[pltpu.VMEM((bpb * SSD_GROUPS, SSD_N, 4 * SSD_P), F32)],
        compiler_params=_cparams(("parallel", "arbitrary")),
        name="ssd_rev" if rev else "ssd_fwd",
    )(*args)


def _mla_proj_kernel(cq_ref, ckv_ref, kr_ref, gq_ref, gkv_ref, wqa_ref, wqb_ref, wk_ref, wv_ref,
                     rk_ref, cos_ref, sin_ref, q_ref, k_ref, v_ref):
    nq = (_rms(cq_ref[...].astype(F32)) * gq_ref[...]).astype(BF16)
    qa = _dot(nq, wqa_ref[...])
    qb = _dot(nq, wqb_ref[...])
    cos_t = cos_ref[...]
    sin_t = sin_ref[...]
    cosf = jnp.concatenate([cos_t] * NH, axis=1)
    sinf = jnp.concatenate([sin_t] * NH, axis=1)
    q_ref[...] = ((qa * cosf + qb * sinf) * MLA_SCALE).astype(BF16)
    nkv = (_rms(ckv_ref[...].astype(F32)) * gkv_ref[...]).astype(BF16)
    kn = _dot(nkv, wk_ref[...])
    v_ref[...] = _dot(nkv, wv_ref[...]).astype(BF16)
    kr = kr_ref[...]
    krr = kr.astype(F32) * cos_t[:, 128:256] + _dot(kr, rk_ref[...]) * sin_t[:, 128:256]
    zero = jnp.zeros_like(krr)
    k_ref[...] = (kn + jnp.concatenate([zero, krr] * NH, axis=1)).astype(BF16)


def _mla_proj(u, gq, gkv, wqa, wqb, wk, wv, rk, cos_t, sin_t):
    nb = u.shape[0]
    const = lambda shape: pl.BlockSpec(shape, lambda b, i: (0, 0))
    return pl.pallas_call(
        _mla_proj_kernel,
        grid=(nb, NRT),
        in_specs=[pl.BlockSpec((None, RT, MLA_Q_RANK), lambda b, i: (b, i, CB_MLA_CQ)),
                  pl.BlockSpec((None, RT, MLA_KV_RANK), lambda b, i: (b, i, CB_MLA_CKV)),
                  pl.BlockSpec((None, RT, 128), lambda b, i: (b, i, CB_MLA_KR)),
                  const((1, MLA_Q_RANK)), const((1, MLA_KV_RANK)),
                  const((MLA_Q_RANK, NH * 256)), const((MLA_Q_RANK, NH * 256)),
                  const((MLA_KV_RANK, NH * 256)), const((MLA_KV_RANK, NH * DH)),
                  const((128, 128)),
                  pl.BlockSpec((RT, 256), lambda b, i: (i, 0)),
                  pl.BlockSpec((RT, 256), lambda b, i: (i, 0))],
        out_specs=[_row_spec(NH * 256), _row_spec(NH * 256), _row_spec(NH * DH)],
        out_shape=[jax.ShapeDtypeStruct((nb, LT, NH * 256), BF16),
                   jax.ShapeDtypeStruct((nb, LT, NH * 256), BF16),
                   jax.ShapeDtypeStruct((nb, LT, NH * DH), BF16)],
        compiler_params=_cparams(("parallel", "parallel")),
        name="mla_proj",
    )(u, u, u, gq, gkv, wqa, wqb, wk, wv, rk, cos_t, sin_t)


AT = 768


def _attn_kernel(q_ref, k_ref, v_ref, o_ref):
    s = _dot_nt(q_ref[...], k_ref[...])

    def finish(sc):
        m = jnp.max(sc, axis=1, keepdims=True)
        p = jnp.exp(sc - m)
        l = jnp.sum(p, axis=1, keepdims=True)
        o_ref[...] = (_dot(p.astype(BF16), v_ref[...]) / l).astype(o_ref.dtype)

    last = pl.program_id(2) == LT // AT - 1

    @pl.when(last)
    def _():
        qrow = lax.broadcasted_iota(jnp.int32, (AT, 1), 0)
        kcol = lax.broadcasted_iota(jnp.int32, (1, LT), 1)
        hide = jnp.logical_and(qrow >= AT - CTX, kcol < SEQ)
        finish(jnp.where(hide, -jnp.inf, s))

    @pl.when(jnp.logical_not(last))
    def _():
        finish(s)


def _attention(q, k, v):
    nb = q.shape[0]
    return pl.pallas_call(
        _attn_kernel,
        grid=(nb, NH, LT // AT),
        in_specs=[pl.BlockSpec((None, AT, 256), lambda b, h, i: (b, i, h)),
                  pl.BlockSpec((None, LT, 256), lambda b, h, i: (b, 0, h)),
                  pl.BlockSpec((None, LT, DH), lambda b, h, i: (b, 0, h))],
        out_specs=pl.BlockSpec((None, AT, DH), lambda b, h, i: (b, i, h)),
        out_shape=jax.ShapeDtypeStruct((nb, LT, NH * DH), BF16),
        compiler_params=_cparams(("parallel", "parallel", "arbitrary")),
        name="mla_attn",
    )(q, k, v)


def _merge_kernel(h_ref, b0_ref, b1_ref, b2_ref, b3_ref, wg_ref, wb_ref, o_ref):
    h = h_ref[...]
    acc = None
    for kk, br in enumerate((b0_ref, b1_ref, b2_ref, b3_ref)):
        gate = _sigmoid(_dot(h, wg_ref[kk]))
        term = gate * _dot(br[...], wb_ref[kk])
        acc = term if acc is None else acc + term
    o_ref[...] = acc.astype(o_ref.dtype)


def _merge(h, branches, wg, wb, rows, tn=256):
    nb = h.shape[0]
    tm = rows // 2
    rmap = lambda i, j: (i // 2, i % 2, 0)
    bspec = pl.BlockSpec((None, tm, BW), rmap)
    return pl.pallas_call(
        _merge_kernel,
        grid=(nb * 2, D // tn),
        in_specs=[pl.BlockSpec((None, tm, D), rmap), bspec, bspec, bspec, bspec,
                  pl.BlockSpec((4, D, tn), lambda i, j: (0, 0, j)),
                  pl.BlockSpec((4, BW, tn), lambda i, j: (0, 0, j))],
        out_specs=pl.BlockSpec((None, tm, tn), lambda i, j: (i // 2, i % 2, j)),
        out_shape=jax.ShapeDtypeStruct((nb, rows, D), BF16),
        compiler_params=_cparams(("parallel", "arbitrary")),
        name="merge",
    )(h, *branches, wg, wb)


def _outproj_kernel(first, a_ref, w_ref, *refs):
    if first:
        x_ref, c_ref = refs[:2]
        x_in = _pick_x(x_ref, c_ref)
        refs = refs[2:]
    else:
        x_in = refs[0][...]
        refs = refs[1:]
    g1_ref, gate_ref, g2_ref, sh_ref, sc_ref, wa_ref, wb_ref, wc_ref, xo_ref, hp_ref, e_ref, wt_ref = refs
    y = _dot(a_ref[...], w_ref[...])
    xn = x_in + gate_ref[...] * (_rms(y) * g1_ref[...])
    xo_ref[...] = xn
    h2 = _rms(xn) * g2_ref[...]
    h2 = h2 * (1.0 + sc_ref[...]) + sh_ref[...]
    hp_ref[...] = h2
    _route_math(h2, wa_ref, wb_ref, wc_ref, e_ref, wt_ref)


def _outproj(acc, w_out, xs, normv, modv, wr3, layer):
    nb, rows, _ = acc.shape
    first = len(xs) == 2
    wspec = pl.BlockSpec((D, 128), lambda b, i: (0, 0))
    return pl.pallas_call(
        functools.partial(_outproj_kernel, first),
        grid=(nb, rows // RT),
        in_specs=[_row_spec(D), pl.BlockSpec((D, D), lambda b, i: (0, 0))]
                 + (_split_x_specs() if first else [_row_spec(D)])
                 + [_norm_spec(layer, 1), _mod_spec(layer, 2), _norm_spec(layer, 2),
                    _mod_spec(layer, 3), _mod_spec(layer, 4), wspec, wspec, wspec],
        out_specs=[_row_spec(D), _row_spec(D), _row_spec(128), _row_spec(128)],
        out_shape=[jax.ShapeDtypeStruct((nb, rows, D), F32), jax.ShapeDtypeStruct((nb, rows, D), F32),
                   jax.ShapeDtypeStruct((nb, rows, 128), jnp.int32), jax.ShapeDtypeStruct((nb, rows, 128), F32)],
        compiler_params=_cparams(("parallel", "parallel")),
        name="outproj",
    )(acc, w_out, *xs, normv, modv, normv, modv, modv, *wr3)


def _route_math(hf, wa_ref, wb_ref, wc_ref, e_ref, w_ref):
    h = hf.astype(BF16)
    hl = (hf - h.astype(F32)).astype(BF16)
    logits = _dot(h, wa_ref[...]) + _dot(h, wb_ref[...]) + _dot(hl, wa_ref[...])
    lane = lax.broadcasted_iota(jnp.int32, (1, 128), 1).astype(F32)
    ninf = -jnp.inf
    is_grp = lane < N_GROUPS
    gl = jnp.where(is_grp, logits, ninf)
    gmax = jnp.max(gl, axis=1, keepdims=True)
    gidx = jnp.min(jnp.where(gl == gmax, lane, 999.0), axis=1, keepdims=True)
    gsum = jnp.sum(jnp.where(is_grp, jnp.exp(gl - gmax), 0.0), axis=1, keepdims=True)
    gp = 1.0 / gsum
    lo = N_GROUPS + EXP_PER_GROUP * gidx
    sel = jnp.logical_and(lane >= lo, lane < lo + EXP_PER_GROUP)
    el = jnp.where(sel, logits, ninf)
    v1 = jnp.max(el, axis=1, keepdims=True)
    i1 = jnp.min(jnp.where(el == v1, lane, 999.0), axis=1, keepdims=True)
    el2 = jnp.where(lane == i1, ninf, el)
    v2 = jnp.max(el2, axis=1, keepdims=True)
    i2 = jnp.min(jnp.where(el2 == v2, lane, 999.0), axis=1, keepdims=True)
    t = jnp.exp(v2 - v1)
    w1 = gp / (1.0 + t)
    w2 = gp * t / (1.0 + t)
    e_ref[...] = jnp.where(lane == 0.0, i1 - N_GROUPS, jnp.where(lane == 1.0, i2 - N_GROUPS, 0.0)).astype(jnp.int32)
    w_ref[...] = jnp.where(lane == 0.0, w1, jnp.where(lane == 1.0, w2, 0.0))


GATHER_UNROLL = 8


def _expert_kernel(te_ref, nu_ref, rt_ref, h_hbm, w1_ref, w3_ref, w2_ref, o_ref, xbuf, hm_sc, w1b, w3b, w2b, sem):
    i = pl.program_id(0)
    n_used = nu_ref[0]
    slot = i % 2

    def issue_tile(j, s):
        base = j * TE

        def body(g, c):
            for uu in range(GATHER_UNROLL):
                r = g * GATHER_UNROLL + uu
                pltpu.make_async_copy(h_hbm.at[pl.ds(rt_ref[base + r], 1), :],
                                      xbuf.at[s, pl.ds(r, 1), :], sem.at[s]).start(priority=1)
            return c
        lax.fori_loop(0, TE // GATHER_UNROLL, body, 0)

    def wait_tile(s):
        pltpu.make_async_copy(h_hbm.at[pl.ds(0, TE), :], xbuf.at[s], sem.at[s]).wait()

    @pl.when(jnp.logical_and(i == 0, n_used > 0))
    def _():
        issue_tile(0, 0)

    @pl.when(i < n_used)
    def _():
        wait_tile(slot)
        nbase = jnp.minimum(i + 1, n_used - 1) * TE
        oslot = 1 - slot

        def issue_rows(r0, r1):
            for r in range(r0, r1):
                pltpu.make_async_copy(h_hbm.at[pl.ds(rt_ref[nbase + r], 1), :],
                                      xbuf.at[oslot, pl.ds(r, 1), :], sem.at[oslot]).start(priority=1)

        @pl.when(jnp.logical_or(i == 0, te_ref[i] != te_ref[jnp.maximum(i - 1, 0)]))
        def _():
            w1b[...] = w1_ref[...].astype(BF16)
            w3b[...] = w3_ref[...].astype(BF16)
            w2b[...] = w2_ref[...].astype(BF16)

        x = xbuf[slot].astype(BF16)
        cw = 256
        n1, n2 = D_EXPERT // cw, D // cw
        per = TE // (n1 + n2) + 1
        done = 0
        for j in range(n1):
            issue_rows(done, min(done + per, TE))
            done = min(done + per, TE)
            cs = slice(j * cw, (j + 1) * cw)
            a = _dot(x, w1b[:, cs])
            b = _dot(x, w3b[:, cs])
            hm_sc[:, cs] = (_silu(a) * b).astype(BF16)
        hm = hm_sc[...]
        for j in range(n2):
            issue_rows(done, min(done + per, TE))
            done = min(done + per, TE)
            cs = slice(j * cw, (j + 1) * cw)
            o_ref[:, cs] = _dot(hm, w2b[:, cs])

        @pl.when(i + 1 >= n_used)
        def _():
            wait_tile(oslot)

    @pl.when(i >= n_used)
    def _():
        o_ref[...] = jnp.zeros_like(o_ref)


def _experts(tile_expert, n_used, row_token, hp, w1, w3, w2, n_tiles, layer):
    wmap = lambda i, te, nu, rt: (layer * N_EXPERTS + te[i], 0, 0)
    grid_spec = pltpu.PrefetchScalarGridSpec(
        num_scalar_prefetch=3,
        grid=(n_tiles,),
        in_specs=[pl.BlockSpec(memory_space=pl.ANY),
                  pl.BlockSpec((None, D, D_EXPERT), wmap),
                  pl.BlockSpec((None, D, D_EXPERT), wmap),
                  pl.BlockSpec((None, D_EXPERT, D), wmap)],
        out_specs=pl.BlockSpec((TE, D), lambda i, te, nu, rt: (i, 0)),
        scratch_shapes=[pltpu.VMEM((2, TE, D), F32), pltpu.VMEM((TE, D_EXPERT), BF16),
                        pltpu.VMEM((D, D_EXPERT), BF16), pltpu.VMEM((D, D_EXPERT), BF16),
                        pltpu.VMEM((D_EXPERT, D), BF16),
                        pltpu.SemaphoreType.DMA((2,))],
    )
    return pl.pallas_call(
        _expert_kernel,
        grid_spec=grid_spec,
        out_shape=jax.ShapeDtypeStruct((n_tiles * TE, D), F32),
        compiler_params=_cparams(("arbitrary",)),
        name="moe_experts",
    )(tile_expert, n_used, row_token, hp, w1, w3, w2)


def _combine_kernel(has_next, pos_ref, ys_hbm, w_ref, x_ref, g3_ref, gate_ref, *rest):
    if has_next:
        g0_ref, sh_ref, sc_ref, xo_ref, ho_ref, buf, sem = rest
    else:
        xo_ref, buf, sem = rest
    step = pl.program_id(0) * pl.num_programs(1) + pl.program_id(1)
    n_steps = pl.num_programs(0) * pl.num_programs(1)
    slot = step % 2

    def issue_tile(j, s):
        t0 = j * RT

        def body(g, c):
            for uu in range(GATHER_UNROLL):
                r = g * GATHER_UNROLL + uu
                for kk in range(2):
                    pltpu.make_async_copy(ys_hbm.at[pl.ds(pos_ref[(t0 + r) * 2 + kk], 1), :],
                                          buf.at[s, kk, pl.ds(r, 1), :], sem.at[s]).start(priority=kk)
            return c
        lax.fori_loop(0, RT // GATHER_UNROLL, body, 0)

    def wait_tile(s):
        for kk in range(2):
            pltpu.make_async_copy(ys_hbm.at[pl.ds(0, RT), :], buf.at[s, kk], sem.at[s]).wait()

    @pl.when(step == 0)
    def _():
        issue_tile(0, 0)

    wait_tile(slot)
    nt0 = jnp.minimum(step + 1, n_steps - 1) * RT
    oslot = 1 - slot
    rows = 32
    for c in range(RT // rows):
        for r in range(c * rows, (c + 1) * rows):
            for kk in range(2):
                pltpu.make_async_copy(ys_hbm.at[pl.ds(pos_ref[(nt0 + r) * 2 + kk], 1), :],
                                      buf.at[oslot, kk, pl.ds(r, 1), :], sem.at[oslot]).start(priority=kk)
        rs = slice(c * rows, (c + 1) * rows)
        w = w_ref[rs, :]
        y = w[:, 0:1] * buf[slot, 0, rs, :] + w[:, 1:2] * buf[slot, 1, rs, :]
        xn = x_ref[rs, :] + gate_ref[...] * (_rms(y) * g3_ref[...])
        xo_ref[rs, :] = xn
        if has_next:
            hn = _rms(xn) * g0_ref[...]
            ho_ref[rs, :] = (hn * (1.0 + sc_ref[...]) + sh_ref[...]).astype(BF16)

    @pl.when(step + 1 >= n_steps)
    def _():
        wait_tile(oslot)


def _combine(pos, ys, wts, xcat, normv, modv, layer):
    nb = xcat.shape[0]
    has_next = layer + 1 < DEPTH

    def mod_spec(lyr, k):
        def imap(b, i, pos_ref):
            row = jnp.where(i == NRT - 1, 8, b)
            return ((lyr * 16 + row) * 6 + k, 0, 0)
        return pl.BlockSpec((None, 1, D), imap)

    def norm_spec(lyr, k):
        return pl.BlockSpec((None, 1, D), lambda b, i, pos_ref: (lyr * 4 + k, 0, 0))

    def row_spec(width):
        return pl.BlockSpec((None, RT, width), lambda b, i, pos_ref: (b, i, 0))

    in_specs = [pl.BlockSpec(memory_space=pl.ANY), row_spec(128), row_spec(D),
                norm_spec(layer, 3), mod_spec(layer, 5)]
    args = [ys, wts, xcat, normv, modv]
    out_specs = [row_spec(D)]
    n_row_tiles = NRT if has_next else SEQ // RT
    out_shape = [jax.ShapeDtypeStruct((nb, n_row_tiles * RT, D), F32)]
    if has_next:
        in_specs += [norm_spec(layer + 1, 0), mod_spec(layer + 1, 0), mod_spec(layer + 1, 1)]
        args += [normv, modv, modv]
        out_specs.append(row_spec(D))
        out_shape.append(jax.ShapeDtypeStruct((nb, LT, D), BF16))
    grid_spec = pltpu.PrefetchScalarGridSpec(
        num_scalar_prefetch=1,
        grid=(nb, n_row_tiles),
        in_specs=in_specs,
        out_specs=out_specs,
        scratch_shapes=[pltpu.VMEM((2, 2, RT, D), F32), pltpu.SemaphoreType.DMA((2,))],
    )
    return pl.pallas_call(
        functools.partial(_combine_kernel, has_next),
        grid_spec=grid_spec,
        out_shape=out_shape,
        compiler_params=_cparams(("arbitrary", "arbitrary")),
        name="moe_combine",
    )(pos, *args)


def _plan_kernel(e_ref, pos_ref, cnt_ref, rank_sc, carry_sc):
    phase = pl.program_id(0)
    i = pl.program_id(1)
    lane = lax.broadcasted_iota(jnp.int32, (1, 128), 1)
    e = e_ref[...]
    oh1 = lane == e[:, 0:1]
    oh2 = lane == e[:, 1:2]

    @pl.when(jnp.logical_and(phase == 0, i == 0))
    def _():
        carry_sc[...] = jnp.zeros_like(carry_sc)

    @pl.when(phase == 0)
    def _():
        s = jnp.where(jnp.logical_or(oh1, oh2), 1.0, 0.0)
        row = lax.broadcasted_iota(jnp.int32, (RT, RT), 0)
        col = lax.broadcasted_iota(jnp.int32, (RT, RT), 1)
        lower = jnp.where(col < row, 1.0, 0.0).astype(BF16)
        carry = carry_sc[0:1, :]
        before = _dot(lower, s.astype(BF16)) + carry
        r1 = jnp.sum(jnp.where(oh1, before, 0.0), axis=1, keepdims=True)
        r2 = jnp.sum(jnp.where(oh2, before, 0.0), axis=1, keepdims=True)
        rank_sc[i] = jnp.where(lane == 0, r1, jnp.where(lane == 1, r2, 0.0))
        carry_sc[...] = jnp.broadcast_to(carry + jnp.sum(s, axis=0, keepdims=True), carry_sc.shape)
        cnt_ref[...] = carry_sc[...]

    @pl.when(phase == 1)
    def _():
        counts = carry_sc[0:1, :]
        padded = jnp.floor((counts + (TE - 1)) * (1.0 / TE)) * TE
        r128 = lax.broadcasted_iota(jnp.int32, (128, 128), 0)
        c128 = lax.broadcasted_iota(jnp.int32, (128, 128), 1)
        upper = jnp.where(r128 < c128, 1.0, 0.0).astype(BF16)
        pa, pb, pc = _split3(jnp.broadcast_to(padded, (8, 128)))
        offs = (_dot(pa, upper) + _dot(pb, upper) + _dot(pc, upper))[0:1, :]
        rank = rank_sc[i]
        p1 = rank[:, 0:1] + jnp.sum(jnp.where(oh1, offs, 0.0), axis=1, keepdims=True)
        p2 = rank[:, 1:2] + jnp.sum(jnp.where(oh2, offs, 0.0), axis=1, keepdims=True)
        pos_ref[...] = jnp.where(lane == 0, p1, jnp.where(lane == 1, p2, 0.0)).astype(jnp.int32)


def _plan(eid, n_row_tiles):
    nb = eid.shape[0]
    nt = nb * n_row_tiles
    emap = lambda p, i: (i // n_row_tiles, i % n_row_tiles, 0)
    return pl.pallas_call(
        _plan_kernel,
        grid=(2, nt),
        in_specs=[pl.BlockSpec((None, RT, 128), emap)],
        out_specs=[pl.BlockSpec((RT, 128), lambda p, i: (p * i, 0)),
                   pl.BlockSpec((8, 128), lambda p, i: (0, 0))],
        out_shape=[jax.ShapeDtypeStruct((nt * RT, 128), jnp.int32), jax.ShapeDtypeStruct((8, 128), F32)],
        scratch_shapes=[pltpu.VMEM((nt, RT, 128), F32), pltpu.VMEM((8, 128), F32)],
        compiler_params=_cparams(("arbitrary", "arbitrary")),
        name="moe_plan",
    )(eid)


def _dispatch_tiles(pos, counts, n_tiles, n_row_tiles, nb):
    cnt = counts[0, :N_EXPERTS].astype(jnp.int32)
    padded = ((cnt + TE - 1) // TE) * TE
    ends = jnp.cumsum(padded)
    tile_start = jnp.arange(n_tiles, dtype=jnp.int32) * TE
    tile_expert = jnp.sum((ends[None, :] <= tile_start[:, None]).astype(jnp.int32), axis=1)
    tile_expert = jnp.minimum(tile_expert, N_EXPERTS - 1)
    n_used = (ends[-1] // TE).astype(jnp.int32).reshape(1)
    pos2 = pos[:, :2]
    tok = jnp.arange(nb * n_row_tiles * RT, dtype=jnp.int32)
    tok2 = jnp.broadcast_to(tok[:, None], pos2.shape)
    row_token = jnp.zeros((n_tiles * TE,), jnp.int32).at[pos2.reshape(-1)].set(tok2.reshape(-1))
    return pos2.reshape(-1), row_token, tile_expert, n_used


def _rope_tables():
    tok = jnp.arange(SEQ)
    pos_row = (tok // GRID_W).astype(F32)
    pos_col = (tok % GRID_W).astype(F32)
    quarter = MLA_ROPE // 4
    inv_freq = ROPE_BASE ** (-jnp.arange(quarter, dtype=F32) / quarter)
    ang_r = pos_row[:, None] * inv_freq
    ang_c = pos_col[:, None] * inv_freq
    cos = jnp.concatenate([jnp.cos(ang_r)] * 2 + [jnp.cos(ang_c)] * 2, axis=1)
    sin = jnp.concatenate([jnp.sin(ang_r)] * 2 + [jnp.sin(ang_c)] * 2, axis=1)
    cos = jnp.concatenate([cos, jnp.ones((CTX, MLA_ROPE), F32)], axis=0)
    sin = jnp.concatenate([sin, jnp.zeros((CTX, MLA_ROPE), F32)], axis=0)
    pad = jnp.zeros((LT, 64), F32)
    cos_t = jnp.concatenate([jnp.ones((LT, 128), F32), cos, pad], axis=1)
    sin_t = jnp.concatenate([jnp.zeros((LT, 128), F32), sin, pad], axis=1)
    return cos_t, sin_t


def _rope_rot_matrix():
    j = jnp.arange(MLA_ROPE)
    first = (j % 32) < 16
    src = jnp.where(first, j + 16, j - 16)
    sign = jnp.where(first, -1.0, 1.0)
    return jnp.zeros((MLA_ROPE, MLA_ROPE), F32).at[src, j].set(sign)


def _layer_params(l, w_main, w_small, ml_f_bias, lb_all, hg_norm_g, mla_g_q, mla_g_kv, mla_w_uq, mla_w_ukv,
                  ssd_conv_w, ssd_conv_b, ssd_a_log, ssd_dt_bias, ssd_d, ssd_norm_g, w_gate, w_br, w_out,
                  moe_w_grp, moe_w_exp):
    p = {}
    p["w_main"] = w_main[l]
    p["w_small"] = w_small[l]
    fb = jnp.zeros((2, 2, NH), F32).at[:, 1, :].set(ml_f_bias[l]).reshape(16)
    ml_bias = jnp.concatenate([fb, jnp.zeros((112,), F32)])
    p["ml_bias_c"] = ml_bias.reshape(1, 128)
    p["ml_bias_t"] = ml_bias[:32].reshape(32, 1)
    dtb = jnp.concatenate([jnp.zeros((16,), F32), ssd_dt_bias[l].reshape(16), jnp.zeros((96,), F32)])
    p["ssd_bias_c"] = dtb.reshape(1, 128)
    p["ssd_bias_t"] = dtb[:32].reshape(32, 1)
    aneg = jnp.concatenate([jnp.zeros((16,), F32), -jnp.exp(ssd_a_log[l].astype(F32)).reshape(16),
                            jnp.zeros((96,), F32)])
    p["ssd_aneg_c"] = aneg.reshape(1, 128)
    p["ssd_aneg_t"] = aneg[:32].reshape(32, 1)
    p["lb"] = lb_all[l].reshape(1, BW)
    p["hg_norm_g"] = hg_norm_g[l].reshape(1, BW)
    p["ssd_dskip"] = jnp.repeat(ssd_d[l], SSD_P).reshape(1, BW)
    p["ssd_norm_g"] = ssd_norm_g[l].reshape(1, BW)
    p["conv_w"] = jnp.concatenate([ssd_conv_w[l], jnp.zeros((5, SSD_XBC), F32)], axis=0)
    p["conv_b"] = ssd_conv_b[l].reshape(1, SSD_XBC)
    rot = _rope_rot_matrix()
    wq = mla_w_uq[l].reshape(MLA_Q_RANK, NH, MLA_NOPE + MLA_ROPE)
    zq = jnp.zeros((MLA_Q_RANK, NH, 64), F32)
    wqa = jnp.concatenate([wq, zq], axis=2).reshape(MLA_Q_RANK, NH * 256)
    wq_rot = jnp.einsum("rhj,jk->rhk", wq[:, :, MLA_NOPE:], rot)
    wqb = jnp.concatenate([jnp.zeros((MLA_Q_RANK, NH, MLA_NOPE), F32), wq_rot, zq], axis=2)
    p["wqa"] = wqa.astype(BF16)
    p["wqb"] = wqb.reshape(MLA_Q_RANK, NH * 256).astype(BF16)
    wkv = mla_w_ukv[l].reshape(MLA_KV_RANK, NH, 2 * DH)
    wk = jnp.concatenate([wkv[:, :, :DH], jnp.zeros((MLA_KV_RANK, NH, DH), F32)], axis=2)
    p["wk"] = wk.reshape(MLA_KV_RANK, NH * 256).astype(BF16)
    p["wv"] = wkv[:, :, DH:].reshape(MLA_KV_RANK, NH * DH).astype(BF16)
    p["rk"] = jnp.zeros((128, 128), F32).at[:MLA_ROPE, :MLA_ROPE].set(rot).astype(BF16)
    p["gq"] = mla_g_q[l].reshape(1, MLA_Q_RANK)
    p["gkv"] = mla_g_kv[l].reshape(1, MLA_KV_RANK)
    p["w_gate"] = w_gate[l].astype(BF16)
    p["w_br"] = w_br[l].astype(BF16)
    p["w_out"] = w_out[l].astype(BF16)
    wr = jnp.concatenate([moe_w_grp[l], moe_w_exp[l], jnp.zeros((D, 128 - N_GROUPS - N_EXPERTS), F32)], axis=1)
    a = wr.astype(BF16)
    r = wr - a.astype(F32)
    b = r.astype(BF16)
    c = (r - b.astype(F32)).astype(BF16)
    p["wr3"] = (a, b, c)
    return p


def _mixer(h, p, cos_t, sin_t, nb, out_rows):
    t = nb * LT
    hf = h.reshape(t, D)
    zero_b = jnp.zeros((1, U_W), F32)
    u = _mm(hf, p["w_main"], zero_b, BF16, 1152, 1024).reshape(nb, LT, U_W)
    usm = _mm(hf, p["w_small"], zero_b[:, :128], F32, 1152, 128).reshape(nb, LT, 128)
    ust = usm[:, :, :32].reshape(nb, NCH, CS, 32).transpose(0, 1, 3, 2)
    ml_f = _mlstm(u, usm, ust, p["ml_bias_c"], p["ml_bias_t"], False)
    ml = _mlstm(u, usm, ust, p["ml_bias_c"], p["ml_bias_t"], True, ml_f)
    hg_f = _hgrn2(u, p["lb"], False)
    hg = _hgrn2(u, p["lb"], True, hg_f, p["hg_norm_g"])
    q, k, v = _mla_proj(u, p["gq"], p["gkv"], p["wqa"], p["wqb"], p["wk"], p["wv"], p["rk"], cos_t, sin_t)
    mla = _attention(q, k, v)
    xbc = _ssd_conv(u, p["conv_w"], p["conv_b"])
    sargs = (xbc, usm, ust, u, p["ssd_bias_c"], p["ssd_bias_t"], p["ssd_aneg_c"], p["ssd_aneg_t"])
    ssd_f = _ssd(*sargs, False)
    ssd = _ssd(*sargs, True, ssd_f, p["ssd_dskip"], p["ssd_norm_g"])
    return _merge(h, (ml, hg, mla, ssd), p["w_gate"], p["w_br"], out_rows)


def kernel(x, c, ctx, c_ctx, w_ada, b_ada, norm_g, w_in, ml_f_bias, hg_lb_logits, hg_norm_g, mla_g_q, mla_g_kv,
           mla_w_uq, mla_w_ukv, ssd_conv_w, ssd_conv_b, ssd_a_log, ssd_dt_bias, ssd_d, ssd_norm_g, w_gate,
           w_br, w_out, moe_w_grp, moe_w_exp, moe_w1, moe_w3, moe_w2):
    nb = x.shape[0]
    t = nb * LT
    lb_all = jnp.cumsum(jax.nn.softmax(hg_lb_logits.astype(F32), axis=0), axis=0)
    cvec = jnp.zeros((16, D), F32).at[:nb].set(c).at[8].set(c_ctx)
    s_vec = jax.nn.silu(cvec).astype(BF16)
    modv = _adaln_mods(s_vec, w_ada, b_ada).reshape(DEPTH * 16 * 6, 1, D)
    normv = norm_g.reshape(DEPTH * 4, 1, D)
    cos_t, sin_t = _rope_tables()
    w_main, w_small = _win_layout(w_in)
    w1 = moe_w1.reshape(DEPTH * N_EXPERTS, D, D_EXPERT)
    w3 = moe_w3.reshape(DEPTH * N_EXPERTS, D, D_EXPERT)
    w2 = moe_w2.reshape(DEPTH * N_EXPERTS, D_EXPERT, D)
    h = _prenorm(x, ctx, normv, modv, 0)
    xs = (x, ctx)
    for l in range(DEPTH):
        p = _layer_params(l, w_main, w_small, ml_f_bias, lb_all, hg_norm_g, mla_g_q, mla_g_kv, mla_w_uq,
                          mla_w_ukv, ssd_conv_w, ssd_conv_b, ssd_a_log, ssd_dt_bias, ssd_d, ssd_norm_g, w_gate,
                          w_br, w_out, moe_w_grp, moe_w_exp)
        rows = LT if l + 1 < DEPTH else SEQ
        n_row_tiles = rows // RT
        acc = _mixer(h, p, cos_t, sin_t, nb, rows)
        xcat, hp, eid, wts = _outproj(acc, p["w_out"], xs, normv, modv, p["wr3"], l)
        n_tiles = (2 * nb * rows + N_EXPERTS * (TE - 1)) // TE + 1
        pos128, counts = _plan(eid, n_row_tiles)
        pos, row_token, tile_expert, n_used = _dispatch_tiles(pos128, counts, n_tiles, n_row_tiles, nb)
        ys = _experts(tile_expert, n_used, row_token, hp.reshape(nb * rows, D), w1, w3, w2, n_tiles, l)
        outs = _combine(pos, ys, wts, xcat, normv, modv, l)
        if l + 1 < DEPTH:
            xcat, h = outs
            xs = (xcat,)
    return outs[0]
```

```python
import functools

import jax
import jax.numpy as jnp
from jax import lax
from jax.experimental import pallas as pl
from jax.experimental.pallas import tpu as pltpu

F32 = jnp.float32
BF16 = jnp.bfloat16

D = 2048
SEQ = 2048
CTX = 256
LT = SEQ + CTX
DEPTH = 2
EPS = 1e-6
GRID_W = 64
NH = 4
DH = 128
BW = 512
MLA_Q_RANK = 512
MLA_KV_RANK = 256
MLA_NOPE = 128
MLA_ROPE = 64
MLA_SCALE = (MLA_NOPE + MLA_ROPE) ** -0.5
ROPE_BASE = 10000.0
SSD_HEADS = 8
SSD_P = 64
SSD_GROUPS = 2
SSD_N = 128
SSD_XBC = 1024
N_GROUPS = 4
EXP_PER_GROUP = 8
N_EXPERTS = 32
D_EXPERT = 512

CS = 128
NLC = SEQ // CS
NCC = CTX // CS
NCH = NLC + NCC
RT = 256
NRT = LT // RT
TE = 256
VMEM_LIMIT = 52 * 1024 * 1024

U_W = 7168
CB_ML_Q, CB_ML_K, CB_ML_V, CB_ML_O = 0, 1, 2, 3
CB_HG_Q, CB_HG_I, CB_HG_G, CB_HG_F = 4, 5, 6, 7
CB_SSD_Z = 9
CB_SSD_XBC = 5
CB_MLA_CQ = 12
CB_MLA_CKV = 26
CB_MLA_KR = 54


def _cparams(sem):
    return pltpu.CompilerParams(dimension_semantics=sem, vmem_limit_bytes=VMEM_LIMIT)


def _dot(a, b):
    return jnp.dot(a, b, preferred_element_type=F32)


def _dot_nt(a, b):
    return lax.dot_general(a, b, (((1,), (1,)), ((), ())), preferred_element_type=F32)


def _sigmoid(x):
    return 1.0 / (1.0 + jnp.exp(-x))


def _silu(x):
    return x * _sigmoid(x)


def _softplus(x):
    return jnp.maximum(x, 0.0) + jnp.log(1.0 + jnp.exp(-jnp.abs(x)))


def _log_sigmoid(x):
    return -_softplus(-x)


def _rms(x):
    return x * lax.rsqrt(jnp.mean(x * x, axis=-1, keepdims=True) + EPS)


def _split3(x):
    a = x.astype(BF16)
    r = x - a.astype(F32)
    b = r.astype(BF16)
    c = (r - b.astype(F32)).astype(BF16)
    return a, b, c


def _cumsum_rows(mb, x):
    a, b, c = _split3(x)
    return _dot(mb, a) + _dot(mb, b) + _dot(mb, c)


def _cumsum_cols(x, mb):
    a, b, c = _split3(x)
    return _dot_nt(a, mb) + _dot_nt(b, mb) + _dot_nt(c, mb)


def _chunk_mask(rev):
    row = lax.broadcasted_iota(jnp.int32, (CS, CS), 0)
    col = lax.broadcasted_iota(jnp.int32, (CS, CS), 1)
    return (col >= row) if rev else (col <= row)


def _chunk_of_step(rev):
    if rev:
        return lambda s: NCH - 1 - s
    return lambda s: (s + NLC) % NCH


def _mm_kernel(a_ref, w_ref, b_ref, o_ref):
    acc = _dot(a_ref[...], w_ref[...].astype(BF16))
    o_ref[...] = (acc + b_ref[...]).astype(o_ref.dtype)


def _mm(a, w, bias, out_dtype, tm, tn):
    m, k = a.shape
    n = w.shape[1]
    return pl.pallas_call(
        _mm_kernel,
        grid=(m // tm, n // tn),
        in_specs=[pl.BlockSpec((tm, k), lambda i, j: (i, 0)),
                  pl.BlockSpec((k, tn), lambda i, j: (0, j)),
                  pl.BlockSpec((1, tn), lambda i, j: (0, j))],
        out_specs=pl.BlockSpec((tm, tn), lambda i, j: (i, j)),
        out_shape=jax.ShapeDtypeStruct((m, n), out_dtype),
        compiler_params=_cparams(("parallel", "arbitrary")),
        name="mm",
    )(a, w, bias)


W_IN_PIECES = ((0, 0, 2048),
               (2048, 2064, 2560),
               (4608, 5456, 512),
               (5120, 5968, 1024),
               (6144, 4624, 512),
               (6656, 5136, 256),
               (6912, 5392, 64),
               (7040, 2048, 16),
               (7056, 6992, 16))
D_IN = 7008


def _win_kernel(w_ref, o_ref, s_ref):
    o_ref[...] = jnp.zeros_like(o_ref)
    for dst, src, width in W_IN_PIECES:
        o_ref[:, dst:dst + width] = w_ref[:, src:src + width].astype(BF16)
    s_ref[...] = o_ref[:, U_W - 128:]


def _win_layout(w_in):
    rows = 256
    return pl.pallas_call(
        _win_kernel,
        grid=(DEPTH, D // rows),
        in_specs=[pl.BlockSpec((None, rows, D_IN), lambda l, i: (l, i, 0))],
        out_specs=[pl.BlockSpec((None, rows, U_W), lambda l, i: (l, i, 0)),
                   pl.BlockSpec((None, rows, 128), lambda l, i: (l, i, 0))],
        out_shape=[jax.ShapeDtypeStruct((DEPTH, D, U_W), BF16), jax.ShapeDtypeStruct((DEPTH, D, 128), BF16)],
        compiler_params=_cparams(("parallel", "parallel")),
        name="win_layout",
    )(w_in)


def _adaln_mods(s_vec, w_ada, b_ada):
    tn = 1024
    return pl.pallas_call(
        _mm_kernel,
        grid=(DEPTH, 6 * D // tn),
        in_specs=[pl.BlockSpec((16, D), lambda l, j: (0, 0)),
                  pl.BlockSpec((None, D, tn), lambda l, j: (l, 0, j)),
                  pl.BlockSpec((None, 1, tn), lambda l, j: (l, 0, j))],
        out_specs=pl.BlockSpec((None, 16, tn), lambda l, j: (l, 0, j)),
        out_shape=jax.ShapeDtypeStruct((DEPTH, 16, 6 * D), F32),
        compiler_params=_cparams(("parallel", "arbitrary")),
        name="adaln_mods",
    )(s_vec, w_ada, b_ada.reshape(DEPTH, 1, 6 * D))


def _mod_spec(layer, k):
    def imap(b, i):
        row = jnp.where(i == NRT - 1, 8, b)
        return ((layer * 16 + row) * 6 + k, 0, 0)
    return pl.BlockSpec((None, 1, D), imap)


def _norm_spec(layer, k):
    return pl.BlockSpec((None, 1, D), lambda b, i: (layer * 4 + k, 0, 0))


def _row_spec(width, dtype_rows=RT):
    return pl.BlockSpec((None, dtype_rows, width), lambda b, i: (b, i, 0))


def _split_x_specs():
    return [pl.BlockSpec((None, RT, D), lambda b, i: (b, jnp.minimum(i, NRT - 2), 0)),
            pl.BlockSpec((None, RT, D), lambda b, i: (b, 0, 0))]


def _pick_x(x_ref, c_ref):
    return jnp.where(pl.program_id(1) == NRT - 1, c_ref[...], x_ref[...])


def _prenorm_kernel(x_ref, c_ref, g_ref, sh_ref, sc_ref, h_ref):
    y = _rms(_pick_x(x_ref, c_ref)) * g_ref[...]
    h_ref[...] = (y * (1.0 + sc_ref[...]) + sh_ref[...]).astype(BF16)


def _prenorm(x, ctx, normv, modv, layer):
    nb = x.shape[0]
    return pl.pallas_call(
        _prenorm_kernel,
        grid=(nb, NRT),
        in_specs=_split_x_specs() + [_norm_spec(layer, 0), _mod_spec(layer, 0), _mod_spec(layer, 1)],
        out_specs=_row_spec(D),
        out_shape=jax.ShapeDtypeStruct((nb, LT, D), BF16),
        compiler_params=_cparams(("parallel", "parallel")),
        name="prenorm",
    )(x, ctx, normv, modv, modv)


def _conv_kernel(x_ref, p_ref, n_ref, w_ref, b_ref, o_ref):
    i = pl.program_id(1)
    x = x_ref[...].astype(F32)
    has_prev = jnp.logical_and(i != 0, i != NRT - 1).astype(F32)
    has_next = jnp.logical_and(i != NRT - 2, i != NRT - 1).astype(F32)
    prev_row = p_ref[15:16, :].astype(F32) * has_prev
    next_row = n_ref[0:1, :].astype(F32) * has_next
    ridx = lax.broadcasted_iota(jnp.int32, (RT, 1), 0)
    xm1 = jnp.where(ridx == 0, prev_row, pltpu.roll(x, 1, 0))
    xp1 = jnp.where(ridx == RT - 1, next_row, pltpu.roll(x, RT - 1, 0))
    w = w_ref[...]
    y = w[0:1, :] * xm1 + w[1:2, :] * x + w[2:3, :] * xp1 + b_ref[...]
    o_ref[...] = _silu(y)


def _ssd_conv(u, conv_w, conv_b):
    nb = u.shape[0]
    hb = RT // 16
    last = LT // 16 - 1
    return pl.pallas_call(
        _conv_kernel,
        grid=(nb, NRT),
        in_specs=[pl.BlockSpec((None, RT, SSD_XBC), lambda b, i: (b, i, CB_SSD_XBC)),
                  pl.BlockSpec((None, 16, SSD_XBC), lambda b, i: (b, jnp.maximum(i * hb - 1, 0), CB_SSD_XBC)),
                  pl.BlockSpec((None, 16, SSD_XBC), lambda b, i: (b, jnp.minimum((i + 1) * hb, last), CB_SSD_XBC)),
                  pl.BlockSpec((8, SSD_XBC), lambda b, i: (0, 0)),
                  pl.BlockSpec((1, SSD_XBC), lambda b, i: (0, 0))],
        out_specs=pl.BlockSpec((None, RT, SSD_XBC), lambda b, i: (b, i, 0)),
        out_shape=jax.ShapeDtypeStruct((nb, LT, SSD_XBC), F32),
        compiler_params=_cparams(("parallel", "parallel")),
        name="ssd_conv",
    )(u, u, u, conv_w, conv_b)


SCAN_BB = 2


def _mlstm_kernel(rev, bpb, *refs):
    if rev:
        (qs_ref, ks_ref, vs_ref, gc_ref, gt_ref, bc_ref, bt_ref, yps_ref, ogs_ref,
         os_ref, c_sc, n_sc, m_sc) = refs
    else:
        (qs_ref, ks_ref, vs_ref, gc_ref, gt_ref, bc_ref, bt_ref,
         os_ref, c_sc, n_sc, m_sc) = refs
    d = 1 if rev else 0

    @pl.when(pl.program_id(1) == 0)
    def _():
        c_sc[...] = jnp.zeros_like(c_sc)
        n_sc[...] = jnp.zeros_like(n_sc)
        m_sc[...] = jnp.full(m_sc.shape, -jnp.inf, F32)

    mask = _chunk_mask(rev)
    mb = jnp.where(mask, 1.0, 0.0).astype(BF16)
    scale = DH ** -0.5
    units = [(bb, h) for bb in range(bpb) for h in range(NH)]
    gates = []
    for bb in range(bpb):
        gc = gc_ref[bb] + bc_ref[...]
        gt = gt_ref[bb] + bt_ref[...]
        flog_c = _log_sigmoid(gc)
        flog_t = _log_sigmoid(gt)
        gates.append((gc, gt, flog_c, _cumsum_rows(mb, flog_c), _cumsum_cols(flog_t, mb)))
    st = []
    for bb, h in units:
        sl = slice(h * DH, (h + 1) * DH)
        q, k, v = qs_ref[bb, :, sl], ks_ref[bb, :, sl], vs_ref[bb, :, sl]
        c_h = c_sc.at[bb * NH + h]
        st.append(dict(q=q, k=k, v=v, sl=sl, qk=_dot_nt(q, k), qc=_dot(q, c_h[...].astype(BF16))))
    for (bb, h), w in zip(units, st):
        gc, gt, flog_c, fcum_c, fcum_t = gates[bb]
        ci = d * 8 + h
        cf = d * 8 + 4 + h
        w["i_col"] = gc[:, ci:ci + 1]
        w["f_col"] = flog_c[:, cf:cf + 1]
        w["fc_col"] = fcum_c[:, cf:cf + 1]
        w["m_prev"] = m_sc[bb * NH + h][0:1, 0:1]
        dm = jnp.where(mask, w["fc_col"] - fcum_t[cf:cf + 1, :] + gt[ci:ci + 1, :], -jnp.inf)
        from_state = w["fc_col"] + w["m_prev"]
        w["m_t"] = jnp.maximum(from_state, jnp.max(dm, axis=1, keepdims=True))
        w["s"] = w["qk"] * scale * jnp.exp(dm - w["m_t"])
        w["w_state"] = jnp.exp(from_state - w["m_t"])
    for (bb, h), w in zip(units, st):
        n_h = n_sc.at[bb * NH + h]
        num = _dot(w["s"].astype(BF16), w["v"]) + w["w_state"] * (w["qc"] * scale)
        qn = jnp.sum(w["q"].astype(F32) * n_h[0:1, :], axis=1, keepdims=True) * scale
        den = jnp.sum(w["s"], axis=1, keepdims=True) + w["w_state"] * qn
        hout = num / jnp.maximum(jnp.abs(den), jnp.exp(-w["m_t"]))
        if rev:
            tot = yps_ref[bb, :, w["sl"]] + hout
            os_ref[bb, :, w["sl"]] = (tot * _sigmoid(ogs_ref[bb, :, w["sl"]].astype(F32))).astype(os_ref.dtype)
        else:
            os_ref[bb, :, w["sl"]] = hout
    for (bb, h), w in zip(units, st):
        c_h, n_h, m_h = c_sc.at[bb * NH + h], n_sc.at[bb * NH + h], m_sc.at[bb * NH + h]
        f_tot = jnp.sum(w["f_col"], axis=0, keepdims=True)
        w_log = f_tot - w["fc_col"] + w["i_col"]
        m_new = jnp.maximum(f_tot + w["m_prev"], jnp.max(w_log, axis=0, keepdims=True))
        decay = jnp.exp(f_tot + w["m_prev"] - m_new)
        kw = w["k"].astype(F32) * jnp.exp(w_log - m_new)
        c_h[...] = decay * c_h[...] + _dot(kw.T.astype(BF16), w["v"])
        n_new = decay * n_h[0:1, :] + jnp.sum(kw, axis=0, keepdims=True)
        n_h[...] = jnp.broadcast_to(n_new, (8, DH))
        m_h[...] = jnp.broadcast_to(m_new, (8, DH))


def _mlstm(u, usm, ust, bias_c, bias_t, rev, yprev=None):
    nb = u.shape[0]
    bpb = SCAN_BB if nb % SCAN_BB == 0 else 1
    cmap = _chunk_of_step(rev)
    uspec = lambda cblk: pl.BlockSpec((bpb, CS, BW), lambda b, s: (b, cmap(s), cblk))
    in_specs = [uspec(CB_ML_Q), uspec(CB_ML_K), uspec(CB_ML_V),
                pl.BlockSpec((bpb, CS, 128), lambda b, s: (b, cmap(s), 0)),
                pl.BlockSpec((bpb, None, 32, CS), lambda b, s: (b, cmap(s), 0, 0)),
                pl.BlockSpec((1, 128), lambda b, s: (0, 0)),
                pl.BlockSpec((32, 1), lambda b, s: (0, 0))]
    args = [u, u, u, usm, ust, bias_c, bias_t]
    if rev:
        in_specs += [uspec(0), uspec(CB_ML_O)]
        args += [yprev, u]
    return pl.pallas_call(
        functools.partial(_mlstm_kernel, rev, bpb),
        grid=(nb // bpb, NCH),
        in_specs=in_specs,
        out_specs=uspec(0),
        out_shape=jax.ShapeDtypeStruct((nb, LT, BW), BF16 if rev else F32),
        scratch_shapes=[pltpu.VMEM((bpb * NH, DH, DH), F32), pltpu.VMEM((bpb * NH, 8, DH), F32),
                        pltpu.VMEM((bpb * NH, 8, DH), F32)],
        compiler_params=_cparams(("parallel", "arbitrary")),
        name="mlstm_rev" if rev else "mlstm_fwd",
    )(*args)


def _block_ref(g, n, off):
    g3 = g.reshape(CS // n, n, DH)
    r = g3[:, off:off + 1, :]
    return jnp.broadcast_to(r, (CS // n, n, DH)).reshape(CS, DH)


def _hgrn2_kernel(rev, bpb, *refs):
    if rev:
        qs_ref, is_ref, fs_ref, lb_ref, yps_ref, gs_ref, ng_ref, os_ref, st_sc = refs
    else:
        qs_ref, is_ref, fs_ref, lb_ref, os_ref, st_sc = refs

    @pl.when(pl.program_id(1) == 0)
    def _():
        st_sc[...] = jnp.zeros_like(st_sc)

    mask = _chunk_mask(rev)
    mb = jnp.where(mask, 1.0, 0.0).astype(BF16)
    row = lax.broadcasted_iota(jnp.int32, (CS, CS), 0)
    col = lax.broadcasted_iota(jnp.int32, (CS, CS), 1)
    tix = lax.broadcasted_iota(jnp.int32, (CS, 1), 0)
    for bb, h in [(bb, h) for bb in range(bpb) for h in range(NH)]:
        q_ref, i_ref, f_ref, o_ref = qs_ref.at[bb], is_ref.at[bb], fs_ref.at[bb], os_ref.at[bb]
        if rev:
            yp_ref, g_ref = yps_ref.at[bb], gs_ref.at[bb]
        st_h = st_sc.at[bb * NH + h]
        sl = slice(h * DH, (h + 1) * DH)
        q = _silu(q_ref[:, sl].astype(F32))
        v = i_ref[:, sl].astype(F32)
        lb = lb_ref[:, sl]
        f = lb + (1.0 - lb) * _sigmoid(f_ref[:, sl].astype(F32))
        k = 1.0 - f
        lf = jnp.log(f)
        g = _cumsum_rows(mb, lf)
        st = st_h[...]
        o = _dot_nt((q * jnp.exp(g)).astype(BF16), st.astype(BF16))
        a = jnp.zeros((CS, CS), F32)
        n = CS
        while n >= 16:
            half = n // 2
            second = jnp.bitwise_and(tix, n - 1) >= half
            rowside = jnp.logical_not(second) if rev else second
            r = _block_ref(g, n, half if rev else half - 1)
            e = jnp.exp(-jnp.abs(g - r))
            qs = jnp.where(rowside, q * e, 0.0).astype(BF16)
            ks = jnp.where(rowside, 0.0, k * e).astype(BF16)
            sh_n = n.bit_length() - 1
            same = jnp.right_shift(row, sh_n) == jnp.right_shift(col, sh_n)
            a = a + jnp.where(same, _dot_nt(qs, ks), 0.0)
            n = half
        o = o + _dot(a.astype(BF16), v.astype(BF16))
        t8 = jnp.bitwise_and(tix, 7)
        k3, g3, v3 = (a3.reshape(CS // 8, 8, DH) for a3 in (k, g, v))
        for dl in range(8):
            if dl == 0:
                ksh, gsh, vsh = k, g, v
            else:
                sh = (8 - dl) if rev else dl
                ksh = pltpu.roll(k3, sh, 1).reshape(CS, DH)
                gsh = pltpu.roll(g3, sh, 1).reshape(CS, DH)
                vsh = pltpu.roll(v3, sh, 1).reshape(CS, DH)
            valid = (t8 + dl <= 7) if rev else (t8 >= dl)
            e = jnp.exp(jnp.minimum(g - gsh, 0.0))
            coef = jnp.sum(q * ksh * e, axis=1, keepdims=True)
            o = o + jnp.where(valid, coef, 0.0) * vsh
        if rev:
            tot = yp_ref[:, sl] + o
            y = _rms(tot) * ng_ref[:, sl]
            o_ref[:, sl] = (y * _sigmoid(g_ref[:, sl].astype(F32))).astype(o_ref.dtype)
        else:
            o_ref[:, sl] = o
        g_tot = jnp.sum(lf, axis=0, keepdims=True)
        kdec = k * jnp.exp(g_tot - g)
        st_h[...] = st * jnp.exp(g_tot) + _dot(v.T.astype(BF16), kdec.astype(BF16))


def _hgrn2(u, lb, rev, yprev=None, norm_g=None):
    nb = u.shape[0]
    bpb = SCAN_BB if nb % SCAN_BB == 0 else 1
    cmap = _chunk_of_step(rev)
    d = 1 if rev else 0
    uspec = lambda cblk: pl.BlockSpec((bpb, CS, BW), lambda b, s: (b, cmap(s), cblk))
    in_specs = [uspec(CB_HG_Q), uspec(CB_HG_I), uspec(CB_HG_F + d),
                pl.BlockSpec((1, BW), lambda b, s: (0, 0))]
    args = [u, u, u, lb]
    if rev:
        in_specs += [uspec(0), uspec(CB_HG_G), pl.BlockSpec((1, BW), lambda b, s: (0, 0))]
        args += [yprev, u, norm_g]
    return pl.pallas_call(
        functools.partial(_hgrn2_kernel, rev, bpb),
        grid=(nb // bpb, NCH),
        in_specs=in_specs,
        out_specs=uspec(0),
        out_shape=jax.ShapeDtypeStruct((nb, LT, BW), BF16 if rev else F32),
        scratch_shapes=[pltpu.VMEM((bpb * NH, DH, DH), F32)],
        compiler_params=_cparams(("parallel", "arbitrary")),
        name="hgrn2_rev" if rev else "hgrn2_fwd",
    )(*args)


def _expand_heads(cols, lane_head):
    r = cols[3]
    for hh in (2, 1, 0):
        r = jnp.where(lane_head == hh, cols[hh], r)
    return r


def _ssd_kernel(rev, bpb, *refs):
    if rev:
        (xs_ref, gcs_ref, gts_ref, bc_ref, bt_ref, ac_ref, at_ref, yps_ref, zs_ref, dk_ref, ng_ref,
         os_ref, ht_sc) = refs
    else:
        xs_ref, gcs_ref, gts_ref, bc_ref, bt_ref, ac_ref, at_ref, os_ref, ht_sc = refs

    @pl.when(pl.program_id(1) == 0)
    def _():
        ht_sc[...] = jnp.zeros_like(ht_sc)

    for bb in range(bpb):
        extra = (yps_ref.at[bb], zs_ref.at[bb], dk_ref, ng_ref) if rev else ()
        _ssd_chunk(rev, xs_ref.at[bb], gcs_ref[bb], gts_ref[bb], bc_ref, bt_ref, ac_ref, at_ref,
                   os_ref.at[bb], ht_sc, bb * SSD_GROUPS, *extra)


def _ssd_chunk(rev, x_ref, gc, gt, bc_ref, bt_ref, ac_ref, at_ref, o_ref, ht_sc, ht0, yp_ref=None, z_ref=None,
               dk_ref=None, ng_ref=None):
    d = 1 if rev else 0
    mask = _chunk_mask(rev)
    mb = jnp.where(mask, 1.0, 0.0).astype(BF16)
    dt_c = _softplus(gc + bc_ref[...])
    da_c = dt_c * ac_ref[...]
    acum_c = _cumsum_rows(mb, da_c)
    dt_t = _softplus(gt + bt_ref[...])
    da_t = dt_t * at_ref[...]
    acum_t = _cumsum_cols(da_t, mb)
    gw = 4 * SSD_P
    lane_head = jnp.right_shift(lax.broadcasted_iota(jnp.int32, (1, gw), 1), 6)
    ys = []
    for grp in range(SSD_GROUPS):
        xg = x_ref[:, grp * gw:(grp + 1) * gw]
        bm = x_ref[:, BW + grp * SSD_N:BW + (grp + 1) * SSD_N]
        cm = x_ref[:, BW + SSD_GROUPS * SSD_N + grp * SSD_N:BW + SSD_GROUPS * SSD_N + (grp + 1) * SSD_N]
        cmb = cm.astype(BF16)
        cb = _dot_nt(cmb, bm.astype(BF16))
        ht = ht_sc[ht0 + grp]
        y_state = _dot(cmb, ht.astype(BF16))
        lanes = [16 + d * 8 + grp * 4 + hh for hh in range(4)]
        dt_cols = [dt_c[:, l:l + 1] for l in lanes]
        a_cols = [acum_c[:, l:l + 1] for l in lanes]
        a_tots = [jnp.sum(da_c[:, l:l + 1], axis=0, keepdims=True) for l in lanes]
        xdt = xg * _expand_heads(dt_cols, lane_head)
        xdtb = xdt.astype(BF16)
        y = jnp.exp(_expand_heads(a_cols, lane_head)) * y_state
        for hh in range(4):
            l = lanes[hh]
            dec = jnp.exp(jnp.where(mask, a_cols[hh] - acum_t[l:l + 1, :], -jnp.inf))
            yh = _dot((cb * dec).astype(BF16), xdtb)
            y = y + jnp.where(lane_head == hh, yh, 0.0)
        ys.append(y)
        w_exp = jnp.exp(_expand_heads([a_tots[hh] - a_cols[hh] for hh in range(4)], lane_head))
        e_row = jnp.exp(_expand_heads(a_tots, lane_head))
        ht_sc[ht0 + grp] = ht * e_row + _dot(bm.T.astype(BF16), (xdt * w_exp).astype(BF16))
    y = jnp.concatenate(ys, axis=1)
    if rev:
        tot = yp_ref[...] + y + dk_ref[...] * x_ref[:, :BW]
        tot = tot * _silu(z_ref[...].astype(F32))
        o_ref[...] = (_rms(tot) * ng_ref[...]).astype(o_ref.dtype)
    else:
        o_ref[...] = y


def _ssd(xbc, usm, ust, u, bias_c, bias_t, aneg_c, aneg_t, rev, yprev=None, dskip=None, norm_g=None):
    nb = xbc.shape[0]
    bpb = SCAN_BB if nb % SCAN_BB == 0 else 1
    cmap = _chunk_of_step(rev)
    in_specs = [pl.BlockSpec((bpb, CS, SSD_XBC), lambda b, s: (b, cmap(s), 0)),
                pl.BlockSpec((bpb, CS, 128), lambda b, s: (b, cmap(s), 0)),
                pl.BlockSpec((bpb, None, 32, CS), lambda b, s: (b, cmap(s), 0, 0)),
                pl.BlockSpec((1, 128), lambda b, s: (0, 0)),
                pl.BlockSpec((32, 1), lambda b, s: (0, 0)),
                pl.BlockSpec((1, 128), lambda b, s: (0, 0)),
                pl.BlockSpec((32, 1), lambda b, s: (0, 0))]
    args = [xbc, usm, ust, bias_c, bias_t, aneg_c, aneg_t]
    if rev:
        in_specs += [pl.BlockSpec((bpb, CS, BW), lambda b, s: (b, cmap(s), 0)),
                     pl.BlockSpec((bpb, CS, BW), lambda b, s: (b, cmap(s), CB_SSD_Z)),
                     pl.BlockSpec((1, BW), lambda b, s: (0, 0)),
                     pl.BlockSpec((1, BW), lambda b, s: (0, 0))]
        args += [yprev, u, dskip, norm_g]
    return pl.pallas_call(
        functools.partial(_ssd_kernel, rev, bpb),
        grid=(nb // bpb, NCH),
        in_specs=in_specs,
        out_specs=pl.BlockSpec((bpb, CS, BW), lambda b, s: (b, cmap(s), 0)),
        out_shape=jax.ShapeDtypeStruct((nb, LT, BW), BF16 if rev else F32),
        scratch_shapes=[pltpu.VMEM((bpb * SSD_GROUPS, SSD_N, 4 * SSD_P), F32)],
        compiler_params=_cparams(("parallel", "arbitrary")),
        name="ssd_rev" if rev else "ssd_fwd",
    )(*args)


def _mla_proj_kernel(cq_ref, ckv_ref, kr_ref, gq_ref, gkv_ref, wqa_ref, wqb_ref, wk_ref, wv_ref,
                     rk_ref, cos_ref, sin_ref, q_ref, k_ref, v_ref):
    nq = (_rms(cq_ref[...].astype(F32)) * gq_ref[...]).astype(BF16)
    qa = _dot(nq, wqa_ref[...])
    qb = _dot(nq, wqb_ref[...])
    cos_t = cos_ref[...]
    sin_t = sin_ref[...]
    cosf = jnp.concatenate([cos_t] * NH, axis=1)
    sinf = jnp.concatenate([sin_t] * NH, axis=1)
    q_ref[...] = ((qa * cosf + qb * sinf) * MLA_SCALE).astype(BF16)
    nkv = (_rms(ckv_ref[...].astype(F32)) * gkv_ref[...]).astype(BF16)
    kn = _dot(nkv, wk_ref[...])
    v_ref[...] = _dot(nkv, wv_ref[...]).astype(BF16)
    kr = kr_ref[...]
    krr = kr.astype(F32) * cos_t[:, 128:256] + _dot(kr, rk_ref[...]) * sin_t[:, 128:256]
    zero = jnp.zeros_like(krr)
    k_ref[...] = (kn + jnp.concatenate([zero, krr] * NH, axis=1)).astype(BF16)


def _mla_proj(u, gq, gkv, wqa, wqb, wk, wv, rk, cos_t, sin_t):
    nb = u.shape[0]
    const = lambda shape: pl.BlockSpec(shape, lambda b, i: (0, 0))
    return pl.pallas_call(
        _mla_proj_kernel,
        grid=(nb, NRT),
        in_specs=[pl.BlockSpec((None, RT, MLA_Q_RANK), lambda b, i: (b, i, CB_MLA_CQ)),
                  pl.BlockSpec((None, RT, MLA_KV_RANK), lambda b, i: (b, i, CB_MLA_CKV)),
                  pl.BlockSpec((None, RT, 128), lambda b, i: (b, i, CB_MLA_KR)),
                  const((1, MLA_Q_RANK)), const((1, MLA_KV_RANK)),
                  const((MLA_Q_RANK, NH * 256)), const((MLA_Q_RANK, NH * 256)),
                  const((MLA_KV_RANK, NH * 256)), const((MLA_KV_RANK, NH * DH)),
                  const((128, 128)),
                  pl.BlockSpec((RT, 256), lambda b, i: (i, 0)),
                  pl.BlockSpec((RT, 256), lambda b, i: (i, 0))],
        out_specs=[_row_spec(NH * 256), _row_spec(NH * 256), _row_spec(NH * DH)],
        out_shape=[jax.ShapeDtypeStruct((nb, LT, NH * 256), BF16),
                   jax.ShapeDtypeStruct((nb, LT, NH * 256), BF16),
                   jax.ShapeDtypeStruct((nb, LT, NH * DH), BF16)],
        compiler_params=_cparams(("parallel", "parallel")),
        name="mla_proj",
    )(u, u, u, gq, gkv, wqa, wqb, wk, wv, rk, cos_t, sin_t)


AT = 768


def _attn_kernel(q_ref, k_ref, v_ref, o_ref):
    s = _dot_nt(q_ref[...], k_ref[...])

    def finish(sc):
        m = jnp.max(sc, axis=1, keepdims=True)
        p = jnp.exp(sc - m)
        l = jnp.sum(p, axis=1, keepdims=True)
        o_ref[...] = (_dot(p.astype(BF16), v_ref[...]) / l).astype(o_ref.dtype)

    last = pl.program_id(2) == LT // AT - 1

    @pl.when(last)
    def _():
        qrow = lax.broadcasted_iota(jnp.int32, (AT, 1), 0)
        kcol = lax.broadcasted_iota(jnp.int32, (1, LT), 1)
        hide = jnp.logical_and(qrow >= AT - CTX, kcol < SEQ)
        finish(jnp.where(hide, -jnp.inf, s))

    @pl.when(jnp.logical_not(last))
    def _():
        finish(s)


def _attention(q, k, v):
    nb = q.shape[0]
    return pl.pallas_call(
        _attn_kernel,
        grid=(nb, NH, LT // AT),
        in_specs=[pl.BlockSpec((None, AT, 256), lambda b, h, i: (b, i, h)),
                  pl.BlockSpec((None, LT, 256), lambda b, h, i: (b, 0, h)),
                  pl.BlockSpec((None, LT, DH), lambda b, h, i: (b, 0, h))],
        out_specs=pl.BlockSpec((None, AT, DH), lambda b, h, i: (b, i, h)),
        out_shape=jax.ShapeDtypeStruct((nb, LT, NH * DH), BF16),
        compiler_params=_cparams(("parallel", "parallel", "arbitrary")),
        name="mla_attn",
    )(q, k, v)


def _merge_kernel(h_ref, b0_ref, b1_ref, b2_ref, b3_ref, wg_ref, wb_ref, o_ref):
    h = h_ref[...]
    acc = None
    for kk, br in enumerate((b0_ref, b1_ref, b2_ref, b3_ref)):
        gate = _sigmoid(_dot(h, wg_ref[kk]))
        term = gate * _dot(br[...], wb_ref[kk])
        acc = term if acc is None else acc + term
    o_ref[...] = acc.astype(o_ref.dtype)


def _merge(h, branches, wg, wb, rows, tn=256):
    nb = h.shape[0]
    tm = rows // 2
    rmap = lambda i, j: (i // 2, i % 2, 0)
    bspec = pl.BlockSpec((None, tm, BW), rmap)
    return pl.pallas_call(
        _merge_kernel,
        grid=(nb * 2, D // tn),
        in_specs=[pl.BlockSpec((None, tm, D), rmap), bspec, bspec, bspec, bspec,
                  pl.BlockSpec((4, D, tn), lambda i, j: (0, 0, j)),
                  pl.BlockSpec((4, BW, tn), lambda i, j: (0, 0, j))],
        out_specs=pl.BlockSpec((None, tm, tn), lambda i, j: (i // 2, i % 2, j)),
        out_shape=jax.ShapeDtypeStruct((nb, rows, D), BF16),
        compiler_params=_cparams(("parallel", "arbitrary")),
        name="merge",
    )(h, *branches, wg, wb)


def _outproj_kernel(first, a_ref, w_ref, *refs):
    if first:
        x_ref, c_ref = refs[:2]
        x_in = _pick_x(x_ref, c_ref)
        refs = refs[2:]
    else:
        x_in = refs[0][...]
        refs = refs[1:]
    g1_ref, gate_ref, g2_ref, sh_ref, sc_ref, wa_ref, wb_ref, wc_ref, xo_ref, hp_ref, e_ref, wt_ref = refs
    y = _dot(a_ref[...], w_ref[...])
    xn = x_in + gate_ref[...] * (_rms(y) * g1_ref[...])
    xo_ref[...] = xn
    h2 = _rms(xn) * g2_ref[...]
    h2 = h2 * (1.0 + sc_ref[...]) + sh_ref[...]
    hp_ref[...] = h2
    _route_math(h2, wa_ref, wb_ref, wc_ref, e_ref, wt_ref)


def _outproj(acc, w_out, xs, normv, modv, wr3, layer):
    nb, rows, _ = acc.shape
    first = len(xs) == 2
    wspec = pl.BlockSpec((D, 128), lambda b, i: (0, 0))
    return pl.pallas_call(
        functools.partial(_outproj_kernel, first),
        grid=(nb, rows // RT),
        in_specs=[_row_spec(D), pl.BlockSpec((D, D), lambda b, i: (0, 0))]
                 + (_split_x_specs() if first else [_row_spec(D)])
                 + [_norm_spec(layer, 1), _mod_spec(layer, 2), _norm_spec(layer, 2),
                    _mod_spec(layer, 3), _mod_spec(layer, 4), wspec, wspec, wspec],
        out_specs=[_row_spec(D), _row_spec(D), _row_spec(128), _row_spec(128)],
        out_shape=[jax.ShapeDtypeStruct((nb, rows, D), F32), jax.ShapeDtypeStruct((nb, rows, D), F32),
                   jax.ShapeDtypeStruct((nb, rows, 128), jnp.int32), jax.ShapeDtypeStruct((nb, rows, 128), F32)],
        compiler_params=_cparams(("parallel", "parallel")),
        name="outproj",
    )(acc, w_out, *xs, normv, modv, normv, modv, modv, *wr3)


def _route_math(hf, wa_ref, wb_ref, wc_ref, e_ref, w_ref):
    h = hf.astype(BF16)
    hl = (hf - h.astype(F32)).astype(BF16)
    logits = _dot(h, wa_ref[...]) + _dot(h, wb_ref[...]) + _dot(hl, wa_ref[...])
    lane = lax.broadcasted_iota(jnp.int32, (1, 128), 1).astype(F32)
    ninf = -jnp.inf
    is_grp = lane < N_GROUPS
    gl = jnp.where(is_grp, logits, ninf)
    gmax = jnp.max(gl, axis=1, keepdims=True)
    gidx = jnp.min(jnp.where(gl == gmax, lane, 999.0), axis=1, keepdims=True)
    gsum = jnp.sum(jnp.where(is_grp, jnp.exp(gl - gmax), 0.0), axis=1, keepdims=True)
    gp = 1.0 / gsum
    lo = N_GROUPS + EXP_PER_GROUP * gidx
    sel = jnp.logical_and(lane >= lo, lane < lo + EXP_PER_GROUP)
    el = jnp.where(sel, logits, ninf)
    v1 = jnp.max(el, axis=1, keepdims=True)
    i1 = jnp.min(jnp.where(el == v1, lane, 999.0), axis=1, keepdims=True)
    el2 = jnp.where(lane == i1, ninf, el)
    v2 = jnp.max(el2, axis=1, keepdims=True)
    i2 = jnp.min(jnp.where(el2 == v2, lane, 999.0), axis=1, keepdims=True)
    t = jnp.exp(v2 - v1)
    w1 = gp / (1.0 + t)
    w2 = gp * t / (1.0 + t)
    e_ref[...] = jnp.where(lane == 0.0, i1 - N_GROUPS, jnp.where(lane == 1.0, i2 - N_GROUPS, 0.0)).astype(jnp.int32)
    w_ref[...] = jnp.where(lane == 0.0, w1, jnp.where(lane == 1.0, w2, 0.0))


GATHER_UNROLL = 8


def _expert_kernel(te_ref, nu_ref, rt_ref, h_hbm, w1_ref, w3_ref, w2_ref, o_ref, xbuf, hm_sc, w1b, w3b, w2b, sem):
    i = pl.program_id(0)
    n_used = nu_ref[0]
    slot = i % 2

    def issue_tile(j, s):
        base = j * TE

        def body(g, c):
            for uu in range(GATHER_UNROLL):
                r = g * GATHER_UNROLL + uu
                pltpu.make_async_copy(h_hbm.at[pl.ds(rt_ref[base + r], 1), :],
                                      xbuf.at[s, pl.ds(r, 1), :], sem.at[s]).start(priority=1)
            return c
        lax.fori_loop(0, TE // GATHER_UNROLL, body, 0)

    def wait_tile(s):
        pltpu.make_async_copy(h_hbm.at[pl.ds(0, TE), :], xbuf.at[s], sem.at[s]).wait()

    @pl.when(jnp.logical_and(i == 0, n_used > 0))
    def _():
        issue_tile(0, 0)

    @pl.when(i < n_used)
    def _():
        wait_tile(slot)
        nbase = jnp.minimum(i + 1, n_used - 1) * TE
        oslot = 1 - slot

        def issue_rows(r0, r1):
            for r in range(r0, r1):
                pltpu.make_async_copy(h_hbm.at[pl.ds(rt_ref[nbase + r], 1), :],
                                      xbuf.at[oslot, pl.ds(r, 1), :], sem.at[oslot]).start(priority=1)

        @pl.when(jnp.logical_or(i == 0, te_ref[i] != te_ref[jnp.maximum(i - 1, 0)]))
        def _():
            w1b[...] = w1_ref[...].astype(BF16)
            w3b[...] = w3_ref[...].astype(BF16)
            w2b[...] = w2_ref[...].astype(BF16)

        x = xbuf[slot].astype(BF16)
        cw = 256
        n1, n2 = D_EXPERT // cw, D // cw
        per = TE // (n1 + n2) + 1
        done = 0
        for j in range(n1):
            issue_rows(done, min(done + per, TE))
            done = min(done + per, TE)
            cs = slice(j * cw, (j + 1) * cw)
            a = _dot(x, w1b[:, cs])
            b = _dot(x, w3b[:, cs])
            hm_sc[:, cs] = (_silu(a) * b).astype(BF16)
        hm = hm_sc[...]
        for j in range(n2):
            issue_rows(done, min(done + per, TE))
            done = min(done + per, TE)
            cs = slice(j * cw, (j + 1) * cw)
            o_ref[:, cs] = _dot(hm, w2b[:, cs])

        @pl.when(i + 1 >= n_used)
        def _():
            wait_tile(oslot)

    @pl.when(i >= n_used)
    def _():
        o_ref[...] = jnp.zeros_like(o_ref)


def _experts(tile_expert, n_used, row_token, hp, w1, w3, w2, n_tiles, layer):
    wmap = lambda i, te, nu, rt: (layer * N_EXPERTS + te[i], 0, 0)
    grid_spec = pltpu.PrefetchScalarGridSpec(
        num_scalar_prefetch=3,
        grid=(n_tiles,),
        in_specs=[pl.BlockSpec(memory_space=pl.ANY),
                  pl.BlockSpec((None, D, D_EXPERT), wmap),
                  pl.BlockSpec((None, D, D_EXPERT), wmap),
                  pl.BlockSpec((None, D_EXPERT, D), wmap)],
        out_specs=pl.BlockSpec((TE, D), lambda i, te, nu, rt: (i, 0)),
        scratch_shapes=[pltpu.VMEM((2, TE, D), F32), pltpu.VMEM((TE, D_EXPERT), BF16),
                        pltpu.VMEM((D, D_EXPERT), BF16), pltpu.VMEM((D, D_EXPERT), BF16),
                        pltpu.VMEM((D_EXPERT, D), BF16),
                        pltpu.SemaphoreType.DMA((2,))],
    )
    return pl.pallas_call(
        _expert_kernel,
        grid_spec=grid_spec,
        out_shape=jax.ShapeDtypeStruct((n_tiles * TE, D), F32),
        compiler_params=_cparams(("arbitrary",)),
        name="moe_experts",
    )(tile_expert, n_used, row_token, hp, w1, w3, w2)


def _combine_kernel(has_next, pos_ref, ys_hbm, w_ref, x_ref, g3_ref, gate_ref, *rest):
    if has_next:
        g0_ref, sh_ref, sc_ref, xo_ref, ho_ref, buf, sem = rest
    else:
        xo_ref, buf, sem = rest
    step = pl.program_id(0) * pl.num_programs(1) + pl.program_id(1)
    n_steps = pl.num_programs(0) * pl.num_programs(1)
    slot = step % 2

    def issue_tile(j, s):
        t0 = j * RT

        def body(g, c):
            for uu in range(GATHER_UNROLL):
                r = g * GATHER_UNROLL + uu
                for kk in range(2):
                    pltpu.make_async_copy(ys_hbm.at[pl.ds(pos_ref[(t0 + r) * 2 + kk], 1), :],
                                          buf.at[s, kk, pl.ds(r, 1), :], sem.at[s]).start(priority=kk)
            return c
        lax.fori_loop(0, RT // GATHER_UNROLL, body, 0)

    def wait_tile(s):
        for kk in range(2):
            pltpu.make_async_copy(ys_hbm.at[pl.ds(0, RT), :], buf.at[s, kk], sem.at[s]).wait()

    @pl.when(step == 0)
    def _():
        issue_tile(0, 0)

    wait_tile(slot)
    nt0 = jnp.minimum(step + 1, n_steps - 1) * RT
    oslot = 1 - slot
    rows = 32
    for c in range(RT // rows):
        for r in range(c * rows, (c + 1) * rows):
            for kk in range(2):
                pltpu.make_async_copy(ys_hbm.at[pl.ds(pos_ref[(nt0 + r) * 2 + kk], 1), :],
                                      buf.at[oslot, kk, pl.ds(r, 1), :], sem.at[oslot]).start(priority=kk)
        rs = slice(c * rows, (c + 1) * rows)
        w = w_ref[rs, :]
        y = w[:, 0:1] * buf[slot, 0, rs, :] + w[:, 1:2] * buf[slot, 1, rs, :]
        xn = x_ref[rs, :] + gate_ref[...] * (_rms(y) * g3_ref[...])
        xo_ref[rs, :] = xn
        if has_next:
            hn = _rms(xn) * g0_ref[...]
            ho_ref[rs, :] = (hn * (1.0 + sc_ref[...]) + sh_ref[...]).astype(BF16)

    @pl.when(step + 1 >= n_steps)
    def _():
        wait_tile(oslot)


def _combine(pos, ys, wts, xcat, normv, modv, layer):
    nb = xcat.shape[0]
    has_next = layer + 1 < DEPTH

    def mod_spec(lyr, k):
        def imap(b, i, pos_ref):
            row = jnp.where(i == NRT - 1, 8, b)
            return ((lyr * 16 + row) * 6 + k, 0, 0)
        return pl.BlockSpec((None, 1, D), imap)

    def norm_spec(lyr, k):
        return pl.BlockSpec((None, 1, D), lambda b, i, pos_ref: (lyr * 4 + k, 0, 0))

    def row_spec(width):
        return pl.BlockSpec((None, RT, width), lambda b, i, pos_ref: (b, i, 0))

    in_specs = [pl.BlockSpec(memory_space=pl.ANY), row_spec(128), row_spec(D),
                norm_spec(layer, 3), mod_spec(layer, 5)]
    args = [ys, wts, xcat, normv, modv]
    out_specs = [row_spec(D)]
    n_row_tiles = NRT if has_next else SEQ // RT
    out_shape = [jax.ShapeDtypeStruct((nb, n_row_tiles * RT, D), F32)]
    if has_next:
        in_specs += [norm_spec(layer + 1, 0), mod_spec(layer + 1, 0), mod_spec(layer + 1, 1)]
        args += [normv, modv, modv]
        out_specs.append(row_spec(D))
        out_shape.append(jax.ShapeDtypeStruct((nb, LT, D), BF16))
    grid_spec = pltpu.PrefetchScalarGridSpec(
        num_scalar_prefetch=1,
        grid=(nb, n_row_tiles),
        in_specs=in_specs,
        out_specs=out_specs,
        scratch_shapes=[pltpu.VMEM((2, 2, RT, D), F32), pltpu.SemaphoreType.DMA((2,))],
    )
    return pl.pallas_call(
        functools.partial(_combine_kernel, has_next),
        grid_spec=grid_spec,
        out_shape=out_shape,
        compiler_params=_cparams(("arbitrary", "arbitrary")),
        name="moe_combine",
    )(pos, *args)


def _plan_kernel(e_ref, pos_ref, cnt_ref, rank_sc, carry_sc):
    phase = pl.program_id(0)
    i = pl.program_id(1)
    lane = lax.broadcasted_iota(jnp.int32, (1, 128), 1)
    e = e_ref[...]
    oh1 = lane == e[:, 0:1]
    oh2 = lane == e[:, 1:2]

    @pl.when(jnp.logical_and(phase == 0, i == 0))
    def _():
        carry_sc[...] = jnp.zeros_like(carry_sc)

    @pl.when(phase == 0)
    def _():
        s = jnp.where(jnp.logical_or(oh1, oh2), 1.0, 0.0)
        row = lax.broadcasted_iota(jnp.int32, (RT, RT), 0)
        col = lax.broadcasted_iota(jnp.int32, (RT, RT), 1)
        lower = jnp.where(col < row, 1.0, 0.0).astype(BF16)
        carry = carry_sc[0:1, :]
        before = _dot(lower, s.astype(BF16)) + carry
        r1 = jnp.sum(jnp.where(oh1, before, 0.0), axis=1, keepdims=True)
        r2 = jnp.sum(jnp.where(oh2, before, 0.0), axis=1, keepdims=True)
        rank_sc[i] = jnp.where(lane == 0, r1, jnp.where(lane == 1, r2, 0.0))
        carry_sc[...] = jnp.broadcast_to(carry + jnp.sum(s, axis=0, keepdims=True), carry_sc.shape)
        cnt_ref[...] = carry_sc[...]

    @pl.when(phase == 1)
    def _():
        counts = carry_sc[0:1, :]
        padded = jnp.floor((counts + (TE - 1)) * (1.0 / TE)) * TE
        r128 = lax.broadcasted_iota(jnp.int32, (128, 128), 0)
        c128 = lax.broadcasted_iota(jnp.int32, (128, 128), 1)
        upper = jnp.where(r128 < c128, 1.0, 0.0).astype(BF16)
        pa, pb, pc = _split3(jnp.broadcast_to(padded, (8, 128)))
        offs = (_dot(pa, upper) + _dot(pb, upper) + _dot(pc, upper))[0:1, :]
        rank = rank_sc[i]
        p1 = rank[:, 0:1] + jnp.sum(jnp.where(oh1, offs, 0.0), axis=1, keepdims=True)
        p2 = rank[:, 1:2] + jnp.sum(jnp.where(oh2, offs, 0.0), axis=1, keepdims=True)
        pos_ref[...] = jnp.where(lane == 0, p1, jnp.where(lane == 1, p2, 0.0)).astype(jnp.int32)


def _plan(eid, n_row_tiles):
    nb = eid.shape[0]
    nt = nb * n_row_tiles
    emap = lambda p, i: (i // n_row_tiles, i % n_row_tiles, 0)
    return pl.pallas_call(
        _plan_kernel,
        grid=(2, nt),
        in_specs=[pl.BlockSpec((None, RT, 128), emap)],
        out_specs=[pl.BlockSpec((RT, 128), lambda p, i: (p * i, 0)),
                   pl.BlockSpec((8, 128), lambda p, i: (0, 0))],
        out_shape=[jax.ShapeDtypeStruct((nt * RT, 128), jnp.int32), jax.ShapeDtypeStruct((8, 128), F32)],
        scratch_shapes=[pltpu.VMEM((nt, RT, 128), F32), pltpu.VMEM((8, 128), F32)],
        compiler_params=_cparams(("arbitrary", "arbitrary")),
        name="moe_plan",
    )(eid)


def _invert_kernel(pos_ref, rt_ref):
    unroll = 8

    def zero(i, c):
        for uu in range(unroll):
            rt_ref[i * unroll + uu] = 0
        return c
    lax.fori_loop(0, rt_ref.shape[0] // unroll, zero, 0)

    def scatter(i, c):
        for uu in range(unroll):
            a = i * unroll + uu
            rt_ref[pos_ref[a]] = jnp.right_shift(a, 1)
        return c
    lax.fori_loop(0, pos_ref.shape[0] // unroll, scatter, 0)


def _invert(pos_flat, n_rows):
    grid_spec = pltpu.PrefetchScalarGridSpec(
        num_scalar_prefetch=1, grid=(1,), in_specs=[],
        out_specs=pl.BlockSpec(memory_space=pltpu.SMEM))
    return pl.pallas_call(
        _invert_kernel,
        grid_spec=grid_spec,
        out_shape=jax.ShapeDtypeStruct((n_rows,), jnp.int32),
        compiler_params=_cparams(("arbitrary",)),
        name="moe_invert",
    )(pos_flat)


def _dispatch_tiles(pos, counts, n_tiles, n_row_tiles, nb):
    cnt = counts[0, :N_EXPERTS].astype(jnp.int32)
    padded = ((cnt + TE - 1) // TE) * TE
    ends = jnp.cumsum(padded)
    tile_start = jnp.arange(n_tiles, dtype=jnp.int32) * TE
    tile_expert = jnp.sum((ends[None, :] <= tile_start[:, None]).astype(jnp.int32), axis=1)
    tile_expert = jnp.minimum(tile_expert, N_EXPERTS - 1)
    n_used = (ends[-1] // TE).astype(jnp.int32).reshape(1)
    pos_flat = pos[:, :2].reshape(-1)
    row_token = _invert(pos_flat, n_tiles * TE)
    return pos_flat, row_token, tile_expert, n_used


def _rope_tables():
    tok = jnp.arange(SEQ)
    pos_row = (tok // GRID_W).astype(F32)
    pos_col = (tok % GRID_W).astype(F32)
    quarter = MLA_ROPE // 4
    inv_freq = ROPE_BASE ** (-jnp.arange(quarter, dtype=F32) / quarter)
    ang_r = pos_row[:, None] * inv_freq
    ang_c = pos_col[:, None] * inv_freq
    cos = jnp.concatenate([jnp.cos(ang_r)] * 2 + [jnp.cos(ang_c)] * 2, axis=1)
    sin = jnp.concatenate([jnp.sin(ang_r)] * 2 + [jnp.sin(ang_c)] * 2, axis=1)
    cos = jnp.concatenate([cos, jnp.ones((CTX, MLA_ROPE), F32)], axis=0)
    sin = jnp.concatenate([sin, jnp.zeros((CTX, MLA_ROPE), F32)], axis=0)
    pad = jnp.zeros((LT, 64), F32)
    cos_t = jnp.concatenate([jnp.ones((LT, 128), F32), cos, pad], axis=1)
    sin_t = jnp.concatenate([jnp.zeros((LT, 128), F32), sin, pad], axis=1)
    return cos_t, sin_t


def _rope_rot_matrix():
    j = jnp.arange(MLA_ROPE)
    first = (j % 32) < 16
    src = jnp.where(first, j + 16, j - 16)
    sign = jnp.where(first, -1.0, 1.0)
    return jnp.zeros((MLA_ROPE, MLA_ROPE), F32).at[src, j].set(sign)


def _layer_params(l, w_main, w_small, ml_f_bias, lb_all, hg_norm_g, mla_g_q, mla_g_kv, mla_w_uq, mla_w_ukv,
                  ssd_conv_w, ssd_conv_b, ssd_a_log, ssd_dt_bias, ssd_d, ssd_norm_g, w_gate, w_br, w_out,
                  moe_w_grp, moe_w_exp):
    p = {}
    p["w_main"] = w_main[l]
    p["w_small"] = w_small[l]
    fb = jnp.zeros((2, 2, NH), F32).at[:, 1, :].set(ml_f_bias[l]).reshape(16)
    ml_bias = jnp.concatenate([fb, jnp.zeros((112,), F32)])
    p["ml_bias_c"] = ml_bias.reshape(1, 128)
    p["ml_bias_t"] = ml_bias[:32].reshape(32, 1)
    dtb = jnp.concatenate([jnp.zeros((16,), F32), ssd_dt_bias[l].reshape(16), jnp.zeros((96,), F32)])
    p["ssd_bias_c"] = dtb.reshape(1, 128)
    p["ssd_bias_t"] = dtb[:32].reshape(32, 1)
    aneg = jnp.concatenate([jnp.zeros((16,), F32), -jnp.exp(ssd_a_log[l].astype(F32)).reshape(16),
                            jnp.zeros((96,), F32)])
    p["ssd_aneg_c"] = aneg.reshape(1, 128)
    p["ssd_aneg_t"] = aneg[:32].reshape(32, 1)
    p["lb"] = lb_all[l].reshape(1, BW)
    p["hg_norm_g"] = hg_norm_g[l].reshape(1, BW)
    p["ssd_dskip"] = jnp.repeat(ssd_d[l], SSD_P).reshape(1, BW)
    p["ssd_norm_g"] = ssd_norm_g[l].reshape(1, BW)
    p["conv_w"] = jnp.concatenate([ssd_conv_w[l], jnp.zeros((5, SSD_XBC), F32)], axis=0)
    p["conv_b"] = ssd_conv_b[l].reshape(1, SSD_XBC)
    rot = _rope_rot_matrix()
    wq = mla_w_uq[l].reshape(MLA_Q_RANK, NH, MLA_NOPE + MLA_ROPE)
    zq = jnp.zeros((MLA_Q_RANK, NH, 64), F32)
    wqa = jnp.concatenate([wq, zq], axis=2).reshape(MLA_Q_RANK, NH * 256)
    wq_rot = jnp.einsum("rhj,jk->rhk", wq[:, :, MLA_NOPE:], rot)
    wqb = jnp.concatenate([jnp.zeros((MLA_Q_RANK, NH, MLA_NOPE), F32), wq_rot, zq], axis=2)
    p["wqa"] = wqa.astype(BF16)
    p["wqb"] = wqb.reshape(MLA_Q_RANK, NH * 256).astype(BF16)
    wkv = mla_w_ukv[l].reshape(MLA_KV_RANK, NH, 2 * DH)
    wk = jnp.concatenate([wkv[:, :, :DH], jnp.zeros((MLA_KV_RANK, NH, DH), F32)], axis=2)
    p["wk"] = wk.reshape(MLA_KV_RANK, NH * 256).astype(BF16)
    p["wv"] = wkv[:, :, DH:].reshape(MLA_KV_RANK, NH * DH).astype(BF16)
    p["rk"] = jnp.zeros((128, 128), F32).at[:MLA_ROPE, :MLA_ROPE].set(rot).astype(BF16)
    p["gq"] = mla_g_q[l].reshape(1, MLA_Q_RANK)
    p["gkv"] = mla_g_kv[l].reshape(1, MLA_KV_RANK)
    p["w_gate"] = w_gate[l].astype(BF16)
    p["w_br"] = w_br[l].astype(BF16)
    p["w_out"] = w_out[l].astype(BF16)
    wr = jnp.concatenate([moe_w_grp[l], moe_w_exp[l], jnp.zeros((D, 128 - N_GROUPS - N_EXPERTS), F32)], axis=1)
    a = wr.astype(BF16)
    r = wr - a.astype(F32)
    b = r.astype(BF16)
    c = (r - b.astype(F32)).astype(BF16)
    p["wr3"] = (a, b, c)
    return p


def _mixer(h, p, cos_t, sin_t, nb, out_rows):
    t = nb * LT
    hf = h.reshape(t, D)
    zero_b = jnp.zeros((1, U_W), F32)
    u = _mm(hf, p["w_main"], zero_b, BF16, 1152, 1024).reshape(nb, LT, U_W)
    usm = _mm(hf, p["w_small"], zero_b[:, :128], F32, 1152, 128).reshape(nb, LT, 128)
    ust = usm[:, :, :32].reshape(nb, NCH, CS, 32).transpose(0, 1, 3, 2)
    ml_f = _mlstm(u, usm, ust, p["ml_bias_c"], p["ml_bias_t"], False)
    ml = _mlstm(u, usm, ust, p["ml_bias_c"], p["ml_bias_t"], True, ml_f)
    hg_f = _hgrn2(u, p["lb"], False)
    hg = _hgrn2(u, p["lb"], True, hg_f, p["hg_norm_g"])
    q, k, v = _mla_proj(u, p["gq"], p["gkv"], p["wqa"], p["wqb"], p["wk"], p["wv"], p["rk"], cos_t, sin_t)
    mla = _attention(q, k, v)
    xbc = _ssd_conv(u, p["conv_w"], p["conv_b"])
    sargs = (xbc, usm, ust, u, p["ssd_bias_c"], p["ssd_bias_t"], p["ssd_aneg_c"], p["ssd_aneg_t"])
    ssd_f = _ssd(*sargs, False)
    ssd = _ssd(*sargs, True, ssd_f, p["ssd_dskip"], p["ssd_norm_g"])
    return _merge(h, (ml, hg, mla, ssd), p["w_gate"], p["w_br"], out_rows)


def kernel(x, c, ctx, c_ctx, w_ada, b_ada, norm_g, w_in, ml_f_bias, hg_lb_logits, hg_norm_g, mla_g_q, mla_g_kv,
           mla_w_uq, mla_w_ukv, ssd_conv_w, ssd_conv_b, ssd_a_log, ssd_dt_bias, ssd_d, ssd_norm_g, w_gate,
           w_br, w_out, moe_w_grp, moe_w_exp, moe_w1, moe_w3, moe_w2):
    nb = x.shape[0]
    t = nb * LT
    lb_all = jnp.cumsum(jax.nn.softmax(hg_lb_logits.astype(F32), axis=0), axis=0)
    cvec = jnp.zeros((16, D), F32).at[:nb].set(c).at[8].set(c_ctx)
    s_vec = jax.nn.silu(cvec).astype(BF16)
    modv = _adaln_mods(s_vec, w_ada, b_ada).reshape(DEPTH * 16 * 6, 1, D)
    normv = norm_g.reshape(DEPTH * 4, 1, D)
    cos_t, sin_t = _rope_tables()
    w_main, w_small = _win_layout(w_in)
    w1 = moe_w1.reshape(DEPTH * N_EXPERTS, D, D_EXPERT)
    w3 = moe_w3.reshape(DEPTH * N_EXPERTS, D, D_EXPERT)
    w2 = moe_w2.reshape(DEPTH * N_EXPERTS, D_EXPERT, D)
    h = _prenorm(x, ctx, normv, modv, 0)
    xs = (x, ctx)
    for l in range(DEPTH):
        p = _layer_params(l, w_main, w_small, ml_f_bias, lb_all, hg_norm_g, mla_g_q, mla_g_kv, mla_w_uq,
                          mla_w_ukv, ssd_conv_w, ssd_conv_b, ssd_a_log, ssd_dt_bias, ssd_d, ssd_norm_g, w_gate,
                          w_br, w_out, moe_w_grp, moe_w_exp)
        rows = LT if l + 1 < DEPTH else SEQ
        n_row_tiles = rows // RT
        acc = _mixer(h, p, cos_t, sin_t, nb, rows)
        xcat, hp, eid, wts = _outproj(acc, p["w_out"], xs, normv, modv, p["wr3"], l)
        n_tiles = (2 * nb * rows + N_EXPERTS * (TE - 1)) // TE + 1
        pos128, counts = _plan(eid, n_row_tiles)
        pos, row_token, tile_expert, n_used = _dispatch_tiles(pos128, counts, n_tiles, n_row_tiles, nb)
        ys = _experts(tile_expert, n_used, row_token, hp.reshape(nb * rows, D), w1, w3, w2, n_tiles, l)
        outs = _combine(pos, ys, wts, xcat, normv, modv, l)
        if l + 1 < DEPTH:
            xcat, h = outs
            xs = (xcat,)
    return outs[0]
```

```python
import functools

import jax
import jax.numpy as jnp
from jax import lax
from jax.experimental import pallas as pl
from jax.experimental.pallas import tpu as pltpu

F32 = jnp.float32
BF16 = jnp.bfloat16

D = 2048
SEQ = 2048
CTX = 256
LT = SEQ + CTX
DEPTH = 2
EPS = 1e-6
GRID_W = 64
NH = 4
DH = 128
BW = 512
MLA_Q_RANK = 512
MLA_KV_RANK = 256
MLA_NOPE = 128
MLA_ROPE = 64
MLA_SCALE = (MLA_NOPE + MLA_ROPE) ** -0.5
ROPE_BASE = 10000.0
SSD_HEADS = 8
SSD_P = 64
SSD_GROUPS = 2
SSD_N = 128
SSD_XBC = 1024
N_GROUPS = 4
EXP_PER_GROUP = 8
N_EXPERTS = 32
D_EXPERT = 512

CS = 128
NLC = SEQ // CS
NCC = CTX // CS
NCH = NLC + NCC
RT = 256
NRT = LT // RT
TE = 256
VMEM_LIMIT = 52 * 1024 * 1024

U_W = 7168
CB_ML_Q, CB_ML_K, CB_ML_V, CB_ML_O = 0, 1, 2, 3
CB_HG_Q, CB_HG_I, CB_HG_G, CB_HG_F = 4, 5, 6, 7
CB_SSD_Z = 9
CB_SSD_XBC = 5
CB_MLA_CQ = 12
CB_MLA_CKV = 26
CB_MLA_KR = 54


def _cparams(sem):
    return pltpu.CompilerParams(dimension_semantics=sem, vmem_limit_bytes=VMEM_LIMIT)


def _dot(a, b):
    return jnp.dot(a, b, preferred_element_type=F32)


def _dot_nt(a, b):
    return lax.dot_general(a, b, (((1,), (1,)), ((), ())), preferred_element_type=F32)


def _sigmoid(x):
    return 1.0 / (1.0 + jnp.exp(-x))


def _silu(x):
    return x * _sigmoid(x)


def _softplus(x):
    return jnp.maximum(x, 0.0) + jnp.log(1.0 + jnp.exp(-jnp.abs(x)))


def _log_sigmoid(x):
    return -_softplus(-x)


def _rms(x):
    return x * lax.rsqrt(jnp.mean(x * x, axis=-1, keepdims=True) + EPS)


def _split3(x):
    a = x.astype(BF16)
    r = x - a.astype(F32)
    b = r.astype(BF16)
    c = (r - b.astype(F32)).astype(BF16)
    return a, b, c


def _cumsum_rows(mb, x):
    a, b, c = _split3(x)
    return _dot(mb, a) + _dot(mb, b) + _dot(mb, c)


def _cumsum_cols(x, mb):
    a, b, c = _split3(x)
    return _dot_nt(a, mb) + _dot_nt(b, mb) + _dot_nt(c, mb)


def _chunk_mask(rev):
    row = lax.broadcasted_iota(jnp.int32, (CS, CS), 0)
    col = lax.broadcasted_iota(jnp.int32, (CS, CS), 1)
    return (col >= row) if rev else (col <= row)


def _chunk_of_step(rev):
    if rev:
        return lambda s: NCH - 1 - s
    return lambda s: (s + NLC) % NCH


def _mm_kernel(a_ref, w_ref, b_ref, o_ref):
    acc = _dot(a_ref[...], w_ref[...].astype(BF16))
    o_ref[...] = (acc + b_ref[...]).astype(o_ref.dtype)


def _inproj_kernel(a_ref, w_ref, o_ref, s_ref):
    acc = _dot(a_ref[...], w_ref[...])
    o_ref[...] = acc.astype(o_ref.dtype)

    @pl.when(pl.program_id(1) == pl.num_programs(1) - 1)
    def _():
        s_ref[...] = acc[:, acc.shape[1] - 128:]


def _inproj(a, w, tm=1152, tn=1024):
    m, k = a.shape
    n = w.shape[1]
    return pl.pallas_call(
        _inproj_kernel,
        grid=(m // tm, n // tn),
        in_specs=[pl.BlockSpec((tm, k), lambda i, j: (i, 0)),
                  pl.BlockSpec((k, tn), lambda i, j: (0, j))],
        out_specs=[pl.BlockSpec((tm, tn), lambda i, j: (i, j)),
                   pl.BlockSpec((tm, 128), lambda i, j: (i, 0))],
        out_shape=[jax.ShapeDtypeStruct((m, n), BF16), jax.ShapeDtypeStruct((m, 128), F32)],
        compiler_params=_cparams(("parallel", "arbitrary")),
        name="inproj",
    )(a, w)


W_IN_PIECES = ((0, 0, 2048),
               (2048, 2064, 2560),
               (4608, 5456, 512),
               (5120, 5968, 1024),
               (6144, 4624, 512),
               (6656, 5136, 256),
               (6912, 5392, 64),
               (7040, 2048, 16),
               (7056, 6992, 16))
D_IN = 7008


def _win_kernel(w_ref, o_ref):
    o_ref[...] = jnp.zeros_like(o_ref)
    for dst, src, width in W_IN_PIECES:
        o_ref[:, dst:dst + width] = w_ref[:, src:src + width].astype(BF16)


def _win_layout(w_in):
    rows = 256
    return pl.pallas_call(
        _win_kernel,
        grid=(DEPTH, D // rows),
        in_specs=[pl.BlockSpec((None, rows, D_IN), lambda l, i: (l, i, 0))],
        out_specs=pl.BlockSpec((None, rows, U_W), lambda l, i: (l, i, 0)),
        out_shape=jax.ShapeDtypeStruct((DEPTH, D, U_W), BF16),
        compiler_params=_cparams(("parallel", "parallel")),
        name="win_layout",
    )(w_in)


def _adaln_mods(s_vec, w_ada, b_ada):
    tn = 1024
    return pl.pallas_call(
        _mm_kernel,
        grid=(DEPTH, 6 * D // tn),
        in_specs=[pl.BlockSpec((16, D), lambda l, j: (0, 0)),
                  pl.BlockSpec((None, D, tn), lambda l, j: (l, 0, j)),
                  pl.BlockSpec((None, 1, tn), lambda l, j: (l, 0, j))],
        out_specs=pl.BlockSpec((None, 16, tn), lambda l, j: (l, 0, j)),
        out_shape=jax.ShapeDtypeStruct((DEPTH, 16, 6 * D), F32),
        compiler_params=_cparams(("parallel", "arbitrary")),
        name="adaln_mods",
    )(s_vec, w_ada, b_ada.reshape(DEPTH, 1, 6 * D))


def _mod_spec(layer, k):
    def imap(b, i):
        row = jnp.where(i == NRT - 1, 8, b)
        return ((layer * 16 + row) * 6 + k, 0, 0)
    return pl.BlockSpec((None, 1, D), imap)


def _norm_spec(layer, k):
    return pl.BlockSpec((None, 1, D), lambda b, i: (layer * 4 + k, 0, 0))


def _row_spec(width, dtype_rows=RT):
    return pl.BlockSpec((None, dtype_rows, width), lambda b, i: (b, i, 0))


def _split_x_specs():
    return [pl.BlockSpec((None, RT, D), lambda b, i: (b, jnp.minimum(i, NRT - 2), 0)),
            pl.BlockSpec((None, RT, D), lambda b, i: (b, 0, 0))]


def _pick_x(x_ref, c_ref):
    return jnp.where(pl.program_id(1) == NRT - 1, c_ref[...], x_ref[...])


def _prenorm_kernel(x_ref, c_ref, g_ref, sh_ref, sc_ref, h_ref):
    y = _rms(_pick_x(x_ref, c_ref)) * g_ref[...]
    h_ref[...] = (y * (1.0 + sc_ref[...]) + sh_ref[...]).astype(BF16)


def _prenorm(x, ctx, normv, modv, layer):
    nb = x.shape[0]
    return pl.pallas_call(
        _prenorm_kernel,
        grid=(nb, NRT),
        in_specs=_split_x_specs() + [_norm_spec(layer, 0), _mod_spec(layer, 0), _mod_spec(layer, 1)],
        out_specs=_row_spec(D),
        out_shape=jax.ShapeDtypeStruct((nb, LT, D), BF16),
        compiler_params=_cparams(("parallel", "parallel")),
        name="prenorm",
    )(x, ctx, normv, modv, modv)


def _conv_kernel(x_ref, p_ref, n_ref, w_ref, b_ref, o_ref):
    i = pl.program_id(1)
    x = x_ref[...].astype(F32)
    has_prev = jnp.logical_and(i != 0, i != NRT - 1).astype(F32)
    has_next = jnp.logical_and(i != NRT - 2, i != NRT - 1).astype(F32)
    prev_row = p_ref[15:16, :].astype(F32) * has_prev
    next_row = n_ref[0:1, :].astype(F32) * has_next
    ridx = lax.broadcasted_iota(jnp.int32, (RT, 1), 0)
    xm1 = jnp.where(ridx == 0, prev_row, pltpu.roll(x, 1, 0))
    xp1 = jnp.where(ridx == RT - 1, next_row, pltpu.roll(x, RT - 1, 0))
    w = w_ref[...]
    y = w[0:1, :] * xm1 + w[1:2, :] * x + w[2:3, :] * xp1 + b_ref[...]
    o_ref[...] = _silu(y)


def _ssd_conv(u, conv_w, conv_b):
    nb = u.shape[0]
    hb = RT // 16
    last = LT // 16 - 1
    return pl.pallas_call(
        _conv_kernel,
        grid=(nb, NRT),
        in_specs=[pl.BlockSpec((None, RT, SSD_XBC), lambda b, i: (b, i, CB_SSD_XBC)),
                  pl.BlockSpec((None, 16, SSD_XBC), lambda b, i: (b, jnp.maximum(i * hb - 1, 0), CB_SSD_XBC)),
                  pl.BlockSpec((None, 16, SSD_XBC), lambda b, i: (b, jnp.minimum((i + 1) * hb, last), CB_SSD_XBC)),
                  pl.BlockSpec((8, SSD_XBC), lambda b, i: (0, 0)),
                  pl.BlockSpec((1, SSD_XBC), lambda b, i: (0, 0))],
        out_specs=pl.BlockSpec((None, RT, SSD_XBC), lambda b, i: (b, i, 0)),
        out_shape=jax.ShapeDtypeStruct((nb, LT, SSD_XBC), F32),
        compiler_params=_cparams(("parallel", "parallel")),
        name="ssd_conv",
    )(u, u, u, conv_w, conv_b)


SCAN_BB = 2


def _mlstm_kernel(rev, bpb, *refs):
    if rev:
        (qs_ref, ks_ref, vs_ref, gc_ref, gt_ref, bc_ref, bt_ref, yps_ref, ogs_ref,
         os_ref, c_sc, n_sc, m_sc) = refs
    else:
        (qs_ref, ks_ref, vs_ref, gc_ref, gt_ref, bc_ref, bt_ref,
         os_ref, c_sc, n_sc, m_sc) = refs
    d = 1 if rev else 0

    @pl.when(pl.program_id(1) == 0)
    def _():
        c_sc[...] = jnp.zeros_like(c_sc)
        n_sc[...] = jnp.zeros_like(n_sc)
        m_sc[...] = jnp.full(m_sc.shape, -jnp.inf, F32)

    mask = _chunk_mask(rev)
    mb = jnp.where(mask, 1.0, 0.0).astype(BF16)
    scale = DH ** -0.5
    units = [(bb, h) for bb in range(bpb) for h in range(NH)]
    gates = []
    for bb in range(bpb):
        gc = gc_ref[bb] + bc_ref[...]
        gt = gt_ref[bb] + bt_ref[...]
        flog_c = _log_sigmoid(gc)
        flog_t = _log_sigmoid(gt)
        gates.append((gc, gt, flog_c, _cumsum_rows(mb, flog_c), _cumsum_cols(flog_t, mb)))
    st = []
    for bb, h in units:
        sl = slice(h * DH, (h + 1) * DH)
        q, k, v = qs_ref[bb, :, sl], ks_ref[bb, :, sl], vs_ref[bb, :, sl]
        c_h = c_sc.at[bb * NH + h]
        st.append(dict(q=q, k=k, v=v, sl=sl, qk=_dot_nt(q, k), qc=_dot(q, c_h[...].astype(BF16))))
    for (bb, h), w in zip(units, st):
        gc, gt, flog_c, fcum_c, fcum_t = gates[bb]
        ci = d * 8 + h
        cf = d * 8 + 4 + h
        w["i_col"] = gc[:, ci:ci + 1]
        w["f_col"] = flog_c[:, cf:cf + 1]
        w["fc_col"] = fcum_c[:, cf:cf + 1]
        w["m_prev"] = m_sc[bb * NH + h][0:1, 0:1]
        dm = jnp.where(mask, w["fc_col"] - fcum_t[cf:cf + 1, :] + gt[ci:ci + 1, :], -jnp.inf)
        from_state = w["fc_col"] + w["m_prev"]
        w["m_t"] = jnp.maximum(from_state, jnp.max(dm, axis=1, keepdims=True))
        w["s"] = w["qk"] * scale * jnp.exp(dm - w["m_t"])
        w["w_state"] = jnp.exp(from_state - w["m_t"])
    for (bb, h), w in zip(units, st):
        n_h = n_sc.at[bb * NH + h]
        num = _dot(w["s"].astype(BF16), w["v"]) + w["w_state"] * (w["qc"] * scale)
        qn = jnp.sum(w["q"].astype(F32) * n_h[0:1, :], axis=1, keepdims=True) * scale
        den = jnp.sum(w["s"], axis=1, keepdims=True) + w["w_state"] * qn
        hout = num / jnp.maximum(jnp.abs(den), jnp.exp(-w["m_t"]))
        if rev:
            tot = yps_ref[bb, :, w["sl"]] + hout
            os_ref[bb, :, w["sl"]] = (tot * _sigmoid(ogs_ref[bb, :, w["sl"]].astype(F32))).astype(os_ref.dtype)
        else:
            os_ref[bb, :, w["sl"]] = hout
    for (bb, h), w in zip(units, st):
        c_h, n_h, m_h = c_sc.at[bb * NH + h], n_sc.at[bb * NH + h], m_sc.at[bb * NH + h]
        f_tot = jnp.sum(w["f_col"], axis=0, keepdims=True)
        w_log = f_tot - w["fc_col"] + w["i_col"]
        m_new = jnp.maximum(f_tot + w["m_prev"], jnp.max(w_log, axis=0, keepdims=True))
        decay = jnp.exp(f_tot + w["m_prev"] - m_new)
        kw = w["k"].astype(F32) * jnp.exp(w_log - m_new)
        c_h[...] = decay * c_h[...] + _dot(kw.T.astype(BF16), w["v"])
        n_new = decay * n_h[0:1, :] + jnp.sum(kw, axis=0, keepdims=True)
        n_h[...] = jnp.broadcast_to(n_new, (8, DH))
        m_h[...] = jnp.broadcast_to(m_new, (8, DH))


def _mlstm(u, usm, ust, bias_c, bias_t, rev, yprev=None):
    nb = u.shape[0]
    bpb = SCAN_BB if nb % SCAN_BB == 0 else 1
    cmap = _chunk_of_step(rev)
    uspec = lambda cblk: pl.BlockSpec((bpb, CS, BW), lambda b, s: (b, cmap(s), cblk))
    in_specs = [uspec(CB_ML_Q), uspec(CB_ML_K), uspec(CB_ML_V),
                pl.BlockSpec((bpb, CS, 128), lambda b, s: (b, cmap(s), 0)),
                pl.BlockSpec((bpb, None, 32, CS), lambda b, s: (b, cmap(s), 0, 0)),
                pl.BlockSpec((1, 128), lambda b, s: (0, 0)),
                pl.BlockSpec((32, 1), lambda b, s: (0, 0))]
    args = [u, u, u, usm, ust, bias_c, bias_t]
    if rev:
        in_specs += [uspec(0), uspec(CB_ML_O)]
        args += [yprev, u]
    return pl.pallas_call(
        functools.partial(_mlstm_kernel, rev, bpb),
        grid=(nb // bpb, NCH),
        in_specs=in_specs,
        out_specs=uspec(0),
        out_shape=jax.ShapeDtypeStruct((nb, LT, BW), BF16 if rev else F32),
        scratch_shapes=[pltpu.VMEM((bpb * NH, DH, DH), F32), pltpu.VMEM((bpb * NH, 8, DH), F32),
                        pltpu.VMEM((bpb * NH, 8, DH), F32)],
        compiler_params=_cparams(("parallel", "arbitrary")),
        name="mlstm_rev" if rev else "mlstm_fwd",
    )(*args)


def _block_ref(g, n, off):
    g3 = g.reshape(CS // n, n, DH)
    r = g3[:, off:off + 1, :]
    return jnp.broadcast_to(r, (CS // n, n, DH)).reshape(CS, DH)


def _hgrn2_kernel(rev, bpb, *refs):
    if rev:
        qs_ref, is_ref, fs_ref, lb_ref, yps_ref, gs_ref, ng_ref, os_ref, st_sc = refs
    else:
        qs_ref, is_ref, fs_ref, lb_ref, os_ref, st_sc = refs

    @pl.when(pl.program_id(1) == 0)
    def _():
        st_sc[...] = jnp.zeros_like(st_sc)

    mask = _chunk_mask(rev)
    mb = jnp.where(mask, 1.0, 0.0).astype(BF16)
    row = lax.broadcasted_iota(jnp.int32, (CS, CS), 0)
    col = lax.broadcasted_iota(jnp.int32, (CS, CS), 1)
    tix = lax.broadcasted_iota(jnp.int32, (CS, 1), 0)
    for bb, h in [(bb, h) for bb in range(bpb) for h in range(NH)]:
        q_ref, i_ref, f_ref, o_ref = qs_ref.at[bb], is_ref.at[bb], fs_ref.at[bb], os_ref.at[bb]
        if rev:
            yp_ref, g_ref = yps_ref.at[bb], gs_ref.at[bb]
        st_h = st_sc.at[bb * NH + h]
        sl = slice(h * DH, (h + 1) * DH)
        q = _silu(q_ref[:, sl].astype(F32))
        v = i_ref[:, sl].astype(F32)
        lb = lb_ref[:, sl]
        f = lb + (1.0 - lb) * _sigmoid(f_ref[:, sl].astype(F32))
        k = 1.0 - f
        lf = jnp.log(f)
        g = _cumsum_rows(mb, lf)
        st = st_h[...]
        o = _dot_nt((q * jnp.exp(g)).astype(BF16), st.astype(BF16))
        a = jnp.zeros((CS, CS), F32)
        n = CS
        while n >= 16:
            half = n // 2
            second = jnp.bitwise_and(tix, n - 1) >= half
            rowside = jnp.logical_not(second) if rev else second
            r = _block_ref(g, n, half if rev else half - 1)
            e = jnp.exp(jnp.where(rowside, g - r, r - g))
            qs = jnp.where(rowside, q * e, 0.0).astype(BF16)
            ks = jnp.where(rowside, 0.0, k * e).astype(BF16)
            sh_n = n.bit_length() - 1
            same = jnp.right_shift(row, sh_n) == jnp.right_shift(col, sh_n)
            a = a + jnp.where(same, _dot_nt(qs, ks), 0.0)
            n = half
        o = o + _dot(a.astype(BF16), v.astype(BF16))
        t8 = jnp.bitwise_and(tix, 7)
        k3, g3, v3 = (a3.reshape(CS // 8, 8, DH) for a3 in (k, g, v))
        for dl in range(8):
            if dl == 0:
                ksh, gsh, vsh = k, g, v
            else:
                sh = (8 - dl) if rev else dl
                ksh = pltpu.roll(k3, sh, 1).reshape(CS, DH)
                gsh = pltpu.roll(g3, sh, 1).reshape(CS, DH)
                vsh = pltpu.roll(v3, sh, 1).reshape(CS, DH)
            valid = (t8 + dl <= 7) if rev else (t8 >= dl)
            e = jnp.exp(jnp.minimum(g - gsh, 0.0))
            coef = jnp.sum(q * ksh * e, axis=1, keepdims=True)
            o = o + jnp.where(valid, coef, 0.0) * vsh
        if rev:
            tot = yp_ref[:, sl] + o
            y = _rms(tot) * ng_ref[:, sl]
            o_ref[:, sl] = (y * _sigmoid(g_ref[:, sl].astype(F32))).astype(o_ref.dtype)
        else:
            o_ref[:, sl] = o
        g_tot = jnp.sum(lf, axis=0, keepdims=True)
        kdec = k * jnp.exp(g_tot - g)
        st_h[...] = st * jnp.exp(g_tot) + _dot(v.T.astype(BF16), kdec.astype(BF16))


def _hgrn2(u, lb, rev, yprev=None, norm_g=None):
    nb = u.shape[0]
    bpb = SCAN_BB if nb % SCAN_BB == 0 else 1
    cmap = _chunk_of_step(rev)
    d = 1 if rev else 0
    uspec = lambda cblk: pl.BlockSpec((bpb, CS, BW), lambda b, s: (b, cmap(s), cblk))
    in_specs = [uspec(CB_HG_Q), uspec(CB_HG_I), uspec(CB_HG_F + d),
                pl.BlockSpec((1, BW), lambda b, s: (0, 0))]
    args = [u, u, u, lb]
    if rev:
        in_specs += [uspec(0), uspec(CB_HG_G), pl.BlockSpec((1, BW), lambda b, s: (0, 0))]
        args += [yprev, u, norm_g]
    return pl.pallas_call(
        functools.partial(_hgrn2_kernel, rev, bpb),
        grid=(nb // bpb, NCH),
        in_specs=in_specs,
        out_specs=uspec(0),
        out_shape=jax.ShapeDtypeStruct((nb, LT, BW), BF16 if rev else F32),
        scratch_shapes=[pltpu.VMEM((bpb * NH, DH, DH), F32)],
        compiler_params=_cparams(("parallel", "arbitrary")),
        name="hgrn2_rev" if rev else "hgrn2_fwd",
    )(*args)


def _expand_heads(cols, lane_head):
    r = cols[3]
    for hh in (2, 1, 0):
        r = jnp.where(lane_head == hh, cols[hh], r)
    return r


def _ssd_kernel(rev, bpb, *refs):
    if rev:
        (xs_ref, gcs_ref, gts_ref, bc_ref, bt_ref, ac_ref, at_ref, yps_ref, zs_ref, dk_ref, ng_ref,
         os_ref, ht_sc) = refs
    else:
        xs_ref, gcs_ref, gts_ref, bc_ref, bt_ref, ac_ref, at_ref, os_ref, ht_sc = refs

    @pl.when(pl.program_id(1) == 0)
    def _():
        ht_sc[...] = jnp.zeros_like(ht_sc)

    for bb in range(bpb):
        extra = (yps_ref.at[bb], zs_ref.at[bb], dk_ref, ng_ref) if rev else ()
        _ssd_chunk(rev, xs_ref.at[bb], gcs_ref[bb], gts_ref[bb], bc_ref, bt_ref, ac_ref, at_ref,
                   os_ref.at[bb], ht_sc, bb * SSD_GROUPS, *extra)


def _ssd_chunk(rev, x_ref, gc, gt, bc_ref, bt_ref, ac_ref, at_ref, o_ref, ht_sc, ht0, yp_ref=None, z_ref=None,
               dk_ref=None, ng_ref=None):
    d = 1 if rev else 0
    mask = _chunk_mask(rev)
    mb = jnp.where(mask, 1.0, 0.0).astype(BF16)
    dt_c = _softplus(gc + bc_ref[...])
    da_c = dt_c * ac_ref[...]
    acum_c = _cumsum_rows(mb, da_c)
    dt_t = _softplus(gt + bt_ref[...])
    da_t = dt_t * at_ref[...]
    acum_t = _cumsum_cols(da_t, mb)
    gw = 4 * SSD_P
    lane_head = jnp.right_shift(lax.broadcasted_iota(jnp.int32, (1, gw), 1), 6)
    ys = []
    for grp in range(SSD_GROUPS):
        xg = x_ref[:, grp * gw:(grp + 1) * gw]
        bm = x_ref[:, BW + grp * SSD_N:BW + (grp + 1) * SSD_N]
        cm = x_ref[:, BW + SSD_GROUPS * SSD_N + grp * SSD_N:BW + SSD_GROUPS * SSD_N + (grp + 1) * SSD_N]
        cmb = cm.astype(BF16)
        cb = _dot_nt(cmb, bm.astype(BF16))
        ht = ht_sc[ht0 + grp]
        y_state = _dot(cmb, ht.astype(BF16))
        lanes = [16 + d * 8 + grp * 4 + hh for hh in range(4)]
        dt_cols = [dt_c[:, l:l + 1] for l in lanes]
        a_cols = [acum_c[:, l:l + 1] for l in lanes]
        a_tots = [jnp.sum(da_c[:, l:l + 1], axis=0, keepdims=True) for l in lanes]
        xdt = xg * _expand_heads(dt_cols, lane_head)
        xdtb = xdt.astype(BF16)
        y = jnp.exp(_expand_heads(a_cols, lane_head)) * y_state
        for hh in range(4):
            l = lanes[hh]
            dec = jnp.exp(jnp.where(mask, a_cols[hh] - acum_t[l:l + 1, :], -jnp.inf))
            yh = _dot((cb * dec).astype(BF16), xdtb)
            y = y + jnp.where(lane_head == hh, yh, 0.0)
        ys.append(y)
        w_exp = jnp.exp(_expand_heads([a_tots[hh] - a_cols[hh] for hh in range(4)], lane_head))
        e_row = jnp.exp(_expand_heads(a_tots, lane_head))
        ht_sc[ht0 + grp] = ht * e_row + _dot(bm.T.astype(BF16), (xdt * w_exp).astype(BF16))
    y = jnp.concatenate(ys, axis=1)
    if rev:
        tot = yp_ref[...] + y + dk_ref[...] * x_ref[:, :BW]
        tot = tot * _silu(z_ref[...].astype(F32))
        o_ref[...] = (_rms(tot) * ng_ref[...]).astype(o_ref.dtype)
    else:
        o_ref[...] = y


def _ssd(xbc, usm, ust, u, bias_c, bias_t, aneg_c, aneg_t, rev, yprev=None, dskip=None, norm_g=None):
    nb = xbc.shape[0]
    bpb = SCAN_BB if nb % SCAN_BB == 0 else 1
    cmap = _chunk_of_step(rev)
    in_specs = [pl.BlockSpec((bpb, CS, SSD_XBC), lambda b, s: (b, cmap(s), 0)),
                pl.BlockSpec((bpb, CS, 128), lambda b, s: (b, cmap(s), 0)),
                pl.BlockSpec((bpb, None, 32, CS), lambda b, s: (b, cmap(s), 0, 0)),
                pl.BlockSpec((1, 128), lambda b, s: (0, 0)),
                pl.BlockSpec((32, 1), lambda b, s: (0, 0)),
                pl.BlockSpec((1, 128), lambda b, s: (0, 0)),
                pl.BlockSpec((32, 1), lambda b, s: (0, 0))]
    args = [xbc, usm, ust, bias_c, bias_t, aneg_c, aneg_t]
    if rev:
        in_specs += [pl.BlockSpec((bpb, CS, BW), lambda b, s: (b, cmap(s), 0)),
                     pl.BlockSpec((bpb, CS, BW), lambda b, s: (b, cmap(s), CB_SSD_Z)),
                     pl.BlockSpec((1, BW), lambda b, s: (0, 0)),
                     pl.BlockSpec((1, BW), lambda b, s: (0, 0))]
        args += [yprev, u, dskip, norm_g]
    return pl.pallas_call(
        functools.partial(_ssd_kernel, rev, bpb),
        grid=(nb // bpb, NCH),
        in_specs=in_specs,
        out_specs=pl.BlockSpec((bpb, CS, BW), lambda b, s: (b, cmap(s), 0)),
        out_shape=jax.ShapeDtypeStruct((nb, LT, BW), BF16 if rev else F32),
        scratch_shapes=[pltpu.VMEM((bpb * SSD_GROUPS, SSD_N, 4 * SSD_P), F32)],
        compiler_params=_cparams(("parallel", "arbitrary")),
        name="ssd_rev" if rev else "ssd_fwd",
    )(*args)


def _mla_proj_kernel(cq_ref, ckv_ref, kr_ref, gq_ref, gkv_ref, wqa_ref, wqb_ref, wk_ref, wv_ref,
                     rk_ref, cos_ref, sin_ref, q_ref, k_ref, v_ref):
    nq = (_rms(cq_ref[...].astype(F32)) * gq_ref[...]).astype(BF16)
    qa = _dot(nq, wqa_ref[...])
    qb = _dot(nq, wqb_ref[...])
    cos_t = cos_ref[...]
    sin_t = sin_ref[...]
    cosf = jnp.concatenate([cos_t] * NH, axis=1)
    sinf = jnp.concatenate([sin_t] * NH, axis=1)
    q_ref[...] = ((qa * cosf + qb * sinf) * MLA_SCALE).astype(BF16)
    nkv = (_rms(ckv_ref[...].astype(F32)) * gkv_ref[...]).astype(BF16)
    kn = _dot(nkv, wk_ref[...])
    v_ref[...] = _dot(nkv, wv_ref[...]).astype(BF16)
    kr = kr_ref[...]
    krr = kr.astype(F32) * cos_t[:, 128:256] + _dot(kr, rk_ref[...]) * sin_t[:, 128:256]
    zero = jnp.zeros_like(krr)
    k_ref[...] = (kn + jnp.concatenate([zero, krr] * NH, axis=1)).astype(BF16)


def _mla_proj(u, gq, gkv, wqa, wqb, wk, wv, rk, cos_t, sin_t):
    nb = u.shape[0]
    const = lambda shape: pl.BlockSpec(shape, lambda b, i: (0, 0))
    return pl.pallas_call(
        _mla_proj_kernel,
        grid=(nb, NRT),
        in_specs=[pl.BlockSpec((None, RT, MLA_Q_RANK), lambda b, i: (b, i, CB_MLA_CQ)),
                  pl.BlockSpec((None, RT, MLA_KV_RANK), lambda b, i: (b, i, CB_MLA_CKV)),
                  pl.BlockSpec((None, RT, 128), lambda b, i: (b, i, CB_MLA_KR)),
                  const((1, MLA_Q_RANK)), const((1, MLA_KV_RANK)),
                  const((MLA_Q_RANK, NH * 256)), const((MLA_Q_RANK, NH * 256)),
                  const((MLA_KV_RANK, NH * 256)), const((MLA_KV_RANK, NH * DH)),
                  const((128, 128)),
                  pl.BlockSpec((RT, 256), lambda b, i: (i, 0)),
                  pl.BlockSpec((RT, 256), lambda b, i: (i, 0))],
        out_specs=[_row_spec(NH * 256), _row_spec(NH * 256), _row_spec(NH * DH)],
        out_shape=[jax.ShapeDtypeStruct((nb, LT, NH * 256), BF16),
                   jax.ShapeDtypeStruct((nb, LT, NH * 256), BF16),
                   jax.ShapeDtypeStruct((nb, LT, NH * DH), BF16)],
        compiler_params=_cparams(("parallel", "parallel")),
        name="mla_proj",
    )(u, u, u, gq, gkv, wqa, wqb, wk, wv, rk, cos_t, sin_t)


AT = 768


def _attn_kernel(q_ref, k_ref, v_ref, o_ref):
    s = _dot_nt(q_ref[...], k_ref[...])

    def finish(sc):
        m = jnp.max(sc, axis=1, keepdims=True)
        p = jnp.exp(sc - m)
        l = jnp.sum(p, axis=1, keepdims=True)
        o_ref[...] = (_dot(p.astype(BF16), v_ref[...]) / l).astype(o_ref.dtype)

    last = pl.program_id(2) == LT // AT - 1

    @pl.when(last)
    def _():
        qrow = lax.broadcasted_iota(jnp.int32, (AT, 1), 0)
        kcol = lax.broadcasted_iota(jnp.int32, (1, LT), 1)
        hide = jnp.logical_and(qrow >= AT - CTX, kcol < SEQ)
        finish(jnp.where(hide, -jnp.inf, s))

    @pl.when(jnp.logical_not(last))
    def _():
        finish(s)


def _attention(q, k, v):
    nb = q.shape[0]
    return pl.pallas_call(
        _attn_kernel,
        grid=(nb, NH, LT // AT),
        in_specs=[pl.BlockSpec((None, AT, 256), lambda b, h, i: (b, i, h)),
                  pl.BlockSpec((None, LT, 256), lambda b, h, i: (b, 0, h)),
                  pl.BlockSpec((None, LT, DH), lambda b, h, i: (b, 0, h))],
        out_specs=pl.BlockSpec((None, AT, DH), lambda b, h, i: (b, i, h)),
        out_shape=jax.ShapeDtypeStruct((nb, LT, NH * DH), BF16),
        compiler_params=_cparams(("parallel", "parallel", "arbitrary")),
        name="mla_attn",
    )(q, k, v)


def _merge_kernel(h_ref, b0_ref, b1_ref, b2_ref, b3_ref, wg_ref, wb_ref, o_ref):
    h = h_ref[...]
    acc = None
    for kk, br in enumerate((b0_ref, b1_ref, b2_ref, b3_ref)):
        gate = _sigmoid(_dot(h, wg_ref[kk]))
        term = gate * _dot(br[...], wb_ref[kk])
        acc = term if acc is None else acc + term
    o_ref[...] = acc.astype(o_ref.dtype)


def _merge(h, branches, wg, wb, rows, tn=256):
    nb = h.shape[0]
    tm = rows // 2
    rmap = lambda i, j: (i // 2, i % 2, 0)
    bspec = pl.BlockSpec((None, tm, BW), rmap)
    return pl.pallas_call(
        _merge_kernel,
        grid=(nb * 2, D // tn),
        in_specs=[pl.BlockSpec((None, tm, D), rmap), bspec, bspec, bspec, bspec,
                  pl.BlockSpec((4, D, tn), lambda i, j: (0, 0, j)),
                  pl.BlockSpec((4, BW, tn), lambda i, j: (0, 0, j))],
        out_specs=pl.BlockSpec((None, tm, tn), lambda i, j: (i // 2, i % 2, j)),
        out_shape=jax.ShapeDtypeStruct((nb, rows, D), BF16),
        compiler_params=_cparams(("parallel", "arbitrary")),
        name="merge",
    )(h, *branches, wg, wb)


def _outproj_kernel(first, a_ref, w_ref, *refs):
    if first:
        x_ref, c_ref = refs[:2]
        x_in = _pick_x(x_ref, c_ref)
        refs = refs[2:]
    else:
        x_in = refs[0][...]
        refs = refs[1:]
    g1_ref, gate_ref, g2_ref, sh_ref, sc_ref, wa_ref, wb_ref, wc_ref, xo_ref, hp_ref, e_ref, wt_ref = refs
    y = _dot(a_ref[...], w_ref[...])
    xn = x_in + gate_ref[...] * (_rms(y) * g1_ref[...])
    xo_ref[...] = xn
    h2 = _rms(xn) * g2_ref[...]
    h2 = h2 * (1.0 + sc_ref[...]) + sh_ref[...]
    hp_ref[...] = h2
    _route_math(h2, wa_ref, wb_ref, wc_ref, e_ref, wt_ref)


def _outproj(acc, w_out, xs, normv, modv, wr3, layer):
    nb, rows, _ = acc.shape
    first = len(xs) == 2
    wspec = pl.BlockSpec((D, 128), lambda b, i: (0, 0))
    return pl.pallas_call(
        functools.partial(_outproj_kernel, first),
        grid=(nb, rows // RT),
        in_specs=[_row_spec(D), pl.BlockSpec((D, D), lambda b, i: (0, 0))]
                 + (_split_x_specs() if first else [_row_spec(D)])
                 + [_norm_spec(layer, 1), _mod_spec(layer, 2), _norm_spec(layer, 2),
                    _mod_spec(layer, 3), _mod_spec(layer, 4), wspec, wspec, wspec],
        out_specs=[_row_spec(D), _row_spec(D), _row_spec(128), _row_spec(128)],
        out_shape=[jax.ShapeDtypeStruct((nb, rows, D), F32), jax.ShapeDtypeStruct((nb, rows, D), F32),
                   jax.ShapeDtypeStruct((nb, rows, 128), jnp.int32), jax.ShapeDtypeStruct((nb, rows, 128), F32)],
        compiler_params=_cparams(("parallel", "parallel")),
        name="outproj",
    )(acc, w_out, *xs, normv, modv, normv, modv, modv, *wr3)


def _route_math(hf, wa_ref, wb_ref, wc_ref, e_ref, w_ref):
    h = hf.astype(BF16)
    hl = (hf - h.astype(F32)).astype(BF16)
    logits = _dot(h, wa_ref[...]) + _dot(h, wb_ref[...]) + _dot(hl, wa_ref[...])
    lane = lax.broadcasted_iota(jnp.int32, (1, 128), 1).astype(F32)
    ninf = -jnp.inf
    is_grp = lane < N_GROUPS
    gl = jnp.where(is_grp, logits, ninf)
    gmax = jnp.max(gl, axis=1, keepdims=True)
    gidx = jnp.min(jnp.where(gl == gmax, lane, 999.0), axis=1, keepdims=True)
    gsum = jnp.sum(jnp.where(is_grp, jnp.exp(gl - gmax), 0.0), axis=1, keepdims=True)
    gp = 1.0 / gsum
    lo = N_GROUPS + EXP_PER_GROUP * gidx
    sel = jnp.logical_and(lane >= lo, lane < lo + EXP_PER_GROUP)
    el = jnp.where(sel, logits, ninf)
    v1 = jnp.max(el, axis=1, keepdims=True)
    i1 = jnp.min(jnp.where(el == v1, lane, 999.0), axis=1, keepdims=True)
    el2 = jnp.where(lane == i1, ninf, el)
    v2 = jnp.max(el2, axis=1, keepdims=True)
    i2 = jnp.min(jnp.where(el2 == v2, lane, 999.0), axis=1, keepdims=True)
    t = jnp.exp(v2 - v1)
    w1 = gp / (1.0 + t)
    w2 = gp * t / (1.0 + t)
    e_ref[...] = jnp.where(lane == 0.0, i1 - N_GROUPS, jnp.where(lane == 1.0, i2 - N_GROUPS, 0.0)).astype(jnp.int32)
    w_ref[...] = jnp.where(lane == 0.0, w1, jnp.where(lane == 1.0, w2, 0.0))


GATHER_UNROLL = 8


def _expert_kernel(te_ref, nu_ref, rt_ref, h_hbm, w1_ref, w3_ref, w2_ref, o_ref, xbuf, hm_sc, w1b, w3b, w2b, sem):
    i = pl.program_id(0)
    n_used = nu_ref[0]
    slot = i % 2

    def issue_tile(j, s):
        base = j * TE

        def body(g, c):
            for uu in range(GATHER_UNROLL):
                r = g * GATHER_UNROLL + uu
                pltpu.make_async_copy(h_hbm.at[pl.ds(rt_ref[base + r], 1), :],
                                      xbuf.at[s, pl.ds(r, 1), :], sem.at[s]).start(priority=1)
            return c
        lax.fori_loop(0, TE // GATHER_UNROLL, body, 0)

    def wait_tile(s):
        pltpu.make_async_copy(h_hbm.at[pl.ds(0, TE), :], xbuf.at[s], sem.at[s]).wait()

    @pl.when(jnp.logical_and(i == 0, n_used > 0))
    def _():
        issue_tile(0, 0)

    @pl.when(i < n_used)
    def _():
        wait_tile(slot)
        nbase = jnp.minimum(i + 1, n_used - 1) * TE
        oslot = 1 - slot

        def issue_rows(r0, r1):
            for r in range(r0, r1):
                pltpu.make_async_copy(h_hbm.at[pl.ds(rt_ref[nbase + r], 1), :],
                                      xbuf.at[oslot, pl.ds(r, 1), :], sem.at[oslot]).start(priority=1)

        @pl.when(jnp.logical_or(i == 0, te_ref[i] != te_ref[jnp.maximum(i - 1, 0)]))
        def _():
            w1b[...] = w1_ref[...].astype(BF16)
            w3b[...] = w3_ref[...].astype(BF16)
            w2b[...] = w2_ref[...].astype(BF16)

        x = xbuf[slot].astype(BF16)
        cw = 256
        n1, n2 = D_EXPERT // cw, D // cw
        per = TE // (n1 + n2) + 1
        done = 0
        for j in range(n1):
            issue_rows(done, min(done + per, TE))
            done = min(done + per, TE)
            cs = slice(j * cw, (j + 1) * cw)
            a = _dot(x, w1b[:, cs])
            b = _dot(x, w3b[:, cs])
            hm_sc[:, cs] = (_silu(a) * b).astype(BF16)
        hm = hm_sc[...]
        for j in range(n2):
            issue_rows(done, min(done + per, TE))
            done = min(done + per, TE)
            cs = slice(j * cw, (j + 1) * cw)
            o_ref[:, cs] = _dot(hm, w2b[:, cs])

        @pl.when(i + 1 >= n_used)
        def _():
            wait_tile(oslot)

    @pl.when(i >= n_used)
    def _():
        o_ref[...] = jnp.zeros_like(o_ref)


def _experts(tile_expert, n_used, row_token, hp, w1, w3, w2, n_tiles, layer):
    wmap = lambda i, te, nu, rt: (layer * N_EXPERTS + te[i], 0, 0)
    grid_spec = pltpu.PrefetchScalarGridSpec(
        num_scalar_prefetch=3,
        grid=(n_tiles,),
        in_specs=[pl.BlockSpec(memory_space=pl.ANY),
                  pl.BlockSpec((None, D, D_EXPERT), wmap),
                  pl.BlockSpec((None, D, D_EXPERT), wmap),
                  pl.BlockSpec((None, D_EXPERT, D), wmap)],
        out_specs=pl.BlockSpec((TE, D), lambda i, te, nu, rt: (i, 0)),
        scratch_shapes=[pltpu.VMEM((2, TE, D), F32), pltpu.VMEM((TE, D_EXPERT), BF16),
                        pltpu.VMEM((D, D_EXPERT), BF16), pltpu.VMEM((D, D_EXPERT), BF16),
                        pltpu.VMEM((D_EXPERT, D), BF16),
                        pltpu.SemaphoreType.DMA((2,))],
    )
    return pl.pallas_call(
        _expert_kernel,
        grid_spec=grid_spec,
        out_shape=jax.ShapeDtypeStruct((n_tiles * TE, D), F32),
        compiler_params=_cparams(("arbitrary",)),
        name="moe_experts",
    )(tile_expert, n_used, row_token, hp, w1, w3, w2)


def _combine_kernel(has_next, pos_ref, ys_hbm, w_ref, x_ref, g3_ref, gate_ref, *rest):
    if has_next:
        g0_ref, sh_ref, sc_ref, xo_ref, ho_ref, buf, sem = rest
    else:
        xo_ref, buf, sem = rest
    step = pl.program_id(0) * pl.num_programs(1) + pl.program_id(1)
    n_steps = pl.num_programs(0) * pl.num_programs(1)
    slot = step % 2

    def issue_tile(j, s):
        t0 = j * RT

        def body(g, c):
            for uu in range(GATHER_UNROLL):
                r = g * GATHER_UNROLL + uu
                for kk in range(2):
                    pltpu.make_async_copy(ys_hbm.at[pl.ds(pos_ref[(t0 + r) * 2 + kk], 1), :],
                                          buf.at[s, kk, pl.ds(r, 1), :], sem.at[s]).start(priority=kk)
            return c
        lax.fori_loop(0, RT // GATHER_UNROLL, body, 0)

    def wait_tile(s):
        for kk in range(2):
            pltpu.make_async_copy(ys_hbm.at[pl.ds(0, RT), :], buf.at[s, kk], sem.at[s]).wait()

    @pl.when(step == 0)
    def _():
        issue_tile(0, 0)

    wait_tile(slot)
    nt0 = jnp.minimum(step + 1, n_steps - 1) * RT
    oslot = 1 - slot
    rows = 32
    for c in range(RT // rows):
        for r in range(c * rows, (c + 1) * rows):
            for kk in range(2):
                pltpu.make_async_copy(ys_hbm.at[pl.ds(pos_ref[(nt0 + r) * 2 + kk], 1), :],
                                      buf.at[oslot, kk, pl.ds(r, 1), :], sem.at[oslot]).start(priority=kk)
        rs = slice(c * rows, (c + 1) * rows)
        w = w_ref[rs, :]
        y = w[:, 0:1] * buf[slot, 0, rs, :] + w[:, 1:2] * buf[slot, 1, rs, :]
        xn = x_ref[rs, :] + gate_ref[...] * (_rms(y) * g3_ref[...])
        xo_ref[rs, :] = xn
        if has_next:
            hn = _rms(xn) * g0_ref[...]
            ho_ref[rs, :] = (hn * (1.0 + sc_ref[...]) + sh_ref[...]).astype(BF16)

    @pl.when(step + 1 >= n_steps)
    def _():
        wait_tile(oslot)


def _combine(pos, ys, wts, xcat, normv, modv, layer):
    nb = xcat.shape[0]
    has_next = layer + 1 < DEPTH

    def mod_spec(lyr, k):
        def imap(b, i, pos_ref):
            row = jnp.where(i == NRT - 1, 8, b)
            return ((lyr * 16 + row) * 6 + k, 0, 0)
        return pl.BlockSpec((None, 1, D), imap)

    def norm_spec(lyr, k):
        return pl.BlockSpec((None, 1, D), lambda b, i, pos_ref: (lyr * 4 + k, 0, 0))

    def row_spec(width):
        return pl.BlockSpec((None, RT, width), lambda b, i, pos_ref: (b, i, 0))

    in_specs = [pl.BlockSpec(memory_space=pl.ANY), row_spec(128), row_spec(D),
                norm_spec(layer, 3), mod_spec(layer, 5)]
    args = [ys, wts, xcat, normv, modv]
    out_specs = [row_spec(D)]
    n_row_tiles = NRT if has_next else SEQ // RT
    out_shape = [jax.ShapeDtypeStruct((nb, n_row_tiles * RT, D), F32)]
    if has_next:
        in_specs += [norm_spec(layer + 1, 0), mod_spec(layer + 1, 0), mod_spec(layer + 1, 1)]
        args += [normv, modv, modv]
        out_specs.append(row_spec(D))
        out_shape.append(jax.ShapeDtypeStruct((nb, LT, D), BF16))
    grid_spec = pltpu.PrefetchScalarGridSpec(
        num_scalar_prefetch=1,
        grid=(nb, n_row_tiles),
        in_specs=in_specs,
        out_specs=out_specs,
        scratch_shapes=[pltpu.VMEM((2, 2, RT, D), F32), pltpu.SemaphoreType.DMA((2,))],
    )
    return pl.pallas_call(
        functools.partial(_combine_kernel, has_next),
        grid_spec=grid_spec,
        out_shape=out_shape,
        compiler_params=_cparams(("arbitrary", "arbitrary")),
        name="moe_combine",
    )(pos, *args)


def _plan_kernel(e_ref, pos_ref, cnt_ref, rank_sc, carry_sc):
    phase = pl.program_id(0)
    i = pl.program_id(1)
    lane = lax.broadcasted_iota(jnp.int32, (1, 128), 1)
    e = e_ref[...]
    oh1 = lane == e[:, 0:1]
    oh2 = lane == e[:, 1:2]

    @pl.when(jnp.logical_and(phase == 0, i == 0))
    def _():
        carry_sc[...] = jnp.zeros_like(carry_sc)

    @pl.when(phase == 0)
    def _():
        s = jnp.where(jnp.logical_or(oh1, oh2), 1.0, 0.0)
        pt = e_ref.shape[0]
        row = lax.broadcasted_iota(jnp.int32, (pt, pt), 0)
        col = lax.broadcasted_iota(jnp.int32, (pt, pt), 1)
        lower = jnp.where(col < row, 1.0, 0.0).astype(BF16)
        carry = carry_sc[0:1, :]
        before = _dot(lower, s.astype(BF16)) + carry
        r1 = jnp.sum(jnp.where(oh1, before, 0.0), axis=1, keepdims=True)
        r2 = jnp.sum(jnp.where(oh2, before, 0.0), axis=1, keepdims=True)
        rank_sc[i] = jnp.where(lane == 0, r1, jnp.where(lane == 1, r2, 0.0))
        carry_sc[...] = jnp.broadcast_to(carry + jnp.sum(s, axis=0, keepdims=True), carry_sc.shape)
        cnt_ref[...] = carry_sc[...]

    @pl.when(phase == 1)
    def _():
        counts = carry_sc[0:1, :]
        padded = jnp.floor((counts + (TE - 1)) * (1.0 / TE)) * TE
        r128 = lax.broadcasted_iota(jnp.int32, (128, 128), 0)
        c128 = lax.broadcasted_iota(jnp.int32, (128, 128), 1)
        upper = jnp.where(r128 < c128, 1.0, 0.0).astype(BF16)
        pa, pb, pc = _split3(jnp.broadcast_to(padded, (8, 128)))
        offs = (_dot(pa, upper) + _dot(pb, upper) + _dot(pc, upper))[0:1, :]
        rank = rank_sc[i]
        p1 = rank[:, 0:1] + jnp.sum(jnp.where(oh1, offs, 0.0), axis=1, keepdims=True)
        p2 = rank[:, 1:2] + jnp.sum(jnp.where(oh2, offs, 0.0), axis=1, keepdims=True)
        pos_ref[...] = jnp.where(lane == 0, p1, jnp.where(lane == 1, p2, 0.0)).astype(jnp.int32)


def _plan(eid):
    nb, rows, _ = eid.shape
    pt = 768 if rows % 768 == 0 else 1024
    tpb = rows // pt
    nt = nb * tpb
    emap = lambda p, i: (i // tpb, i % tpb, 0)
    return pl.pallas_call(
        _plan_kernel,
        grid=(2, nt),
        in_specs=[pl.BlockSpec((None, pt, 128), emap)],
        out_specs=[pl.BlockSpec((pt, 128), lambda p, i: (p * i, 0)),
                   pl.BlockSpec((8, 128), lambda p, i: (0, 0))],
        out_shape=[jax.ShapeDtypeStruct((nt * pt, 128), jnp.int32), jax.ShapeDtypeStruct((8, 128), F32)],
        scratch_shapes=[pltpu.VMEM((nt, pt, 128), F32), pltpu.VMEM((8, 128), F32)],
        compiler_params=_cparams(("arbitrary", "arbitrary")),
        name="moe_plan",
    )(eid)


def _invert_kernel(pos_ref, rt_ref):
    unroll = 8

    def zero(i, c):
        for uu in range(unroll):
            rt_ref[i * unroll + uu] = 0
        return c
    lax.fori_loop(0, rt_ref.shape[0] // unroll, zero, 0)

    def scatter(i, c):
        for uu in range(unroll):
            a = i * unroll + uu
            rt_ref[pos_ref[a]] = jnp.right_shift(a, 1)
        return c
    lax.fori_loop(0, pos_ref.shape[0] // unroll, scatter, 0)


def _invert(pos_flat, n_rows):
    grid_spec = pltpu.PrefetchScalarGridSpec(
        num_scalar_prefetch=1, grid=(1,), in_specs=[],
        out_specs=pl.BlockSpec(memory_space=pltpu.SMEM))
    return pl.pallas_call(
        _invert_kernel,
        grid_spec=grid_spec,
        out_shape=jax.ShapeDtypeStruct((n_rows,), jnp.int32),
        compiler_params=_cparams(("arbitrary",)),
        name="moe_invert",
    )(pos_flat)


def _dispatch_tiles(pos, counts, n_tiles, n_row_tiles, nb):
    cnt = counts[0, :N_EXPERTS].astype(jnp.int32)
    padded = ((cnt + TE - 1) // TE) * TE
    ends = jnp.cumsum(padded)
    tile_start = jnp.arange(n_tiles, dtype=jnp.int32) * TE
    tile_expert = jnp.sum((ends[None, :] <= tile_start[:, None]).astype(jnp.int32), axis=1)
    tile_expert = jnp.minimum(tile_expert, N_EXPERTS - 1)
    n_used = (ends[-1] // TE).astype(jnp.int32).reshape(1)
    pos_flat = pos[:, :2].reshape(-1)
    row_token = _invert(pos_flat, n_tiles * TE)
    return pos_flat, row_token, tile_expert, n_used


def _rope_tables():
    tok = jnp.arange(SEQ)
    pos_row = (tok // GRID_W).astype(F32)
    pos_col = (tok % GRID_W).astype(F32)
    quarter = MLA_ROPE // 4
    inv_freq = ROPE_BASE ** (-jnp.arange(quarter, dtype=F32) / quarter)
    ang_r = pos_row[:, None] * inv_freq
    ang_c = pos_col[:, None] * inv_freq
    cos = jnp.concatenate([jnp.cos(ang_r)] * 2 + [jnp.cos(ang_c)] * 2, axis=1)
    sin = jnp.concatenate([jnp.sin(ang_r)] * 2 + [jnp.sin(ang_c)] * 2, axis=1)
    cos = jnp.concatenate([cos, jnp.ones((CTX, MLA_ROPE), F32)], axis=0)
    sin = jnp.concatenate([sin, jnp.zeros((CTX, MLA_ROPE), F32)], axis=0)
    pad = jnp.zeros((LT, 64), F32)
    cos_t = jnp.concatenate([jnp.ones((LT, 128), F32), cos, pad], axis=1)
    sin_t = jnp.concatenate([jnp.zeros((LT, 128), F32), sin, pad], axis=1)
    return cos_t, sin_t


def _rope_rot_matrix():
    j = jnp.arange(MLA_ROPE)
    first = (j % 32) < 16
    src = jnp.where(first, j + 16, j - 16)
    sign = jnp.where(first, -1.0, 1.0)
    return jnp.zeros((MLA_ROPE, MLA_ROPE), F32).at[src, j].set(sign)


def _layer_params(l, w_main, ml_f_bias, lb_all, hg_norm_g, mla_g_q, mla_g_kv, mla_w_uq, mla_w_ukv,
                  ssd_conv_w, ssd_conv_b, ssd_a_log, ssd_dt_bias, ssd_d, ssd_norm_g, w_gate, w_br, w_out,
                  moe_w_grp, moe_w_exp):
    p = {}
    p["w_main"] = w_main[l]
    fb = jnp.zeros((2, 2, NH), F32).at[:, 1, :].set(ml_f_bias[l]).reshape(16)
    ml_bias = jnp.concatenate([fb, jnp.zeros((112,), F32)])
    p["ml_bias_c"] = ml_bias.reshape(1, 128)
    p["ml_bias_t"] = ml_bias[:32].reshape(32, 1)
    dtb = jnp.concatenate([jnp.zeros((16,), F32), ssd_dt_bias[l].reshape(16), jnp.zeros((96,), F32)])
    p["ssd_bias_c"] = dtb.reshape(1, 128)
    p["ssd_bias_t"] = dtb[:32].reshape(32, 1)
    aneg = jnp.concatenate([jnp.zeros((16,), F32), -jnp.exp(ssd_a_log[l].astype(F32)).reshape(16),
                            jnp.zeros((96,), F32)])
    p["ssd_aneg_c"] = aneg.reshape(1, 128)
    p["ssd_aneg_t"] = aneg[:32].reshape(32, 1)
    p["lb"] = lb_all[l].reshape(1, BW)
    p["hg_norm_g"] = hg_norm_g[l].reshape(1, BW)
    p["ssd_dskip"] = jnp.repeat(ssd_d[l], SSD_P).reshape(1, BW)
    p["ssd_norm_g"] = ssd_norm_g[l].reshape(1, BW)
    p["conv_w"] = jnp.concatenate([ssd_conv_w[l], jnp.zeros((5, SSD_XBC), F32)], axis=0)
    p["conv_b"] = ssd_conv_b[l].reshape(1, SSD_XBC)
    rot = _rope_rot_matrix()
    wq = mla_w_uq[l].reshape(MLA_Q_RANK, NH, MLA_NOPE + MLA_ROPE)
    zq = jnp.zeros((MLA_Q_RANK, NH, 64), F32)
    wqa = jnp.concatenate([wq, zq], axis=2).reshape(MLA_Q_RANK, NH * 256)
    wq_rot = jnp.einsum("rhj,jk->rhk", wq[:, :, MLA_NOPE:], rot)
    wqb = jnp.concatenate([jnp.zeros((MLA_Q_RANK, NH, MLA_NOPE), F32), wq_rot, zq], axis=2)
    p["wqa"] = wqa.astype(BF16)
    p["wqb"] = wqb.reshape(MLA_Q_RANK, NH * 256).astype(BF16)
    wkv = mla_w_ukv[l].reshape(MLA_KV_RANK, NH, 2 * DH)
    wk = jnp.concatenate([wkv[:, :, :DH], jnp.zeros((MLA_KV_RANK, NH, DH), F32)], axis=2)
    p["wk"] = wk.reshape(MLA_KV_RANK, NH * 256).astype(BF16)
    p["wv"] = wkv[:, :, DH:].reshape(MLA_KV_RANK, NH * DH).astype(BF16)
    p["rk"] = jnp.zeros((128, 128), F32).at[:MLA_ROPE, :MLA_ROPE].set(rot).astype(BF16)
    p["gq"] = mla_g_q[l].reshape(1, MLA_Q_RANK)
    p["gkv"] = mla_g_kv[l].reshape(1, MLA_KV_RANK)
    p["w_gate"] = w_gate[l].astype(BF16)
    p["w_br"] = w_br[l].astype(BF16)
    p["w_out"] = w_out[l].astype(BF16)
    wr = jnp.concatenate([moe_w_grp[l], moe_w_exp[l], jnp.zeros((D, 128 - N_GROUPS - N_EXPERTS), F32)], axis=1)
    a = wr.astype(BF16)
    r = wr - a.astype(F32)
    b = r.astype(BF16)
    c = (r - b.astype(F32)).astype(BF16)
    p["wr3"] = (a, b, c)
    return p


def _mixer(h, p, cos_t, sin_t, nb, out_rows):
    t = nb * LT
    hf = h.reshape(t, D)
    u, usm = _inproj(hf, p["w_main"])
    u = u.reshape(nb, LT, U_W)
    usm = usm.reshape(nb, LT, 128)
    ust = usm[:, :, :32].reshape(nb, NCH, CS, 32).transpose(0, 1, 3, 2)
    ml_f = _mlstm(u, usm, ust, p["ml_bias_c"], p["ml_bias_t"], False)
    ml = _mlstm(u, usm, ust, p["ml_bias_c"], p["ml_bias_t"], True, ml_f)
    hg_f = _hgrn2(u, p["lb"], False)
    hg = _hgrn2(u, p["lb"], True, hg_f, p["hg_norm_g"])
    q, k, v = _mla_proj(u, p["gq"], p["gkv"], p["wqa"], p["wqb"], p["wk"], p["wv"], p["rk"], cos_t, sin_t)
    mla = _attention(q, k, v)
    xbc = _ssd_conv(u, p["conv_w"], p["conv_b"])
    sargs = (xbc, usm, ust, u, p["ssd_bias_c"], p["ssd_bias_t"], p["ssd_aneg_c"], p["ssd_aneg_t"])
    ssd_f = _ssd(*sargs, False)
    ssd = _ssd(*sargs, True, ssd_f, p["ssd_dskip"], p["ssd_norm_g"])
    return _merge(h, (ml, hg, mla, ssd), p["w_gate"], p["w_br"], out_rows)


def kernel(x, c, ctx, c_ctx, w_ada, b_ada, norm_g, w_in, ml_f_bias, hg_lb_logits, hg_norm_g, mla_g_q, mla_g_kv,
           mla_w_uq, mla_w_ukv, ssd_conv_w, ssd_conv_b, ssd_a_log, ssd_dt_bias, ssd_d, ssd_norm_g, w_gate,
           w_br, w_out, moe_w_grp, moe_w_exp, moe_w1, moe_w3, moe_w2):
    nb = x.shape[0]
    t = nb * LT
    lb_all = jnp.cumsum(jax.nn.softmax(hg_lb_logits.astype(F32), axis=0), axis=0)
    cvec = jnp.zeros((16, D), F32).at[:nb].set(c).at[8].set(c_ctx)
    s_vec = jax.nn.silu(cvec).astype(BF16)
    modv = _adaln_mods(s_vec, w_ada, b_ada).reshape(DEPTH * 16 * 6, 1, D)
    normv = norm_g.reshape(DEPTH * 4, 1, D)
    cos_t, sin_t = _rope_tables()
    w_main = _win_layout(w_in)
    w1 = moe_w1.reshape(DEPTH * N_EXPERTS, D, D_EXPERT)
    w3 = moe_w3.reshape(DEPTH * N_EXPERTS, D, D_EXPERT)
    w2 = moe_w2.reshape(DEPTH * N_EXPERTS, D_EXPERT, D)
    h = _prenorm(x, ctx, normv, modv, 0)
    xs = (x, ctx)
    for l in range(DEPTH):
        p = _layer_params(l, w_main, ml_f_bias, lb_all, hg_norm_g, mla_g_q, mla_g_kv, mla_w_uq,
                          mla_w_ukv, ssd_conv_w, ssd_conv_b, ssd_a_log, ssd_dt_bias, ssd_d, ssd_norm_g, w_gate,
                          w_br, w_out, moe_w_grp, moe_w_exp)
        rows = LT if l + 1 < DEPTH else SEQ
        n_row_tiles = rows // RT
        acc = _mixer(h, p, cos_t, sin_t, nb, rows)
        xcat, hp, eid, wts = _outproj(acc, p["w_out"], xs, normv, modv, p["wr3"], l)
        n_tiles = (2 * nb * rows + N_EXPERTS * (TE - 1)) // TE + 1
        pos128, counts = _plan(eid)
        pos, row_token, tile_expert, n_used = _dispatch_tiles(pos128, counts, n_tiles, n_row_tiles, nb)
        ys = _experts(tile_expert, n_used, row_token, hp.reshape(nb * rows, D), w1, w3, w2, n_tiles, l)
        outs = _combine(pos, ys, wts, xcat, normv, modv, l)
        if l + 1 < DEPTH:
            xcat, h = outs
            xs = (xcat,)
    return outs[0]
```

```python
import functools

import jax
import jax.numpy as jnp
from jax import lax
from jax.experimental import pallas as pl
from jax.experimental.pallas import tpu as pltpu

F32 = jnp.float32
BF16 = jnp.bfloat16

D = 2048
SEQ = 2048
CTX = 256
LT = SEQ + CTX
DEPTH = 2
EPS = 1e-6
GRID_W = 64
NH = 4
DH = 128
BW = 512
MLA_Q_RANK = 512
MLA_KV_RANK = 256
MLA_NOPE = 128
MLA_ROPE = 64
MLA_SCALE = (MLA_NOPE + MLA_ROPE) ** -0.5
ROPE_BASE = 10000.0
SSD_HEADS = 8
SSD_P = 64
SSD_GROUPS = 2
SSD_N = 128
SSD_XBC = 1024
N_GROUPS = 4
EXP_PER_GROUP = 8
N_EXPERTS = 32
D_EXPERT = 512

CS = 128
NLC = SEQ // CS
NCC = CTX // CS
NCH = NLC + NCC
RT = 256
NRT = LT // RT
TE = 256
VMEM_LIMIT = 52 * 1024 * 1024

U_W = 7168
CB_ML_Q, CB_ML_K, CB_ML_V, CB_ML_O = 0, 1, 2, 3
CB_HG_Q, CB_HG_I, CB_HG_G, CB_HG_F = 4, 5, 6, 7
CB_SSD_Z = 9
CB_SSD_XBC = 5
CB_MLA_CQ = 12
CB_MLA_CKV = 26
CB_MLA_KR = 54


def _cparams(sem):
    return pltpu.CompilerParams(dimension_semantics=sem, vmem_limit_bytes=VMEM_LIMIT)


def _dot(a, b):
    return jnp.dot(a, b, preferred_element_type=F32)


def _dot_nt(a, b):
    return lax.dot_general(a, b, (((1,), (1,)), ((), ())), preferred_element_type=F32)


def _sigmoid(x):
    return 1.0 / (1.0 + jnp.exp(-x))


def _silu(x):
    return x * _sigmoid(x)


def _softplus(x):
    return jnp.maximum(x, 0.0) + jnp.log(1.0 + jnp.exp(-jnp.abs(x)))


def _log_sigmoid(x):
    return -_softplus(-x)


def _rms(x):
    return x * lax.rsqrt(jnp.mean(x * x, axis=-1, keepdims=True) + EPS)


def _split3(x):
    a = x.astype(BF16)
    r = x - a.astype(F32)
    b = r.astype(BF16)
    c = (r - b.astype(F32)).astype(BF16)
    return a, b, c


def _cumsum_rows(mb, x):
    a, b, c = _split3(x)
    return _dot(mb, a) + _dot(mb, b) + _dot(mb, c)


def _cumsum_cols(x, mb):
    a, b, c = _split3(x)
    return _dot_nt(a, mb) + _dot_nt(b, mb) + _dot_nt(c, mb)


def _chunk_mask(rev):
    row = lax.broadcasted_iota(jnp.int32, (CS, CS), 0)
    col = lax.broadcasted_iota(jnp.int32, (CS, CS), 1)
    return (col >= row) if rev else (col <= row)


def _chunk_of_step(rev):
    if rev:
        return lambda s: NCH - 1 - s
    return lambda s: (s + NLC) % NCH


def _mm_kernel(a_ref, w_ref, b_ref, o_ref):
    acc = _dot(a_ref[...], w_ref[...].astype(BF16))
    o_ref[...] = (acc + b_ref[...]).astype(o_ref.dtype)


def _inproj_kernel(a_ref, w_ref, o_ref, s_ref):
    acc = _dot(a_ref[...], w_ref[...])
    o_ref[...] = acc.astype(o_ref.dtype)

    @pl.when(pl.program_id(1) == pl.num_programs(1) - 1)
    def _():
        s_ref[...] = acc[:, acc.shape[1] - 128:]


def _inproj(a, w, tm=1152, tn=1024):
    m, k = a.shape
    n = w.shape[1]
    return pl.pallas_call(
        _inproj_kernel,
        grid=(m // tm, n // tn),
        in_specs=[pl.BlockSpec((tm, k), lambda i, j: (i, 0)),
                  pl.BlockSpec((k, tn), lambda i, j: (0, j))],
        out_specs=[pl.BlockSpec((tm, tn), lambda i, j: (i, j)),
                   pl.BlockSpec((tm, 128), lambda i, j: (i, 0))],
        out_shape=[jax.ShapeDtypeStruct((m, n), BF16), jax.ShapeDtypeStruct((m, 128), F32)],
        compiler_params=_cparams(("parallel", "arbitrary")),
        name="inproj",
    )(a, w)


W_IN_PIECES = ((0, 0, 2048),
               (2048, 2064, 2560),
               (4608, 5456, 512),
               (5120, 5968, 1024),
               (6144, 4624, 512),
               (6656, 5136, 256),
               (6912, 5392, 64),
               (7040, 2048, 16),
               (7056, 6992, 16))
D_IN = 7008


def _win_kernel(w_ref, o_ref):
    o_ref[...] = jnp.zeros_like(o_ref)
    for dst, src, width in W_IN_PIECES:
        o_ref[:, dst:dst + width] = w_ref[:, src:src + width].astype(BF16)


def _win_layout(w_in):
    rows = 256
    return pl.pallas_call(
        _win_kernel,
        grid=(DEPTH, D // rows),
        in_specs=[pl.BlockSpec((None, rows, D_IN), lambda l, i: (l, i, 0))],
        out_specs=pl.BlockSpec((None, rows, U_W), lambda l, i: (l, i, 0)),
        out_shape=jax.ShapeDtypeStruct((DEPTH, D, U_W), BF16),
        compiler_params=_cparams(("parallel", "parallel")),
        name="win_layout",
    )(w_in)


def _adaln_mods(s_vec, w_ada, b_ada):
    tn = 1024
    return pl.pallas_call(
        _mm_kernel,
        grid=(DEPTH, 6 * D // tn),
        in_specs=[pl.BlockSpec((16, D), lambda l, j: (0, 0)),
                  pl.BlockSpec((None, D, tn), lambda l, j: (l, 0, j)),
                  pl.BlockSpec((None, 1, tn), lambda l, j: (l, 0, j))],
        out_specs=pl.BlockSpec((None, 16, tn), lambda l, j: (l, 0, j)),
        out_shape=jax.ShapeDtypeStruct((DEPTH, 16, 6 * D), F32),
        compiler_params=_cparams(("parallel", "arbitrary")),
        name="adaln_mods",
    )(s_vec, w_ada, b_ada.reshape(DEPTH, 1, 6 * D))


def _mod_spec(layer, k):
    def imap(b, i):
        row = jnp.where(i == NRT - 1, 8, b)
        return ((layer * 16 + row) * 6 + k, 0, 0)
    return pl.BlockSpec((None, 1, D), imap)


def _norm_spec(layer, k):
    return pl.BlockSpec((None, 1, D), lambda b, i: (layer * 4 + k, 0, 0))


def _row_spec(width, dtype_rows=RT):
    return pl.BlockSpec((None, dtype_rows, width), lambda b, i: (b, i, 0))


def _split_x_specs():
    return [pl.BlockSpec((None, RT, D), lambda b, i: (b, jnp.minimum(i, NRT - 2), 0)),
            pl.BlockSpec((None, RT, D), lambda b, i: (b, 0, 0))]


def _pick_x(x_ref, c_ref):
    return jnp.where(pl.program_id(1) == NRT - 1, c_ref[...], x_ref[...])


def _prenorm_kernel(x_ref, c_ref, g_ref, sh_ref, sc_ref, h_ref):
    y = _rms(_pick_x(x_ref, c_ref)) * g_ref[...]
    h_ref[...] = (y * (1.0 + sc_ref[...]) + sh_ref[...]).astype(BF16)


def _prenorm(x, ctx, normv, modv, layer):
    nb = x.shape[0]
    return pl.pallas_call(
        _prenorm_kernel,
        grid=(nb, NRT),
        in_specs=_split_x_specs() + [_norm_spec(layer, 0), _mod_spec(layer, 0), _mod_spec(layer, 1)],
        out_specs=_row_spec(D),
        out_shape=jax.ShapeDtypeStruct((nb, LT, D), BF16),
        compiler_params=_cparams(("parallel", "parallel")),
        name="prenorm",
    )(x, ctx, normv, modv, modv)


def _conv_kernel(x_ref, p_ref, n_ref, w_ref, b_ref, o_ref):
    i = pl.program_id(1)
    x = x_ref[...].astype(F32)
    has_prev = jnp.logical_and(i != 0, i != NRT - 1).astype(F32)
    has_next = jnp.logical_and(i != NRT - 2, i != NRT - 1).astype(F32)
    prev_row = p_ref[15:16, :].astype(F32) * has_prev
    next_row = n_ref[0:1, :].astype(F32) * has_next
    ridx = lax.broadcasted_iota(jnp.int32, (RT, 1), 0)
    xm1 = jnp.where(ridx == 0, prev_row, pltpu.roll(x, 1, 0))
    xp1 = jnp.where(ridx == RT - 1, next_row, pltpu.roll(x, RT - 1, 0))
    w = w_ref[...]
    y = w[0:1, :] * xm1 + w[1:2, :] * x + w[2:3, :] * xp1 + b_ref[...]
    o_ref[...] = _silu(y)


def _ssd_conv(u, conv_w, conv_b):
    nb = u.shape[0]
    hb = RT // 16
    last = LT // 16 - 1
    return pl.pallas_call(
        _conv_kernel,
        grid=(nb, NRT),
        in_specs=[pl.BlockSpec((None, RT, SSD_XBC), lambda b, i: (b, i, CB_SSD_XBC)),
                  pl.BlockSpec((None, 16, SSD_XBC), lambda b, i: (b, jnp.maximum(i * hb - 1, 0), CB_SSD_XBC)),
                  pl.BlockSpec((None, 16, SSD_XBC), lambda b, i: (b, jnp.minimum((i + 1) * hb, last), CB_SSD_XBC)),
                  pl.BlockSpec((8, SSD_XBC), lambda b, i: (0, 0)),
                  pl.BlockSpec((1, SSD_XBC), lambda b, i: (0, 0))],
        out_specs=pl.BlockSpec((None, RT, SSD_XBC), lambda b, i: (b, i, 0)),
        out_shape=jax.ShapeDtypeStruct((nb, LT, SSD_XBC), F32),
        compiler_params=_cparams(("parallel", "parallel")),
        name="ssd_conv",
    )(u, u, u, conv_w, conv_b)


SCAN_BB = 2


def _mlstm_kernel(rev, bpb, *refs):
    if rev:
        (qs_ref, ks_ref, vs_ref, gc_ref, gt_ref, bc_ref, bt_ref, yps_ref, ogs_ref,
         os_ref, c_sc, n_sc, m_sc) = refs
    else:
        (qs_ref, ks_ref, vs_ref, gc_ref, gt_ref, bc_ref, bt_ref,
         os_ref, c_sc, n_sc, m_sc) = refs
    d = 1 if rev else 0

    @pl.when(pl.program_id(1) == 0)
    def _():
        c_sc[...] = jnp.zeros_like(c_sc)
        n_sc[...] = jnp.zeros_like(n_sc)
        m_sc[...] = jnp.full(m_sc.shape, -jnp.inf, F32)

    mask = _chunk_mask(rev)
    mb = jnp.where(mask, 1.0, 0.0).astype(BF16)
    scale = DH ** -0.5
    units = [(bb, h) for bb in range(bpb) for h in range(NH)]
    gates = []
    for bb in range(bpb):
        gc = gc_ref[bb] + bc_ref[...]
        gt = gt_ref[bb] + bt_ref[...]
        flog_c = _log_sigmoid(gc)
        flog_t = _log_sigmoid(gt)
        gates.append((gc, gt, flog_c, _cumsum_rows(mb, flog_c), _cumsum_cols(flog_t, mb)))
    st = []
    for bb, h in units:
        sl = slice(h * DH, (h + 1) * DH)
        q, k, v = qs_ref[bb, :, sl], ks_ref[bb, :, sl], vs_ref[bb, :, sl]
        c_h = c_sc.at[bb * NH + h]
        st.append(dict(q=q, k=k, v=v, sl=sl, qk=_dot_nt(q, k), qc=_dot(q, c_h[...].astype(BF16))))
    for (bb, h), w in zip(units, st):
        gc, gt, flog_c, fcum_c, fcum_t = gates[bb]
        ci = d * 8 + h
        cf = d * 8 + 4 + h
        w["i_col"] = gc[:, ci:ci + 1]
        w["f_col"] = flog_c[:, cf:cf + 1]
        w["fc_col"] = fcum_c[:, cf:cf + 1]
        w["m_prev"] = m_sc[bb * NH + h][0:1, 0:1]
        dm = jnp.where(mask, w["fc_col"] - fcum_t[cf:cf + 1, :] + gt[ci:ci + 1, :], -jnp.inf)
        from_state = w["fc_col"] + w["m_prev"]
        w["m_t"] = jnp.maximum(from_state, jnp.max(dm, axis=1, keepdims=True))
        w["s"] = w["qk"] * scale * jnp.exp(dm - w["m_t"])
        w["w_state"] = jnp.exp(from_state - w["m_t"])
    for (bb, h), w in zip(units, st):
        n_h = n_sc.at[bb * NH + h]
        num = _dot(w["s"].astype(BF16), w["v"]) + w["w_state"] * (w["qc"] * scale)
        qn = jnp.sum(w["q"].astype(F32) * n_h[0:1, :], axis=1, keepdims=True) * scale
        den = jnp.sum(w["s"], axis=1, keepdims=True) + w["w_state"] * qn
        hout = num / jnp.maximum(jnp.abs(den), jnp.exp(-w["m_t"]))
        if rev:
            tot = yps_ref[bb, :, w["sl"]] + hout
            os_ref[bb, :, w["sl"]] = (tot * _sigmoid(ogs_ref[bb, :, w["sl"]].astype(F32))).astype(os_ref.dtype)
        else:
            os_ref[bb, :, w["sl"]] = hout
    for (bb, h), w in zip(units, st):
        c_h, n_h, m_h = c_sc.at[bb * NH + h], n_sc.at[bb * NH + h], m_sc.at[bb * NH + h]
        f_tot = jnp.sum(w["f_col"], axis=0, keepdims=True)
        w_log = f_tot - w["fc_col"] + w["i_col"]
        m_new = jnp.maximum(f_tot + w["m_prev"], jnp.max(w_log, axis=0, keepdims=True))
        decay = jnp.exp(f_tot + w["m_prev"] - m_new)
        kw = w["k"].astype(F32) * jnp.exp(w_log - m_new)
        c_h[...] = decay * c_h[...] + _dot(kw.T.astype(BF16), w["v"])
        n_new = decay * n_h[0:1, :] + jnp.sum(kw, axis=0, keepdims=True)
        n_h[...] = jnp.broadcast_to(n_new, (8, DH))
        m_h[...] = jnp.broadcast_to(m_new, (8, DH))


def _mlstm(u, usm, ust, bias_c, bias_t, rev, yprev=None):
    nb = u.shape[0]
    bpb = SCAN_BB if nb % SCAN_BB == 0 else 1
    cmap = _chunk_of_step(rev)
    uspec = lambda cblk: pl.BlockSpec((bpb, CS, BW), lambda b, s: (b, cmap(s), cblk))
    in_specs = [uspec(CB_ML_Q), uspec(CB_ML_K), uspec(CB_ML_V),
                pl.BlockSpec((bpb, CS, 128), lambda b, s: (b, cmap(s), 0)),
                pl.BlockSpec((bpb, None, 32, CS), lambda b, s: (b, cmap(s), 0, 0)),
                pl.BlockSpec((1, 128), lambda b, s: (0, 0)),
                pl.BlockSpec((32, 1), lambda b, s: (0, 0))]
    args = [u, u, u, usm, ust, bias_c, bias_t]
    if rev:
        in_specs += [uspec(0), uspec(CB_ML_O)]
        args += [yprev, u]
    return pl.pallas_call(
        functools.partial(_mlstm_kernel, rev, bpb),
        grid=(nb // bpb, NCH),
        in_specs=in_specs,
        out_specs=uspec(0),
        out_shape=jax.ShapeDtypeStruct((nb, LT, BW), BF16 if rev else F32),
        scratch_shapes=[pltpu.VMEM((bpb * NH, DH, DH), F32), pltpu.VMEM((bpb * NH, 8, DH), F32),
                        pltpu.VMEM((bpb * NH, 8, DH), F32)],
        compiler_params=_cparams(("parallel", "arbitrary")),
        name="mlstm_rev" if rev else "mlstm_fwd",
    )(*args)


def _block_ref(g, n, off):
    g3 = g.reshape(CS // n, n, DH)
    r = g3[:, off:off + 1, :]
    return jnp.broadcast_to(r, (CS // n, n, DH)).reshape(CS, DH)


def _hgrn2_kernel(rev, bpb, *refs):
    if rev:
        qs_ref, is_ref, fs_ref, lb_ref, yps_ref, gs_ref, ng_ref, os_ref, st_sc = refs
    else:
        qs_ref, is_ref, fs_ref, lb_ref, os_ref, st_sc = refs

    @pl.when(pl.program_id(1) == 0)
    def _():
        st_sc[...] = jnp.zeros_like(st_sc)

    mask = _chunk_mask(rev)
    mb = jnp.where(mask, 1.0, 0.0).astype(BF16)
    row = lax.broadcasted_iota(jnp.int32, (CS, CS), 0)
    col = lax.broadcasted_iota(jnp.int32, (CS, CS), 1)
    tix = lax.broadcasted_iota(jnp.int32, (CS, 1), 0)
    for bb, h in [(bb, h) for bb in range(bpb) for h in range(NH)]:
        q_ref, i_ref, f_ref, o_ref = qs_ref.at[bb], is_ref.at[bb], fs_ref.at[bb], os_ref.at[bb]
        if rev:
            yp_ref, g_ref = yps_ref.at[bb], gs_ref.at[bb]
        st_h = st_sc.at[bb * NH + h]
        sl = slice(h * DH, (h + 1) * DH)
        q = _silu(q_ref[:, sl].astype(F32))
        v = i_ref[:, sl].astype(F32)
        lb = lb_ref[:, sl]
        f = lb + (1.0 - lb) * _sigmoid(f_ref[:, sl].astype(F32))
        k = 1.0 - f
        lf = jnp.log(f)
        g = _cumsum_rows(mb, lf)
        st = st_h[...]
        o = _dot_nt((q * jnp.exp(g)).astype(BF16), st.astype(BF16))
        a = jnp.zeros((CS, CS), F32)
        n = CS
        while n >= 16:
            half = n // 2
            second = jnp.bitwise_and(tix, n - 1) >= half
            rowside = jnp.logical_not(second) if rev else second
            r = _block_ref(g, n, half if rev else half - 1)
            e = jnp.exp(jnp.where(rowside, g - r, r - g))
            qs = jnp.where(rowside, q * e, 0.0).astype(BF16)
            ks = jnp.where(rowside, 0.0, k * e).astype(BF16)
            sh_n = n.bit_length() - 1
            same = jnp.right_shift(row, sh_n) == jnp.right_shift(col, sh_n)
            a = a + jnp.where(same, _dot_nt(qs, ks), 0.0)
            n = half
        o = o + _dot(a.astype(BF16), v.astype(BF16))
        t8 = jnp.bitwise_and(tix, 7)
        k3, g3, v3 = (a3.reshape(CS // 8, 8, DH) for a3 in (k, g, v))
        for dl in range(8):
            if dl == 0:
                ksh, gsh, vsh = k, g, v
            else:
                sh = (8 - dl) if rev else dl
                ksh = pltpu.roll(k3, sh, 1).reshape(CS, DH)
                gsh = pltpu.roll(g3, sh, 1).reshape(CS, DH)
                vsh = pltpu.roll(v3, sh, 1).reshape(CS, DH)
            valid = (t8 + dl <= 7) if rev else (t8 >= dl)
            e = jnp.exp(jnp.minimum(g - gsh, 0.0))
            coef = jnp.sum(q * ksh * e, axis=1, keepdims=True)
            o = o + jnp.where(valid, coef, 0.0) * vsh
        if rev:
            tot = yp_ref[:, sl] + o
            y = _rms(tot) * ng_ref[:, sl]
            o_ref[:, sl] = (y * _sigmoid(g_ref[:, sl].astype(F32))).astype(o_ref.dtype)
        else:
            o_ref[:, sl] = o
        g_tot = jnp.sum(lf, axis=0, keepdims=True)
        kdec = k * jnp.exp(g_tot - g)
        st_h[...] = st * jnp.exp(g_tot) + _dot(v.T.astype(BF16), kdec.astype(BF16))


def _hgrn2(u, lb, rev, yprev=None, norm_g=None):
    nb = u.shape[0]
    bpb = SCAN_BB if nb % SCAN_BB == 0 else 1
    cmap = _chunk_of_step(rev)
    d = 1 if rev else 0
    uspec = lambda cblk: pl.BlockSpec((bpb, CS, BW), lambda b, s: (b, cmap(s), cblk))
    in_specs = [uspec(CB_HG_Q), uspec(CB_HG_I), uspec(CB_HG_F + d),
                pl.BlockSpec((1, BW), lambda b, s: (0, 0))]
    args = [u, u, u, lb]
    if rev:
        in_specs += [uspec(0), uspec(CB_HG_G), pl.BlockSpec((1, BW), lambda b, s: (0, 0))]
        args += [yprev, u, norm_g]
    return pl.pallas_call(
        functools.partial(_hgrn2_kernel, rev, bpb),
        grid=(nb // bpb, NCH),
        in_specs=in_specs,
        out_specs=uspec(0),
        out_shape=jax.ShapeDtypeStruct((nb, LT, BW), BF16 if rev else F32),
        scratch_shapes=[pltpu.VMEM((bpb * NH, DH, DH), F32)],
        compiler_params=_cparams(("parallel", "arbitrary")),
        name="hgrn2_rev" if rev else "hgrn2_fwd",
    )(*args)


def _expand_heads(cols, lane_head):
    r = cols[3]
    for hh in (2, 1, 0):
        r = jnp.where(lane_head == hh, cols[hh], r)
    return r


def _ssd_kernel(rev, bpb, *refs):
    if rev:
        (xs_ref, gcs_ref, gts_ref, bc_ref, bt_ref, ac_ref, at_ref, yps_ref, zs_ref, dk_ref, ng_ref,
         os_ref, ht_sc) = refs
    else:
        xs_ref, gcs_ref, gts_ref, bc_ref, bt_ref, ac_ref, at_ref, os_ref, ht_sc = refs

    @pl.when(pl.program_id(1) == 0)
    def _():
        ht_sc[...] = jnp.zeros_like(ht_sc)

    for bb in range(bpb):
        extra = (yps_ref.at[bb], zs_ref.at[bb], dk_ref, ng_ref) if rev else ()
        _ssd_chunk(rev, xs_ref.at[bb], gcs_ref[bb], gts_ref[bb], bc_ref, bt_ref, ac_ref, at_ref,
                   os_ref.at[bb], ht_sc, bb * SSD_GROUPS, *extra)


def _ssd_chunk(rev, x_ref, gc, gt, bc_ref, bt_ref, ac_ref, at_ref, o_ref, ht_sc, ht0, yp_ref=None, z_ref=None,
               dk_ref=None, ng_ref=None):
    d = 1 if rev else 0
    mask = _chunk_mask(rev)
    mb = jnp.where(mask, 1.0, 0.0).astype(BF16)
    dt_c = _softplus(gc + bc_ref[...])
    da_c = dt_c * ac_ref[...]
    acum_c = _cumsum_rows(mb, da_c)
    dt_t = _softplus(gt + bt_ref[...])
    da_t = dt_t * at_ref[...]
    acum_t = _cumsum_cols(da_t, mb)
    gw = 4 * SSD_P
    lane_head = jnp.right_shift(lax.broadcasted_iota(jnp.int32, (1, gw), 1), 6)
    ys = []
    for grp in range(SSD_GROUPS):
        xg = x_ref[:, grp * gw:(grp + 1) * gw]
        bm = x_ref[:, BW + grp * SSD_N:BW + (grp + 1) * SSD_N]
        cm = x_ref[:, BW + SSD_GROUPS * SSD_N + grp * SSD_N:BW + SSD_GROUPS * SSD_N + (grp + 1) * SSD_N]
        cmb = cm.astype(BF16)
        cb = _dot_nt(cmb, bm.astype(BF16))
        ht = ht_sc[ht0 + grp]
        y_state = _dot(cmb, ht.astype(BF16))
        lanes = [16 + d * 8 + grp * 4 + hh for hh in range(4)]
        dt_cols = [dt_c[:, l:l + 1] for l in lanes]
        a_cols = [acum_c[:, l:l + 1] for l in lanes]
        a_tots = [jnp.sum(da_c[:, l:l + 1], axis=0, keepdims=True) for l in lanes]
        xdt = xg * _expand_heads(dt_cols, lane_head)
        xdtb = xdt.astype(BF16)
        y = jnp.exp(_expand_heads(a_cols, lane_head)) * y_state
        for hh in range(4):
            l = lanes[hh]
            dec = jnp.exp(jnp.where(mask, a_cols[hh] - acum_t[l:l + 1, :], -jnp.inf))
            yh = _dot((cb * dec).astype(BF16), xdtb)
            y = y + jnp.where(lane_head == hh, yh, 0.0)
        ys.append(y)
        w_exp = jnp.exp(_expand_heads([a_tots[hh] - a_cols[hh] for hh in range(4)], lane_head))
        e_row = jnp.exp(_expand_heads(a_tots, lane_head))
        ht_sc[ht0 + grp] = ht * e_row + _dot(bm.T.astype(BF16), (xdt * w_exp).astype(BF16))
    y = jnp.concatenate(ys, axis=1)
    if rev:
        tot = yp_ref[...] + y + dk_ref[...] * x_ref[:, :BW]
        tot = tot * _silu(z_ref[...].astype(F32))
        o_ref[...] = (_rms(tot) * ng_ref[...]).astype(o_ref.dtype)
    else:
        o_ref[...] = y


def _ssd(xbc, usm, ust, u, bias_c, bias_t, aneg_c, aneg_t, rev, yprev=None, dskip=None, norm_g=None):
    nb = xbc.shape[0]
    bpb = SCAN_BB if nb % SCAN_BB == 0 else 1
    cmap = _chunk_of_step(rev)
    in_specs = [pl.BlockSpec((bpb, CS, SSD_XBC), lambda b, s: (b, cmap(s), 0)),
                pl.BlockSpec((bpb, CS, 128), lambda b, s: (b, cmap(s), 0)),
                pl.BlockSpec((bpb, None, 32, CS), lambda b, s: (b, cmap(s), 0, 0)),
                pl.BlockSpec((1, 128), lambda b, s: (0, 0)),
                pl.BlockSpec((32, 1), lambda b, s: (0, 0)),
                pl.BlockSpec((1, 128), lambda b, s: (0, 0)),
                pl.BlockSpec((32, 1), lambda b, s: (0, 0))]
    args = [xbc, usm, ust, bias_c, bias_t, aneg_c, aneg_t]
    if rev:
        in_specs += [pl.BlockSpec((bpb, CS, BW), lambda b, s: (b, cmap(s), 0)),
                     pl.BlockSpec((bpb, CS, BW), lambda b, s: (b, cmap(s), CB_SSD_Z)),
                     pl.BlockSpec((1, BW), lambda b, s: (0, 0)),
                     pl.BlockSpec((1, BW), lambda b, s: (0, 0))]
        args += [yprev, u, dskip, norm_g]
    return pl.pallas_call(
        functools.partial(_ssd_kernel, rev, bpb),
        grid=(nb // bpb, NCH),
        in_specs=in_specs,
        out_specs=pl.BlockSpec((bpb, CS, BW), lambda b, s: (b, cmap(s), 0)),
        out_shape=jax.ShapeDtypeStruct((nb, LT, BW), BF16 if rev else F32),
        scratch_shapes=[pltpu.VMEM((bpb * SSD_GROUPS, SSD_N, 4 * SSD_P), F32)],
        compiler_params=_cparams(("parallel", "arbitrary")),
        name="ssd_rev" if rev else "ssd_fwd",
    )(*args)


def _mla_proj_kernel(cq_ref, ckv_ref, kr_ref, gq_ref, gkv_ref, wqa_ref, wqb_ref, wk_ref, wv_ref,
                     rk_ref, cos_ref, sin_ref, q_ref, k_ref, v_ref):
    nq = (_rms(cq_ref[...].astype(F32)) * gq_ref[...]).astype(BF16)
    qa = _dot(nq, wqa_ref[...])
    qb = _dot(nq, wqb_ref[...])
    cos_t = cos_ref[...]
    sin_t = sin_ref[...]
    cosf = jnp.concatenate([cos_t] * NH, axis=1)
    sinf = jnp.concatenate([sin_t] * NH, axis=1)
    q_ref[...] = ((qa * cosf + qb * sinf) * MLA_SCALE).astype(BF16)
    nkv = (_rms(ckv_ref[...].astype(F32)) * gkv_ref[...]).astype(BF16)
    kn = _dot(nkv, wk_ref[...])
    v_ref[...] = _dot(nkv, wv_ref[...]).astype(BF16)
    kr = kr_ref[...]
    krr = kr.astype(F32) * cos_t[:, 128:256] + _dot(kr, rk_ref[...]) * sin_t[:, 128:256]
    zero = jnp.zeros_like(krr)
    k_ref[...] = (kn + jnp.concatenate([zero, krr] * NH, axis=1)).astype(BF16)


def _mla_proj(u, gq, gkv, wqa, wqb, wk, wv, rk, cos_t, sin_t):
    nb = u.shape[0]
    const = lambda shape: pl.BlockSpec(shape, lambda b, i: (0, 0))
    return pl.pallas_call(
        _mla_proj_kernel,
        grid=(nb, LT // AT),
        in_specs=[pl.BlockSpec((None, AT, MLA_Q_RANK), lambda b, i: (b, i, CB_MLA_CQ)),
                  pl.BlockSpec((None, AT, MLA_KV_RANK), lambda b, i: (b, i, CB_MLA_CKV)),
                  pl.BlockSpec((None, AT, 128), lambda b, i: (b, i, CB_MLA_KR)),
                  const((1, MLA_Q_RANK)), const((1, MLA_KV_RANK)),
                  const((MLA_Q_RANK, NH * 256)), const((MLA_Q_RANK, NH * 256)),
                  const((MLA_KV_RANK, NH * 256)), const((MLA_KV_RANK, NH * DH)),
                  const((128, 128)),
                  pl.BlockSpec((AT, 256), lambda b, i: (i, 0)),
                  pl.BlockSpec((AT, 256), lambda b, i: (i, 0))],
        out_specs=[_row_spec(NH * 256, AT), _row_spec(NH * 256, AT), _row_spec(NH * DH, AT)],
        out_shape=[jax.ShapeDtypeStruct((nb, LT, NH * 256), BF16),
                   jax.ShapeDtypeStruct((nb, LT, NH * 256), BF16),
                   jax.ShapeDtypeStruct((nb, LT, NH * DH), BF16)],
        compiler_params=_cparams(("parallel", "parallel")),
        name="mla_proj",
    )(u, u, u, gq, gkv, wqa, wqb, wk, wv, rk, cos_t, sin_t)


AT = 768


def _attn_kernel(q_ref, k_ref, v_ref, o_ref):
    s = _dot_nt(q_ref[...], k_ref[...])

    def finish(sc):
        m = jnp.max(sc, axis=1, keepdims=True)
        p = jnp.exp(sc - m)
        l = jnp.sum(p, axis=1, keepdims=True)
        o_ref[...] = (_dot(p.astype(BF16), v_ref[...]) / l).astype(o_ref.dtype)

    last = pl.program_id(2) == LT // AT - 1

    @pl.when(last)
    def _():
        qrow = lax.broadcasted_iota(jnp.int32, (AT, 1), 0)
        kcol = lax.broadcasted_iota(jnp.int32, (1, LT), 1)
        hide = jnp.logical_and(qrow >= AT - CTX, kcol < SEQ)
        finish(jnp.where(hide, -jnp.inf, s))

    @pl.when(jnp.logical_not(last))
    def _():
        finish(s)


def _attention(q, k, v):
    nb = q.shape[0]
    return pl.pallas_call(
        _attn_kernel,
        grid=(nb, NH, LT // AT),
        in_specs=[pl.BlockSpec((None, AT, 256), lambda b, h, i: (b, i, h)),
                  pl.BlockSpec((None, LT, 256), lambda b, h, i: (b, 0, h)),
                  pl.BlockSpec((None, LT, DH), lambda b, h, i: (b, 0, h))],
        out_specs=pl.BlockSpec((None, AT, DH), lambda b, h, i: (b, i, h)),
        out_shape=jax.ShapeDtypeStruct((nb, LT, NH * DH), BF16),
        compiler_params=_cparams(("parallel", "parallel", "arbitrary")),
        name="mla_attn",
    )(q, k, v)


def _merge_kernel(h_ref, b0_ref, b1_ref, b2_ref, b3_ref, wg_ref, wb_ref, o_ref):
    h = h_ref[...]
    acc = None
    for kk, br in enumerate((b0_ref, b1_ref, b2_ref, b3_ref)):
        gate = _sigmoid(_dot(h, wg_ref[kk]))
        term = gate * _dot(br[...], wb_ref[kk])
        acc = term if acc is None else acc + term
    o_ref[...] = acc.astype(o_ref.dtype)


def _merge(h, branches, wg, wb, rows, tn=256):
    nb = h.shape[0]
    tm = rows // 2
    rmap = lambda i, j: (i // 2, i % 2, 0)
    bspec = pl.BlockSpec((None, tm, BW), rmap)
    return pl.pallas_call(
        _merge_kernel,
        grid=(nb * 2, D // tn),
        in_specs=[pl.BlockSpec((None, tm, D), rmap), bspec, bspec, bspec, bspec,
                  pl.BlockSpec((4, D, tn), lambda i, j: (0, 0, j)),
                  pl.BlockSpec((4, BW, tn), lambda i, j: (0, 0, j))],
        out_specs=pl.BlockSpec((None, tm, tn), lambda i, j: (i // 2, i % 2, j)),
        out_shape=jax.ShapeDtypeStruct((nb, rows, D), BF16),
        compiler_params=_cparams(("parallel", "arbitrary")),
        name="merge",
    )(h, *branches, wg, wb)


def _outproj_kernel(first, a_ref, w_ref, *refs):
    if first:
        x_ref, c_ref = refs[:2]
        x_in = _pick_x(x_ref, c_ref)
        refs = refs[2:]
    else:
        x_in = refs[0][...]
        refs = refs[1:]
    g1_ref, gate_ref, g2_ref, sh_ref, sc_ref, wa_ref, wb_ref, wc_ref, xo_ref, hp_ref, e_ref, wt_ref = refs
    y = _dot(a_ref[...], w_ref[...])
    xn = x_in + gate_ref[...] * (_rms(y) * g1_ref[...])
    xo_ref[...] = xn
    h2 = _rms(xn) * g2_ref[...]
    h2 = h2 * (1.0 + sc_ref[...]) + sh_ref[...]
    hp_ref[...] = h2
    _route_math(h2, wa_ref, wb_ref, wc_ref, e_ref, wt_ref)


def _outproj(acc, w_out, xs, normv, modv, wr3, layer):
    nb, rows, _ = acc.shape
    first = len(xs) == 2
    wspec = pl.BlockSpec((D, 128), lambda b, i: (0, 0))
    return pl.pallas_call(
        functools.partial(_outproj_kernel, first),
        grid=(nb, rows // RT),
        in_specs=[_row_spec(D), pl.BlockSpec((D, D), lambda b, i: (0, 0))]
                 + (_split_x_specs() if first else [_row_spec(D)])
                 + [_norm_spec(layer, 1), _mod_spec(layer, 2), _norm_spec(layer, 2),
                    _mod_spec(layer, 3), _mod_spec(layer, 4), wspec, wspec, wspec],
        out_specs=[_row_spec(D), _row_spec(D), _row_spec(128), _row_spec(128)],
        out_shape=[jax.ShapeDtypeStruct((nb, rows, D), F32), jax.ShapeDtypeStruct((nb, rows, D), F32),
                   jax.ShapeDtypeStruct((nb, rows, 128), jnp.int32), jax.ShapeDtypeStruct((nb, rows, 128), F32)],
        compiler_params=_cparams(("parallel", "parallel")),
        name="outproj",
    )(acc, w_out, *xs, normv, modv, normv, modv, modv, *wr3)


def _route_math(hf, wa_ref, wb_ref, wc_ref, e_ref, w_ref):
    h = hf.astype(BF16)
    hl = (hf - h.astype(F32)).astype(BF16)
    logits = _dot(h, wa_ref[...]) + _dot(h, wb_ref[...]) + _dot(hl, wa_ref[...])
    lane = lax.broadcasted_iota(jnp.int32, (1, 128), 1).astype(F32)
    ninf = -jnp.inf
    is_grp = lane < N_GROUPS
    gl = jnp.where(is_grp, logits, ninf)
    gmax = jnp.max(gl, axis=1, keepdims=True)
    gidx = jnp.min(jnp.where(gl == gmax, lane, 999.0), axis=1, keepdims=True)
    gsum = jnp.sum(jnp.where(is_grp, jnp.exp(gl - gmax), 0.0), axis=1, keepdims=True)
    gp = 1.0 / gsum
    lo = N_GROUPS + EXP_PER_GROUP * gidx
    sel = jnp.logical_and(lane >= lo, lane < lo + EXP_PER_GROUP)
    el = jnp.where(sel, logits, ninf)
    v1 = jnp.max(el, axis=1, keepdims=True)
    i1 = jnp.min(jnp.where(el == v1, lane, 999.0), axis=1, keepdims=True)
    el2 = jnp.where(lane == i1, ninf, el)
    v2 = jnp.max(el2, axis=1, keepdims=True)
    i2 = jnp.min(jnp.where(el2 == v2, lane, 999.0), axis=1, keepdims=True)
    t = jnp.exp(v2 - v1)
    w1 = gp / (1.0 + t)
    w2 = gp * t / (1.0 + t)
    e_ref[...] = jnp.where(lane == 0.0, i1 - N_GROUPS, jnp.where(lane == 1.0, i2 - N_GROUPS, 0.0)).astype(jnp.int32)
    w_ref[...] = jnp.where(lane == 0.0, w1, jnp.where(lane == 1.0, w2, 0.0))


GATHER_UNROLL = 8


def _expert_kernel(te_ref, nu_ref, rt_ref, h_hbm, w1_ref, w3_ref, w2_ref, o_ref, xbuf, hm_sc, w1b, w3b, w2b, sem):
    i = pl.program_id(0)
    n_used = nu_ref[0]
    slot = i % 2

    def issue_tile(j, s):
        base = j * TE

        def body(g, c):
            for uu in range(GATHER_UNROLL):
                r = g * GATHER_UNROLL + uu
                pltpu.make_async_copy(h_hbm.at[pl.ds(rt_ref[base + r], 1), :],
                                      xbuf.at[s, pl.ds(r, 1), :], sem.at[s]).start(priority=1)
            return c
        lax.fori_loop(0, TE // GATHER_UNROLL, body, 0)

    def wait_tile(s):
        pltpu.make_async_copy(h_hbm.at[pl.ds(0, TE), :], xbuf.at[s], sem.at[s]).wait()

    @pl.when(jnp.logical_and(i == 0, n_used > 0))
    def _():
        issue_tile(0, 0)

    @pl.when(i < n_used)
    def _():
        wait_tile(slot)
        nbase = jnp.minimum(i + 1, n_used - 1) * TE
        oslot = 1 - slot

        def issue_rows(r0, r1):
            for r in range(r0, r1):
                pltpu.make_async_copy(h_hbm.at[pl.ds(rt_ref[nbase + r], 1), :],
                                      xbuf.at[oslot, pl.ds(r, 1), :], sem.at[oslot]).start(priority=1)

        @pl.when(jnp.logical_or(i == 0, te_ref[i] != te_ref[jnp.maximum(i - 1, 0)]))
        def _():
            w1b[...] = w1_ref[...].astype(BF16)
            w3b[...] = w3_ref[...].astype(BF16)
            w2b[...] = w2_ref[...].astype(BF16)

        x = xbuf[slot].astype(BF16)
        cw = 256
        n1, n2 = D_EXPERT // cw, D // cw
        per = TE // (n1 + n2) + 1
        done = 0
        for j in range(n1):
            issue_rows(done, min(done + per, TE))
            done = min(done + per, TE)
            cs = slice(j * cw, (j + 1) * cw)
            a = _dot(x, w1b[:, cs])
            b = _dot(x, w3b[:, cs])
            hm_sc[:, cs] = (_silu(a) * b).astype(BF16)
        hm = hm_sc[...]
        for j in range(n2):
            issue_rows(done, min(done + per, TE))
            done = min(done + per, TE)
            cs = slice(j * cw, (j + 1) * cw)
            o_ref[:, cs] = _dot(hm, w2b[:, cs])

        @pl.when(i + 1 >= n_used)
        def _():
            wait_tile(oslot)

    @pl.when(i >= n_used)
    def _():
        o_ref[...] = jnp.zeros_like(o_ref)


def _experts(tile_expert, n_used, row_token, hp, w1, w3, w2, n_tiles, layer):
    wmap = lambda i, te, nu, rt: (layer * N_EXPERTS + te[i], 0, 0)
    grid_spec = pltpu.PrefetchScalarGridSpec(
        num_scalar_prefetch=3,
        grid=(n_tiles,),
        in_specs=[pl.BlockSpec(memory_space=pl.ANY),
                  pl.BlockSpec((None, D, D_EXPERT), wmap),
                  pl.BlockSpec((None, D, D_EXPERT), wmap),
                  pl.BlockSpec((None, D_EXPERT, D), wmap)],
        out_specs=pl.BlockSpec((TE, D), lambda i, te, nu, rt: (i, 0)),
        scratch_shapes=[pltpu.VMEM((2, TE, D), F32), pltpu.VMEM((TE, D_EXPERT), BF16),
                        pltpu.VMEM((D, D_EXPERT), BF16), pltpu.VMEM((D, D_EXPERT), BF16),
                        pltpu.VMEM((D_EXPERT, D), BF16),
                        pltpu.SemaphoreType.DMA((2,))],
    )
    return pl.pallas_call(
        _expert_kernel,
        grid_spec=grid_spec,
        out_shape=jax.ShapeDtypeStruct((n_tiles * TE, D), F32),
        compiler_params=_cparams(("arbitrary",)),
        name="moe_experts",
    )(tile_expert, n_used, row_token, hp, w1, w3, w2)


def _combine_kernel(has_next, pos_ref, ys_hbm, w_ref, x_ref, g3_ref, gate_ref, *rest):
    if has_next:
        g0_ref, sh_ref, sc_ref, xo_ref, ho_ref, buf, sem = rest
    else:
        xo_ref, buf, sem = rest
    step = pl.program_id(0) * pl.num_programs(1) + pl.program_id(1)
    n_steps = pl.num_programs(0) * pl.num_programs(1)
    slot = step % 2

    def issue_tile(j, s):
        t0 = j * RT

        def body(g, c):
            for uu in range(GATHER_UNROLL):
                r = g * GATHER_UNROLL + uu
                for kk in range(2):
                    pltpu.make_async_copy(ys_hbm.at[pl.ds(pos_ref[(t0 + r) * 2 + kk], 1), :],
                                          buf.at[s, kk, pl.ds(r, 1), :], sem.at[s]).start(priority=kk)
            return c
        lax.fori_loop(0, RT // GATHER_UNROLL, body, 0)

    def wait_tile(s):
        for kk in range(2):
            pltpu.make_async_copy(ys_hbm.at[pl.ds(0, RT), :], buf.at[s, kk], sem.at[s]).wait()

    @pl.when(step == 0)
    def _():
        issue_tile(0, 0)

    wait_tile(slot)
    nt0 = jnp.minimum(step + 1, n_steps - 1) * RT
    oslot = 1 - slot
    rows = 32
    for c in range(RT // rows):
        for r in range(c * rows, (c + 1) * rows):
            for kk in range(2):
                pltpu.make_async_copy(ys_hbm.at[pl.ds(pos_ref[(nt0 + r) * 2 + kk], 1), :],
                                      buf.at[oslot, kk, pl.ds(r, 1), :], sem.at[oslot]).start(priority=kk)
        rs = slice(c * rows, (c + 1) * rows)
        w = w_ref[rs, :]
        y = w[:, 0:1] * buf[slot, 0, rs, :] + w[:, 1:2] * buf[slot, 1, rs, :]
        xn = x_ref[rs, :] + gate_ref[...] * (_rms(y) * g3_ref[...])
        xo_ref[rs, :] = xn
        if has_next:
            hn = _rms(xn) * g0_ref[...]
            ho_ref[rs, :] = (hn * (1.0 + sc_ref[...]) + sh_ref[...]).astype(BF16)

    @pl.when(step + 1 >= n_steps)
    def _():
        wait_tile(oslot)


def _combine(pos, ys, wts, xcat, normv, modv, layer):
    nb = xcat.shape[0]
    has_next = layer + 1 < DEPTH

    def mod_spec(lyr, k):
        def imap(b, i, pos_ref):
            row = jnp.where(i == NRT - 1, 8, b)
            return ((lyr * 16 + row) * 6 + k, 0, 0)
        return pl.BlockSpec((None, 1, D), imap)

    def norm_spec(lyr, k):
        return pl.BlockSpec((None, 1, D), lambda b, i, pos_ref: (lyr * 4 + k, 0, 0))

    def row_spec(width):
        return pl.BlockSpec((None, RT, width), lambda b, i, pos_ref: (b, i, 0))

    in_specs = [pl.BlockSpec(memory_space=pl.ANY), row_spec(128), row_spec(D),
                norm_spec(layer, 3), mod_spec(layer, 5)]
    args = [ys, wts, xcat, normv, modv]
    out_specs = [row_spec(D)]
    n_row_tiles = NRT if has_next else SEQ // RT
    out_shape = [jax.ShapeDtypeStruct((nb, n_row_tiles * RT, D), F32)]
    if has_next:
        in_specs += [norm_spec(layer + 1, 0), mod_spec(layer + 1, 0), mod_spec(layer + 1, 1)]
        args += [normv, modv, modv]
        out_specs.append(row_spec(D))
        out_shape.append(jax.ShapeDtypeStruct((nb, LT, D), BF16))
    grid_spec = pltpu.PrefetchScalarGridSpec(
        num_scalar_prefetch=1,
        grid=(nb, n_row_tiles),
        in_specs=in_specs,
        out_specs=out_specs,
        scratch_shapes=[pltpu.VMEM((2, 2, RT, D), F32), pltpu.SemaphoreType.DMA((2,))],
    )
    return pl.pallas_call(
        functools.partial(_combine_kernel, has_next),
        grid_spec=grid_spec,
        out_shape=out_shape,
        compiler_params=_cparams(("arbitrary", "arbitrary")),
        name="moe_combine",
    )(pos, *args)


def _plan_kernel(e_ref, pos_ref, cnt_ref, rank_sc, carry_sc):
    phase = pl.program_id(0)
    i = pl.program_id(1)
    lane = lax.broadcasted_iota(jnp.int32, (1, 128), 1)
    e = e_ref[...]
    oh1 = lane == e[:, 0:1]
    oh2 = lane == e[:, 1:2]

    @pl.when(jnp.logical_and(phase == 0, i == 0))
    def _():
        carry_sc[...] = jnp.zeros_like(carry_sc)

    @pl.when(phase == 0)
    def _():
        s = jnp.where(jnp.logical_or(oh1, oh2), 1.0, 0.0)
        pt = e_ref.shape[0]
        row = lax.broadcasted_iota(jnp.int32, (pt, pt), 0)
        col = lax.broadcasted_iota(jnp.int32, (pt, pt), 1)
        lower = jnp.where(col < row, 1.0, 0.0).astype(BF16)
        carry = carry_sc[0:1, :]
        before = _dot(lower, s.astype(BF16)) + carry
        r1 = jnp.sum(jnp.where(oh1, before, 0.0), axis=1, keepdims=True)
        r2 = jnp.sum(jnp.where(oh2, before, 0.0), axis=1, keepdims=True)
        rank_sc[i] = jnp.where(lane == 0, r1, jnp.where(lane == 1, r2, 0.0))
        carry_sc[...] = jnp.broadcast_to(carry + jnp.sum(s, axis=0, keepdims=True), carry_sc.shape)
        cnt_ref[...] = carry_sc[...]

    @pl.when(phase == 1)
    def _():
        counts = carry_sc[0:1, :]
        padded = jnp.floor((counts + (TE - 1)) * (1.0 / TE)) * TE
        r128 = lax.broadcasted_iota(jnp.int32, (128, 128), 0)
        c128 = lax.broadcasted_iota(jnp.int32, (128, 128), 1)
        upper = jnp.where(r128 < c128, 1.0, 0.0).astype(BF16)
        pa, pb, pc = _split3(jnp.broadcast_to(padded, (8, 128)))
        offs = (_dot(pa, upper) + _dot(pb, upper) + _dot(pc, upper))[0:1, :]
        rank = rank_sc[i]
        p1 = rank[:, 0:1] + jnp.sum(jnp.where(oh1, offs, 0.0), axis=1, keepdims=True)
        p2 = rank[:, 1:2] + jnp.sum(jnp.where(oh2, offs, 0.0), axis=1, keepdims=True)
        pos_ref[...] = jnp.where(lane == 0, p1, jnp.where(lane == 1, p2, 0.0)).astype(jnp.int32)


def _plan(eid):
    nb, rows, _ = eid.shape
    pt = 768 if rows % 768 == 0 else 1024
    tpb = rows // pt
    nt = nb * tpb
    emap = lambda p, i: (i // tpb, i % tpb, 0)
    return pl.pallas_call(
        _plan_kernel,
        grid=(2, nt),
        in_specs=[pl.BlockSpec((None, pt, 128), emap)],
        out_specs=[pl.BlockSpec((pt, 128), lambda p, i: (p * i, 0)),
                   pl.BlockSpec((8, 128), lambda p, i: (0, 0))],
        out_shape=[jax.ShapeDtypeStruct((nt * pt, 128), jnp.int32), jax.ShapeDtypeStruct((8, 128), F32)],
        scratch_shapes=[pltpu.VMEM((nt, pt, 128), F32), pltpu.VMEM((8, 128), F32)],
        compiler_params=_cparams(("arbitrary", "arbitrary")),
        name="moe_plan",
    )(eid)


def _invert_kernel(pos_ref, rt_ref):
    unroll = 32

    def zero(i, c):
        for uu in range(unroll):
            rt_ref[i * unroll + uu] = 0
        return c
    lax.fori_loop(0, rt_ref.shape[0] // unroll, zero, 0)

    def scatter(i, c):
        for uu in range(unroll):
            a = i * unroll + uu
            rt_ref[pos_ref[a]] = jnp.right_shift(a, 1)
        return c
    lax.fori_loop(0, pos_ref.shape[0] // unroll, scatter, 0)


def _invert(pos_flat, n_rows):
    grid_spec = pltpu.PrefetchScalarGridSpec(
        num_scalar_prefetch=1, grid=(1,), in_specs=[],
        out_specs=pl.BlockSpec(memory_space=pltpu.SMEM))
    return pl.pallas_call(
        _invert_kernel,
        grid_spec=grid_spec,
        out_shape=jax.ShapeDtypeStruct((n_rows,), jnp.int32),
        compiler_params=_cparams(("arbitrary",)),
        name="moe_invert",
    )(pos_flat)


def _dispatch_tiles(pos, counts, n_tiles, n_row_tiles, nb):
    cnt = counts[0, :N_EXPERTS].astype(jnp.int32)
    padded = ((cnt + TE - 1) // TE) * TE
    ends = jnp.cumsum(padded)
    tile_start = jnp.arange(n_tiles, dtype=jnp.int32) * TE
    tile_expert = jnp.sum((ends[None, :] <= tile_start[:, None]).astype(jnp.int32), axis=1)
    tile_expert = jnp.minimum(tile_expert, N_EXPERTS - 1)
    n_used = (ends[-1] // TE).astype(jnp.int32).reshape(1)
    pos_flat = pos[:, :2].reshape(-1)
    row_token = _invert(pos_flat, n_tiles * TE)
    return pos_flat, row_token, tile_expert, n_used


def _rope_tables():
    tok = jnp.arange(SEQ)
    pos_row = (tok // GRID_W).astype(F32)
    pos_col = (tok % GRID_W).astype(F32)
    quarter = MLA_ROPE // 4
    inv_freq = ROPE_BASE ** (-jnp.arange(quarter, dtype=F32) / quarter)
    ang_r = pos_row[:, None] * inv_freq
    ang_c = pos_col[:, None] * inv_freq
    cos = jnp.concatenate([jnp.cos(ang_r)] * 2 + [jnp.cos(ang_c)] * 2, axis=1)
    sin = jnp.concatenate([jnp.sin(ang_r)] * 2 + [jnp.sin(ang_c)] * 2, axis=1)
    cos = jnp.concatenate([cos, jnp.ones((CTX, MLA_ROPE), F32)], axis=0)
    sin = jnp.concatenate([sin, jnp.zeros((CTX, MLA_ROPE), F32)], axis=0)
    pad = jnp.zeros((LT, 64), F32)
    cos_t = jnp.concatenate([jnp.ones((LT, 128), F32), cos, pad], axis=1)
    sin_t = jnp.concatenate([jnp.zeros((LT, 128), F32), sin, pad], axis=1)
    return cos_t, sin_t


def _rope_rot_matrix():
    j = jnp.arange(MLA_ROPE)
    first = (j % 32) < 16
    src = jnp.where(first, j + 16, j - 16)
    sign = jnp.where(first, -1.0, 1.0)
    return jnp.zeros((MLA_ROPE, MLA_ROPE), F32).at[src, j].set(sign)


def _layer_params(l, w_main, ml_f_bias, lb_all, hg_norm_g, mla_g_q, mla_g_kv, mla_w_uq, mla_w_ukv,
                  ssd_conv_w, ssd_conv_b, ssd_a_log, ssd_dt_bias, ssd_d, ssd_norm_g, w_gate, w_br, w_out,
                  moe_w_grp, moe_w_exp):
    p = {}
    p["w_main"] = w_main[l]
    fb = jnp.zeros((2, 2, NH), F32).at[:, 1, :].set(ml_f_bias[l]).reshape(16)
    ml_bias = jnp.concatenate([fb, jnp.zeros((112,), F32)])
    p["ml_bias_c"] = ml_bias.reshape(1, 128)
    p["ml_bias_t"] = ml_bias[:32].reshape(32, 1)
    dtb = jnp.concatenate([jnp.zeros((16,), F32), ssd_dt_bias[l].reshape(16), jnp.zeros((96,), F32)])
    p["ssd_bias_c"] = dtb.reshape(1, 128)
    p["ssd_bias_t"] = dtb[:32].reshape(32, 1)
    aneg = jnp.concatenate([jnp.zeros((16,), F32), -jnp.exp(ssd_a_log[l].astype(F32)).reshape(16),
                            jnp.zeros((96,), F32)])
    p["ssd_aneg_c"] = aneg.reshape(1, 128)
    p["ssd_aneg_t"] = aneg[:32].reshape(32, 1)
    p["lb"] = lb_all[l].reshape(1, BW)
    p["hg_norm_g"] = hg_norm_g[l].reshape(1, BW)
    p["ssd_dskip"] = jnp.repeat(ssd_d[l], SSD_P).reshape(1, BW)
    p["ssd_norm_g"] = ssd_norm_g[l].reshape(1, BW)
    p["conv_w"] = jnp.concatenate([ssd_conv_w[l], jnp.zeros((5, SSD_XBC), F32)], axis=0)
    p["conv_b"] = ssd_conv_b[l].reshape(1, SSD_XBC)
    rot = _rope_rot_matrix()
    wq = mla_w_uq[l].reshape(MLA_Q_RANK, NH, MLA_NOPE + MLA_ROPE)
    zq = jnp.zeros((MLA_Q_RANK, NH, 64), F32)
    wqa = jnp.concatenate([wq, zq], axis=2).reshape(MLA_Q_RANK, NH * 256)
    wq_rot = jnp.einsum("rhj,jk->rhk", wq[:, :, MLA_NOPE:], rot)
    wqb = jnp.concatenate([jnp.zeros((MLA_Q_RANK, NH, MLA_NOPE), F32), wq_rot, zq], axis=2)
    p["wqa"] = wqa.astype(BF16)
    p["wqb"] = wqb.reshape(MLA_Q_RANK, NH * 256).astype(BF16)
    wkv = mla_w_ukv[l].reshape(MLA_KV_RANK, NH, 2 * DH)
    wk = jnp.concatenate([wkv[:, :, :DH], jnp.zeros((MLA_KV_RANK, NH, DH), F32)], axis=2)
    p["wk"] = wk.reshape(MLA_KV_RANK, NH * 256).astype(BF16)
    p["wv"] = wkv[:, :, DH:].reshape(MLA_KV_RANK, NH * DH).astype(BF16)
    p["rk"] = jnp.zeros((128, 128), F32).at[:MLA_ROPE, :MLA_ROPE].set(rot).astype(BF16)
    p["gq"] = mla_g_q[l].reshape(1, MLA_Q_RANK)
    p["gkv"] = mla_g_kv[l].reshape(1, MLA_KV_RANK)
    p["w_gate"] = w_gate[l].astype(BF16)
    p["w_br"] = w_br[l].astype(BF16)
    p["w_out"] = w_out[l].astype(BF16)
    wr = jnp.concatenate([moe_w_grp[l], moe_w_exp[l], jnp.zeros((D, 128 - N_GROUPS - N_EXPERTS), F32)], axis=1)
    a = wr.astype(BF16)
    r = wr - a.astype(F32)
    b = r.astype(BF16)
    c = (r - b.astype(F32)).astype(BF16)
    p["wr3"] = (a, b, c)
    return p


def _mixer(h, p, cos_t, sin_t, nb, out_rows):
    t = nb * LT
    hf = h.reshape(t, D)
    u, usm = _inproj(hf, p["w_main"])
    u = u.reshape(nb, LT, U_W)
    usm = usm.reshape(nb, LT, 128)
    ust = usm[:, :, :32].reshape(nb, NCH, CS, 32).transpose(0, 1, 3, 2)
    ml_f = _mlstm(u, usm, ust, p["ml_bias_c"], p["ml_bias_t"], False)
    ml = _mlstm(u, usm, ust, p["ml_bias_c"], p["ml_bias_t"], True, ml_f)
    hg_f = _hgrn2(u, p["lb"], False)
    hg = _hgrn2(u, p["lb"], True, hg_f, p["hg_norm_g"])
    q, k, v = _mla_proj(u, p["gq"], p["gkv"], p["wqa"], p["wqb"], p["wk"], p["wv"], p["rk"], cos_t, sin_t)
    mla = _attention(q, k, v)
    xbc = _ssd_conv(u, p["conv_w"], p["conv_b"])
    sargs = (xbc, usm, ust, u, p["ssd_bias_c"], p["ssd_bias_t"], p["ssd_aneg_c"], p["ssd_aneg_t"])
    ssd_f = _ssd(*sargs, False)
    ssd = _ssd(*sargs, True, ssd_f, p["ssd_dskip"], p["ssd_norm_g"])
    return _merge(h, (ml, hg, mla, ssd), p["w_gate"], p["w_br"], out_rows)


def kernel(x, c, ctx, c_ctx, w_ada, b_ada, norm_g, w_in, ml_f_bias, hg_lb_logits, hg_norm_g, mla_g_q, mla_g_kv,
           mla_w_uq, mla_w_ukv, ssd_conv_w, ssd_conv_b, ssd_a_log, ssd_dt_bias, ssd_d, ssd_norm_g, w_gate,
           w_br, w_out, moe_w_grp, moe_w_exp, moe_w1, moe_w3, moe_w2):
    nb = x.shape[0]
    t = nb * LT
    lb_all = jnp.cumsum(jax.nn.softmax(hg_lb_logits.astype(F32), axis=0), axis=0)
    cvec = jnp.zeros((16, D), F32).at[:nb].set(c).at[8].set(c_ctx)
    s_vec = jax.nn.silu(cvec).astype(BF16)
    modv = _adaln_mods(s_vec, w_ada, b_ada).reshape(DEPTH * 16 * 6, 1, D)
    normv = norm_g.reshape(DEPTH * 4, 1, D)
    cos_t, sin_t = _rope_tables()
    w_main = _win_layout(w_in)
    w1 = moe_w1.reshape(DEPTH * N_EXPERTS, D, D_EXPERT)
    w3 = moe_w3.reshape(DEPTH * N_EXPERTS, D, D_EXPERT)
    w2 = moe_w2.reshape(DEPTH * N_EXPERTS, D_EXPERT, D)
    h = _prenorm(x, ctx, normv, modv, 0)
    xs = (x, ctx)
    for l in range(DEPTH):
        p = _layer_params(l, w_main, ml_f_bias, lb_all, hg_norm_g, mla_g_q, mla_g_kv, mla_w_uq,
                          mla_w_ukv, ssd_conv_w, ssd_conv_b, ssd_a_log, ssd_dt_bias, ssd_d, ssd_norm_g, w_gate,
                          w_br, w_out, moe_w_grp, moe_w_exp)
        rows = LT if l + 1 < DEPTH else SEQ
        n_row_tiles = rows // RT
        acc = _mixer(h, p, cos_t, sin_t, nb, rows)
        xcat, hp, eid, wts = _outproj(acc, p["w_out"], xs, normv, modv, p["wr3"], l)
        n_tiles = (2 * nb * rows + N_EXPERTS * (TE - 1)) // TE + 1
        pos128, counts = _plan(eid)
        pos, row_token, tile_expert, n_used = _dispatch_tiles(pos128, counts, n_tiles, n_row_tiles, nb)
        ys = _experts(tile_expert, n_used, row_token, hp.reshape(nb * rows, D), w1, w3, w2, n_tiles, l)
        outs = _combine(pos, ys, wts, xcat, normv, modv, l)
        if l + 1 < DEPTH:
            xcat, h = outs
            xs = (xcat,)
    return outs[0]
```

```python
import functools

import jax
import jax.numpy as jnp
from jax import lax
from jax.experimental import pallas as pl
from jax.experimental.pallas import tpu as pltpu

F32 = jnp.float32
BF16 = jnp.bfloat16

D = 2048
SEQ = 2048
CTX = 256
LT = SEQ + CTX
DEPTH = 2
EPS = 1e-6
GRID_W = 64
NH = 4
DH = 128
BW = 512
MLA_Q_RANK = 512
MLA_KV_RANK = 256
MLA_NOPE = 128
MLA_ROPE = 64
MLA_SCALE = (MLA_NOPE + MLA_ROPE) ** -0.5
ROPE_BASE = 10000.0
SSD_HEADS = 8
SSD_P = 64
SSD_GROUPS = 2
SSD_N = 128
SSD_XBC = 1024
N_GROUPS = 4
EXP_PER_GROUP = 8
N_EXPERTS = 32
D_EXPERT = 512

CS = 128
NLC = SEQ // CS
NCC = CTX // CS
NCH = NLC + NCC
RT = 256
NRT = LT // RT
TE = 256
VMEM_LIMIT = 52 * 1024 * 1024

U_W = 7168
CB_ML_Q, CB_ML_K, CB_ML_V, CB_ML_O = 0, 1, 2, 3
CB_HG_Q, CB_HG_I, CB_HG_G, CB_HG_F = 4, 5, 6, 7
CB_SSD_Z = 9
CB_SSD_XBC = 5
CB_MLA_CQ = 12
CB_MLA_CKV = 26
CB_MLA_KR = 54


def _cparams(sem):
    return pltpu.CompilerParams(dimension_semantics=sem, vmem_limit_bytes=VMEM_LIMIT)


def _dot(a, b):
    return jnp.dot(a, b, preferred_element_type=F32)


def _dot_nt(a, b):
    return lax.dot_general(a, b, (((1,), (1,)), ((), ())), preferred_element_type=F32)


def _sigmoid(x):
    return 1.0 / (1.0 + jnp.exp(-x))


def _silu(x):
    return x * _sigmoid(x)


def _softplus(x):
    return jnp.maximum(x, 0.0) + jnp.log(1.0 + jnp.exp(-jnp.abs(x)))


def _log_sigmoid(x):
    return -_softplus(-x)


def _rms(x):
    return x * lax.rsqrt(jnp.mean(x * x, axis=-1, keepdims=True) + EPS)


def _split3(x):
    a = x.astype(BF16)
    r = x - a.astype(F32)
    b = r.astype(BF16)
    c = (r - b.astype(F32)).astype(BF16)
    return a, b, c


def _cumsum_rows(mb, x):
    a, b, c = _split3(x)
    return _dot(mb, a) + _dot(mb, b) + _dot(mb, c)


def _cumsum_cols(x, mb):
    a, b, c = _split3(x)
    return _dot_nt(a, mb) + _dot_nt(b, mb) + _dot_nt(c, mb)


def _chunk_mask(rev):
    row = lax.broadcasted_iota(jnp.int32, (CS, CS), 0)
    col = lax.broadcasted_iota(jnp.int32, (CS, CS), 1)
    return (col >= row) if rev else (col <= row)


def _chunk_of_step(rev):
    if rev:
        return lambda s: NCH - 1 - s
    return lambda s: (s + NLC) % NCH


def _mm_kernel(a_ref, w_ref, b_ref, o_ref):
    acc = _dot(a_ref[...], w_ref[...].astype(BF16))
    o_ref[...] = (acc + b_ref[...]).astype(o_ref.dtype)


def _inproj_kernel(a_ref, w_ref, o_ref, s_ref):
    acc = _dot(a_ref[...], w_ref[...])
    o_ref[...] = acc.astype(o_ref.dtype)

    @pl.when(pl.program_id(1) == pl.num_programs(1) - 1)
    def _():
        s_ref[...] = acc[:, acc.shape[1] - 128:]


def _inproj(a, w, tm=1152, tn=1024):
    m, k = a.shape
    n = w.shape[1]
    return pl.pallas_call(
        _inproj_kernel,
        grid=(m // tm, n // tn),
        in_specs=[pl.BlockSpec((tm, k), lambda i, j: (i, 0)),
                  pl.BlockSpec((k, tn), lambda i, j: (0, j))],
        out_specs=[pl.BlockSpec((tm, tn), lambda i, j: (i, j)),
                   pl.BlockSpec((tm, 128), lambda i, j: (i, 0))],
        out_shape=[jax.ShapeDtypeStruct((m, n), BF16), jax.ShapeDtypeStruct((m, 128), F32)],
        compiler_params=_cparams(("parallel", "arbitrary")),
        name="inproj",
    )(a, w)


W_IN_PIECES = ((0, 0, 2048),
               (2048, 2064, 2560),
               (4608, 5456, 512),
               (5120, 5968, 1024),
               (6144, 4624, 512),
               (6656, 5136, 256),
               (6912, 5392, 64),
               (7040, 2048, 16),
               (7056, 6992, 16))
D_IN = 7008


def _win_kernel(w_ref, o_ref):
    o_ref[...] = jnp.zeros_like(o_ref)
    for dst, src, width in W_IN_PIECES:
        o_ref[:, dst:dst + width] = w_ref[:, src:src + width].astype(BF16)


def _win_layout(w_in):
    rows = 256
    return pl.pallas_call(
        _win_kernel,
        grid=(DEPTH, D // rows),
        in_specs=[pl.BlockSpec((None, rows, D_IN), lambda l, i: (l, i, 0))],
        out_specs=pl.BlockSpec((None, rows, U_W), lambda l, i: (l, i, 0)),
        out_shape=jax.ShapeDtypeStruct((DEPTH, D, U_W), BF16),
        compiler_params=_cparams(("parallel", "parallel")),
        name="win_layout",
    )(w_in)


def _adaln_mods(s_vec, w_ada, b_ada):
    tn = 1024
    return pl.pallas_call(
        _mm_kernel,
        grid=(DEPTH, 6 * D // tn),
        in_specs=[pl.BlockSpec((16, D), lambda l, j: (0, 0)),
                  pl.BlockSpec((None, D, tn), lambda l, j: (l, 0, j)),
                  pl.BlockSpec((None, 1, tn), lambda l, j: (l, 0, j))],
        out_specs=pl.BlockSpec((None, 16, tn), lambda l, j: (l, 0, j)),
        out_shape=jax.ShapeDtypeStruct((DEPTH, 16, 6 * D), F32),
        compiler_params=_cparams(("parallel", "arbitrary")),
        name="adaln_mods",
    )(s_vec, w_ada, b_ada.reshape(DEPTH, 1, 6 * D))


def _mod_spec(layer, k):
    def imap(b, i):
        row = jnp.where(i == NRT - 1, 8, b)
        return ((layer * 16 + row) * 6 + k, 0, 0)
    return pl.BlockSpec((None, 1, D), imap)


def _norm_spec(layer, k):
    return pl.BlockSpec((None, 1, D), lambda b, i: (layer * 4 + k, 0, 0))


def _row_spec(width, dtype_rows=RT):
    return pl.BlockSpec((None, dtype_rows, width), lambda b, i: (b, i, 0))


def _split_x_specs():
    return [pl.BlockSpec((None, RT, D), lambda b, i: (b, jnp.minimum(i, NRT - 2), 0)),
            pl.BlockSpec((None, RT, D), lambda b, i: (b, 0, 0))]


def _pick_x(x_ref, c_ref):
    return jnp.where(pl.program_id(1) == NRT - 1, c_ref[...], x_ref[...])


def _prenorm_kernel(x_ref, c_ref, g_ref, sh_ref, sc_ref, h_ref):
    y = _rms(_pick_x(x_ref, c_ref)) * g_ref[...]
    h_ref[...] = (y * (1.0 + sc_ref[...]) + sh_ref[...]).astype(BF16)


def _prenorm(x, ctx, normv, modv, layer):
    nb = x.shape[0]
    return pl.pallas_call(
        _prenorm_kernel,
        grid=(nb, NRT),
        in_specs=_split_x_specs() + [_norm_spec(layer, 0), _mod_spec(layer, 0), _mod_spec(layer, 1)],
        out_specs=_row_spec(D),
        out_shape=jax.ShapeDtypeStruct((nb, LT, D), BF16),
        compiler_params=_cparams(("parallel", "parallel")),
        name="prenorm",
    )(x, ctx, normv, modv, modv)


def _conv_kernel(x_ref, p_ref, n_ref, w_ref, b_ref, o_ref):
    i = pl.program_id(1)
    x = x_ref[...].astype(F32)
    has_prev = jnp.logical_and(i != 0, i != NRT - 1).astype(F32)
    has_next = jnp.logical_and(i != NRT - 2, i != NRT - 1).astype(F32)
    prev_row = p_ref[15:16, :].astype(F32) * has_prev
    next_row = n_ref[0:1, :].astype(F32) * has_next
    ridx = lax.broadcasted_iota(jnp.int32, (RT, 1), 0)
    xm1 = jnp.where(ridx == 0, prev_row, pltpu.roll(x, 1, 0))
    xp1 = jnp.where(ridx == RT - 1, next_row, pltpu.roll(x, RT - 1, 0))
    w = w_ref[...]
    y = w[0:1, :] * xm1 + w[1:2, :] * x + w[2:3, :] * xp1 + b_ref[...]
    o_ref[...] = _silu(y)


def _ssd_conv(u, conv_w, conv_b):
    nb = u.shape[0]
    hb = RT // 16
    last = LT // 16 - 1
    return pl.pallas_call(
        _conv_kernel,
        grid=(nb, NRT),
        in_specs=[pl.BlockSpec((None, RT, SSD_XBC), lambda b, i: (b, i, CB_SSD_XBC)),
                  pl.BlockSpec((None, 16, SSD_XBC), lambda b, i: (b, jnp.maximum(i * hb - 1, 0), CB_SSD_XBC)),
                  pl.BlockSpec((None, 16, SSD_XBC), lambda b, i: (b, jnp.minimum((i + 1) * hb, last), CB_SSD_XBC)),
                  pl.BlockSpec((8, SSD_XBC), lambda b, i: (0, 0)),
                  pl.BlockSpec((1, SSD_XBC), lambda b, i: (0, 0))],
        out_specs=pl.BlockSpec((None, RT, SSD_XBC), lambda b, i: (b, i, 0)),
        out_shape=jax.ShapeDtypeStruct((nb, LT, SSD_XBC), F32),
        compiler_params=_cparams(("parallel", "parallel")),
        name="ssd_conv",
    )(u, u, u, conv_w, conv_b)


SCAN_BB = 2


def _mlstm_kernel(rev, bpb, *refs):
    if rev:
        (qs_ref, ks_ref, vs_ref, gc_ref, gt_ref, bc_ref, bt_ref, yps_ref, ogs_ref,
         os_ref, c_sc, n_sc, m_sc) = refs
    else:
        (qs_ref, ks_ref, vs_ref, gc_ref, gt_ref, bc_ref, bt_ref,
         os_ref, c_sc, n_sc, m_sc) = refs
    d = 1 if rev else 0

    @pl.when(pl.program_id(1) == 0)
    def _():
        c_sc[...] = jnp.zeros_like(c_sc)
        n_sc[...] = jnp.zeros_like(n_sc)
        m_sc[...] = jnp.full(m_sc.shape, -jnp.inf, F32)

    mask = _chunk_mask(rev)
    mb = jnp.where(mask, 1.0, 0.0).astype(BF16)
    scale = DH ** -0.5
    units = [(bb, h) for bb in range(bpb) for h in range(NH)]
    gates = []
    for bb in range(bpb):
        gc = gc_ref[bb] + bc_ref[...]
        gt = gt_ref[bb] + bt_ref[...]
        flog_c = _log_sigmoid(gc)
        flog_t = _log_sigmoid(gt)
        gates.append((gc, gt, flog_c, _cumsum_rows(mb, flog_c), _cumsum_cols(flog_t, mb)))
    st = []
    for bb, h in units:
        sl = slice(h * DH, (h + 1) * DH)
        q, k, v = qs_ref[bb, :, sl], ks_ref[bb, :, sl], vs_ref[bb, :, sl]
        c_h = c_sc.at[bb * NH + h]
        st.append(dict(q=q, k=k, v=v, sl=sl, qk=_dot_nt(q, k), qc=_dot(q, c_h[...].astype(BF16))))
    for (bb, h), w in zip(units, st):
        gc, gt, flog_c, fcum_c, fcum_t = gates[bb]
        ci = d * 8 + h
        cf = d * 8 + 4 + h
        w["i_col"] = gc[:, ci:ci + 1]
        w["f_col"] = flog_c[:, cf:cf + 1]
        w["fc_col"] = fcum_c[:, cf:cf + 1]
        w["m_prev"] = m_sc[bb * NH + h][0:1, 0:1]
        dm = jnp.where(mask, w["fc_col"] - fcum_t[cf:cf + 1, :] + gt[ci:ci + 1, :], -jnp.inf)
        from_state = w["fc_col"] + w["m_prev"]
        w["m_t"] = jnp.maximum(from_state, jnp.max(dm, axis=1, keepdims=True))
        w["s"] = w["qk"] * scale * jnp.exp(dm - w["m_t"])
        w["w_state"] = jnp.exp(from_state - w["m_t"])
    for (bb, h), w in zip(units, st):
        n_h = n_sc.at[bb * NH + h]
        num = _dot(w["s"].astype(BF16), w["v"]) + w["w_state"] * (w["qc"] * scale)
        qn = jnp.sum(w["q"].astype(F32) * n_h[0:1, :], axis=1, keepdims=True) * scale
        den = jnp.sum(w["s"], axis=1, keepdims=True) + w["w_state"] * qn
        hout = num / jnp.maximum(jnp.abs(den), jnp.exp(-w["m_t"]))
        if rev:
            tot = yps_ref[bb, :, w["sl"]] + hout
            os_ref[bb, :, w["sl"]] = (tot * _sigmoid(ogs_ref[bb, :, w["sl"]].astype(F32))).astype(os_ref.dtype)
        else:
            os_ref[bb, :, w["sl"]] = hout
    for (bb, h), w in zip(units, st):
        c_h, n_h, m_h = c_sc.at[bb * NH + h], n_sc.at[bb * NH + h], m_sc.at[bb * NH + h]
        f_tot = jnp.sum(w["f_col"], axis=0, keepdims=True)
        w_log = f_tot - w["fc_col"] + w["i_col"]
        m_new = jnp.maximum(f_tot + w["m_prev"], jnp.max(w_log, axis=0, keepdims=True))
        decay = jnp.exp(f_tot + w["m_prev"] - m_new)
        kw = w["k"].astype(F32) * jnp.exp(w_log - m_new)
        c_h[...] = decay * c_h[...] + _dot(kw.T.astype(BF16), w["v"])
        n_new = decay * n_h[0:1, :] + jnp.sum(kw, axis=0, keepdims=True)
        n_h[...] = jnp.broadcast_to(n_new, (8, DH))
        m_h[...] = jnp.broadcast_to(m_new, (8, DH))


def _mlstm(u, usm, ust, bias_c, bias_t, rev, yprev=None):
    nb = u.shape[0]
    bpb = SCAN_BB if nb % SCAN_BB == 0 else 1
    cmap = _chunk_of_step(rev)
    uspec = lambda cblk: pl.BlockSpec((bpb, CS, BW), lambda b, s: (b, cmap(s), cblk))
    in_specs = [uspec(CB_ML_Q), uspec(CB_ML_K), uspec(CB_ML_V),
                pl.BlockSpec((bpb, CS, 128), lambda b, s: (b, cmap(s), 0)),
                pl.BlockSpec((bpb, None, 32, CS), lambda b, s: (b, cmap(s), 0, 0)),
                pl.BlockSpec((1, 128), lambda b, s: (0, 0)),
                pl.BlockSpec((32, 1), lambda b, s: (0, 0))]
    args = [u, u, u, usm, ust, bias_c, bias_t]
    if rev:
        in_specs += [uspec(0), uspec(CB_ML_O)]
        args += [yprev, u]
    return pl.pallas_call(
        functools.partial(_mlstm_kernel, rev, bpb),
        grid=(nb // bpb, NCH),
        in_specs=in_specs,
        out_specs=uspec(0),
        out_shape=jax.ShapeDtypeStruct((nb, LT, BW), BF16 if rev else F32),
        scratch_shapes=[pltpu.VMEM((bpb * NH, DH, DH), F32), pltpu.VMEM((bpb * NH, 8, DH), F32),
                        pltpu.VMEM((bpb * NH, 8, DH), F32)],
        compiler_params=_cparams(("parallel", "arbitrary")),
        name="mlstm_rev" if rev else "mlstm_fwd",
    )(*args)


HG_BAND = 2


def _block_ref(g, n, off):
    g3 = g.reshape(CS // n, n, DH)
    r = g3[:, off:off + 1, :]
    return jnp.broadcast_to(r, (CS // n, n, DH)).reshape(CS, DH)


def _hgrn2_kernel(rev, bpb, *refs):
    if rev:
        qs_ref, is_ref, fs_ref, lb_ref, yps_ref, gs_ref, ng_ref, os_ref, st_sc = refs
    else:
        qs_ref, is_ref, fs_ref, lb_ref, os_ref, st_sc = refs

    @pl.when(pl.program_id(1) == 0)
    def _():
        st_sc[...] = jnp.zeros_like(st_sc)

    mask = _chunk_mask(rev)
    mb = jnp.where(mask, 1.0, 0.0).astype(BF16)
    row = lax.broadcasted_iota(jnp.int32, (CS, CS), 0)
    col = lax.broadcasted_iota(jnp.int32, (CS, CS), 1)
    tix = lax.broadcasted_iota(jnp.int32, (CS, 1), 0)
    for bb, h in [(bb, h) for bb in range(bpb) for h in range(NH)]:
        q_ref, i_ref, f_ref, o_ref = qs_ref.at[bb], is_ref.at[bb], fs_ref.at[bb], os_ref.at[bb]
        if rev:
            yp_ref, g_ref = yps_ref.at[bb], gs_ref.at[bb]
        st_h = st_sc.at[bb * NH + h]
        sl = slice(h * DH, (h + 1) * DH)
        q = _silu(q_ref[:, sl].astype(F32))
        v = i_ref[:, sl].astype(F32)
        lb = lb_ref[:, sl]
        f = lb + (1.0 - lb) * _sigmoid(f_ref[:, sl].astype(F32))
        k = 1.0 - f
        lf = jnp.log(f)
        g = _cumsum_rows(mb, lf)
        st = st_h[...]
        o = _dot_nt((q * jnp.exp(g)).astype(BF16), st.astype(BF16))
        a = jnp.zeros((CS, CS), F32)
        n = CS
        while n >= 2 * HG_BAND:
            half = n // 2
            second = jnp.bitwise_and(tix, n - 1) >= half
            rowside = jnp.logical_not(second) if rev else second
            r = _block_ref(g, n, half if rev else half - 1)
            e = jnp.exp(jnp.where(rowside, g - r, r - g))
            qs = jnp.where(rowside, q * e, 0.0).astype(BF16)
            ks = jnp.where(rowside, 0.0, k * e).astype(BF16)
            sh_n = n.bit_length() - 1
            same = jnp.right_shift(row, sh_n) == jnp.right_shift(col, sh_n)
            a = a + jnp.where(same, _dot_nt(qs, ks), 0.0)
            n = half
        o = o + _dot(a.astype(BF16), v.astype(BF16))
        t8 = jnp.bitwise_and(tix, HG_BAND - 1)
        k3, g3, v3 = (a3.reshape(CS // 8, 8, DH) for a3 in (k, g, v))
        for dl in range(HG_BAND):
            if dl == 0:
                ksh, gsh, vsh = k, g, v
            else:
                sh = (8 - dl) if rev else dl
                ksh = pltpu.roll(k3, sh, 1).reshape(CS, DH)
                gsh = pltpu.roll(g3, sh, 1).reshape(CS, DH)
                vsh = pltpu.roll(v3, sh, 1).reshape(CS, DH)
            valid = (t8 + dl <= HG_BAND - 1) if rev else (t8 >= dl)
            e = jnp.exp(jnp.minimum(g - gsh, 0.0))
            coef = jnp.sum(q * ksh * e, axis=1, keepdims=True)
            o = o + jnp.where(valid, coef, 0.0) * vsh
        if rev:
            tot = yp_ref[:, sl] + o
            y = _rms(tot) * ng_ref[:, sl]
            o_ref[:, sl] = (y * _sigmoid(g_ref[:, sl].astype(F32))).astype(o_ref.dtype)
        else:
            o_ref[:, sl] = o
        g_tot = jnp.sum(lf, axis=0, keepdims=True)
        kdec = k * jnp.exp(g_tot - g)
        st_h[...] = st * jnp.exp(g_tot) + _dot(v.T.astype(BF16), kdec.astype(BF16))


def _hgrn2(u, lb, rev, yprev=None, norm_g=None):
    nb = u.shape[0]
    bpb = SCAN_BB if nb % SCAN_BB == 0 else 1
    cmap = _chunk_of_step(rev)
    d = 1 if rev else 0
    uspec = lambda cblk: pl.BlockSpec((bpb, CS, BW), lambda b, s: (b, cmap(s), cblk))
    in_specs = [uspec(CB_HG_Q), uspec(CB_HG_I), uspec(CB_HG_F + d),
                pl.BlockSpec((1, BW), lambda b, s: (0, 0))]
    args = [u, u, u, lb]
    if rev:
        in_specs += [uspec(0), uspec(CB_HG_G), pl.BlockSpec((1, BW), lambda b, s: (0, 0))]
        args += [yprev, u, norm_g]
    return pl.pallas_call(
        functools.partial(_hgrn2_kernel, rev, bpb),
        grid=(nb // bpb, NCH),
        in_specs=in_specs,
        out_specs=uspec(0),
        out_shape=jax.ShapeDtypeStruct((nb, LT, BW), BF16 if rev else F32),
        scratch_shapes=[pltpu.VMEM((bpb * NH, DH, DH), F32)],
        compiler_params=_cparams(("parallel", "arbitrary")),
        name="hgrn2_rev" if rev else "hgrn2_fwd",
    )(*args)


def _expand_heads(cols, lane_head):
    r = cols[3]
    for hh in (2, 1, 0):
        r = jnp.where(lane_head == hh, cols[hh], r)
    return r


def _ssd_kernel(rev, bpb, *refs):
    if rev:
        (xs_ref, gcs_ref, gts_ref, bc_ref, bt_ref, ac_ref, at_ref, yps_ref, zs_ref, dk_ref, ng_ref,
         os_ref, ht_sc) = refs
    else:
        xs_ref, gcs_ref, gts_ref, bc_ref, bt_ref, ac_ref, at_ref, os_ref, ht_sc = refs

    @pl.when(pl.program_id(1) == 0)
    def _():
        ht_sc[...] = jnp.zeros_like(ht_sc)

    for bb in range(bpb):
        extra = (yps_ref.at[bb], zs_ref.at[bb], dk_ref, ng_ref) if rev else ()
        _ssd_chunk(rev, xs_ref.at[bb], gcs_ref[bb], gts_ref[bb], bc_ref, bt_ref, ac_ref, at_ref,
                   os_ref.at[bb], ht_sc, bb * SSD_GROUPS, *extra)


def _ssd_chunk(rev, x_ref, gc, gt, bc_ref, bt_ref, ac_ref, at_ref, o_ref, ht_sc, ht0, yp_ref=None, z_ref=None,
               dk_ref=None, ng_ref=None):
    d = 1 if rev else 0
    mask = _chunk_mask(rev)
    mb = jnp.where(mask, 1.0, 0.0).astype(BF16)
    dt_c = _softplus(gc + bc_ref[...])
    da_c = dt_c * ac_ref[...]
    acum_c = _cumsum_rows(mb, da_c)
    dt_t = _softplus(gt + bt_ref[...])
    da_t = dt_t * at_ref[...]
    acum_t = _cumsum_cols(da_t, mb)
    gw = 4 * SSD_P
    lane_head = jnp.right_shift(lax.broadcasted_iota(jnp.int32, (1, gw), 1), 6)
    ys = []
    for grp in range(SSD_GROUPS):
        xg = x_ref[:, grp * gw:(grp + 1) * gw]
        bm = x_ref[:, BW + grp * SSD_N:BW + (grp + 1) * SSD_N]
        cm = x_ref[:, BW + SSD_GROUPS * SSD_N + grp * SSD_N:BW + SSD_GROUPS * SSD_N + (grp + 1) * SSD_N]
        cmb = cm.astype(BF16)
        cb = _dot_nt(cmb, bm.astype(BF16))
        ht = ht_sc[ht0 + grp]
        y_state = _dot(cmb, ht.astype(BF16))
        lanes = [16 + d * 8 + grp * 4 + hh for hh in range(4)]
        dt_cols = [dt_c[:, l:l + 1] for l in lanes]
        a_cols = [acum_c[:, l:l + 1] for l in lanes]
        a_tots = [jnp.sum(da_c[:, l:l + 1], axis=0, keepdims=True) for l in lanes]
        xdt = xg * _expand_heads(dt_cols, lane_head)
        xdtb = xdt.astype(BF16)
        y = jnp.exp(_expand_heads(a_cols, lane_head)) * y_state
        for hh in range(4):
            l = lanes[hh]
            dec = jnp.exp(jnp.where(mask, a_cols[hh] - acum_t[l:l + 1, :], -jnp.inf))
            yh = _dot((cb * dec).astype(BF16), xdtb)
            y = y + jnp.where(lane_head == hh, yh, 0.0)
        ys.append(y)
        w_exp = jnp.exp(_expand_heads([a_tots[hh] - a_cols[hh] for hh in range(4)], lane_head))
        e_row = jnp.exp(_expand_heads(a_tots, lane_head))
        ht_sc[ht0 + grp] = ht * e_row + _dot(bm.T.astype(BF16), (xdt * w_exp).astype(BF16))
    y = jnp.concatenate(ys, axis=1)
    if rev:
        tot = yp_ref[...] + y + dk_ref[...] * x_ref[:, :BW]
        tot = tot * _silu(z_ref[...].astype(F32))
        o_ref[...] = (_rms(tot) * ng_ref[...]).astype(o_ref.dtype)
    else:
        o_ref[...] = y


def _ssd(xbc, usm, ust, u, bias_c, bias_t, aneg_c, aneg_t, rev, yprev=None, dskip=None, norm_g=None):
    nb = xbc.shape[0]
    bpb = SCAN_BB if nb % SCAN_BB == 0 else 1
    cmap = _chunk_of_step(rev)
    in_specs = [pl.BlockSpec((bpb, CS, SSD_XBC), lambda b, s: (b, cmap(s), 0)),
                pl.BlockSpec((bpb, CS, 128), lambda b, s: (b, cmap(s), 0)),
                pl.BlockSpec((bpb, None, 32, CS), lambda b, s: (b, cmap(s), 0, 0)),
                pl.BlockSpec((1, 128), lambda b, s: (0, 0)),
                pl.BlockSpec((32, 1), lambda b, s: (0, 0)),
                pl.BlockSpec((1, 128), lambda b, s: (0, 0)),
                pl.BlockSpec((32, 1), lambda b, s: (0, 0))]
    args = [xbc, usm, ust, bias_c, bias_t, aneg_c, aneg_t]
    if rev:
        in_specs += [pl.BlockSpec((bpb, CS, BW), lambda b, s: (b, cmap(s), 0)),
                     pl.BlockSpec((bpb, CS, BW), lambda b, s: (b, cmap(s), CB_SSD_Z)),
                     pl.BlockSpec((1, BW), lambda b, s: (0, 0)),
                     pl.BlockSpec((1, BW), lambda b, s: (0, 0))]
        args += [yprev, u, dskip, norm_g]
    return pl.pallas_call(
        functools.partial(_ssd_kernel, rev, bpb),
        grid=(nb // bpb, NCH),
        in_specs=in_specs,
        out_specs=pl.BlockSpec((bpb, CS, BW), lambda b, s: (b, cmap(s), 0)),
        out_shape=jax.ShapeDtypeStruct((nb, LT, BW), BF16 if rev else F32),
        scratch_shapes=[pltpu.VMEM((bpb * SSD_GROUPS, SSD_N, 4 * SSD_P), F32)],
        compiler_params=_cparams(("parallel", "arbitrary")),
        name="ssd_rev" if rev else "ssd_fwd",
    )(*args)


def _mla_proj_kernel(cq_ref, ckv_ref, kr_ref, gq_ref, gkv_ref, wqa_ref, wqb_ref, wk_ref, wv_ref,
                     rk_ref, cos_ref, sin_ref, q_ref, k_ref, v_ref):
    nq = (_rms(cq_ref[...].astype(F32)) * gq_ref[...]).astype(BF16)
    qa = _dot(nq, wqa_ref[...])
    qb = _dot(nq, wqb_ref[...])
    cos_t = cos_ref[...]
    sin_t = sin_ref[...]
    cosf = jnp.concatenate([cos_t] * NH, axis=1)
    sinf = jnp.concatenate([sin_t] * NH, axis=1)
    q_ref[...] = ((qa * cosf + qb * sinf) * MLA_SCALE).astype(BF16)
    nkv = (_rms(ckv_ref[...].astype(F32)) * gkv_ref[...]).astype(BF16)
    kn = _dot(nkv, wk_ref[...])
    v_ref[...] = _dot(nkv, wv_ref[...]).astype(BF16)
    kr = kr_ref[...]
    krr = kr.astype(F32) * cos_t[:, 128:256] + _dot(kr, rk_ref[...]) * sin_t[:, 128:256]
    zero = jnp.zeros_like(krr)
    k_ref[...] = (kn + jnp.concatenate([zero, krr] * NH, axis=1)).astype(BF16)


def _mla_proj(u, gq, gkv, wqa, wqb, wk, wv, rk, cos_t, sin_t):
    nb = u.shape[0]
    const = lambda shape: pl.BlockSpec(shape, lambda b, i: (0, 0))
    return pl.pallas_call(
        _mla_proj_kernel,
        grid=(nb, LT // AT),
        in_specs=[pl.BlockSpec((None, AT, MLA_Q_RANK), lambda b, i: (b, i, CB_MLA_CQ)),
                  pl.BlockSpec((None, AT, MLA_KV_RANK), lambda b, i: (b, i, CB_MLA_CKV)),
                  pl.BlockSpec((None, AT, 128), lambda b, i: (b, i, CB_MLA_KR)),
                  const((1, MLA_Q_RANK)), const((1, MLA_KV_RANK)),
                  const((MLA_Q_RANK, NH * 256)), const((MLA_Q_RANK, NH * 256)),
                  const((MLA_KV_RANK, NH * 256)), const((MLA_KV_RANK, NH * DH)),
                  const((128, 128)),
                  pl.BlockSpec((AT, 256), lambda b, i: (i, 0)),
                  pl.BlockSpec((AT, 256), lambda b, i: (i, 0))],
        out_specs=[_row_spec(NH * 256, AT), _row_spec(NH * 256, AT), _row_spec(NH * DH, AT)],
        out_shape=[jax.ShapeDtypeStruct((nb, LT, NH * 256), BF16),
                   jax.ShapeDtypeStruct((nb, LT, NH * 256), BF16),
                   jax.ShapeDtypeStruct((nb, LT, NH * DH), BF16)],
        compiler_params=_cparams(("parallel", "parallel")),
        name="mla_proj",
    )(u, u, u, gq, gkv, wqa, wqb, wk, wv, rk, cos_t, sin_t)


AT = 768


def _attn_kernel(q_ref, k_ref, v_ref, o_ref):
    s = _dot_nt(q_ref[...], k_ref[...])

    def finish(sc):
        m = jnp.max(sc, axis=1, keepdims=True)
        p = jnp.exp(sc - m)
        l = jnp.sum(p, axis=1, keepdims=True)
        o_ref[...] = (_dot(p.astype(BF16), v_ref[...]) / l).astype(o_ref.dtype)

    last = pl.program_id(2) == LT // AT - 1

    @pl.when(last)
    def _():
        qrow = lax.broadcasted_iota(jnp.int32, (AT, 1), 0)
        kcol = lax.broadcasted_iota(jnp.int32, (1, LT), 1)
        hide = jnp.logical_and(qrow >= AT - CTX, kcol < SEQ)
        finish(jnp.where(hide, -jnp.inf, s))

    @pl.when(jnp.logical_not(last))
    def _():
        finish(s)


def _attention(q, k, v):
    nb = q.shape[0]
    return pl.pallas_call(
        _attn_kernel,
        grid=(nb, NH, LT // AT),
        in_specs=[pl.BlockSpec((None, AT, 256), lambda b, h, i: (b, i, h)),
                  pl.BlockSpec((None, LT, 256), lambda b, h, i: (b, 0, h)),
                  pl.BlockSpec((None, LT, DH), lambda b, h, i: (b, 0, h))],
        out_specs=pl.BlockSpec((None, AT, DH), lambda b, h, i: (b, i, h)),
        out_shape=jax.ShapeDtypeStruct((nb, LT, NH * DH), BF16),
        compiler_params=_cparams(("parallel", "parallel", "arbitrary")),
        name="mla_attn",
    )(q, k, v)


def _merge_kernel(h_ref, b0_ref, b1_ref, b2_ref, b3_ref, wg_ref, wb_ref, o_ref):
    h = h_ref[...]
    acc = None
    for kk, br in enumerate((b0_ref, b1_ref, b2_ref, b3_ref)):
        gate = _sigmoid(_dot(h, wg_ref[kk]))
        term = gate * _dot(br[...], wb_ref[kk])
        acc = term if acc is None else acc + term
    o_ref[...] = acc.astype(o_ref.dtype)


def _merge(h, branches, wg, wb, rows, tn=256):
    nb = h.shape[0]
    tm = rows // 2
    rmap = lambda i, j: (i // 2, i % 2, 0)
    bspec = pl.BlockSpec((None, tm, BW), rmap)
    return pl.pallas_call(
        _merge_kernel,
        grid=(nb * 2, D // tn),
        in_specs=[pl.BlockSpec((None, tm, D), rmap), bspec, bspec, bspec, bspec,
                  pl.BlockSpec((4, D, tn), lambda i, j: (0, 0, j)),
                  pl.BlockSpec((4, BW, tn), lambda i, j: (0, 0, j))],
        out_specs=pl.BlockSpec((None, tm, tn), lambda i, j: (i // 2, i % 2, j)),
        out_shape=jax.ShapeDtypeStruct((nb, rows, D), BF16),
        compiler_params=_cparams(("parallel", "arbitrary")),
        name="merge",
    )(h, *branches, wg, wb)


def _outproj_kernel(first, a_ref, w_ref, *refs):
    if first:
        x_ref, c_ref = refs[:2]
        x_in = _pick_x(x_ref, c_ref)
        refs = refs[2:]
    else:
        x_in = refs[0][...]
        refs = refs[1:]
    g1_ref, gate_ref, g2_ref, sh_ref, sc_ref, wa_ref, wb_ref, wc_ref, xo_ref, hp_ref, e_ref, wt_ref = refs
    y = _dot(a_ref[...], w_ref[...])
    xn = x_in + gate_ref[...] * (_rms(y) * g1_ref[...])
    xo_ref[...] = xn
    h2 = _rms(xn) * g2_ref[...]
    h2 = h2 * (1.0 + sc_ref[...]) + sh_ref[...]
    hp_ref[...] = h2
    _route_math(h2, wa_ref, wb_ref, wc_ref, e_ref, wt_ref)


def _outproj(acc, w_out, xs, normv, modv, wr3, layer):
    nb, rows, _ = acc.shape
    first = len(xs) == 2
    wspec = pl.BlockSpec((D, 128), lambda b, i: (0, 0))
    return pl.pallas_call(
        functools.partial(_outproj_kernel, first),
        grid=(nb, rows // RT),
        in_specs=[_row_spec(D), pl.BlockSpec((D, D), lambda b, i: (0, 0))]
                 + (_split_x_specs() if first else [_row_spec(D)])
                 + [_norm_spec(layer, 1), _mod_spec(layer, 2), _norm_spec(layer, 2),
                    _mod_spec(layer, 3), _mod_spec(layer, 4), wspec, wspec, wspec],
        out_specs=[_row_spec(D), _row_spec(D), _row_spec(128), _row_spec(128)],
        out_shape=[jax.ShapeDtypeStruct((nb, rows, D), F32), jax.ShapeDtypeStruct((nb, rows, D), F32),
                   jax.ShapeDtypeStruct((nb, rows, 128), jnp.int32), jax.ShapeDtypeStruct((nb, rows, 128), F32)],
        compiler_params=_cparams(("parallel", "parallel")),
        name="outproj",
    )(acc, w_out, *xs, normv, modv, normv, modv, modv, *wr3)


def _route_math(hf, wa_ref, wb_ref, wc_ref, e_ref, w_ref):
    h = hf.astype(BF16)
    hl = (hf - h.astype(F32)).astype(BF16)
    logits = _dot(h, wa_ref[...]) + _dot(h, wb_ref[...]) + _dot(hl, wa_ref[...])
    lane = lax.broadcasted_iota(jnp.int32, (1, 128), 1).astype(F32)
    ninf = -jnp.inf
    is_grp = lane < N_GROUPS
    gl = jnp.where(is_grp, logits, ninf)
    gmax = jnp.max(gl, axis=1, keepdims=True)
    gidx = jnp.min(jnp.where(gl == gmax, lane, 999.0), axis=1, keepdims=True)
    gsum = jnp.sum(jnp.where(is_grp, jnp.exp(gl - gmax), 0.0), axis=1, keepdims=True)
    gp = 1.0 / gsum
    lo = N_GROUPS + EXP_PER_GROUP * gidx
    sel = jnp.logical_and(lane >= lo, lane < lo + EXP_PER_GROUP)
    el = jnp.where(sel, logits, ninf)
    v1 = jnp.max(el, axis=1, keepdims=True)
    i1 = jnp.min(jnp.where(el == v1, lane, 999.0), axis=1, keepdims=True)
    el2 = jnp.where(lane == i1, ninf, el)
    v2 = jnp.max(el2, axis=1, keepdims=True)
    i2 = jnp.min(jnp.where(el2 == v2, lane, 999.0), axis=1, keepdims=True)
    t = jnp.exp(v2 - v1)
    w1 = gp / (1.0 + t)
    w2 = gp * t / (1.0 + t)
    e_ref[...] = jnp.where(lane == 0.0, i1 - N_GROUPS, jnp.where(lane == 1.0, i2 - N_GROUPS, 0.0)).astype(jnp.int32)
    w_ref[...] = jnp.where(lane == 0.0, w1, jnp.where(lane == 1.0, w2, 0.0))


GATHER_UNROLL = 8


def _expert_kernel(te_ref, nu_ref, rt_ref, h_hbm, w1_ref, w3_ref, w2_ref, o_ref, xbuf, hm_sc, w1b, w3b, w2b, sem):
    i = pl.program_id(0)
    n_used = nu_ref[0]
    slot = i % 2

    def issue_tile(j, s):
        base = j * TE

        def body(g, c):
            for uu in range(GATHER_UNROLL):
                r = g * GATHER_UNROLL + uu
                pltpu.make_async_copy(h_hbm.at[pl.ds(rt_ref[base + r], 1), :],
                                      xbuf.at[s, pl.ds(r, 1), :], sem.at[s]).start(priority=1)
            return c
        lax.fori_loop(0, TE // GATHER_UNROLL, body, 0)

    def wait_tile(s):
        pltpu.make_async_copy(h_hbm.at[pl.ds(0, TE), :], xbuf.at[s], sem.at[s]).wait()

    @pl.when(jnp.logical_and(i == 0, n_used > 0))
    def _():
        issue_tile(0, 0)

    @pl.when(i < n_used)
    def _():
        wait_tile(slot)
        nbase = jnp.minimum(i + 1, n_used - 1) * TE
        oslot = 1 - slot

        def issue_rows(r0, r1):
            for r in range(r0, r1):
                pltpu.make_async_copy(h_hbm.at[pl.ds(rt_ref[nbase + r], 1), :],
                                      xbuf.at[oslot, pl.ds(r, 1), :], sem.at[oslot]).start(priority=1)

        @pl.when(jnp.logical_or(i == 0, te_ref[i] != te_ref[jnp.maximum(i - 1, 0)]))
        def _():
            w1b[...] = w1_ref[...].astype(BF16)
            w3b[...] = w3_ref[...].astype(BF16)
            w2b[...] = w2_ref[...].astype(BF16)

        x = xbuf[slot].astype(BF16)
        cw = 256
        n1, n2 = D_EXPERT // cw, D // cw
        per = TE // (n1 + n2) + 1
        done = 0
        for j in range(n1):
            issue_rows(done, min(done + per, TE))
            done = min(done + per, TE)
            cs = slice(j * cw, (j + 1) * cw)
            a = _dot(x, w1b[:, cs])
            b = _dot(x, w3b[:, cs])
            hm_sc[:, cs] = (_silu(a) * b).astype(BF16)
        hm = hm_sc[...]
        for j in range(n2):
            issue_rows(done, min(done + per, TE))
            done = min(done + per, TE)
            cs = slice(j * cw, (j + 1) * cw)
            o_ref[:, cs] = _dot(hm, w2b[:, cs])

        @pl.when(i + 1 >= n_used)
        def _():
            wait_tile(oslot)

    @pl.when(i >= n_used)
    def _():
        o_ref[...] = jnp.zeros_like(o_ref)


def _experts(tile_expert, n_used, row_token, hp, w1, w3, w2, n_tiles, layer):
    wmap = lambda i, te, nu, rt: (layer * N_EXPERTS + te[i], 0, 0)
    grid_spec = pltpu.PrefetchScalarGridSpec(
        num_scalar_prefetch=3,
        grid=(n_tiles,),
        in_specs=[pl.BlockSpec(memory_space=pl.ANY),
                  pl.BlockSpec((None, D, D_EXPERT), wmap),
                  pl.BlockSpec((None, D, D_EXPERT), wmap),
                  pl.BlockSpec((None, D_EXPERT, D), wmap)],
        out_specs=pl.BlockSpec((TE, D), lambda i, te, nu, rt: (i, 0)),
        scratch_shapes=[pltpu.VMEM((2, TE, D), F32), pltpu.VMEM((TE, D_EXPERT), BF16),
                        pltpu.VMEM((D, D_EXPERT), BF16), pltpu.VMEM((D, D_EXPERT), BF16),
                        pltpu.VMEM((D_EXPERT, D), BF16),
                        pltpu.SemaphoreType.DMA((2,))],
    )
    return pl.pallas_call(
        _expert_kernel,
        grid_spec=grid_spec,
        out_shape=jax.ShapeDtypeStruct((n_tiles * TE, D), F32),
        compiler_params=_cparams(("arbitrary",)),
        name="moe_experts",
    )(tile_expert, n_used, row_token, hp, w1, w3, w2)


def _combine_kernel(has_next, pos_ref, ys_hbm, w_ref, x_ref, g3_ref, gate_ref, *rest):
    if has_next:
        g0_ref, sh_ref, sc_ref, xo_ref, ho_ref, buf, sem = rest
    else:
        xo_ref, buf, sem = rest
    step = pl.program_id(0) * pl.num_programs(1) + pl.program_id(1)
    n_steps = pl.num_programs(0) * pl.num_programs(1)
    slot = step % 2

    def issue_tile(j, s):
        t0 = j * RT

        def body(g, c):
            for uu in range(GATHER_UNROLL):
                r = g * GATHER_UNROLL + uu
                for kk in range(2):
                    pltpu.make_async_copy(ys_hbm.at[pl.ds(pos_ref[(t0 + r) * 2 + kk], 1), :],
                                          buf.at[s, kk, pl.ds(r, 1), :], sem.at[s]).start(priority=kk)
            return c
        lax.fori_loop(0, RT // GATHER_UNROLL, body, 0)

    def wait_tile(s):
        for kk in range(2):
            pltpu.make_async_copy(ys_hbm.at[pl.ds(0, RT), :], buf.at[s, kk], sem.at[s]).wait()

    @pl.when(step == 0)
    def _():
        issue_tile(0, 0)

    wait_tile(slot)
    nt0 = jnp.minimum(step + 1, n_steps - 1) * RT
    oslot = 1 - slot
    rows = 32
    for c in range(RT // rows):
        for r in range(c * rows, (c + 1) * rows):
            for kk in range(2):
                pltpu.make_async_copy(ys_hbm.at[pl.ds(pos_ref[(nt0 + r) * 2 + kk], 1), :],
                                      buf.at[oslot, kk, pl.ds(r, 1), :], sem.at[oslot]).start(priority=kk)
        rs = slice(c * rows, (c + 1) * rows)
        w = w_ref[rs, :]
        y = w[:, 0:1] * buf[slot, 0, rs, :] + w[:, 1:2] * buf[slot, 1, rs, :]
        xn = x_ref[rs, :] + gate_ref[...] * (_rms(y) * g3_ref[...])
        xo_ref[rs, :] = xn
        if has_next:
            hn = _rms(xn) * g0_ref[...]
            ho_ref[rs, :] = (hn * (1.0 + sc_ref[...]) + sh_ref[...]).astype(BF16)

    @pl.when(step + 1 >= n_steps)
    def _():
        wait_tile(oslot)


def _combine(pos, ys, wts, xcat, normv, modv, layer):
    nb = xcat.shape[0]
    has_next = layer + 1 < DEPTH

    def mod_spec(lyr, k):
        def imap(b, i, pos_ref):
            row = jnp.where(i == NRT - 1, 8, b)
            return ((lyr * 16 + row) * 6 + k, 0, 0)
        return pl.BlockSpec((None, 1, D), imap)

    def norm_spec(lyr, k):
        return pl.BlockSpec((None, 1, D), lambda b, i, pos_ref: (lyr * 4 + k, 0, 0))

    def row_spec(width):
        return pl.BlockSpec((None, RT, width), lambda b, i, pos_ref: (b, i, 0))

    in_specs = [pl.BlockSpec(memory_space=pl.ANY), row_spec(128), row_spec(D),
                norm_spec(layer, 3), mod_spec(layer, 5)]
    args = [ys, wts, xcat, normv, modv]
    out_specs = [row_spec(D)]
    n_row_tiles = NRT if has_next else SEQ // RT
    out_shape = [jax.ShapeDtypeStruct((nb, n_row_tiles * RT, D), F32)]
    if has_next:
        in_specs += [norm_spec(layer + 1, 0), mod_spec(layer + 1, 0), mod_spec(layer + 1, 1)]
        args += [normv, modv, modv]
        out_specs.append(row_spec(D))
        out_shape.append(jax.ShapeDtypeStruct((nb, LT, D), BF16))
    grid_spec = pltpu.PrefetchScalarGridSpec(
        num_scalar_prefetch=1,
        grid=(nb, n_row_tiles),
        in_specs=in_specs,
        out_specs=out_specs,
        scratch_shapes=[pltpu.VMEM((2, 2, RT, D), F32), pltpu.SemaphoreType.DMA((2,))],
    )
    return pl.pallas_call(
        functools.partial(_combine_kernel, has_next),
        grid_spec=grid_spec,
        out_shape=out_shape,
        compiler_params=_cparams(("arbitrary", "arbitrary")),
        name="moe_combine",
    )(pos, *args)


def _plan_kernel(e_ref, pos_ref, cnt_ref, rank_sc, carry_sc):
    phase = pl.program_id(0)
    i = pl.program_id(1)
    lane = lax.broadcasted_iota(jnp.int32, (1, 128), 1)
    e = e_ref[...]
    oh1 = lane == e[:, 0:1]
    oh2 = lane == e[:, 1:2]

    @pl.when(jnp.logical_and(phase == 0, i == 0))
    def _():
        carry_sc[...] = jnp.zeros_like(carry_sc)

    @pl.when(phase == 0)
    def _():
        s = jnp.where(jnp.logical_or(oh1, oh2), 1.0, 0.0)
        pt = e_ref.shape[0]
        row = lax.broadcasted_iota(jnp.int32, (pt, pt), 0)
        col = lax.broadcasted_iota(jnp.int32, (pt, pt), 1)
        lower = jnp.where(col < row, 1.0, 0.0).astype(BF16)
        carry = carry_sc[0:1, :]
        before = _dot(lower, s.astype(BF16)) + carry
        r1 = jnp.sum(jnp.where(oh1, before, 0.0), axis=1, keepdims=True)
        r2 = jnp.sum(jnp.where(oh2, before, 0.0), axis=1, keepdims=True)
        rank_sc[i] = jnp.where(lane == 0, r1, jnp.where(lane == 1, r2, 0.0))
        carry_sc[...] = jnp.broadcast_to(carry + jnp.sum(s, axis=0, keepdims=True), carry_sc.shape)
        cnt_ref[...] = carry_sc[...]

    @pl.when(phase == 1)
    def _():
        counts = carry_sc[0:1, :]
        padded = jnp.floor((counts + (TE - 1)) * (1.0 / TE)) * TE
        r128 = lax.broadcasted_iota(jnp.int32, (128, 128), 0)
        c128 = lax.broadcasted_iota(jnp.int32, (128, 128), 1)
        upper = jnp.where(r128 < c128, 1.0, 0.0).astype(BF16)
        pa, pb, pc = _split3(jnp.broadcast_to(padded, (8, 128)))
        offs = (_dot(pa, upper) + _dot(pb, upper) + _dot(pc, upper))[0:1, :]
        rank = rank_sc[i]
        p1 = rank[:, 0:1] + jnp.sum(jnp.where(oh1, offs, 0.0), axis=1, keepdims=True)
        p2 = rank[:, 1:2] + jnp.sum(jnp.where(oh2, offs, 0.0), axis=1, keepdims=True)
        pos_ref[...] = jnp.where(lane == 0, p1, jnp.where(lane == 1, p2, 0.0)).astype(jnp.int32)


def _plan(eid):
    nb, rows, _ = eid.shape
    pt = 768 if rows % 768 == 0 else 1024
    tpb = rows // pt
    nt = nb * tpb
    emap = lambda p, i: (i // tpb, i % tpb, 0)
    return pl.pallas_call(
        _plan_kernel,
        grid=(2, nt),
        in_specs=[pl.BlockSpec((None, pt, 128), emap)],
        out_specs=[pl.BlockSpec((pt, 128), lambda p, i: (p * i, 0)),
                   pl.BlockSpec((8, 128), lambda p, i: (0, 0))],
        out_shape=[jax.ShapeDtypeStruct((nt * pt, 128), jnp.int32), jax.ShapeDtypeStruct((8, 128), F32)],
        scratch_shapes=[pltpu.VMEM((nt, pt, 128), F32), pltpu.VMEM((8, 128), F32)],
        compiler_params=_cparams(("arbitrary", "arbitrary")),
        name="moe_plan",
    )(eid)


def _invert_kernel(pos_ref, rt_ref):
    unroll = 32

    def zero(i, c):
        for uu in range(unroll):
            rt_ref[i * unroll + uu] = 0
        return c
    lax.fori_loop(0, rt_ref.shape[0] // unroll, zero, 0)

    def scatter(i, c):
        for uu in range(unroll):
            a = i * unroll + uu
            rt_ref[pos_ref[a]] = jnp.right_shift(a, 1)
        return c
    lax.fori_loop(0, pos_ref.shape[0] // unroll, scatter, 0)


def _invert(pos_flat, n_rows):
    grid_spec = pltpu.PrefetchScalarGridSpec(
        num_scalar_prefetch=1, grid=(1,), in_specs=[],
        out_specs=pl.BlockSpec(memory_space=pltpu.SMEM))
    return pl.pallas_call(
        _invert_kernel,
        grid_spec=grid_spec,
        out_shape=jax.ShapeDtypeStruct((n_rows,), jnp.int32),
        compiler_params=_cparams(("arbitrary",)),
        name="moe_invert",
    )(pos_flat)


def _dispatch_tiles(pos, counts, n_tiles, n_row_tiles, nb):
    cnt = counts[0, :N_EXPERTS].astype(jnp.int32)
    padded = ((cnt + TE - 1) // TE) * TE
    ends = jnp.cumsum(padded)
    tile_start = jnp.arange(n_tiles, dtype=jnp.int32) * TE
    tile_expert = jnp.sum((ends[None, :] <= tile_start[:, None]).astype(jnp.int32), axis=1)
    tile_expert = jnp.minimum(tile_expert, N_EXPERTS - 1)
    n_used = (ends[-1] // TE).astype(jnp.int32).reshape(1)
    pos_flat = pos[:, :2].reshape(-1)
    row_token = _invert(pos_flat, n_tiles * TE)
    return pos_flat, row_token, tile_expert, n_used


def _rope_tables():
    tok = jnp.arange(SEQ)
    pos_row = (tok // GRID_W).astype(F32)
    pos_col = (tok % GRID_W).astype(F32)
    quarter = MLA_ROPE // 4
    inv_freq = ROPE_BASE ** (-jnp.arange(quarter, dtype=F32) / quarter)
    ang_r = pos_row[:, None] * inv_freq
    ang_c = pos_col[:, None] * inv_freq
    cos = jnp.concatenate([jnp.cos(ang_r)] * 2 + [jnp.cos(ang_c)] * 2, axis=1)
    sin = jnp.concatenate([jnp.sin(ang_r)] * 2 + [jnp.sin(ang_c)] * 2, axis=1)
    cos = jnp.concatenate([cos, jnp.ones((CTX, MLA_ROPE), F32)], axis=0)
    sin = jnp.concatenate([sin, jnp.zeros((CTX, MLA_ROPE), F32)], axis=0)
    pad = jnp.zeros((LT, 64), F32)
    cos_t = jnp.concatenate([jnp.ones((LT, 128), F32), cos, pad], axis=1)
    sin_t = jnp.concatenate([jnp.zeros((LT, 128), F32), sin, pad], axis=1)
    return cos_t, sin_t


def _rope_rot_matrix():
    j = jnp.arange(MLA_ROPE)
    first = (j % 32) < 16
    src = jnp.where(first, j + 16, j - 16)
    sign = jnp.where(first, -1.0, 1.0)
    return jnp.zeros((MLA_ROPE, MLA_ROPE), F32).at[src, j].set(sign)


def _layer_params(l, w_main, ml_f_bias, lb_all, hg_norm_g, mla_g_q, mla_g_kv, mla_w_uq, mla_w_ukv,
                  ssd_conv_w, ssd_conv_b, ssd_a_log, ssd_dt_bias, ssd_d, ssd_norm_g, w_gate, w_br, w_out,
                  moe_w_grp, moe_w_exp):
    p = {}
    p["w_main"] = w_main[l]
    fb = jnp.zeros((2, 2, NH), F32).at[:, 1, :].set(ml_f_bias[l]).reshape(16)
    ml_bias = jnp.concatenate([fb, jnp.zeros((112,), F32)])
    p["ml_bias_c"] = ml_bias.reshape(1, 128)
    p["ml_bias_t"] = ml_bias[:32].reshape(32, 1)
    dtb = jnp.concatenate([jnp.zeros((16,), F32), ssd_dt_bias[l].reshape(16), jnp.zeros((96,), F32)])
    p["ssd_bias_c"] = dtb.reshape(1, 128)
    p["ssd_bias_t"] = dtb[:32].reshape(32, 1)
    aneg = jnp.concatenate([jnp.zeros((16,), F32), -jnp.exp(ssd_a_log[l].astype(F32)).reshape(16),
                            jnp.zeros((96,), F32)])
    p["ssd_aneg_c"] = aneg.reshape(1, 128)
    p["ssd_aneg_t"] = aneg[:32].reshape(32, 1)
    p["lb"] = lb_all[l].reshape(1, BW)
    p["hg_norm_g"] = hg_norm_g[l].reshape(1, BW)
    p["ssd_dskip"] = jnp.repeat(ssd_d[l], SSD_P).reshape(1, BW)
    p["ssd_norm_g"] = ssd_norm_g[l].reshape(1, BW)
    p["conv_w"] = jnp.concatenate([ssd_conv_w[l], jnp.zeros((5, SSD_XBC), F32)], axis=0)
    p["conv_b"] = ssd_conv_b[l].reshape(1, SSD_XBC)
    rot = _rope_rot_matrix()
    wq = mla_w_uq[l].reshape(MLA_Q_RANK, NH, MLA_NOPE + MLA_ROPE)
    zq = jnp.zeros((MLA_Q_RANK, NH, 64), F32)
    wqa = jnp.concatenate([wq, zq], axis=2).reshape(MLA_Q_RANK, NH * 256)
    wq_rot = jnp.einsum("rhj,jk->rhk", wq[:, :, MLA_NOPE:], rot)
    wqb = jnp.concatenate([jnp.zeros((MLA_Q_RANK, NH, MLA_NOPE), F32), wq_rot, zq], axis=2)
    p["wqa"] = wqa.astype(BF16)
    p["wqb"] = wqb.reshape(MLA_Q_RANK, NH * 256).astype(BF16)
    wkv = mla_w_ukv[l].reshape(MLA_KV_RANK, NH, 2 * DH)
    wk = jnp.concatenate([wkv[:, :, :DH], jnp.zeros((MLA_KV_RANK, NH, DH), F32)], axis=2)
    p["wk"] = wk.reshape(MLA_KV_RANK, NH * 256).astype(BF16)
    p["wv"] = wkv[:, :, DH:].reshape(MLA_KV_RANK, NH * DH).astype(BF16)
    p["rk"] = jnp.zeros((128, 128), F32).at[:MLA_ROPE, :MLA_ROPE].set(rot).astype(BF16)
    p["gq"] = mla_g_q[l].reshape(1, MLA_Q_RANK)
    p["gkv"] = mla_g_kv[l].reshape(1, MLA_KV_RANK)
    p["w_gate"] = w_gate[l].astype(BF16)
    p["w_br"] = w_br[l].astype(BF16)
    p["w_out"] = w_out[l].astype(BF16)
    wr = jnp.concatenate([moe_w_grp[l], moe_w_exp[l], jnp.zeros((D, 128 - N_GROUPS - N_EXPERTS), F32)], axis=1)
    a = wr.astype(BF16)
    r = wr - a.astype(F32)
    b = r.astype(BF16)
    c = (r - b.astype(F32)).astype(BF16)
    p["wr3"] = (a, b, c)
    return p


def _mixer(h, p, cos_t, sin_t, nb, out_rows):
    t = nb * LT
    hf = h.reshape(t, D)
    u, usm = _inproj(hf, p["w_main"])
    u = u.reshape(nb, LT, U_W)
    usm = usm.reshape(nb, LT, 128)
    ust = usm[:, :, :32].reshape(nb, NCH, CS, 32).transpose(0, 1, 3, 2)
    ml_f = _mlstm(u, usm, ust, p["ml_bias_c"], p["ml_bias_t"], False)
    ml = _mlstm(u, usm, ust, p["ml_bias_c"], p["ml_bias_t"], True, ml_f)
    hg_f = _hgrn2(u, p["lb"], False)
    hg = _hgrn2(u, p["lb"], True, hg_f, p["hg_norm_g"])
    q, k, v = _mla_proj(u, p["gq"], p["gkv"], p["wqa"], p["wqb"], p["wk"], p["wv"], p["rk"], cos_t, sin_t)
    mla = _attention(q, k, v)
    xbc = _ssd_conv(u, p["conv_w"], p["conv_b"])
    sargs = (xbc, usm, ust, u, p["ssd_bias_c"], p["ssd_bias_t"], p["ssd_aneg_c"], p["ssd_aneg_t"])
    ssd_f = _ssd(*sargs, False)
    ssd = _ssd(*sargs, True, ssd_f, p["ssd_dskip"], p["ssd_norm_g"])
    return _merge(h, (ml, hg, mla, ssd), p["w_gate"], p["w_br"], out_rows)


def kernel(x, c, ctx, c_ctx, w_ada, b_ada, norm_g, w_in, ml_f_bias, hg_lb_logits, hg_norm_g, mla_g_q, mla_g_kv,
           mla_w_uq, mla_w_ukv, ssd_conv_w, ssd_conv_b, ssd_a_log, ssd_dt_bias, ssd_d, ssd_norm_g, w_gate,
           w_br, w_out, moe_w_grp, moe_w_exp, moe_w1, moe_w3, moe_w2):
    nb = x.shape[0]
    t = nb * LT
    lb_all = jnp.cumsum(jax.nn.softmax(hg_lb_logits.astype(F32), axis=0), axis=0)
    cvec = jnp.zeros((16, D), F32).at[:nb].set(c).at[8].set(c_ctx)
    s_vec = jax.nn.silu(cvec).astype(BF16)
    modv = _adaln_mods(s_vec, w_ada, b_ada).reshape(DEPTH * 16 * 6, 1, D)
    normv = norm_g.reshape(DEPTH * 4, 1, D)
    cos_t, sin_t = _rope_tables()
    w_main = _win_layout(w_in)
    w1 = moe_w1.reshape(DEPTH * N_EXPERTS, D, D_EXPERT)
    w3 = moe_w3.reshape(DEPTH * N_EXPERTS, D, D_EXPERT)
    w2 = moe_w2.reshape(DEPTH * N_EXPERTS, D_EXPERT, D)
    h = _prenorm(x, ctx, normv, modv, 0)
    xs = (x, ctx)
    for l in range(DEPTH):
        p = _layer_params(l, w_main, ml_f_bias, lb_all, hg_norm_g, mla_g_q, mla_g_kv, mla_w_uq,
                          mla_w_ukv, ssd_conv_w, ssd_conv_b, ssd_a_log, ssd_dt_bias, ssd_d, ssd_norm_g, w_gate,
                          w_br, w_out, moe_w_grp, moe_w_exp)
        rows = LT if l + 1 < DEPTH else SEQ
        n_row_tiles = rows // RT
        acc = _mixer(h, p, cos_t, sin_t, nb, rows)
        xcat, hp, eid, wts = _outproj(acc, p["w_out"], xs, normv, modv, p["wr3"], l)
        n_tiles = (2 * nb * rows + N_EXPERTS * (TE - 1)) // TE + 1
        pos128, counts = _plan(eid)
        pos, row_token, tile_expert, n_used = _dispatch_tiles(pos128, counts, n_tiles, n_row_tiles, nb)
        ys = _experts(tile_expert, n_used, row_token, hp.reshape(nb * rows, D), w1, w3, w2, n_tiles, l)
        outs = _combine(pos, ys, wts, xcat, normv, modv, l)
        if l + 1 < DEPTH:
            xcat, h = outs
            xs = (xcat,)
    return outs[0]
```

```python
import functools

import jax
import jax.numpy as jnp
from jax import lax
from jax.experimental import pallas as pl
from jax.experimental.pallas import tpu as pltpu

F32 = jnp.float32
BF16 = jnp.bfloat16

D = 2048
SEQ = 2048
CTX = 256
LT = SEQ + CTX
DEPTH = 2
EPS = 1e-6
GRID_W = 64
NH = 4
DH = 128
BW = 512
MLA_Q_RANK = 512
MLA_KV_RANK = 256
MLA_NOPE = 128
MLA_ROPE = 64
MLA_SCALE = (MLA_NOPE + MLA_ROPE) ** -0.5
ROPE_BASE = 10000.0
SSD_HEADS = 8
SSD_P = 64
SSD_GROUPS = 2
SSD_N = 128
SSD_XBC = 1024
N_GROUPS = 4
EXP_PER_GROUP = 8
N_EXPERTS = 32
D_EXPERT = 512

CS = 128
NLC = SEQ // CS
NCC = CTX // CS
NCH = NLC + NCC
RT = 256
NRT = LT // RT
TE = 256
VMEM_LIMIT = 52 * 1024 * 1024

U_W = 7168
CB_ML_Q, CB_ML_K, CB_ML_V, CB_ML_O = 0, 1, 2, 3
CB_HG_Q, CB_HG_I, CB_HG_G, CB_HG_F = 4, 5, 6, 7
CB_SSD_Z = 9
CB_SSD_XBC = 5
CB_MLA_CQ = 12
CB_MLA_CKV = 26
CB_MLA_KR = 54


def _cparams(sem):
    return pltpu.CompilerParams(dimension_semantics=sem, vmem_limit_bytes=VMEM_LIMIT)


def _dot(a, b):
    return jnp.dot(a, b, preferred_element_type=F32)


def _dot_nt(a, b):
    return lax.dot_general(a, b, (((1,), (1,)), ((), ())), preferred_element_type=F32)


def _sigmoid(x):
    return 1.0 / (1.0 + jnp.exp(-x))


def _silu(x):
    return x * _sigmoid(x)


def _softplus(x):
    return jnp.maximum(x, 0.0) + jnp.log(1.0 + jnp.exp(-jnp.abs(x)))


def _log_sigmoid(x):
    return -_softplus(-x)


def _rms(x):
    return x * lax.rsqrt(jnp.mean(x * x, axis=-1, keepdims=True) + EPS)


def _split3(x):
    a = x.astype(BF16)
    r = x - a.astype(F32)
    b = r.astype(BF16)
    c = (r - b.astype(F32)).astype(BF16)
    return a, b, c


def _cumsum_rows(mb, x):
    a, b, c = _split3(x)
    return _dot(mb, a) + _dot(mb, b) + _dot(mb, c)


def _cumsum_cols(x, mb):
    a, b, c = _split3(x)
    return _dot_nt(a, mb) + _dot_nt(b, mb) + _dot_nt(c, mb)


def _chunk_mask(rev):
    row = lax.broadcasted_iota(jnp.int32, (CS, CS), 0)
    col = lax.broadcasted_iota(jnp.int32, (CS, CS), 1)
    return (col >= row) if rev else (col <= row)


def _chunk_of_step(rev):
    if rev:
        return lambda s: NCH - 1 - s
    return lambda s: (s + NLC) % NCH


def _mm_kernel(a_ref, w_ref, b_ref, o_ref):
    acc = _dot(a_ref[...], w_ref[...].astype(BF16))
    o_ref[...] = (acc + b_ref[...]).astype(o_ref.dtype)


def _inproj_kernel(a_ref, w_ref, o_ref, s_ref):
    acc = _dot(a_ref[...], w_ref[...])
    o_ref[...] = acc.astype(o_ref.dtype)

    @pl.when(pl.program_id(1) == pl.num_programs(1) - 1)
    def _():
        s_ref[...] = acc[:, acc.shape[1] - 128:]


def _inproj(a, w, tm=LT, tn=1024):
    m, k = a.shape
    n = w.shape[1]
    return pl.pallas_call(
        _inproj_kernel,
        grid=(m // tm, n // tn),
        in_specs=[pl.BlockSpec((tm, k), lambda i, j: (i, 0)),
                  pl.BlockSpec((k, tn), lambda i, j: (0, j))],
        out_specs=[pl.BlockSpec((tm, tn), lambda i, j: (i, j)),
                   pl.BlockSpec((tm, 128), lambda i, j: (i, 0))],
        out_shape=[jax.ShapeDtypeStruct((m, n), BF16), jax.ShapeDtypeStruct((m, 128), F32)],
        compiler_params=_cparams(("parallel", "arbitrary")),
        name="inproj",
    )(a, w)


W_IN_PIECES = ((0, 0, 2048),
               (2048, 2064, 2560),
               (4608, 5456, 512),
               (5120, 5968, 1024),
               (6144, 4624, 512),
               (6656, 5136, 256),
               (6912, 5392, 64),
               (7040, 2048, 16),
               (7056, 6992, 16))
D_IN = 7008


def _win_kernel(w_ref, o_ref):
    o_ref[...] = jnp.zeros_like(o_ref)
    for dst, src, width in W_IN_PIECES:
        o_ref[:, dst:dst + width] = w_ref[:, src:src + width].astype(BF16)


def _win_layout(w_in):
    rows = 256
    return pl.pallas_call(
        _win_kernel,
        grid=(DEPTH, D // rows),
        in_specs=[pl.BlockSpec((None, rows, D_IN), lambda l, i: (l, i, 0))],
        out_specs=pl.BlockSpec((None, rows, U_W), lambda l, i: (l, i, 0)),
        out_shape=jax.ShapeDtypeStruct((DEPTH, D, U_W), BF16),
        compiler_params=_cparams(("parallel", "parallel")),
        name="win_layout",
    )(w_in)


def _adaln_mods(s_vec, w_ada, b_ada):
    tn = 1024
    return pl.pallas_call(
        _mm_kernel,
        grid=(DEPTH, 6 * D // tn),
        in_specs=[pl.BlockSpec((16, D), lambda l, j: (0, 0)),
                  pl.BlockSpec((None, D, tn), lambda l, j: (l, 0, j)),
                  pl.BlockSpec((None, 1, tn), lambda l, j: (l, 0, j))],
        out_specs=pl.BlockSpec((None, 16, tn), lambda l, j: (l, 0, j)),
        out_shape=jax.ShapeDtypeStruct((DEPTH, 16, 6 * D), F32),
        compiler_params=_cparams(("parallel", "arbitrary")),
        name="adaln_mods",
    )(s_vec, w_ada, b_ada.reshape(DEPTH, 1, 6 * D))


def _mod_spec(layer, k):
    def imap(b, i):
        row = jnp.where(i == NRT - 1, 8, b)
        return ((layer * 16 + row) * 6 + k, 0, 0)
    return pl.BlockSpec((None, 1, D), imap)


def _norm_spec(layer, k):
    return pl.BlockSpec((None, 1, D), lambda b, i: (layer * 4 + k, 0, 0))


def _row_spec(width, dtype_rows=RT):
    return pl.BlockSpec((None, dtype_rows, width), lambda b, i: (b, i, 0))


def _split_x_specs():
    return [pl.BlockSpec((None, RT, D), lambda b, i: (b, jnp.minimum(i, NRT - 2), 0)),
            pl.BlockSpec((None, RT, D), lambda b, i: (b, 0, 0))]


def _pick_x(x_ref, c_ref):
    return jnp.where(pl.program_id(1) == NRT - 1, c_ref[...], x_ref[...])


def _prenorm_kernel(x_ref, c_ref, g_ref, sh_ref, sc_ref, h_ref):
    y = _rms(_pick_x(x_ref, c_ref)) * g_ref[...]
    h_ref[...] = (y * (1.0 + sc_ref[...]) + sh_ref[...]).astype(BF16)


def _prenorm(x, ctx, normv, modv, layer):
    nb = x.shape[0]
    return pl.pallas_call(
        _prenorm_kernel,
        grid=(nb, NRT),
        in_specs=_split_x_specs() + [_norm_spec(layer, 0), _mod_spec(layer, 0), _mod_spec(layer, 1)],
        out_specs=_row_spec(D),
        out_shape=jax.ShapeDtypeStruct((nb, LT, D), BF16),
        compiler_params=_cparams(("parallel", "parallel")),
        name="prenorm",
    )(x, ctx, normv, modv, modv)


def _conv_kernel(x_ref, p_ref, n_ref, w_ref, b_ref, o_ref):
    i = pl.program_id(1)
    x = x_ref[...].astype(F32)
    has_prev = jnp.logical_and(i != 0, i != NRT - 1).astype(F32)
    has_next = jnp.logical_and(i != NRT - 2, i != NRT - 1).astype(F32)
    prev_row = p_ref[15:16, :].astype(F32) * has_prev
    next_row = n_ref[0:1, :].astype(F32) * has_next
    ridx = lax.broadcasted_iota(jnp.int32, (RT, 1), 0)
    xm1 = jnp.where(ridx == 0, prev_row, pltpu.roll(x, 1, 0))
    xp1 = jnp.where(ridx == RT - 1, next_row, pltpu.roll(x, RT - 1, 0))
    w = w_ref[...]
    y = w[0:1, :] * xm1 + w[1:2, :] * x + w[2:3, :] * xp1 + b_ref[...]
    o_ref[...] = _silu(y)


def _ssd_conv(u, conv_w, conv_b):
    nb = u.shape[0]
    hb = RT // 16
    last = LT // 16 - 1
    return pl.pallas_call(
        _conv_kernel,
        grid=(nb, NRT),
        in_specs=[pl.BlockSpec((None, RT, SSD_XBC), lambda b, i: (b, i, CB_SSD_XBC)),
                  pl.BlockSpec((None, 16, SSD_XBC), lambda b, i: (b, jnp.maximum(i * hb - 1, 0), CB_SSD_XBC)),
                  pl.BlockSpec((None, 16, SSD_XBC), lambda b, i: (b, jnp.minimum((i + 1) * hb, last), CB_SSD_XBC)),
                  pl.BlockSpec((8, SSD_XBC), lambda b, i: (0, 0)),
                  pl.BlockSpec((1, SSD_XBC), lambda b, i: (0, 0))],
        out_specs=pl.BlockSpec((None, RT, SSD_XBC), lambda b, i: (b, i, 0)),
        out_shape=jax.ShapeDtypeStruct((nb, LT, SSD_XBC), F32),
        compiler_params=_cparams(("parallel", "parallel")),
        name="ssd_conv",
    )(u, u, u, conv_w, conv_b)


SCAN_BB = 2


def _mlstm_kernel(rev, bpb, *refs):
    if rev:
        (qs_ref, ks_ref, vs_ref, gc_ref, gt_ref, bc_ref, bt_ref, yps_ref, ogs_ref,
         os_ref, c_sc, n_sc, m_sc) = refs
    else:
        (qs_ref, ks_ref, vs_ref, gc_ref, gt_ref, bc_ref, bt_ref,
         os_ref, c_sc, n_sc, m_sc) = refs
    d = 1 if rev else 0

    @pl.when(pl.program_id(1) == 0)
    def _():
        c_sc[...] = jnp.zeros_like(c_sc)
        n_sc[...] = jnp.zeros_like(n_sc)
        m_sc[...] = jnp.full(m_sc.shape, -jnp.inf, F32)

    mask = _chunk_mask(rev)
    mb = jnp.where(mask, 1.0, 0.0).astype(BF16)
    scale = DH ** -0.5
    units = [(bb, h) for bb in range(bpb) for h in range(NH)]
    gates = []
    for bb in range(bpb):
        gc = gc_ref[bb] + bc_ref[...]
        gt = gt_ref[bb] + bt_ref[...]
        flog_c = _log_sigmoid(gc)
        flog_t = _log_sigmoid(gt)
        gates.append((gc, gt, flog_c, _cumsum_rows(mb, flog_c), _cumsum_cols(flog_t, mb)))
    st = []
    for bb, h in units:
        sl = slice(h * DH, (h + 1) * DH)
        q, k, v = qs_ref[bb, :, sl], ks_ref[bb, :, sl], vs_ref[bb, :, sl]
        c_h = c_sc.at[bb * NH + h]
        st.append(dict(q=q, k=k, v=v, sl=sl, qk=_dot_nt(q, k), qc=_dot(q, c_h[...].astype(BF16))))
    for (bb, h), w in zip(units, st):
        gc, gt, flog_c, fcum_c, fcum_t = gates[bb]
        ci = d * 8 + h
        cf = d * 8 + 4 + h
        w["i_col"] = gc[:, ci:ci + 1]
        w["f_col"] = flog_c[:, cf:cf + 1]
        w["fc_col"] = fcum_c[:, cf:cf + 1]
        w["m_prev"] = m_sc[bb * NH + h][0:1, 0:1]
        dm = jnp.where(mask, w["fc_col"] - fcum_t[cf:cf + 1, :] + gt[ci:ci + 1, :], -jnp.inf)
        from_state = w["fc_col"] + w["m_prev"]
        w["m_t"] = jnp.maximum(from_state, jnp.max(dm, axis=1, keepdims=True))
        w["s"] = w["qk"] * scale * jnp.exp(dm - w["m_t"])
        w["w_state"] = jnp.exp(from_state - w["m_t"])
    for (bb, h), w in zip(units, st):
        n_h = n_sc.at[bb * NH + h]
        num = _dot(w["s"].astype(BF16), w["v"]) + w["w_state"] * (w["qc"] * scale)
        qn = jnp.sum(w["q"].astype(F32) * n_h[0:1, :], axis=1, keepdims=True) * scale
        den = jnp.sum(w["s"], axis=1, keepdims=True) + w["w_state"] * qn
        hout = num / jnp.maximum(jnp.abs(den), jnp.exp(-w["m_t"]))
        if rev:
            tot = yps_ref[bb, :, w["sl"]] + hout
            os_ref[bb, :, w["sl"]] = (tot * _sigmoid(ogs_ref[bb, :, w["sl"]].astype(F32))).astype(os_ref.dtype)
        else:
            os_ref[bb, :, w["sl"]] = hout
    for (bb, h), w in zip(units, st):
        c_h, n_h, m_h = c_sc.at[bb * NH + h], n_sc.at[bb * NH + h], m_sc.at[bb * NH + h]
        f_tot = jnp.sum(w["f_col"], axis=0, keepdims=True)
        w_log = f_tot - w["fc_col"] + w["i_col"]
        m_new = jnp.maximum(f_tot + w["m_prev"], jnp.max(w_log, axis=0, keepdims=True))
        decay = jnp.exp(f_tot + w["m_prev"] - m_new)
        kw = w["k"].astype(F32) * jnp.exp(w_log - m_new)
        c_h[...] = decay * c_h[...] + _dot(kw.T.astype(BF16), w["v"])
        n_new = decay * n_h[0:1, :] + jnp.sum(kw, axis=0, keepdims=True)
        n_h[...] = jnp.broadcast_to(n_new, (8, DH))
        m_h[...] = jnp.broadcast_to(m_new, (8, DH))


def _mlstm(u, usm, ust, bias_c, bias_t, rev, yprev=None):
    nb = u.shape[0]
    bpb = SCAN_BB if nb % SCAN_BB == 0 else 1
    cmap = _chunk_of_step(rev)
    uspec = lambda cblk: pl.BlockSpec((bpb, CS, BW), lambda b, s: (b, cmap(s), cblk))
    in_specs = [uspec(CB_ML_Q), uspec(CB_ML_K), uspec(CB_ML_V),
                pl.BlockSpec((bpb, CS, 128), lambda b, s: (b, cmap(s), 0)),
                pl.BlockSpec((bpb, None, 32, CS), lambda b, s: (b, cmap(s), 0, 0)),
                pl.BlockSpec((1, 128), lambda b, s: (0, 0)),
                pl.BlockSpec((32, 1), lambda b, s: (0, 0))]
    args = [u, u, u, usm, ust, bias_c, bias_t]
    if rev:
        in_specs += [uspec(0), uspec(CB_ML_O)]
        args += [yprev, u]
    return pl.pallas_call(
        functools.partial(_mlstm_kernel, rev, bpb),
        grid=(nb // bpb, NCH),
        in_specs=in_specs,
        out_specs=uspec(0),
        out_shape=jax.ShapeDtypeStruct((nb, LT, BW), BF16 if rev else F32),
        scratch_shapes=[pltpu.VMEM((bpb * NH, DH, DH), F32), pltpu.VMEM((bpb * NH, 8, DH), F32),
                        pltpu.VMEM((bpb * NH, 8, DH), F32)],
        compiler_params=_cparams(("parallel", "arbitrary")),
        name="mlstm_rev" if rev else "mlstm_fwd",
    )(*args)


HG_BAND = 2


def _block_ref(g, n, off):
    g3 = g.reshape(CS // n, n, DH)
    r = g3[:, off:off + 1, :]
    return jnp.broadcast_to(r, (CS // n, n, DH)).reshape(CS, DH)


def _hgrn2_kernel(rev, bpb, *refs):
    if rev:
        qs_ref, is_ref, fs_ref, lb_ref, yps_ref, gs_ref, ng_ref, os_ref, st_sc = refs
    else:
        qs_ref, is_ref, fs_ref, lb_ref, os_ref, st_sc = refs

    @pl.when(pl.program_id(1) == 0)
    def _():
        st_sc[...] = jnp.zeros_like(st_sc)

    mask = _chunk_mask(rev)
    mb = jnp.where(mask, 1.0, 0.0).astype(BF16)
    row = lax.broadcasted_iota(jnp.int32, (CS, CS), 0)
    col = lax.broadcasted_iota(jnp.int32, (CS, CS), 1)
    tix = lax.broadcasted_iota(jnp.int32, (CS, 1), 0)
    for bb, h in [(bb, h) for bb in range(bpb) for h in range(NH)]:
        q_ref, i_ref, f_ref, o_ref = qs_ref.at[bb], is_ref.at[bb], fs_ref.at[bb], os_ref.at[bb]
        if rev:
            yp_ref, g_ref = yps_ref.at[bb], gs_ref.at[bb]
        st_h = st_sc.at[bb * NH + h]
        sl = slice(h * DH, (h + 1) * DH)
        q = _silu(q_ref[:, sl].astype(F32))
        v = i_ref[:, sl].astype(F32)
        lb = lb_ref[:, sl]
        f = lb + (1.0 - lb) * _sigmoid(f_ref[:, sl].astype(F32))
        k = 1.0 - f
        lf = jnp.log(f)
        g = _cumsum_rows(mb, lf)
        st = st_h[...]
        o = _dot_nt((q * jnp.exp(g)).astype(BF16), st.astype(BF16))
        a = jnp.zeros((CS, CS), F32)
        n = CS
        while n >= 2 * HG_BAND:
            half = n // 2
            second = jnp.bitwise_and(tix, n - 1) >= half
            rowside = jnp.logical_not(second) if rev else second
            r = _block_ref(g, n, half if rev else half - 1)
            e = jnp.exp(jnp.where(rowside, g - r, r - g))
            qs = jnp.where(rowside, q * e, 0.0).astype(BF16)
            ks = jnp.where(rowside, 0.0, k * e).astype(BF16)
            sh_n = n.bit_length() - 1
            same = jnp.right_shift(row, sh_n) == jnp.right_shift(col, sh_n)
            a = a + jnp.where(same, _dot_nt(qs, ks), 0.0)
            n = half
        o = o + _dot(a.astype(BF16), v.astype(BF16))
        t8 = jnp.bitwise_and(tix, HG_BAND - 1)
        k3, g3, v3 = (a3.reshape(CS // 8, 8, DH) for a3 in (k, g, v))
        for dl in range(HG_BAND):
            if dl == 0:
                ksh, gsh, vsh = k, g, v
            else:
                sh = (8 - dl) if rev else dl
                ksh = pltpu.roll(k3, sh, 1).reshape(CS, DH)
                gsh = pltpu.roll(g3, sh, 1).reshape(CS, DH)
                vsh = pltpu.roll(v3, sh, 1).reshape(CS, DH)
            valid = (t8 + dl <= HG_BAND - 1) if rev else (t8 >= dl)
            e = jnp.exp(jnp.minimum(g - gsh, 0.0))
            coef = jnp.sum(q * ksh * e, axis=1, keepdims=True)
            o = o + jnp.where(valid, coef, 0.0) * vsh
        if rev:
            tot = yp_ref[:, sl] + o
            y = _rms(tot) * ng_ref[:, sl]
            o_ref[:, sl] = (y * _sigmoid(g_ref[:, sl].astype(F32))).astype(o_ref.dtype)
        else:
            o_ref[:, sl] = o
        g_tot = jnp.sum(lf, axis=0, keepdims=True)
        kdec = k * jnp.exp(g_tot - g)
        st_h[...] = st * jnp.exp(g_tot) + _dot(v.T.astype(BF16), kdec.astype(BF16))


def _hgrn2(u, lb, rev, yprev=None, norm_g=None):
    nb = u.shape[0]
    bpb = SCAN_BB if nb % SCAN_BB == 0 else 1
    cmap = _chunk_of_step(rev)
    d = 1 if rev else 0
    uspec = lambda cblk: pl.BlockSpec((bpb, CS, BW), lambda b, s: (b, cmap(s), cblk))
    in_specs = [uspec(CB_HG_Q), uspec(CB_HG_I), uspec(CB_HG_F + d),
                pl.BlockSpec((1, BW), lambda b, s: (0, 0))]
    args = [u, u, u, lb]
    if rev:
        in_specs += [uspec(0), uspec(CB_HG_G), pl.BlockSpec((1, BW), lambda b, s: (0, 0))]
        args += [yprev, u, norm_g]
    return pl.pallas_call(
        functools.partial(_hgrn2_kernel, rev, bpb),
        grid=(nb // bpb, NCH),
        in_specs=in_specs,
        out_specs=uspec(0),
        out_shape=jax.ShapeDtypeStruct((nb, LT, BW), BF16 if rev else F32),
        scratch_shapes=[pltpu.VMEM((bpb * NH, DH, DH), F32)],
        compiler_params=_cparams(("parallel", "arbitrary")),
        name="hgrn2_rev" if rev else "hgrn2_fwd",
    )(*args)


def _expand_heads(cols, lane_head):
    r = cols[3]
    for hh in (2, 1, 0):
        r = jnp.where(lane_head == hh, cols[hh], r)
    return r


def _ssd_kernel(rev, bpb, *refs):
    if rev:
        (xs_ref, gcs_ref, gts_ref, bc_ref, bt_ref, ac_ref, at_ref, yps_ref, zs_ref, dk_ref, ng_ref,
         os_ref, ht_sc) = refs
    else:
        xs_ref, gcs_ref, gts_ref, bc_ref, bt_ref, ac_ref, at_ref, os_ref, ht_sc = refs

    @pl.when(pl.program_id(1) == 0)
    def _():
        ht_sc[...] = jnp.zeros_like(ht_sc)

    for bb in range(bpb):
        extra = (yps_ref.at[bb], zs_ref.at[bb], dk_ref, ng_ref) if rev else ()
        _ssd_chunk(rev, xs_ref.at[bb], gcs_ref[bb], gts_ref[bb], bc_ref, bt_ref, ac_ref, at_ref,
                   os_ref.at[bb], ht_sc, bb * SSD_GROUPS, *extra)


def _ssd_chunk(rev, x_ref, gc, gt, bc_ref, bt_ref, ac_ref, at_ref, o_ref, ht_sc, ht0, yp_ref=None, z_ref=None,
               dk_ref=None, ng_ref=None):
    d = 1 if rev else 0
    mask = _chunk_mask(rev)
    mb = jnp.where(mask, 1.0, 0.0).astype(BF16)
    dt_c = _softplus(gc + bc_ref[...])
    da_c = dt_c * ac_ref[...]
    acum_c = _cumsum_rows(mb, da_c)
    dt_t = _softplus(gt + bt_ref[...])
    da_t = dt_t * at_ref[...]
    acum_t = _cumsum_cols(da_t, mb)
    gw = 4 * SSD_P
    lane_head = jnp.right_shift(lax.broadcasted_iota(jnp.int32, (1, gw), 1), 6)
    ys = []
    for grp in range(SSD_GROUPS):
        xg = x_ref[:, grp * gw:(grp + 1) * gw]
        bm = x_ref[:, BW + grp * SSD_N:BW + (grp + 1) * SSD_N]
        cm = x_ref[:, BW + SSD_GROUPS * SSD_N + grp * SSD_N:BW + SSD_GROUPS * SSD_N + (grp + 1) * SSD_N]
        cmb = cm.astype(BF16)
        cb = _dot_nt(cmb, bm.astype(BF16))
        ht = ht_sc[ht0 + grp]
        y_state = _dot(cmb, ht.astype(BF16))
        lanes = [16 + d * 8 + grp * 4 + hh for hh in range(4)]
        dt_cols = [dt_c[:, l:l + 1] for l in lanes]
        a_cols = [acum_c[:, l:l + 1] for l in lanes]
        a_tots = [jnp.sum(da_c[:, l:l + 1], axis=0, keepdims=True) for l in lanes]
        xdt = xg * _expand_heads(dt_cols, lane_head)
        xdtb = xdt.astype(BF16)
        y = jnp.exp(_expand_heads(a_cols, lane_head)) * y_state
        for hh in range(4):
            l = lanes[hh]
            dec = jnp.exp(jnp.where(mask, a_cols[hh] - acum_t[l:l + 1, :], -jnp.inf))
            yh = _dot((cb * dec).astype(BF16), xdtb)
            y = y + jnp.where(lane_head == hh, yh, 0.0)
        ys.append(y)
        w_exp = jnp.exp(_expand_heads([a_tots[hh] - a_cols[hh] for hh in range(4)], lane_head))
        e_row = jnp.exp(_expand_heads(a_tots, lane_head))
        ht_sc[ht0 + grp] = ht * e_row + _dot(bm.T.astype(BF16), (xdt * w_exp).astype(BF16))
    y = jnp.concatenate(ys, axis=1)
    if rev:
        tot = yp_ref[...] + y + dk_ref[...] * x_ref[:, :BW]
        tot = tot * _silu(z_ref[...].astype(F32))
        o_ref[...] = (_rms(tot) * ng_ref[...]).astype(o_ref.dtype)
    else:
        o_ref[...] = y


def _ssd(xbc, usm, ust, u, bias_c, bias_t, aneg_c, aneg_t, rev, yprev=None, dskip=None, norm_g=None):
    nb = xbc.shape[0]
    bpb = SCAN_BB if nb % SCAN_BB == 0 else 1
    cmap = _chunk_of_step(rev)
    in_specs = [pl.BlockSpec((bpb, CS, SSD_XBC), lambda b, s: (b, cmap(s), 0)),
                pl.BlockSpec((bpb, CS, 128), lambda b, s: (b, cmap(s), 0)),
                pl.BlockSpec((bpb, None, 32, CS), lambda b, s: (b, cmap(s), 0, 0)),
                pl.BlockSpec((1, 128), lambda b, s: (0, 0)),
                pl.BlockSpec((32, 1), lambda b, s: (0, 0)),
                pl.BlockSpec((1, 128), lambda b, s: (0, 0)),
                pl.BlockSpec((32, 1), lambda b, s: (0, 0))]
    args = [xbc, usm, ust, bias_c, bias_t, aneg_c, aneg_t]
    if rev:
        in_specs += [pl.BlockSpec((bpb, CS, BW), lambda b, s: (b, cmap(s), 0)),
                     pl.BlockSpec((bpb, CS, BW), lambda b, s: (b, cmap(s), CB_SSD_Z)),
                     pl.BlockSpec((1, BW), lambda b, s: (0, 0)),
                     pl.BlockSpec((1, BW), lambda b, s: (0, 0))]
        args += [yprev, u, dskip, norm_g]
    return pl.pallas_call(
        functools.partial(_ssd_kernel, rev, bpb),
        grid=(nb // bpb, NCH),
        in_specs=in_specs,
        out_specs=pl.BlockSpec((bpb, CS, BW), lambda b, s: (b, cmap(s), 0)),
        out_shape=jax.ShapeDtypeStruct((nb, LT, BW), BF16 if rev else F32),
        scratch_shapes=[pltpu.VMEM((bpb * SSD_GROUPS, SSD_N, 4 * SSD_P), F32)],
        compiler_params=_cparams(("parallel", "arbitrary")),
        name="ssd_rev" if rev else "ssd_fwd",
    )(*args)


def _mla_proj_kernel(cq_ref, ckv_ref, kr_ref, gq_ref, gkv_ref, wqa_ref, wqb_ref, wk_ref, wv_ref,
                     rk_ref, cos_ref, sin_ref, q_ref, k_ref, v_ref):
    nq = (_rms(cq_ref[...].astype(F32)) * gq_ref[...]).astype(BF16)
    qa = _dot(nq, wqa_ref[...])
    qb = _dot(nq, wqb_ref[...])
    cos_t = cos_ref[...]
    sin_t = sin_ref[...]
    cosf = jnp.concatenate([cos_t] * NH, axis=1)
    sinf = jnp.concatenate([sin_t] * NH, axis=1)
    q_ref[...] = ((qa * cosf + qb * sinf) * MLA_SCALE).astype(BF16)
    nkv = (_rms(ckv_ref[...].astype(F32)) * gkv_ref[...]).astype(BF16)
    kn = _dot(nkv, wk_ref[...])
    v_ref[...] = _dot(nkv, wv_ref[...]).astype(BF16)
    kr = kr_ref[...]
    krr = kr.astype(F32) * cos_t[:, 128:256] + _dot(kr, rk_ref[...]) * sin_t[:, 128:256]
    zero = jnp.zeros_like(krr)
    k_ref[...] = (kn + jnp.concatenate([zero, krr] * NH, axis=1)).astype(BF16)


def _mla_proj(u, gq, gkv, wqa, wqb, wk, wv, rk, cos_t, sin_t):
    nb = u.shape[0]
    const = lambda shape: pl.BlockSpec(shape, lambda b, i: (0, 0))
    return pl.pallas_call(
        _mla_proj_kernel,
        grid=(nb, LT // AT),
        in_specs=[pl.BlockSpec((None, AT, MLA_Q_RANK), lambda b, i: (b, i, CB_MLA_CQ)),
                  pl.BlockSpec((None, AT, MLA_KV_RANK), lambda b, i: (b, i, CB_MLA_CKV)),
                  pl.BlockSpec((None, AT, 128), lambda b, i: (b, i, CB_MLA_KR)),
                  const((1, MLA_Q_RANK)), const((1, MLA_KV_RANK)),
                  const((MLA_Q_RANK, NH * 256)), const((MLA_Q_RANK, NH * 256)),
                  const((MLA_KV_RANK, NH * 256)), const((MLA_KV_RANK, NH * DH)),
                  const((128, 128)),
                  pl.BlockSpec((AT, 256), lambda b, i: (i, 0)),
                  pl.BlockSpec((AT, 256), lambda b, i: (i, 0))],
        out_specs=[_row_spec(NH * 256, AT), _row_spec(NH * 256, AT), _row_spec(NH * DH, AT)],
        out_shape=[jax.ShapeDtypeStruct((nb, LT, NH * 256), BF16),
                   jax.ShapeDtypeStruct((nb, LT, NH * 256), BF16),
                   jax.ShapeDtypeStruct((nb, LT, NH * DH), BF16)],
        compiler_params=_cparams(("parallel", "parallel")),
        name="mla_proj",
    )(u, u, u, gq, gkv, wqa, wqb, wk, wv, rk, cos_t, sin_t)


AT = 1152


def _attn_kernel(q_ref, k_ref, v_ref, o_ref):
    s = _dot_nt(q_ref[...], k_ref[...])

    def finish(sc):
        m = jnp.max(sc, axis=1, keepdims=True)
        p = jnp.exp(sc - m)
        l = jnp.sum(p, axis=1, keepdims=True)
        o_ref[...] = (_dot(p.astype(BF16), v_ref[...]) / l).astype(o_ref.dtype)

    last = pl.program_id(2) == LT // AT - 1

    @pl.when(last)
    def _():
        qrow = lax.broadcasted_iota(jnp.int32, (AT, 1), 0)
        kcol = lax.broadcasted_iota(jnp.int32, (1, LT), 1)
        hide = jnp.logical_and(qrow >= AT - CTX, kcol < SEQ)
        finish(jnp.where(hide, -jnp.inf, s))

    @pl.when(jnp.logical_not(last))
    def _():
        finish(s)


def _attention(q, k, v):
    nb = q.shape[0]
    return pl.pallas_call(
        _attn_kernel,
        grid=(nb, NH, LT // AT),
        in_specs=[pl.BlockSpec((None, AT, 256), lambda b, h, i: (b, i, h)),
                  pl.BlockSpec((None, LT, 256), lambda b, h, i: (b, 0, h)),
                  pl.BlockSpec((None, LT, DH), lambda b, h, i: (b, 0, h))],
        out_specs=pl.BlockSpec((None, AT, DH), lambda b, h, i: (b, i, h)),
        out_shape=jax.ShapeDtypeStruct((nb, LT, NH * DH), BF16),
        compiler_params=_cparams(("parallel", "parallel", "arbitrary")),
        name="mla_attn",
    )(q, k, v)


def _merge_kernel(h_ref, b0_ref, b1_ref, b2_ref, b3_ref, wg_ref, wb_ref, o_ref):
    h = h_ref[...]
    acc = None
    for kk, br in enumerate((b0_ref, b1_ref, b2_ref, b3_ref)):
        gate = _sigmoid(_dot(h, wg_ref[kk]))
        term = gate * _dot(br[...], wb_ref[kk])
        acc = term if acc is None else acc + term
    o_ref[...] = acc.astype(o_ref.dtype)


def _merge(h, branches, wg, wb, rows, tn=256):
    nb = h.shape[0]
    tm = rows // 2
    rmap = lambda i, j: (i // 2, i % 2, 0)
    bspec = pl.BlockSpec((None, tm, BW), rmap)
    return pl.pallas_call(
        _merge_kernel,
        grid=(nb * 2, D // tn),
        in_specs=[pl.BlockSpec((None, tm, D), rmap), bspec, bspec, bspec, bspec,
                  pl.BlockSpec((4, D, tn), lambda i, j: (0, 0, j)),
                  pl.BlockSpec((4, BW, tn), lambda i, j: (0, 0, j))],
        out_specs=pl.BlockSpec((None, tm, tn), lambda i, j: (i // 2, i % 2, j)),
        out_shape=jax.ShapeDtypeStruct((nb, rows, D), BF16),
        compiler_params=_cparams(("parallel", "arbitrary")),
        name="merge",
    )(h, *branches, wg, wb)


def _outproj_kernel(first, a_ref, w_ref, *refs):
    if first:
        x_ref, c_ref = refs[:2]
        x_in = _pick_x(x_ref, c_ref)
        refs = refs[2:]
    else:
        x_in = refs[0][...]
        refs = refs[1:]
    g1_ref, gate_ref, g2_ref, sh_ref, sc_ref, wa_ref, wb_ref, xo_ref, hp_ref, e_ref, wt_ref = refs
    y = _dot(a_ref[...], w_ref[...])
    xn = x_in + gate_ref[...] * (_rms(y) * g1_ref[...])
    xo_ref[...] = xn
    h2 = _rms(xn) * g2_ref[...]
    h2 = h2 * (1.0 + sc_ref[...]) + sh_ref[...]
    hp_ref[...] = h2
    _route_math(h2, wa_ref, wb_ref, e_ref, wt_ref)


def _outproj(acc, w_out, xs, normv, modv, wr2, layer):
    nb, rows, _ = acc.shape
    first = len(xs) == 2
    wspec = pl.BlockSpec((D, 128), lambda b, i: (0, 0))
    return pl.pallas_call(
        functools.partial(_outproj_kernel, first),
        grid=(nb, rows // RT),
        in_specs=[_row_spec(D), pl.BlockSpec((D, D), lambda b, i: (0, 0))]
                 + (_split_x_specs() if first else [_row_spec(D)])
                 + [_norm_spec(layer, 1), _mod_spec(layer, 2), _norm_spec(layer, 2),
                    _mod_spec(layer, 3), _mod_spec(layer, 4), wspec, wspec],
        out_specs=[_row_spec(D), _row_spec(D), _row_spec(128), _row_spec(128)],
        out_shape=[jax.ShapeDtypeStruct((nb, rows, D), F32), jax.ShapeDtypeStruct((nb, rows, D), F32),
                   jax.ShapeDtypeStruct((nb, rows, 128), jnp.int32), jax.ShapeDtypeStruct((nb, rows, 128), F32)],
        compiler_params=_cparams(("parallel", "parallel")),
        name="outproj",
    )(acc, w_out, *xs, normv, modv, normv, modv, modv, *wr2)


def _route_math(hf, wa_ref, wb_ref, e_ref, w_ref):
    h = hf.astype(BF16)
    hl = (hf - h.astype(F32)).astype(BF16)
    logits = _dot(h, wa_ref[...]) + _dot(h, wb_ref[...]) + _dot(hl, wa_ref[...])
    lane = lax.broadcasted_iota(jnp.int32, (1, 128), 1).astype(F32)
    ninf = -jnp.inf
    is_grp = lane < N_GROUPS
    gl = jnp.where(is_grp, logits, ninf)
    gmax = jnp.max(gl, axis=1, keepdims=True)
    gidx = jnp.min(jnp.where(gl == gmax, lane, 999.0), axis=1, keepdims=True)
    gsum = jnp.sum(jnp.where(is_grp, jnp.exp(gl - gmax), 0.0), axis=1, keepdims=True)
    gp = 1.0 / gsum
    lo = N_GROUPS + EXP_PER_GROUP * gidx
    sel = jnp.logical_and(lane >= lo, lane < lo + EXP_PER_GROUP)
    el = jnp.where(sel, logits, ninf)
    v1 = jnp.max(el, axis=1, keepdims=True)
    i1 = jnp.min(jnp.where(el == v1, lane, 999.0), axis=1, keepdims=True)
    el2 = jnp.where(lane == i1, ninf, el)
    v2 = jnp.max(el2, axis=1, keepdims=True)
    i2 = jnp.min(jnp.where(el2 == v2, lane, 999.0), axis=1, keepdims=True)
    t = jnp.exp(v2 - v1)
    w1 = gp / (1.0 + t)
    w2 = gp * t / (1.0 + t)
    e_ref[...] = jnp.where(lane == 0.0, i1 - N_GROUPS, jnp.where(lane == 1.0, i2 - N_GROUPS, 0.0)).astype(jnp.int32)
    w_ref[...] = jnp.where(lane == 0.0, w1, jnp.where(lane == 1.0, w2, 0.0))


GATHER_UNROLL = 8


def _expert_kernel(te_ref, nu_ref, rt_ref, h_hbm, w1_ref, w3_ref, w2_ref, o_ref, xbuf, hm_sc, w1b, w3b, w2b, sem):
    i = pl.program_id(0)
    n_used = nu_ref[0]
    slot = i % 2

    def issue_tile(j, s):
        base = j * TE

        def body(g, c):
            for uu in range(GATHER_UNROLL):
                r = g * GATHER_UNROLL + uu
                pltpu.make_async_copy(h_hbm.at[pl.ds(rt_ref[base + r], 1), :],
                                      xbuf.at[s, pl.ds(r, 1), :], sem.at[s]).start(priority=1)
            return c
        lax.fori_loop(0, TE // GATHER_UNROLL, body, 0)

    def wait_tile(s):
        pltpu.make_async_copy(h_hbm.at[pl.ds(0, TE), :], xbuf.at[s], sem.at[s]).wait()

    @pl.when(jnp.logical_and(i == 0, n_used > 0))
    def _():
        issue_tile(0, 0)

    @pl.when(i < n_used)
    def _():
        wait_tile(slot)
        nbase = jnp.minimum(i + 1, n_used - 1) * TE
        oslot = 1 - slot

        def issue_rows(r0, r1):
            for r in range(r0, r1):
                pltpu.make_async_copy(h_hbm.at[pl.ds(rt_ref[nbase + r], 1), :],
                                      xbuf.at[oslot, pl.ds(r, 1), :], sem.at[oslot]).start(priority=1)

        @pl.when(jnp.logical_or(i == 0, te_ref[i] != te_ref[jnp.maximum(i - 1, 0)]))
        def _():
            w1b[...] = w1_ref[...].astype(BF16)
            w3b[...] = w3_ref[...].astype(BF16)
            w2b[...] = w2_ref[...].astype(BF16)

        x = xbuf[slot].astype(BF16)
        cw = 256
        n1, n2 = D_EXPERT // cw, D // cw
        per = TE // (n1 + n2) + 1
        done = 0
        for j in range(n1):
            issue_rows(done, min(done + per, TE))
            done = min(done + per, TE)
            cs = slice(j * cw, (j + 1) * cw)
            a = _dot(x, w1b[:, cs])
            b = _dot(x, w3b[:, cs])
            hm_sc[:, cs] = (_silu(a) * b).astype(BF16)
        hm = hm_sc[...]
        for j in range(n2):
            issue_rows(done, min(done + per, TE))
            done = min(done + per, TE)
            cs = slice(j * cw, (j + 1) * cw)
            o_ref[:, cs] = _dot(hm, w2b[:, cs])

        @pl.when(i + 1 >= n_used)
        def _():
            wait_tile(oslot)

    @pl.when(i >= n_used)
    def _():
        o_ref[...] = jnp.zeros_like(o_ref)


def _experts(tile_expert, n_used, row_token, hp, w1, w3, w2, n_tiles, layer):
    wmap = lambda i, te, nu, rt: (layer * N_EXPERTS + te[i], 0, 0)
    grid_spec = pltpu.PrefetchScalarGridSpec(
        num_scalar_prefetch=3,
        grid=(n_tiles,),
        in_specs=[pl.BlockSpec(memory_space=pl.ANY),
                  pl.BlockSpec((None, D, D_EXPERT), wmap),
                  pl.BlockSpec((None, D, D_EXPERT), wmap),
                  pl.BlockSpec((None, D_EXPERT, D), wmap)],
        out_specs=pl.BlockSpec((TE, D), lambda i, te, nu, rt: (i, 0)),
        scratch_shapes=[pltpu.VMEM((2, TE, D), F32), pltpu.VMEM((TE, D_EXPERT), BF16),
                        pltpu.VMEM((D, D_EXPERT), BF16), pltpu.VMEM((D, D_EXPERT), BF16),
                        pltpu.VMEM((D_EXPERT, D), BF16),
                        pltpu.SemaphoreType.DMA((2,))],
    )
    return pl.pallas_call(
        _expert_kernel,
        grid_spec=grid_spec,
        out_shape=jax.ShapeDtypeStruct((n_tiles * TE, D), F32),
        compiler_params=_cparams(("arbitrary",)),
        name="moe_experts",
    )(tile_expert, n_used, row_token, hp, w1, w3, w2)


def _combine_kernel(has_next, pos_ref, ys_hbm, w_ref, x_ref, g3_ref, gate_ref, *rest):
    if has_next:
        g0_ref, sh_ref, sc_ref, xo_ref, ho_ref, buf, sem = rest
    else:
        xo_ref, buf, sem = rest
    step = pl.program_id(0) * pl.num_programs(1) + pl.program_id(1)
    n_steps = pl.num_programs(0) * pl.num_programs(1)
    slot = step % 2

    def issue_tile(j, s):
        t0 = j * RT

        def body(g, c):
            for uu in range(GATHER_UNROLL):
                r = g * GATHER_UNROLL + uu
                for kk in range(2):
                    pltpu.make_async_copy(ys_hbm.at[pl.ds(pos_ref[(t0 + r) * 2 + kk], 1), :],
                                          buf.at[s, kk, pl.ds(r, 1), :], sem.at[s]).start(priority=kk)
            return c
        lax.fori_loop(0, RT // GATHER_UNROLL, body, 0)

    def wait_tile(s):
        for kk in range(2):
            pltpu.make_async_copy(ys_hbm.at[pl.ds(0, RT), :], buf.at[s, kk], sem.at[s]).wait()

    @pl.when(step == 0)
    def _():
        issue_tile(0, 0)

    wait_tile(slot)
    nt0 = jnp.minimum(step + 1, n_steps - 1) * RT
    oslot = 1 - slot
    rows = 32
    for c in range(RT // rows):
        for r in range(c * rows, (c + 1) * rows):
            for kk in range(2):
                pltpu.make_async_copy(ys_hbm.at[pl.ds(pos_ref[(nt0 + r) * 2 + kk], 1), :],
                                      buf.at[oslot, kk, pl.ds(r, 1), :], sem.at[oslot]).start(priority=kk)
        rs = slice(c * rows, (c + 1) * rows)
        w = w_ref[rs, :]
        y = w[:, 0:1] * buf[slot, 0, rs, :] + w[:, 1:2] * buf[slot, 1, rs, :]
        xn = x_ref[rs, :] + gate_ref[...] * (_rms(y) * g3_ref[...])
        xo_ref[rs, :] = xn
        if has_next:
            hn = _rms(xn) * g0_ref[...]
            ho_ref[rs, :] = (hn * (1.0 + sc_ref[...]) + sh_ref[...]).astype(BF16)

    @pl.when(step + 1 >= n_steps)
    def _():
        wait_tile(oslot)


def _combine(pos, ys, wts, xcat, normv, modv, layer):
    nb = xcat.shape[0]
    has_next = layer + 1 < DEPTH

    def mod_spec(lyr, k):
        def imap(b, i, pos_ref):
            row = jnp.where(i == NRT - 1, 8, b)
            return ((lyr * 16 + row) * 6 + k, 0, 0)
        return pl.BlockSpec((None, 1, D), imap)

    def norm_spec(lyr, k):
        return pl.BlockSpec((None, 1, D), lambda b, i, pos_ref: (lyr * 4 + k, 0, 0))

    def row_spec(width):
        return pl.BlockSpec((None, RT, width), lambda b, i, pos_ref: (b, i, 0))

    in_specs = [pl.BlockSpec(memory_space=pl.ANY), row_spec(128), row_spec(D),
                norm_spec(layer, 3), mod_spec(layer, 5)]
    args = [ys, wts, xcat, normv, modv]
    out_specs = [row_spec(D)]
    n_row_tiles = NRT if has_next else SEQ // RT
    out_shape = [jax.ShapeDtypeStruct((nb, n_row_tiles * RT, D), F32)]
    if has_next:
        in_specs += [norm_spec(layer + 1, 0), mod_spec(layer + 1, 0), mod_spec(layer + 1, 1)]
        args += [normv, modv, modv]
        out_specs.append(row_spec(D))
        out_shape.append(jax.ShapeDtypeStruct((nb, LT, D), BF16))
    grid_spec = pltpu.PrefetchScalarGridSpec(
        num_scalar_prefetch=1,
        grid=(nb, n_row_tiles),
        in_specs=in_specs,
        out_specs=out_specs,
        scratch_shapes=[pltpu.VMEM((2, 2, RT, D), F32), pltpu.SemaphoreType.DMA((2,))],
    )
    return pl.pallas_call(
        functools.partial(_combine_kernel, has_next),
        grid_spec=grid_spec,
        out_shape=out_shape,
        compiler_params=_cparams(("arbitrary", "arbitrary")),
        name="moe_combine",
    )(pos, *args)


def _plan_kernel(e_ref, pos_ref, cnt_ref, rank_sc, carry_sc):
    phase = pl.program_id(0)
    i = pl.program_id(1)
    lane = lax.broadcasted_iota(jnp.int32, (1, 128), 1)
    e = e_ref[...]
    oh1 = lane == e[:, 0:1]
    oh2 = lane == e[:, 1:2]

    @pl.when(jnp.logical_and(phase == 0, i == 0))
    def _():
        carry_sc[...] = jnp.zeros_like(carry_sc)

    @pl.when(phase == 0)
    def _():
        s = jnp.where(jnp.logical_or(oh1, oh2), 1.0, 0.0)
        pt = e_ref.shape[0]
        row = lax.broadcasted_iota(jnp.int32, (pt, pt), 0)
        col = lax.broadcasted_iota(jnp.int32, (pt, pt), 1)
        lower = jnp.where(col < row, 1.0, 0.0).astype(BF16)
        carry = carry_sc[0:1, :]
        before = _dot(lower, s.astype(BF16)) + carry
        r1 = jnp.sum(jnp.where(oh1, before, 0.0), axis=1, keepdims=True)
        r2 = jnp.sum(jnp.where(oh2, before, 0.0), axis=1, keepdims=True)
        rank_sc[i] = jnp.where(lane == 0, r1, jnp.where(lane == 1, r2, 0.0))
        carry_sc[...] = jnp.broadcast_to(carry + jnp.sum(s, axis=0, keepdims=True), carry_sc.shape)
        cnt_ref[...] = carry_sc[...]

    @pl.when(phase == 1)
    def _():
        counts = carry_sc[0:1, :]
        padded = jnp.floor((counts + (TE - 1)) * (1.0 / TE)) * TE
        r128 = lax.broadcasted_iota(jnp.int32, (128, 128), 0)
        c128 = lax.broadcasted_iota(jnp.int32, (128, 128), 1)
        upper = jnp.where(r128 < c128, 1.0, 0.0).astype(BF16)
        pa, pb, pc = _split3(jnp.broadcast_to(padded, (8, 128)))
        offs = (_dot(pa, upper) + _dot(pb, upper) + _dot(pc, upper))[0:1, :]
        rank = rank_sc[i]
        p1 = rank[:, 0:1] + jnp.sum(jnp.where(oh1, offs, 0.0), axis=1, keepdims=True)
        p2 = rank[:, 1:2] + jnp.sum(jnp.where(oh2, offs, 0.0), axis=1, keepdims=True)
        pos_ref[...] = jnp.where(lane == 0, p1, jnp.where(lane == 1, p2, 0.0)).astype(jnp.int32)


def _plan(eid):
    nb, rows, _ = eid.shape
    pt = 768 if rows % 768 == 0 else 1024
    tpb = rows // pt
    nt = nb * tpb
    emap = lambda p, i: (i // tpb, i % tpb, 0)
    return pl.pallas_call(
        _plan_kernel,
        grid=(2, nt),
        in_specs=[pl.BlockSpec((None, pt, 128), emap)],
        out_specs=[pl.BlockSpec((pt, 128), lambda p, i: (p * i, 0)),
                   pl.BlockSpec((8, 128), lambda p, i: (0, 0))],
        out_shape=[jax.ShapeDtypeStruct((nt * pt, 128), jnp.int32), jax.ShapeDtypeStruct((8, 128), F32)],
        scratch_shapes=[pltpu.VMEM((nt, pt, 128), F32), pltpu.VMEM((8, 128), F32)],
        compiler_params=_cparams(("arbitrary", "arbitrary")),
        name="moe_plan",
    )(eid)


def _invert_kernel(pos_ref, rt_ref):
    unroll = 32

    def zero(i, c):
        for uu in range(unroll):
            rt_ref[i * unroll + uu] = 0
        return c
    lax.fori_loop(0, rt_ref.shape[0] // unroll, zero, 0)

    def scatter(i, c):
        for uu in range(unroll):
            a = i * unroll + uu
            rt_ref[pos_ref[a]] = jnp.right_shift(a, 1)
        return c
    lax.fori_loop(0, pos_ref.shape[0] // unroll, scatter, 0)


def _invert(pos_flat, n_rows):
    grid_spec = pltpu.PrefetchScalarGridSpec(
        num_scalar_prefetch=1, grid=(1,), in_specs=[],
        out_specs=pl.BlockSpec(memory_space=pltpu.SMEM))
    return pl.pallas_call(
        _invert_kernel,
        grid_spec=grid_spec,
        out_shape=jax.ShapeDtypeStruct((n_rows,), jnp.int32),
        compiler_params=_cparams(("arbitrary",)),
        name="moe_invert",
    )(pos_flat)


def _dispatch_tiles(pos, counts, n_tiles, n_row_tiles, nb):
    cnt = counts[0, :N_EXPERTS].astype(jnp.int32)
    padded = ((cnt + TE - 1) // TE) * TE
    ends = jnp.cumsum(padded)
    tile_start = jnp.arange(n_tiles, dtype=jnp.int32) * TE
    tile_expert = jnp.sum((ends[None, :] <= tile_start[:, None]).astype(jnp.int32), axis=1)
    tile_expert = jnp.minimum(tile_expert, N_EXPERTS - 1)
    n_used = (ends[-1] // TE).astype(jnp.int32).reshape(1)
    pos_flat = pos[:, :2].reshape(-1)
    row_token = _invert(pos_flat, n_tiles * TE)
    return pos_flat, row_token, tile_expert, n_used


def _rope_tables():
    tok = jnp.arange(SEQ)
    pos_row = (tok // GRID_W).astype(F32)
    pos_col = (tok % GRID_W).astype(F32)
    quarter = MLA_ROPE // 4
    inv_freq = ROPE_BASE ** (-jnp.arange(quarter, dtype=F32) / quarter)
    ang_r = pos_row[:, None] * inv_freq
    ang_c = pos_col[:, None] * inv_freq
    cos = jnp.concatenate([jnp.cos(ang_r)] * 2 + [jnp.cos(ang_c)] * 2, axis=1)
    sin = jnp.concatenate([jnp.sin(ang_r)] * 2 + [jnp.sin(ang_c)] * 2, axis=1)
    cos = jnp.concatenate([cos, jnp.ones((CTX, MLA_ROPE), F32)], axis=0)
    sin = jnp.concatenate([sin, jnp.zeros((CTX, MLA_ROPE), F32)], axis=0)
    pad = jnp.zeros((LT, 64), F32)
    cos_t = jnp.concatenate([jnp.ones((LT, 128), F32), cos, pad], axis=1)
    sin_t = jnp.concatenate([jnp.zeros((LT, 128), F32), sin, pad], axis=1)
    return cos_t, sin_t


def _rope_rot_matrix():
    j = jnp.arange(MLA_ROPE)
    first = (j % 32) < 16
    src = jnp.where(first, j + 16, j - 16)
    sign = jnp.where(first, -1.0, 1.0)
    return jnp.zeros((MLA_ROPE, MLA_ROPE), F32).at[src, j].set(sign)


def _layer_params(l, w_main, ml_f_bias, lb_all, hg_norm_g, mla_g_q, mla_g_kv, mla_w_uq, mla_w_ukv,
                  ssd_conv_w, ssd_conv_b, ssd_a_log, ssd_dt_bias, ssd_d, ssd_norm_g, w_gate, w_br, w_out,
                  moe_w_grp, moe_w_exp):
    p = {}
    p["w_main"] = w_main[l]
    fb = jnp.zeros((2, 2, NH), F32).at[:, 1, :].set(ml_f_bias[l]).reshape(16)
    ml_bias = jnp.concatenate([fb, jnp.zeros((112,), F32)])
    p["ml_bias_c"] = ml_bias.reshape(1, 128)
    p["ml_bias_t"] = ml_bias[:32].reshape(32, 1)
    dtb = jnp.concatenate([jnp.zeros((16,), F32), ssd_dt_bias[l].reshape(16), jnp.zeros((96,), F32)])
    p["ssd_bias_c"] = dtb.reshape(1, 128)
    p["ssd_bias_t"] = dtb[:32].reshape(32, 1)
    aneg = jnp.concatenate([jnp.zeros((16,), F32), -jnp.exp(ssd_a_log[l].astype(F32)).reshape(16),
                            jnp.zeros((96,), F32)])
    p["ssd_aneg_c"] = aneg.reshape(1, 128)
    p["ssd_aneg_t"] = aneg[:32].reshape(32, 1)
    p["lb"] = lb_all[l].reshape(1, BW)
    p["hg_norm_g"] = hg_norm_g[l].reshape(1, BW)
    p["ssd_dskip"] = jnp.repeat(ssd_d[l], SSD_P).reshape(1, BW)
    p["ssd_norm_g"] = ssd_norm_g[l].reshape(1, BW)
    p["conv_w"] = jnp.concatenate([ssd_conv_w[l], jnp.zeros((5, SSD_XBC), F32)], axis=0)
    p["conv_b"] = ssd_conv_b[l].reshape(1, SSD_XBC)
    rot = _rope_rot_matrix()
    wq = mla_w_uq[l].reshape(MLA_Q_RANK, NH, MLA_NOPE + MLA_ROPE)
    zq = jnp.zeros((MLA_Q_RANK, NH, 64), F32)
    wqa = jnp.concatenate([wq, zq], axis=2).reshape(MLA_Q_RANK, NH * 256)
    wq_rot = jnp.einsum("rhj,jk->rhk", wq[:, :, MLA_NOPE:], rot)
    wqb = jnp.concatenate([jnp.zeros((MLA_Q_RANK, NH, MLA_NOPE), F32), wq_rot, zq], axis=2)
    p["wqa"] = wqa.astype(BF16)
    p["wqb"] = wqb.reshape(MLA_Q_RANK, NH * 256).astype(BF16)
    wkv = mla_w_ukv[l].reshape(MLA_KV_RANK, NH, 2 * DH)
    wk = jnp.concatenate([wkv[:, :, :DH], jnp.zeros((MLA_KV_RANK, NH, DH), F32)], axis=2)
    p["wk"] = wk.reshape(MLA_KV_RANK, NH * 256).astype(BF16)
    p["wv"] = wkv[:, :, DH:].reshape(MLA_KV_RANK, NH * DH).astype(BF16)
    p["rk"] = jnp.zeros((128, 128), F32).at[:MLA_ROPE, :MLA_ROPE].set(rot).astype(BF16)
    p["gq"] = mla_g_q[l].reshape(1, MLA_Q_RANK)
    p["gkv"] = mla_g_kv[l].reshape(1, MLA_KV_RANK)
    p["w_gate"] = w_gate[l].astype(BF16)
    p["w_br"] = w_br[l].astype(BF16)
    p["w_out"] = w_out[l].astype(BF16)
    wr = jnp.concatenate([moe_w_grp[l], moe_w_exp[l], jnp.zeros((D, 128 - N_GROUPS - N_EXPERTS), F32)], axis=1)
    a = wr.astype(BF16)
    b = (wr - a.astype(F32)).astype(BF16)
    p["wr2"] = (a, b)
    return p


def _mixer(h, p, cos_t, sin_t, nb, out_rows):
    t = nb * LT
    hf = h.reshape(t, D)
    u, usm = _inproj(hf, p["w_main"])
    u = u.reshape(nb, LT, U_W)
    usm = usm.reshape(nb, LT, 128)
    ust = usm[:, :, :32].reshape(nb, NCH, CS, 32).transpose(0, 1, 3, 2)
    ml_f = _mlstm(u, usm, ust, p["ml_bias_c"], p["ml_bias_t"], False)
    ml = _mlstm(u, usm, ust, p["ml_bias_c"], p["ml_bias_t"], True, ml_f)
    hg_f = _hgrn2(u, p["lb"], False)
    hg = _hgrn2(u, p["lb"], True, hg_f, p["hg_norm_g"])
    q, k, v = _mla_proj(u, p["gq"], p["gkv"], p["wqa"], p["wqb"], p["wk"], p["wv"], p["rk"], cos_t, sin_t)
    mla = _attention(q, k, v)
    xbc = _ssd_conv(u, p["conv_w"], p["conv_b"])
    sargs = (xbc, usm, ust, u, p["ssd_bias_c"], p["ssd_bias_t"], p["ssd_aneg_c"], p["ssd_aneg_t"])
    ssd_f = _ssd(*sargs, False)
    ssd = _ssd(*sargs, True, ssd_f, p["ssd_dskip"], p["ssd_norm_g"])
    return _merge(h, (ml, hg, mla, ssd), p["w_gate"], p["w_br"], out_rows)


def kernel(x, c, ctx, c_ctx, w_ada, b_ada, norm_g, w_in, ml_f_bias, hg_lb_logits, hg_norm_g, mla_g_q, mla_g_kv,
           mla_w_uq, mla_w_ukv, ssd_conv_w, ssd_conv_b, ssd_a_log, ssd_dt_bias, ssd_d, ssd_norm_g, w_gate,
           w_br, w_out, moe_w_grp, moe_w_exp, moe_w1, moe_w3, moe_w2):
    nb = x.shape[0]
    t = nb * LT
    lb_all = jnp.cumsum(jax.nn.softmax(hg_lb_logits.astype(F32), axis=0), axis=0)
    cvec = jnp.zeros((16, D), F32).at[:nb].set(c).at[8].set(c_ctx)
    s_vec = jax.nn.silu(cvec).astype(BF16)
    modv = _adaln_mods(s_vec, w_ada, b_ada).reshape(DEPTH * 16 * 6, 1, D)
    normv = norm_g.reshape(DEPTH * 4, 1, D)
    cos_t, sin_t = _rope_tables()
    w_main = _win_layout(w_in)
    w1 = moe_w1.reshape(DEPTH * N_EXPERTS, D, D_EXPERT)
    w3 = moe_w3.reshape(DEPTH * N_EXPERTS, D, D_EXPERT)
    w2 = moe_w2.reshape(DEPTH * N_EXPERTS, D_EXPERT, D)
    h = _prenorm(x, ctx, normv, modv, 0)
    xs = (x, ctx)
    for l in range(DEPTH):
        p = _layer_params(l, w_main, ml_f_bias, lb_all, hg_norm_g, mla_g_q, mla_g_kv, mla_w_uq,
                          mla_w_ukv, ssd_conv_w, ssd_conv_b, ssd_a_log, ssd_dt_bias, ssd_d, ssd_norm_g, w_gate,
                          w_br, w_out, moe_w_grp, moe_w_exp)
        rows = LT if l + 1 < DEPTH else SEQ
        n_row_tiles = rows // RT
        acc = _mixer(h, p, cos_t, sin_t, nb, rows)
        xcat, hp, eid, wts = _outproj(acc, p["w_out"], xs, normv, modv, p["wr2"], l)
        n_tiles = (2 * nb * rows + N_EXPERTS * (TE - 1)) // TE + 1
        pos128, counts = _plan(eid)
        pos, row_token, tile_expert, n_used = _dispatch_tiles(pos128, counts, n_tiles, n_row_tiles, nb)
        ys = _experts(tile_expert, n_used, row_token, hp.reshape(nb * rows, D), w1, w3, w2, n_tiles, l)
        outs = _combine(pos, ys, wts, xcat, normv, modv, l)
        if l + 1 < DEPTH:
            xcat, h = outs
            xs = (xcat,)
    return outs[0]
```

```python
import functools

import jax
import jax.numpy as jnp
from jax import lax
from jax.experimental import pallas as pl
from jax.experimental.pallas import tpu as pltpu

F32 = jnp.float32
BF16 = jnp.bfloat16

D = 2048
SEQ = 2048
CTX = 256
LT = SEQ + CTX
DEPTH = 2
EPS = 1e-6
GRID_W = 64
NH = 4
DH = 128
BW = 512
MLA_Q_RANK = 512
MLA_KV_RANK = 256
MLA_NOPE = 128
MLA_ROPE = 64
MLA_SCALE = (MLA_NOPE + MLA_ROPE) ** -0.5
ROPE_BASE = 10000.0
SSD_HEADS = 8
SSD_P = 64
SSD_GROUPS = 2
SSD_N = 128
SSD_XBC = 1024
N_GROUPS = 4
EXP_PER_GROUP = 8
N_EXPERTS = 32
D_EXPERT = 512

CS = 128
NLC = SEQ // CS
NCC = CTX // CS
NCH = NLC + NCC
RT = 256
NRT = LT // RT
TE = 256
VMEM_LIMIT = 52 * 1024 * 1024

U_W = 7168
CB_ML_Q, CB_ML_K, CB_ML_V, CB_ML_O = 0, 1, 2, 3
CB_HG_Q, CB_HG_I, CB_HG_G, CB_HG_F = 4, 5, 6, 7
CB_SSD_Z = 9
CB_SSD_XBC = 5
CB_MLA_CQ = 12
CB_MLA_CKV = 26
CB_MLA_KR = 54


def _cparams(sem):
    return pltpu.CompilerParams(dimension_semantics=sem, vmem_limit_bytes=VMEM_LIMIT)


def _dot(a, b):
    return jnp.dot(a, b, preferred_element_type=F32)


def _dot_nt(a, b):
    return lax.dot_general(a, b, (((1,), (1,)), ((), ())), preferred_element_type=F32)


def _sigmoid(x):
    return 1.0 / (1.0 + jnp.exp(-x))


def _silu(x):
    return x * _sigmoid(x)


def _softplus(x):
    return jnp.maximum(x, 0.0) + jnp.log(1.0 + jnp.exp(-jnp.abs(x)))


def _log_sigmoid(x):
    return -_softplus(-x)


def _rms(x):
    return x * lax.rsqrt(jnp.mean(x * x, axis=-1, keepdims=True) + EPS)


def _split3(x):
    a = x.astype(BF16)
    r = x - a.astype(F32)
    b = r.astype(BF16)
    c = (r - b.astype(F32)).astype(BF16)
    return a, b, c


def _cumsum_rows(mb, x):
    a, b, c = _split3(x)
    return _dot(mb, a) + _dot(mb, b) + _dot(mb, c)


def _cumsum_cols(x, mb):
    a, b, c = _split3(x)
    return _dot_nt(a, mb) + _dot_nt(b, mb) + _dot_nt(c, mb)


def _chunk_mask(rev):
    row = lax.broadcasted_iota(jnp.int32, (CS, CS), 0)
    col = lax.broadcasted_iota(jnp.int32, (CS, CS), 1)
    return (col >= row) if rev else (col <= row)


def _chunk_of_step(rev):
    if rev:
        return lambda s: NCH - 1 - s
    return lambda s: (s + NLC) % NCH


def _mm_kernel(a_ref, w_ref, b_ref, o_ref):
    acc = _dot(a_ref[...], w_ref[...].astype(BF16))
    o_ref[...] = (acc + b_ref[...]).astype(o_ref.dtype)


def _inproj_kernel(a_ref, w_ref, o_ref, s_ref):
    acc = _dot(a_ref[...], w_ref[...])
    o_ref[...] = acc.astype(o_ref.dtype)

    @pl.when(pl.program_id(1) == pl.num_programs(1) - 1)
    def _():
        s_ref[...] = acc[:, acc.shape[1] - 128:]


def _inproj(a, w, tm=LT, tn=1024):
    m, k = a.shape
    n = w.shape[1]
    return pl.pallas_call(
        _inproj_kernel,
        grid=(m // tm, n // tn),
        in_specs=[pl.BlockSpec((tm, k), lambda i, j: (i, 0)),
                  pl.BlockSpec((k, tn), lambda i, j: (0, j))],
        out_specs=[pl.BlockSpec((tm, tn), lambda i, j: (i, j)),
                   pl.BlockSpec((tm, 128), lambda i, j: (i, 0))],
        out_shape=[jax.ShapeDtypeStruct((m, n), BF16), jax.ShapeDtypeStruct((m, 128), F32)],
        compiler_params=_cparams(("parallel", "arbitrary")),
        name="inproj",
    )(a, w)


W_IN_PIECES = ((0, 0, 2048),
               (2048, 2064, 2560),
               (4608, 5456, 512),
               (5120, 5968, 1024),
               (6144, 4624, 512),
               (6656, 5136, 256),
               (6912, 5392, 64),
               (7040, 2048, 16),
               (7056, 6992, 16))
D_IN = 7008


def _win_kernel(w_ref, o_ref):
    o_ref[...] = jnp.zeros_like(o_ref)
    for dst, src, width in W_IN_PIECES:
        o_ref[:, dst:dst + width] = w_ref[:, src:src + width].astype(BF16)


def _win_layout(w_in):
    rows = 256
    return pl.pallas_call(
        _win_kernel,
        grid=(DEPTH, D // rows),
        in_specs=[pl.BlockSpec((None, rows, D_IN), lambda l, i: (l, i, 0))],
        out_specs=pl.BlockSpec((None, rows, U_W), lambda l, i: (l, i, 0)),
        out_shape=jax.ShapeDtypeStruct((DEPTH, D, U_W), BF16),
        compiler_params=_cparams(("parallel", "parallel")),
        name="win_layout",
    )(w_in)


def _adaln_mods(s_vec, w_ada, b_ada):
    tn = 1024
    return pl.pallas_call(
        _mm_kernel,
        grid=(DEPTH, 6 * D // tn),
        in_specs=[pl.BlockSpec((16, D), lambda l, j: (0, 0)),
                  pl.BlockSpec((None, D, tn), lambda l, j: (l, 0, j)),
                  pl.BlockSpec((None, 1, tn), lambda l, j: (l, 0, j))],
        out_specs=pl.BlockSpec((None, 16, tn), lambda l, j: (l, 0, j)),
        out_shape=jax.ShapeDtypeStruct((DEPTH, 16, 6 * D), F32),
        compiler_params=_cparams(("parallel", "arbitrary")),
        name="adaln_mods",
    )(s_vec, w_ada, b_ada.reshape(DEPTH, 1, 6 * D))


def _mod_spec(layer, k):
    def imap(b, i):
        row = jnp.where(i == NRT - 1, 8, b)
        return ((layer * 16 + row) * 6 + k, 0, 0)
    return pl.BlockSpec((None, 1, D), imap)


def _norm_spec(layer, k):
    return pl.BlockSpec((None, 1, D), lambda b, i: (layer * 4 + k, 0, 0))


def _row_spec(width, dtype_rows=RT):
    return pl.BlockSpec((None, dtype_rows, width), lambda b, i: (b, i, 0))


def _split_x_specs():
    return [pl.BlockSpec((None, RT, D), lambda b, i: (b, jnp.minimum(i, NRT - 2), 0)),
            pl.BlockSpec((None, RT, D), lambda b, i: (b, 0, 0))]


def _pick_x(x_ref, c_ref):
    return jnp.where(pl.program_id(1) == NRT - 1, c_ref[...], x_ref[...])


def _prenorm_kernel(x_ref, c_ref, g_ref, sh_ref, sc_ref, h_ref):
    y = _rms(_pick_x(x_ref, c_ref)) * g_ref[...]
    h_ref[...] = (y * (1.0 + sc_ref[...]) + sh_ref[...]).astype(BF16)


def _prenorm(x, ctx, normv, modv, layer):
    nb = x.shape[0]
    return pl.pallas_call(
        _prenorm_kernel,
        grid=(nb, NRT),
        in_specs=_split_x_specs() + [_norm_spec(layer, 0), _mod_spec(layer, 0), _mod_spec(layer, 1)],
        out_specs=_row_spec(D),
        out_shape=jax.ShapeDtypeStruct((nb, LT, D), BF16),
        compiler_params=_cparams(("parallel", "parallel")),
        name="prenorm",
    )(x, ctx, normv, modv, modv)


def _conv_kernel(x_ref, p_ref, n_ref, w_ref, b_ref, o_ref):
    i = pl.program_id(1)
    x = x_ref[...].astype(F32)
    has_prev = jnp.logical_and(i != 0, i != NRT - 1).astype(F32)
    has_next = jnp.logical_and(i != NRT - 2, i != NRT - 1).astype(F32)
    prev_row = p_ref[15:16, :].astype(F32) * has_prev
    next_row = n_ref[0:1, :].astype(F32) * has_next
    ridx = lax.broadcasted_iota(jnp.int32, (RT, 1), 0)
    xm1 = jnp.where(ridx == 0, prev_row, pltpu.roll(x, 1, 0))
    xp1 = jnp.where(ridx == RT - 1, next_row, pltpu.roll(x, RT - 1, 0))
    w = w_ref[...]
    y = w[0:1, :] * xm1 + w[1:2, :] * x + w[2:3, :] * xp1 + b_ref[...]
    o_ref[...] = _silu(y)


def _ssd_conv(u, conv_w, conv_b):
    nb = u.shape[0]
    hb = RT // 16
    last = LT // 16 - 1
    return pl.pallas_call(
        _conv_kernel,
        grid=(nb, NRT),
        in_specs=[pl.BlockSpec((None, RT, SSD_XBC), lambda b, i: (b, i, CB_SSD_XBC)),
                  pl.BlockSpec((None, 16, SSD_XBC), lambda b, i: (b, jnp.maximum(i * hb - 1, 0), CB_SSD_XBC)),
                  pl.BlockSpec((None, 16, SSD_XBC), lambda b, i: (b, jnp.minimum((i + 1) * hb, last), CB_SSD_XBC)),
                  pl.BlockSpec((8, SSD_XBC), lambda b, i: (0, 0)),
                  pl.BlockSpec((1, SSD_XBC), lambda b, i: (0, 0))],
        out_specs=pl.BlockSpec((None, RT, SSD_XBC), lambda b, i: (b, i, 0)),
        out_shape=jax.ShapeDtypeStruct((nb, LT, SSD_XBC), F32),
        compiler_params=_cparams(("parallel", "parallel")),
        name="ssd_conv",
    )(u, u, u, conv_w, conv_b)


SCAN_BB = 2


def _mlstm_kernel(rev, bpb, *refs):
    if rev:
        (qs_ref, ks_ref, vs_ref, gc_ref, gt_ref, bc_ref, bt_ref, yps_ref, ogs_ref,
         os_ref, c_sc, n_sc, m_sc) = refs
    else:
        (qs_ref, ks_ref, vs_ref, gc_ref, gt_ref, bc_ref, bt_ref,
         os_ref, c_sc, n_sc, m_sc) = refs
    d = 1 if rev else 0

    @pl.when(pl.program_id(1) == 0)
    def _():
        c_sc[...] = jnp.zeros_like(c_sc)
        n_sc[...] = jnp.zeros_like(n_sc)
        m_sc[...] = jnp.full(m_sc.shape, -jnp.inf, F32)

    mask = _chunk_mask(rev)
    mb = jnp.where(mask, 1.0, 0.0).astype(BF16)
    scale = DH ** -0.5
    units = [(bb, h) for bb in range(bpb) for h in range(NH)]
    gates = []
    for bb in range(bpb):
        gc = gc_ref[bb] + bc_ref[...]
        gt = gt_ref[bb] + bt_ref[...]
        flog_c = _log_sigmoid(gc)
        flog_t = _log_sigmoid(gt)
        gates.append((gc, gt, flog_c, _cumsum_rows(mb, flog_c), _cumsum_cols(flog_t, mb)))
    st = []
    for bb, h in units:
        sl = slice(h * DH, (h + 1) * DH)
        q, k, v = qs_ref[bb, :, sl], ks_ref[bb, :, sl], vs_ref[bb, :, sl]
        c_h = c_sc.at[bb * NH + h]
        st.append(dict(q=q, k=k, v=v, sl=sl, qk=_dot_nt(q, k), qc=_dot(q, c_h[...].astype(BF16))))
    for (bb, h), w in zip(units, st):
        gc, gt, flog_c, fcum_c, fcum_t = gates[bb]
        ci = d * 8 + h
        cf = d * 8 + 4 + h
        w["i_col"] = gc[:, ci:ci + 1]
        w["f_col"] = flog_c[:, cf:cf + 1]
        w["fc_col"] = fcum_c[:, cf:cf + 1]
        w["m_prev"] = m_sc[bb * NH + h][0:1, 0:1]
        dm = jnp.where(mask, w["fc_col"] - fcum_t[cf:cf + 1, :] + gt[ci:ci + 1, :], -jnp.inf)
        from_state = w["fc_col"] + w["m_prev"]
        w["m_t"] = jnp.maximum(from_state, jnp.max(dm, axis=1, keepdims=True))
        w["s"] = w["qk"] * scale * jnp.exp(dm - w["m_t"])
        w["w_state"] = jnp.exp(from_state - w["m_t"])
    for (bb, h), w in zip(units, st):
        n_h = n_sc.at[bb * NH + h]
        num = _dot(w["s"].astype(BF16), w["v"]) + w["w_state"] * (w["qc"] * scale)
        qn = jnp.sum(w["q"].astype(F32) * n_h[0:1, :], axis=1, keepdims=True) * scale
        den = jnp.sum(w["s"], axis=1, keepdims=True) + w["w_state"] * qn
        hout = num / jnp.maximum(jnp.abs(den), jnp.exp(-w["m_t"]))
        if rev:
            tot = yps_ref[bb, :, w["sl"]] + hout
            os_ref[bb, :, w["sl"]] = (tot * _sigmoid(ogs_ref[bb, :, w["sl"]].astype(F32))).astype(os_ref.dtype)
        else:
            os_ref[bb, :, w["sl"]] = hout
    for (bb, h), w in zip(units, st):
        c_h, n_h, m_h = c_sc.at[bb * NH + h], n_sc.at[bb * NH + h], m_sc.at[bb * NH + h]
        f_tot = jnp.sum(w["f_col"], axis=0, keepdims=True)
        w_log = f_tot - w["fc_col"] + w["i_col"]
        m_new = jnp.maximum(f_tot + w["m_prev"], jnp.max(w_log, axis=0, keepdims=True))
        decay = jnp.exp(f_tot + w["m_prev"] - m_new)
        kw = w["k"].astype(F32) * jnp.exp(w_log - m_new)
        c_h[...] = decay * c_h[...] + _dot(kw.T.astype(BF16), w["v"])
        n_new = decay * n_h[0:1, :] + jnp.sum(kw, axis=0, keepdims=True)
        n_h[...] = jnp.broadcast_to(n_new, (8, DH))
        m_h[...] = jnp.broadcast_to(m_new, (8, DH))


def _mlstm(u, usm, ust, bias_c, bias_t, rev, yprev=None):
    nb = u.shape[0]
    bpb = SCAN_BB if nb % SCAN_BB == 0 else 1
    cmap = _chunk_of_step(rev)
    uspec = lambda cblk: pl.BlockSpec((bpb, CS, BW), lambda b, s: (b, cmap(s), cblk))
    in_specs = [uspec(CB_ML_Q), uspec(CB_ML_K), uspec(CB_ML_V),
                pl.BlockSpec((bpb, CS, 128), lambda b, s: (b, cmap(s), 0)),
                pl.BlockSpec((bpb, None, 32, CS), lambda b, s: (b, cmap(s), 0, 0)),
                pl.BlockSpec((1, 128), lambda b, s: (0, 0)),
                pl.BlockSpec((32, 1), lambda b, s: (0, 0))]
    args = [u, u, u, usm, ust, bias_c, bias_t]
    if rev:
        in_specs += [uspec(0), uspec(CB_ML_O)]
        args += [yprev, u]
    return pl.pallas_call(
        functools.partial(_mlstm_kernel, rev, bpb),
        grid=(nb // bpb, NCH),
        in_specs=in_specs,
        out_specs=uspec(0),
        out_shape=jax.ShapeDtypeStruct((nb, LT, BW), BF16 if rev else F32),
        scratch_shapes=[pltpu.VMEM((bpb * NH, DH, DH), F32), pltpu.VMEM((bpb * NH, 8, DH), F32),
                        pltpu.VMEM((bpb * NH, 8, DH), F32)],
        compiler_params=_cparams(("parallel", "arbitrary")),
        name="mlstm_rev" if rev else "mlstm_fwd",
    )(*args)


HG_BAND = 2


def _block_ref(g, n, off):
    g3 = g.reshape(CS // n, n, DH)
    r = g3[:, off:off + 1, :]
    return jnp.broadcast_to(r, (CS // n, n, DH)).reshape(CS, DH)


def _hgrn2_kernel(rev, bpb, *refs):
    if rev:
        qs_ref, is_ref, fs_ref, lb_ref, yps_ref, gs_ref, ng_ref, os_ref, st_sc = refs
    else:
        qs_ref, is_ref, fs_ref, lb_ref, os_ref, st_sc = refs

    @pl.when(pl.program_id(1) == 0)
    def _():
        st_sc[...] = jnp.zeros_like(st_sc)

    mask = _chunk_mask(rev)
    mb = jnp.where(mask, 1.0, 0.0).astype(BF16)
    row = lax.broadcasted_iota(jnp.int32, (CS, CS), 0)
    col = lax.broadcasted_iota(jnp.int32, (CS, CS), 1)
    tix = lax.broadcasted_iota(jnp.int32, (CS, 1), 0)
    for bb, h in [(bb, h) for bb in range(bpb) for h in range(NH)]:
        q_ref, i_ref, f_ref, o_ref = qs_ref.at[bb], is_ref.at[bb], fs_ref.at[bb], os_ref.at[bb]
        if rev:
            yp_ref, g_ref = yps_ref.at[bb], gs_ref.at[bb]
        st_h = st_sc.at[bb * NH + h]
        sl = slice(h * DH, (h + 1) * DH)
        q = _silu(q_ref[:, sl].astype(F32))
        v = i_ref[:, sl].astype(F32)
        lb = lb_ref[:, sl]
        f = lb + (1.0 - lb) * _sigmoid(f_ref[:, sl].astype(F32))
        k = 1.0 - f
        lf = jnp.log(f)
        g = _cumsum_rows(mb, lf)
        st = st_h[...]
        o = _dot_nt((q * jnp.exp(g)).astype(BF16), st.astype(BF16))
        a = jnp.zeros((CS, CS), F32)
        n = CS
        while n >= 2 * HG_BAND:
            half = n // 2
            second = jnp.bitwise_and(tix, n - 1) >= half
            rowside = jnp.logical_not(second) if rev else second
            r = _block_ref(g, n, half if rev else half - 1)
            e = jnp.exp(jnp.where(rowside, g - r, r - g))
            qs = jnp.where(rowside, q * e, 0.0).astype(BF16)
            ks = jnp.where(rowside, 0.0, k * e).astype(BF16)
            sh_n = n.bit_length() - 1
            same = jnp.right_shift(row, sh_n) == jnp.right_shift(col, sh_n)
            a = a + jnp.where(same, _dot_nt(qs, ks), 0.0)
            n = half
        o = o + _dot(a.astype(BF16), v.astype(BF16))
        t8 = jnp.bitwise_and(tix, HG_BAND - 1)
        k3, g3, v3 = (a3.reshape(CS // 8, 8, DH) for a3 in (k, g, v))
        for dl in range(HG_BAND):
            if dl == 0:
                ksh, gsh, vsh = k, g, v
            else:
                sh = (8 - dl) if rev else dl
                ksh = pltpu.roll(k3, sh, 1).reshape(CS, DH)
                gsh = pltpu.roll(g3, sh, 1).reshape(CS, DH)
                vsh = pltpu.roll(v3, sh, 1).reshape(CS, DH)
            valid = (t8 + dl <= HG_BAND - 1) if rev else (t8 >= dl)
            e = jnp.exp(jnp.minimum(g - gsh, 0.0))
            coef = jnp.sum(q * ksh * e, axis=1, keepdims=True)
            o = o + jnp.where(valid, coef, 0.0) * vsh
        if rev:
            tot = yp_ref[:, sl] + o
            y = _rms(tot) * ng_ref[:, sl]
            o_ref[:, sl] = (y * _sigmoid(g_ref[:, sl].astype(F32))).astype(o_ref.dtype)
        else:
            o_ref[:, sl] = o
        g_tot = jnp.sum(lf, axis=0, keepdims=True)
        kdec = k * jnp.exp(g_tot - g)
        st_h[...] = st * jnp.exp(g_tot) + _dot(v.T.astype(BF16), kdec.astype(BF16))


def _hgrn2(u, lb, rev, yprev=None, norm_g=None):
    nb = u.shape[0]
    bpb = SCAN_BB if nb % SCAN_BB == 0 else 1
    cmap = _chunk_of_step(rev)
    d = 1 if rev else 0
    uspec = lambda cblk: pl.BlockSpec((bpb, CS, BW), lambda b, s: (b, cmap(s), cblk))
    in_specs = [uspec(CB_HG_Q), uspec(CB_HG_I), uspec(CB_HG_F + d),
                pl.BlockSpec((1, BW), lambda b, s: (0, 0))]
    args = [u, u, u, lb]
    if rev:
        in_specs += [uspec(0), uspec(CB_HG_G), pl.BlockSpec((1, BW), lambda b, s: (0, 0))]
        args += [yprev, u, norm_g]
    return pl.pallas_call(
        functools.partial(_hgrn2_kernel, rev, bpb),
        grid=(nb // bpb, NCH),
        in_specs=in_specs,
        out_specs=uspec(0),
        out_shape=jax.ShapeDtypeStruct((nb, LT, BW), BF16 if rev else F32),
        scratch_shapes=[pltpu.VMEM((bpb * NH, DH, DH), F32)],
        compiler_params=_cparams(("parallel", "arbitrary")),
        name="hgrn2_rev" if rev else "hgrn2_fwd",
    )(*args)


def _expand_heads(cols, lane_head):
    r = cols[3]
    for hh in (2, 1, 0):
        r = jnp.where(lane_head == hh, cols[hh], r)
    return r


def _ssd_kernel(rev, bpb, *refs):
    if rev:
        (xs_ref, gcs_ref, gts_ref, bc_ref, bt_ref, ac_ref, at_ref, yps_ref, zs_ref, dk_ref, ng_ref,
         os_ref, ht_sc) = refs
    else:
        xs_ref, gcs_ref, gts_ref, bc_ref, bt_ref, ac_ref, at_ref, os_ref, ht_sc = refs

    @pl.when(pl.program_id(1) == 0)
    def _():
        ht_sc[...] = jnp.zeros_like(ht_sc)

    for bb in range(bpb):
        extra = (yps_ref.at[bb], zs_ref.at[bb], dk_ref, ng_ref) if rev else ()
        _ssd_chunk(rev, xs_ref.at[bb], gcs_ref[bb], gts_ref[bb], bc_ref, bt_ref, ac_ref, at_ref,
                   os_ref.at[bb], ht_sc, bb * SSD_GROUPS, *extra)


def _ssd_chunk(rev, x_ref, gc, gt, bc_ref, bt_ref, ac_ref, at_ref, o_ref, ht_sc, ht0, yp_ref=None, z_ref=None,
               dk_ref=None, ng_ref=None):
    d = 1 if rev else 0
    mask = _chunk_mask(rev)
    mb = jnp.where(mask, 1.0, 0.0).astype(BF16)
    dt_c = _softplus(gc + bc_ref[...])
    da_c = dt_c * ac_ref[...]
    acum_c = _cumsum_rows(mb, da_c)
    dt_t = _softplus(gt + bt_ref[...])
    da_t = dt_t * at_ref[...]
    acum_t = _cumsum_cols(da_t, mb)
    gw = 4 * SSD_P
    lane_head = jnp.right_shift(lax.broadcasted_iota(jnp.int32, (1, gw), 1), 6)
    ys = []
    for grp in range(SSD_GROUPS):
        xg = x_ref[:, grp * gw:(grp + 1) * gw]
        bm = x_ref[:, BW + grp * SSD_N:BW + (grp + 1) * SSD_N]
        cm = x_ref[:, BW + SSD_GROUPS * SSD_N + grp * SSD_N:BW + SSD_GROUPS * SSD_N + (grp + 1) * SSD_N]
        cmb = cm.astype(BF16)
        cb = _dot_nt(cmb, bm.astype(BF16))
        ht = ht_sc[ht0 + grp]
        y_state = _dot(cmb, ht.astype(BF16))
        lanes = [16 + d * 8 + grp * 4 + hh for hh in range(4)]
        dt_cols = [dt_c[:, l:l + 1] for l in lanes]
        a_cols = [acum_c[:, l:l + 1] for l in lanes]
        a_tots = [jnp.sum(da_c[:, l:l + 1], axis=0, keepdims=True) for l in lanes]
        xdt = xg * _expand_heads(dt_cols, lane_head)
        xdtb = xdt.astype(BF16)
        y = jnp.exp(_expand_heads(a_cols, lane_head)) * y_state
        for hh in range(4):
            l = lanes[hh]
            dec = jnp.exp(jnp.where(mask, a_cols[hh] - acum_t[l:l + 1, :], -jnp.inf))
            yh = _dot((cb * dec).astype(BF16), xdtb)
            y = y + jnp.where(lane_head == hh, yh, 0.0)
        ys.append(y)
        w_exp = jnp.exp(_expand_heads([a_tots[hh] - a_cols[hh] for hh in range(4)], lane_head))
        e_row = jnp.exp(_expand_heads(a_tots, lane_head))
        ht_sc[ht0 + grp] = ht * e_row + _dot(bm.T.astype(BF16), (xdt * w_exp).astype(BF16))
    y = jnp.concatenate(ys, axis=1)
    if rev:
        tot = yp_ref[...] + y + dk_ref[...] * x_ref[:, :BW]
        tot = tot * _silu(z_ref[...].astype(F32))
        o_ref[...] = (_rms(tot) * ng_ref[...]).astype(o_ref.dtype)
    else:
        o_ref[...] = y


def _ssd(xbc, usm, ust, u, bias_c, bias_t, aneg_c, aneg_t, rev, yprev=None, dskip=None, norm_g=None):
    nb = xbc.shape[0]
    bpb = SCAN_BB if nb % SCAN_BB == 0 else 1
    cmap = _chunk_of_step(rev)
    in_specs = [pl.BlockSpec((bpb, CS, SSD_XBC), lambda b, s: (b, cmap(s), 0)),
                pl.BlockSpec((bpb, CS, 128), lambda b, s: (b, cmap(s), 0)),
                pl.BlockSpec((bpb, None, 32, CS), lambda b, s: (b, cmap(s), 0, 0)),
                pl.BlockSpec((1, 128), lambda b, s: (0, 0)),
                pl.BlockSpec((32, 1), lambda b, s: (0, 0)),
                pl.BlockSpec((1, 128), lambda b, s: (0, 0)),
                pl.BlockSpec((32, 1), lambda b, s: (0, 0))]
    args = [xbc, usm, ust, bias_c, bias_t, aneg_c, aneg_t]
    if rev:
        in_specs += [pl.BlockSpec((bpb, CS, BW), lambda b, s: (b, cmap(s), 0)),
                     pl.BlockSpec((bpb, CS, BW), lambda b, s: (b, cmap(s), CB_SSD_Z)),
                     pl.BlockSpec((1, BW), lambda b, s: (0, 0)),
                     pl.BlockSpec((1, BW), lambda b, s: (0, 0))]
        args += [yprev, u, dskip, norm_g]
    return pl.pallas_call(
        functools.partial(_ssd_kernel, rev, bpb),
        grid=(nb // bpb, NCH),
        in_specs=in_specs,
        out_specs=pl.BlockSpec((bpb, CS, BW), lambda b, s: (b, cmap(s), 0)),
        out_shape=jax.ShapeDtypeStruct((nb, LT, BW), BF16 if rev else F32),
        scratch_shapes=[pltpu.VMEM((bpb * SSD_GROUPS, SSD_N, 4 * SSD_P), F32)],
        compiler_params=_cparams(("parallel", "arbitrary")),
        name="ssd_rev" if rev else "ssd_fwd",
    )(*args)


def _mla_proj_kernel(cq_ref, ckv_ref, kr_ref, gq_ref, gkv_ref, wqa_ref, wqb_ref, wk_ref, wv_ref,
                     rk_ref, cos_ref, sin_ref, q_ref, k_ref, v_ref):
    nq = (_rms(cq_ref[...].astype(F32)) * gq_ref[...]).astype(BF16)
    qa = _dot(nq, wqa_ref[...])
    qb = _dot(nq, wqb_ref[...])
    cos_t = cos_ref[...]
    sin_t = sin_ref[...]
    cosf = jnp.concatenate([cos_t] * NH, axis=1)
    sinf = jnp.concatenate([sin_t] * NH, axis=1)
    q_ref[...] = ((qa * cosf + qb * sinf) * MLA_SCALE).astype(BF16)
    nkv = (_rms(ckv_ref[...].astype(F32)) * gkv_ref[...]).astype(BF16)
    kn = _dot(nkv, wk_ref[...])
    v_ref[...] = _dot(nkv, wv_ref[...]).astype(BF16)
    kr = kr_ref[...]
    krr = kr.astype(F32) * cos_t[:, 128:256] + _dot(kr, rk_ref[...]) * sin_t[:, 128:256]
    zero = jnp.zeros_like(krr)
    k_ref[...] = (kn + jnp.concatenate([zero, krr] * NH, axis=1)).astype(BF16)


def _mla_proj(u, gq, gkv, wqa, wqb, wk, wv, rk, cos_t, sin_t):
    nb = u.shape[0]
    const = lambda shape: pl.BlockSpec(shape, lambda b, i: (0, 0))
    return pl.pallas_call(
        _mla_proj_kernel,
        grid=(nb, LT // AT),
        in_specs=[pl.BlockSpec((None, AT, MLA_Q_RANK), lambda b, i: (b, i, CB_MLA_CQ)),
                  pl.BlockSpec((None, AT, MLA_KV_RANK), lambda b, i: (b, i, CB_MLA_CKV)),
                  pl.BlockSpec((None, AT, 128), lambda b, i: (b, i, CB_MLA_KR)),
                  const((1, MLA_Q_RANK)), const((1, MLA_KV_RANK)),
                  const((MLA_Q_RANK, NH * 256)), const((MLA_Q_RANK, NH * 256)),
                  const((MLA_KV_RANK, NH * 256)), const((MLA_KV_RANK, NH * DH)),
                  const((128, 128)),
                  pl.BlockSpec((AT, 256), lambda b, i: (i, 0)),
                  pl.BlockSpec((AT, 256), lambda b, i: (i, 0))],
        out_specs=[_row_spec(NH * 256, AT), _row_spec(NH * 256, AT), _row_spec(NH * DH, AT)],
        out_shape=[jax.ShapeDtypeStruct((nb, LT, NH * 256), BF16),
                   jax.ShapeDtypeStruct((nb, LT, NH * 256), BF16),
                   jax.ShapeDtypeStruct((nb, LT, NH * DH), BF16)],
        compiler_params=_cparams(("parallel", "parallel")),
        name="mla_proj",
    )(u, u, u, gq, gkv, wqa, wqb, wk, wv, rk, cos_t, sin_t)


AT = 1152


def _attn_kernel(q_ref, k_ref, v_ref, o_ref):
    s = _dot_nt(q_ref[...], k_ref[...])

    def finish(sc):
        m = jnp.max(sc, axis=1, keepdims=True)
        p = jnp.exp(sc - m)
        l = jnp.sum(p, axis=1, keepdims=True)
        o_ref[...] = (_dot(p.astype(BF16), v_ref[...]) / l).astype(o_ref.dtype)

    last = pl.program_id(2) == LT // AT - 1

    @pl.when(last)
    def _():
        qrow = lax.broadcasted_iota(jnp.int32, (AT, 1), 0)
        kcol = lax.broadcasted_iota(jnp.int32, (1, LT), 1)
        hide = jnp.logical_and(qrow >= AT - CTX, kcol < SEQ)
        finish(jnp.where(hide, -jnp.inf, s))

    @pl.when(jnp.logical_not(last))
    def _():
        finish(s)


def _attention(q, k, v):
    nb = q.shape[0]
    return pl.pallas_call(
        _attn_kernel,
        grid=(nb, NH, LT // AT),
        in_specs=[pl.BlockSpec((None, AT, 256), lambda b, h, i: (b, i, h)),
                  pl.BlockSpec((None, LT, 256), lambda b, h, i: (b, 0, h)),
                  pl.BlockSpec((None, LT, DH), lambda b, h, i: (b, 0, h))],
        out_specs=pl.BlockSpec((None, AT, DH), lambda b, h, i: (b, i, h)),
        out_shape=jax.ShapeDtypeStruct((nb, LT, NH * DH), BF16),
        compiler_params=_cparams(("parallel", "parallel", "arbitrary")),
        name="mla_attn",
    )(q, k, v)


def _merge_kernel(h_ref, b0_ref, b1_ref, b2_ref, b3_ref, wg_ref, wb_ref, o_ref):
    h = h_ref[...]
    acc = None
    for kk, br in enumerate((b0_ref, b1_ref, b2_ref, b3_ref)):
        gate = _sigmoid(_dot(h, wg_ref[kk]))
        term = gate * _dot(br[...], wb_ref[kk])
        acc = term if acc is None else acc + term
    o_ref[...] = acc.astype(o_ref.dtype)


def _merge(h, branches, wg, wb, rows, tn=256):
    nb = h.shape[0]
    tm = rows // 2
    rmap = lambda i, j: (i // 2, i % 2, 0)
    bspec = pl.BlockSpec((None, tm, BW), rmap)
    return pl.pallas_call(
        _merge_kernel,
        grid=(nb * 2, D // tn),
        in_specs=[pl.BlockSpec((None, tm, D), rmap), bspec, bspec, bspec, bspec,
                  pl.BlockSpec((4, D, tn), lambda i, j: (0, 0, j)),
                  pl.BlockSpec((4, BW, tn), lambda i, j: (0, 0, j))],
        out_specs=pl.BlockSpec((None, tm, tn), lambda i, j: (i // 2, i % 2, j)),
        out_shape=jax.ShapeDtypeStruct((nb, rows, D), BF16),
        compiler_params=_cparams(("parallel", "arbitrary")),
        name="merge",
    )(h, *branches, wg, wb)


def _outproj_kernel(first, a_ref, w_ref, *refs):
    if first:
        x_ref, c_ref = refs[:2]
        x_in = _pick_x(x_ref, c_ref)
        refs = refs[2:]
    else:
        x_in = refs[0][...]
        refs = refs[1:]
    g1_ref, gate_ref, g2_ref, sh_ref, sc_ref, wa_ref, wb_ref, xo_ref, hp_ref, e_ref, wt_ref = refs
    y = _dot(a_ref[...], w_ref[...])
    xn = x_in + gate_ref[...] * (_rms(y) * g1_ref[...])
    xo_ref[...] = xn
    h2 = _rms(xn) * g2_ref[...]
    h2 = h2 * (1.0 + sc_ref[...]) + sh_ref[...]
    hp_ref[...] = h2
    _route_math(h2, wa_ref, wb_ref, e_ref, wt_ref)


def _outproj(acc, w_out, xs, normv, modv, wr2, layer):
    nb, rows, _ = acc.shape
    first = len(xs) == 2
    wspec = pl.BlockSpec((D, 128), lambda b, i: (0, 0))
    return pl.pallas_call(
        functools.partial(_outproj_kernel, first),
        grid=(nb, rows // RT),
        in_specs=[_row_spec(D), pl.BlockSpec((D, D), lambda b, i: (0, 0))]
                 + (_split_x_specs() if first else [_row_spec(D)])
                 + [_norm_spec(layer, 1), _mod_spec(layer, 2), _norm_spec(layer, 2),
                    _mod_spec(layer, 3), _mod_spec(layer, 4), wspec, wspec],
        out_specs=[_row_spec(D), _row_spec(D), _row_spec(128), _row_spec(128)],
        out_shape=[jax.ShapeDtypeStruct((nb, rows, D), F32), jax.ShapeDtypeStruct((nb, rows, D), F32),
                   jax.ShapeDtypeStruct((nb, rows, 128), jnp.int32), jax.ShapeDtypeStruct((nb, rows, 128), F32)],
        compiler_params=_cparams(("parallel", "parallel")),
        name="outproj",
    )(acc, w_out, *xs, normv, modv, normv, modv, modv, *wr2)


def _route_math(hf, wa_ref, wb_ref, e_ref, w_ref):
    h = hf.astype(BF16)
    hl = (hf - h.astype(F32)).astype(BF16)
    logits = _dot(h, wa_ref[...]) + _dot(h, wb_ref[...]) + _dot(hl, wa_ref[...])
    lane = lax.broadcasted_iota(jnp.int32, (1, 128), 1).astype(F32)
    ninf = -jnp.inf
    is_grp = lane < N_GROUPS
    gl = jnp.where(is_grp, logits, ninf)
    gmax = jnp.max(gl, axis=1, keepdims=True)
    gidx = jnp.min(jnp.where(gl == gmax, lane, 999.0), axis=1, keepdims=True)
    gsum = jnp.sum(jnp.where(is_grp, jnp.exp(gl - gmax), 0.0), axis=1, keepdims=True)
    gp = 1.0 / gsum
    lo = N_GROUPS + EXP_PER_GROUP * gidx
    sel = jnp.logical_and(lane >= lo, lane < lo + EXP_PER_GROUP)
    el = jnp.where(sel, logits, ninf)
    v1 = jnp.max(el, axis=1, keepdims=True)
    i1 = jnp.min(jnp.where(el == v1, lane, 999.0), axis=1, keepdims=True)
    el2 = jnp.where(lane == i1, ninf, el)
    v2 = jnp.max(el2, axis=1, keepdims=True)
    i2 = jnp.min(jnp.where(el2 == v2, lane, 999.0), axis=1, keepdims=True)
    t = jnp.exp(v2 - v1)
    w1 = gp / (1.0 + t)
    w2 = gp * t / (1.0 + t)
    e_ref[...] = jnp.where(lane == 0.0, i1 - N_GROUPS, jnp.where(lane == 1.0, i2 - N_GROUPS, 0.0)).astype(jnp.int32)
    w_ref[...] = jnp.where(lane == 0.0, w1, jnp.where(lane == 1.0, w2, 0.0))


GATHER_UNROLL = 8


def _expert_kernel(te_ref, nu_ref, rt_ref, h_hbm, w1_ref, w3_ref, w2_ref, o_ref, xbuf, hm_sc, w1b, w3b, w2b, sem):
    i = pl.program_id(0)
    n_used = nu_ref[0]
    slot = i % 2

    def issue_tile(j, s):
        base = j * TE

        def body(g, c):
            for uu in range(GATHER_UNROLL):
                r = g * GATHER_UNROLL + uu
                pltpu.make_async_copy(h_hbm.at[pl.ds(rt_ref[base + r], 1), :],
                                      xbuf.at[s, pl.ds(r, 1), :], sem.at[s]).start(priority=1)
            return c
        lax.fori_loop(0, TE // GATHER_UNROLL, body, 0)

    def wait_tile(s):
        pltpu.make_async_copy(h_hbm.at[pl.ds(0, TE), :], xbuf.at[s], sem.at[s]).wait()

    @pl.when(jnp.logical_and(i == 0, n_used > 0))
    def _():
        issue_tile(0, 0)

    @pl.when(i < n_used)
    def _():
        wait_tile(slot)
        nbase = jnp.minimum(i + 1, n_used - 1) * TE
        oslot = 1 - slot

        def issue_rows(r0, r1):
            for r in range(r0, r1):
                pltpu.make_async_copy(h_hbm.at[pl.ds(rt_ref[nbase + r], 1), :],
                                      xbuf.at[oslot, pl.ds(r, 1), :], sem.at[oslot]).start(priority=r % 2)

        @pl.when(jnp.logical_or(i == 0, te_ref[i] != te_ref[jnp.maximum(i - 1, 0)]))
        def _():
            w1b[...] = w1_ref[...].astype(BF16)
            w3b[...] = w3_ref[...].astype(BF16)
            w2b[...] = w2_ref[...].astype(BF16)

        x = xbuf[slot].astype(BF16)
        cw = 256
        n1, n2 = D_EXPERT // cw, D // cw
        per = TE // (n1 + n2) + 1
        done = 0
        for j in range(n1):
            issue_rows(done, min(done + per, TE))
            done = min(done + per, TE)
            cs = slice(j * cw, (j + 1) * cw)
            a = _dot(x, w1b[:, cs])
            b = _dot(x, w3b[:, cs])
            hm_sc[:, cs] = (_silu(a) * b).astype(BF16)
        hm = hm_sc[...]
        for j in range(n2):
            issue_rows(done, min(done + per, TE))
            done = min(done + per, TE)
            cs = slice(j * cw, (j + 1) * cw)
            o_ref[:, cs] = _dot(hm, w2b[:, cs])

        @pl.when(i + 1 >= n_used)
        def _():
            wait_tile(oslot)

    @pl.when(i >= n_used)
    def _():
        o_ref[...] = jnp.zeros_like(o_ref)


def _experts(tile_expert, n_used, row_token, hp, w1, w3, w2, n_tiles, layer):
    wmap = lambda i, te, nu, rt: (layer * N_EXPERTS + te[i], 0, 0)
    grid_spec = pltpu.PrefetchScalarGridSpec(
        num_scalar_prefetch=3,
        grid=(n_tiles,),
        in_specs=[pl.BlockSpec(memory_space=pl.ANY),
                  pl.BlockSpec((None, D, D_EXPERT), wmap),
                  pl.BlockSpec((None, D, D_EXPERT), wmap),
                  pl.BlockSpec((None, D_EXPERT, D), wmap)],
        out_specs=pl.BlockSpec((TE, D), lambda i, te, nu, rt: (i, 0)),
        scratch_shapes=[pltpu.VMEM((2, TE, D), F32), pltpu.VMEM((TE, D_EXPERT), BF16),
                        pltpu.VMEM((D, D_EXPERT), BF16), pltpu.VMEM((D, D_EXPERT), BF16),
                        pltpu.VMEM((D_EXPERT, D), BF16),
                        pltpu.SemaphoreType.DMA((2,))],
    )
    return pl.pallas_call(
        _expert_kernel,
        grid_spec=grid_spec,
        out_shape=jax.ShapeDtypeStruct((n_tiles * TE, D), F32),
        compiler_params=_cparams(("arbitrary",)),
        name="moe_experts",
    )(tile_expert, n_used, row_token, hp, w1, w3, w2)


def _combine_kernel(has_next, pos_ref, ys_hbm, w_ref, x_ref, g3_ref, gate_ref, *rest):
    if has_next:
        g0_ref, sh_ref, sc_ref, xo_ref, ho_ref, buf, sem = rest
    else:
        xo_ref, buf, sem = rest
    step = pl.program_id(0) * pl.num_programs(1) + pl.program_id(1)
    n_steps = pl.num_programs(0) * pl.num_programs(1)
    slot = step % 2

    def issue_tile(j, s):
        t0 = j * RT

        def body(g, c):
            for uu in range(GATHER_UNROLL):
                r = g * GATHER_UNROLL + uu
                for kk in range(2):
                    pltpu.make_async_copy(ys_hbm.at[pl.ds(pos_ref[(t0 + r) * 2 + kk], 1), :],
                                          buf.at[s, kk, pl.ds(r, 1), :], sem.at[s]).start(priority=kk)
            return c
        lax.fori_loop(0, RT // GATHER_UNROLL, body, 0)

    def wait_tile(s):
        for kk in range(2):
            pltpu.make_async_copy(ys_hbm.at[pl.ds(0, RT), :], buf.at[s, kk], sem.at[s]).wait()

    @pl.when(step == 0)
    def _():
        issue_tile(0, 0)

    wait_tile(slot)
    nt0 = jnp.minimum(step + 1, n_steps - 1) * RT
    oslot = 1 - slot
    rows = 32
    for c in range(RT // rows):
        for r in range(c * rows, (c + 1) * rows):
            for kk in range(2):
                pltpu.make_async_copy(ys_hbm.at[pl.ds(pos_ref[(nt0 + r) * 2 + kk], 1), :],
                                      buf.at[oslot, kk, pl.ds(r, 1), :], sem.at[oslot]).start(priority=kk)
        rs = slice(c * rows, (c + 1) * rows)
        w = w_ref[rs, :]
        y = w[:, 0:1] * buf[slot, 0, rs, :] + w[:, 1:2] * buf[slot, 1, rs, :]
        xn = x_ref[rs, :] + gate_ref[...] * (_rms(y) * g3_ref[...])
        xo_ref[rs, :] = xn
        if has_next:
            hn = _rms(xn) * g0_ref[...]
            ho_ref[rs, :] = (hn * (1.0 + sc_ref[...]) + sh_ref[...]).astype(BF16)

    @pl.when(step + 1 >= n_steps)
    def _():
        wait_tile(oslot)


def _combine(pos, ys, wts, xcat, normv, modv, layer):
    nb = xcat.shape[0]
    has_next = layer + 1 < DEPTH

    def mod_spec(lyr, k):
        def imap(b, i, pos_ref):
            row = jnp.where(i == NRT - 1, 8, b)
            return ((lyr * 16 + row) * 6 + k, 0, 0)
        return pl.BlockSpec((None, 1, D), imap)

    def norm_spec(lyr, k):
        return pl.BlockSpec((None, 1, D), lambda b, i, pos_ref: (lyr * 4 + k, 0, 0))

    def row_spec(width):
        return pl.BlockSpec((None, RT, width), lambda b, i, pos_ref: (b, i, 0))

    in_specs = [pl.BlockSpec(memory_space=pl.ANY), row_spec(128), row_spec(D),
                norm_spec(layer, 3), mod_spec(layer, 5)]
    args = [ys, wts, xcat, normv, modv]
    out_specs = [row_spec(D)]
    n_row_tiles = NRT if has_next else SEQ // RT
    out_shape = [jax.ShapeDtypeStruct((nb, n_row_tiles * RT, D), F32)]
    if has_next:
        in_specs += [norm_spec(layer + 1, 0), mod_spec(layer + 1, 0), mod_spec(layer + 1, 1)]
        args += [normv, modv, modv]
        out_specs.append(row_spec(D))
        out_shape.append(jax.ShapeDtypeStruct((nb, LT, D), BF16))
    grid_spec = pltpu.PrefetchScalarGridSpec(
        num_scalar_prefetch=1,
        grid=(nb, n_row_tiles),
        in_specs=in_specs,
        out_specs=out_specs,
        scratch_shapes=[pltpu.VMEM((2, 2, RT, D), F32), pltpu.SemaphoreType.DMA((2,))],
    )
    return pl.pallas_call(
        functools.partial(_combine_kernel, has_next),
        grid_spec=grid_spec,
        out_shape=out_shape,
        compiler_params=_cparams(("arbitrary", "arbitrary")),
        name="moe_combine",
    )(pos, *args)


def _plan_kernel(e_ref, pos_ref, cnt_ref, rank_sc, carry_sc):
    phase = pl.program_id(0)
    i = pl.program_id(1)
    lane = lax.broadcasted_iota(jnp.int32, (1, 128), 1)
    e = e_ref[...]
    oh1 = lane == e[:, 0:1]
    oh2 = lane == e[:, 1:2]

    @pl.when(jnp.logical_and(phase == 0, i == 0))
    def _():
        carry_sc[...] = jnp.zeros_like(carry_sc)

    @pl.when(phase == 0)
    def _():
        s = jnp.where(jnp.logical_or(oh1, oh2), 1.0, 0.0)
        pt = e_ref.shape[0]
        row = lax.broadcasted_iota(jnp.int32, (pt, pt), 0)
        col = lax.broadcasted_iota(jnp.int32, (pt, pt), 1)
        lower = jnp.where(col < row, 1.0, 0.0).astype(BF16)
        carry = carry_sc[0:1, :]
        before = _dot(lower, s.astype(BF16)) + carry
        r1 = jnp.sum(jnp.where(oh1, before, 0.0), axis=1, keepdims=True)
        r2 = jnp.sum(jnp.where(oh2, before, 0.0), axis=1, keepdims=True)
        rank_sc[i] = jnp.where(lane == 0, r1, jnp.where(lane == 1, r2, 0.0))
        carry_sc[...] = jnp.broadcast_to(carry + jnp.sum(s, axis=0, keepdims=True), carry_sc.shape)
        cnt_ref[...] = carry_sc[...]

    @pl.when(phase == 1)
    def _():
        counts = carry_sc[0:1, :]
        padded = jnp.floor((counts + (TE - 1)) * (1.0 / TE)) * TE
        r128 = lax.broadcasted_iota(jnp.int32, (128, 128), 0)
        c128 = lax.broadcasted_iota(jnp.int32, (128, 128), 1)
        upper = jnp.where(r128 < c128, 1.0, 0.0).astype(BF16)
        pa, pb, pc = _split3(jnp.broadcast_to(padded, (8, 128)))
        offs = (_dot(pa, upper) + _dot(pb, upper) + _dot(pc, upper))[0:1, :]
        rank = rank_sc[i]
        p1 = rank[:, 0:1] + jnp.sum(jnp.where(oh1, offs, 0.0), axis=1, keepdims=True)
        p2 = rank[:, 1:2] + jnp.sum(jnp.where(oh2, offs, 0.0), axis=1, keepdims=True)
        pos_ref[...] = jnp.where(lane == 0, p1, jnp.where(lane == 1, p2, 0.0)).astype(jnp.int32)


def _plan(eid):
    nb, rows, _ = eid.shape
    pt = 768 if rows % 768 == 0 else 1024
    tpb = rows // pt
    nt = nb * tpb
    emap = lambda p, i: (i // tpb, i % tpb, 0)
    return pl.pallas_call(
        _plan_kernel,
        grid=(2, nt),
        in_specs=[pl.BlockSpec((None, pt, 128), emap)],
        out_specs=[pl.BlockSpec((pt, 128), lambda p, i: (p * i, 0)),
                   pl.BlockSpec((8, 128), lambda p, i: (0, 0))],
        out_shape=[jax.ShapeDtypeStruct((nt * pt, 128), jnp.int32), jax.ShapeDtypeStruct((8, 128), F32)],
        scratch_shapes=[pltpu.VMEM((nt, pt, 128), F32), pltpu.VMEM((8, 128), F32)],
        compiler_params=_cparams(("arbitrary", "arbitrary")),
        name="moe_plan",
    )(eid)


def _invert_kernel(pos_ref, rt_ref):
    unroll = 32

    def zero(i, c):
        for uu in range(unroll):
            rt_ref[i * unroll + uu] = 0
        return c
    lax.fori_loop(0, rt_ref.shape[0] // unroll, zero, 0)

    def scatter(i, c):
        for uu in range(unroll):
            a = i * unroll + uu
            rt_ref[pos_ref[a]] = jnp.right_shift(a, 1)
        return c
    lax.fori_loop(0, pos_ref.shape[0] // unroll, scatter, 0)


def _invert(pos_flat, n_rows):
    grid_spec = pltpu.PrefetchScalarGridSpec(
        num_scalar_prefetch=1, grid=(1,), in_specs=[],
        out_specs=pl.BlockSpec(memory_space=pltpu.SMEM))
    return pl.pallas_call(
        _invert_kernel,
        grid_spec=grid_spec,
        out_shape=jax.ShapeDtypeStruct((n_rows,), jnp.int32),
        compiler_params=_cparams(("arbitrary",)),
        name="moe_invert",
    )(pos_flat)


def _dispatch_tiles(pos, counts, n_tiles, n_row_tiles, nb):
    cnt = counts[0, :N_EXPERTS].astype(jnp.int32)
    padded = ((cnt + TE - 1) // TE) * TE
    ends = jnp.cumsum(padded)
    tile_start = jnp.arange(n_tiles, dtype=jnp.int32) * TE
    tile_expert = jnp.sum((ends[None, :] <= tile_start[:, None]).astype(jnp.int32), axis=1)
    tile_expert = jnp.minimum(tile_expert, N_EXPERTS - 1)
    n_used = (ends[-1] // TE).astype(jnp.int32).reshape(1)
    pos_flat = pos[:, :2].reshape(-1)
    row_token = _invert(pos_flat, n_tiles * TE)
    return pos_flat, row_token, tile_expert, n_used


def _rope_tables():
    tok = jnp.arange(SEQ)
    pos_row = (tok // GRID_W).astype(F32)
    pos_col = (tok % GRID_W).astype(F32)
    quarter = MLA_ROPE // 4
    inv_freq = ROPE_BASE ** (-jnp.arange(quarter, dtype=F32) / quarter)
    ang_r = pos_row[:, None] * inv_freq
    ang_c = pos_col[:, None] * inv_freq
    cos = jnp.concatenate([jnp.cos(ang_r)] * 2 + [jnp.cos(ang_c)] * 2, axis=1)
    sin = jnp.concatenate([jnp.sin(ang_r)] * 2 + [jnp.sin(ang_c)] * 2, axis=1)
    cos = jnp.concatenate([cos, jnp.ones((CTX, MLA_ROPE), F32)], axis=0)
    sin = jnp.concatenate([sin, jnp.zeros((CTX, MLA_ROPE), F32)], axis=0)
    pad = jnp.zeros((LT, 64), F32)
    cos_t = jnp.concatenate([jnp.ones((LT, 128), F32), cos, pad], axis=1)
    sin_t = jnp.concatenate([jnp.zeros((LT, 128), F32), sin, pad], axis=1)
    return cos_t, sin_t


def _rope_rot_matrix():
    j = jnp.arange(MLA_ROPE)
    first = (j % 32) < 16
    src = jnp.where(first, j + 16, j - 16)
    sign = jnp.where(first, -1.0, 1.0)
    return jnp.zeros((MLA_ROPE, MLA_ROPE), F32).at[src, j].set(sign)


def _layer_params(l, w_main, ml_f_bias, lb_all, hg_norm_g, mla_g_q, mla_g_kv, mla_w_uq, mla_w_ukv,
                  ssd_conv_w, ssd_conv_b, ssd_a_log, ssd_dt_bias, ssd_d, ssd_norm_g, w_gate, w_br, w_out,
                  moe_w_grp, moe_w_exp):
    p = {}
    p["w_main"] = w_main[l]
    fb = jnp.zeros((2, 2, NH), F32).at[:, 1, :].set(ml_f_bias[l]).reshape(16)
    ml_bias = jnp.concatenate([fb, jnp.zeros((112,), F32)])
    p["ml_bias_c"] = ml_bias.reshape(1, 128)
    p["ml_bias_t"] = ml_bias[:32].reshape(32, 1)
    dtb = jnp.concatenate([jnp.zeros((16,), F32), ssd_dt_bias[l].reshape(16), jnp.zeros((96,), F32)])
    p["ssd_bias_c"] = dtb.reshape(1, 128)
    p["ssd_bias_t"] = dtb[:32].reshape(32, 1)
    aneg = jnp.concatenate([jnp.zeros((16,), F32), -jnp.exp(ssd_a_log[l].astype(F32)).reshape(16),
                            jnp.zeros((96,), F32)])
    p["ssd_aneg_c"] = aneg.reshape(1, 128)
    p["ssd_aneg_t"] = aneg[:32].reshape(32, 1)
    p["lb"] = lb_all[l].reshape(1, BW)
    p["hg_norm_g"] = hg_norm_g[l].reshape(1, BW)
    p["ssd_dskip"] = jnp.repeat(ssd_d[l], SSD_P).reshape(1, BW)
    p["ssd_norm_g"] = ssd_norm_g[l].reshape(1, BW)
    p["conv_w"] = jnp.concatenate([ssd_conv_w[l], jnp.zeros((5, SSD_XBC), F32)], axis=0)
    p["conv_b"] = ssd_conv_b[l].reshape(1, SSD_XBC)
    rot = _rope_rot_matrix()
    wq = mla_w_uq[l].reshape(MLA_Q_RANK, NH, MLA_NOPE + MLA_ROPE)
    zq = jnp.zeros((MLA_Q_RANK, NH, 64), F32)
    wqa = jnp.concatenate([wq, zq], axis=2).reshape(MLA_Q_RANK, NH * 256)
    wq_rot = jnp.einsum("rhj,jk->rhk", wq[:, :, MLA_NOPE:], rot)
    wqb = jnp.concatenate([jnp.zeros((MLA_Q_RANK, NH, MLA_NOPE), F32), wq_rot, zq], axis=2)
    p["wqa"] = wqa.astype(BF16)
    p["wqb"] = wqb.reshape(MLA_Q_RANK, NH * 256).astype(BF16)
    wkv = mla_w_ukv[l].reshape(MLA_KV_RANK, NH, 2 * DH)
    wk = jnp.concatenate([wkv[:, :, :DH], jnp.zeros((MLA_KV_RANK, NH, DH), F32)], axis=2)
    p["wk"] = wk.reshape(MLA_KV_RANK, NH * 256).astype(BF16)
    p["wv"] = wkv[:, :, DH:].reshape(MLA_KV_RANK, NH * DH).astype(BF16)
    p["rk"] = jnp.zeros((128, 128), F32).at[:MLA_ROPE, :MLA_ROPE].set(rot).astype(BF16)
    p["gq"] = mla_g_q[l].reshape(1, MLA_Q_RANK)
    p["gkv"] = mla_g_kv[l].reshape(1, MLA_KV_RANK)
    p["w_gate"] = w_gate[l].astype(BF16)
    p["w_br"] = w_br[l].astype(BF16)
    p["w_out"] = w_out[l].astype(BF16)
    wr = jnp.concatenate([moe_w_grp[l], moe_w_exp[l], jnp.zeros((D, 128 - N_GROUPS - N_EXPERTS), F32)], axis=1)
    a = wr.astype(BF16)
    b = (wr - a.astype(F32)).astype(BF16)
    p["wr2"] = (a, b)
    return p


def _mixer(h, p, cos_t, sin_t, nb, out_rows):
    t = nb * LT
    hf = h.reshape(t, D)
    u, usm = _inproj(hf, p["w_main"])
    u = u.reshape(nb, LT, U_W)
    usm = usm.reshape(nb, LT, 128)
    ust = usm[:, :, :32].reshape(nb, NCH, CS, 32).transpose(0, 1, 3, 2)
    ml_f = _mlstm(u, usm, ust, p["ml_bias_c"], p["ml_bias_t"], False)
    ml = _mlstm(u, usm, ust, p["ml_bias_c"], p["ml_bias_t"], True, ml_f)
    hg_f = _hgrn2(u, p["lb"], False)
    hg = _hgrn2(u, p["lb"], True, hg_f, p["hg_norm_g"])
    q, k, v = _mla_proj(u, p["gq"], p["gkv"], p["wqa"], p["wqb"], p["wk"], p["wv"], p["rk"], cos_t, sin_t)
    mla = _attention(q, k, v)
    xbc = _ssd_conv(u, p["conv_w"], p["conv_b"])
    sargs = (xbc, usm, ust, u, p["ssd_bias_c"], p["ssd_bias_t"], p["ssd_aneg_c"], p["ssd_aneg_t"])
    ssd_f = _ssd(*sargs, False)
    ssd = _ssd(*sargs, True, ssd_f, p["ssd_dskip"], p["ssd_norm_g"])
    return _merge(h, (ml, hg, mla, ssd), p["w_gate"], p["w_br"], out_rows)


def kernel(x, c, ctx, c_ctx, w_ada, b_ada, norm_g, w_in, ml_f_bias, hg_lb_logits, hg_norm_g, mla_g_q, mla_g_kv,
           mla_w_uq, mla_w_ukv, ssd_conv_w, ssd_conv_b, ssd_a_log, ssd_dt_bias, ssd_d, ssd_norm_g, w_gate,
           w_br, w_out, moe_w_grp, moe_w_exp, moe_w1, moe_w3, moe_w2):
    nb = x.shape[0]
    t = nb * LT
    lb_all = jnp.cumsum(jax.nn.softmax(hg_lb_logits.astype(F32), axis=0), axis=0)
    cvec = jnp.zeros((16, D), F32).at[:nb].set(c).at[8].set(c_ctx)
    s_vec = jax.nn.silu(cvec).astype(BF16)
    modv = _adaln_mods(s_vec, w_ada, b_ada).reshape(DEPTH * 16 * 6, 1, D)
    normv = norm_g.reshape(DEPTH * 4, 1, D)
    cos_t, sin_t = _rope_tables()
    w_main = _win_layout(w_in)
    w1 = moe_w1.reshape(DEPTH * N_EXPERTS, D, D_EXPERT)
    w3 = moe_w3.reshape(DEPTH * N_EXPERTS, D, D_EXPERT)
    w2 = moe_w2.reshape(DEPTH * N_EXPERTS, D_EXPERT, D)
    h = _prenorm(x, ctx, normv, modv, 0)
    xs = (x, ctx)
    for l in range(DEPTH):
        p = _layer_params(l, w_main, ml_f_bias, lb_all, hg_norm_g, mla_g_q, mla_g_kv, mla_w_uq,
                          mla_w_ukv, ssd_conv_w, ssd_conv_b, ssd_a_log, ssd_dt_bias, ssd_d, ssd_norm_g, w_gate,
                          w_br, w_out, moe_w_grp, moe_w_exp)
        rows = LT if l + 1 < DEPTH else SEQ
        n_row_tiles = rows // RT
        acc = _mixer(h, p, cos_t, sin_t, nb, rows)
        xcat, hp, eid, wts = _outproj(acc, p["w_out"], xs, normv, modv, p["wr2"], l)
        n_tiles = (2 * nb * rows + N_EXPERTS * (TE - 1)) // TE + 1
        pos128, counts = _plan(eid)
        pos, row_token, tile_expert, n_used = _dispatch_tiles(pos128, counts, n_tiles, n_row_tiles, nb)
        ys = _experts(tile_expert, n_used, row_token, hp.reshape(nb * rows, D), w1, w3, w2, n_tiles, l)
        outs = _combine(pos, ys, wts, xcat, normv, modv, l)
        if l + 1 < DEPTH:
            xcat, h = outs
            xs = (xcat,)
    return outs[0]
```
